```python
import jax, jax.numpy as jnp
from jax import lax
import numpy as np

D_MODEL = 1024
BATCH = 2
SEQ = 8192
DEPTH = 4

CTX_LEN = 256
GRID_W = 64
N_GROUPS = 4
GROUP_W = D_MODEL // N_GROUPS
MIX_W = N_GROUPS * GROUP_W
GLA_HEADS = 4
GLA_DK = GROUP_W // GLA_HEADS
GLA_W = GLA_HEADS * GLA_DK
GLA_GATE_RANK = 16
GLA_TAU = 16.0
GLA_CHUNK = 64
GMLP_GROUPS = 4
GMLP_W = GROUP_W
GMLP_CHUNK = 128
LRU_W = GROUP_W
LRU_BLOCKS = 4
LRU_C = 8.0
CONV_W = 4
NAT_HEADS = 4
NAT_DH = GROUP_W // NAT_HEADS
NAT_W = NAT_HEADS * NAT_DH
NAT_KR_MAX = 8
NAT_KC = 16
NAT_QC = 16
NAT_KB = NAT_QC + NAT_KC
N_EXPERTS = 16
EXPERT_FF = 2 * D_MODEL
EC_CAPACITY = 2
ROPE_BASE = 10000.0
EPS = 1e-6

IN_SIZES = (GLA_W, GLA_W, GLA_W, GLA_W, GLA_GATE_RANK, GLA_GATE_RANK,
            GMLP_W, GMLP_W, LRU_W, LRU_W, NAT_W, NAT_W, NAT_W)
IN_COLS = sum(IN_SIZES)
IN_OFFSETS = tuple(sum(IN_SIZES[:i + 1]) for i in range(len(IN_SIZES) - 1))

kernel_name = "hybrid_gla_gmlp_rglru_natten_ec_moe_dit"


def rms_norm(x, g):
    xf = x.astype(jnp.float32)
    y = xf * lax.rsqrt(jnp.mean(xf * xf, axis=-1, keepdims=True) + EPS)
    return (y * g.astype(jnp.float32)).astype(x.dtype)


def split_heads(t, h):
    return t.reshape(t.shape[0], t.shape[1], h, -1)


def axial_rope(t, n):
    half = t.shape[-1] // 2
    quarter = half // 2
    pos = jnp.arange(n)
    row = (pos // GRID_W).astype(jnp.float32)
    col = (pos % GRID_W).astype(jnp.float32)
    inv = ROPE_BASE ** (-jnp.arange(quarter, dtype=jnp.float32) / quarter)

    def rot(u, p):
        ang = p[:, None] * inv[None, :]
        cos = jnp.cos(ang)[None, :, None, :].astype(u.dtype)
        sin = jnp.sin(ang)[None, :, None, :].astype(u.dtype)
        u1, u2 = u[..., :quarter], u[..., quarter:]
        return jnp.concatenate([u1 * cos - u2 * sin, u1 * sin + u2 * cos], axis=-1)

    return jnp.concatenate([rot(t[..., :half], row), rot(t[..., half:], col)], axis=-1)


def gla_log_decay(z, w, b):
    a = jnp.einsum('blr,rk->blk', z, w) + b
    la = jax.nn.log_sigmoid(a.astype(jnp.float32)) / GLA_TAU
    return la.reshape(z.shape[0], z.shape[1], GLA_HEADS, GLA_DK)


def gla_chunk_scan(q, k, v, la, s0):
    B, L, H, _ = q.shape
    nc = L // GLA_CHUNK

    def blk(t):
        return t.astype(jnp.float32).reshape(B, nc, GLA_CHUNK, H, t.shape[-1]).transpose(1, 0, 3, 2, 4)

    qf, kf, vf, gf = blk(q), blk(k), blk(v), blk(la)
    b = jnp.cumsum(gf, axis=3)
    b_last = b[:, :, :, -1:, :]
    q_in = qf * jnp.exp(b)
    k_in = kf * jnp.exp(-b)
    k_end = kf * jnp.exp(b_last - b)
    mask = jnp.tril(jnp.ones((GLA_CHUNK, GLA_CHUNK), dtype=bool))
    att = jnp.where(mask, jnp.einsum('nbhcd,nbhsd->nbhcs', q_in, k_in), 0.0)
    o_intra = jnp.einsum('nbhcs,nbhse->nbhce', att, vf)
    dec = jnp.exp(b_last)

    def step(S, inp):
        q_i, k_e, v_i, d_i = inp
        o = jnp.einsum('bhcd,bhde->bhce', q_i, S)
        S = S * d_i[:, :, 0, :, None] + jnp.einsum('bhcd,bhce->bhde', k_e, v_i)
        return S, o

    s_fin, o_inter = lax.scan(step, s0.astype(jnp.float32), (q_in, k_end, vf, dec))
    o = (o_intra + o_inter).transpose(1, 0, 3, 2, 4).reshape(B, L, H, -1)
    return o, s_fin


def gla_direction(q, k, v, la, cq, ck, cv, cla, reverse):
    if reverse:
        q, k, v, la, cq, ck, cv, cla = [jnp.flip(t, axis=1) for t in (q, k, v, la, cq, ck, cv, cla)]
    s0 = jnp.zeros((q.shape[0], GLA_HEADS, GLA_DK, GLA_DK), jnp.float32)
    oc, s_ctx = gla_chunk_scan(cq, ck, cv, cla, s0)
    o, _ = gla_chunk_scan(q, k, v, la, s_ctx)
    if reverse:
        o, oc = jnp.flip(o, axis=1), jnp.flip(oc, axis=1)
    return o, oc


def spatial_gate(u, v, norm_g, w_s, b_s):
    B, L, _ = v.shape
    vn = rms_norm(v, norm_g).reshape(B, L // GMLP_CHUNK, GMLP_CHUNK, GMLP_GROUPS, GMLP_W // GMLP_GROUPS)
    mixed = jnp.einsum('gij,bnjgd->bnigd', w_s, vn) + b_s.T[None, None, :, :, None]
    return u * mixed.reshape(B, L, GMLP_W)


def short_conv(x, w, b):
    y = lax.conv_general_dilated(x, w[:, None, :].astype(x.dtype), window_strides=(1,),
                                 padding=[(CONV_W // 2, CONV_W - 1 - CONV_W // 2)],
                                 dimension_numbers=('NWC', 'WIO', 'NWC'),
                                 feature_group_count=x.shape[-1])
    return y + b


def rglru_gates(xc, w_r, b_r, w_i, b_i, lam):
    B, L, _ = xc.shape
    xb = xc.reshape(B, L, LRU_BLOCKS, -1)
    r = jax.nn.sigmoid(jnp.einsum('blgi,gij->blgj', xb, w_r).reshape(B, L, LRU_W) + b_r)
    i = jax.nn.sigmoid(jnp.einsum('blgi,gij->blgj', xb, w_i).reshape(B, L, LRU_W) + b_i)
    log_a = -LRU_C * r.astype(jnp.float32) * jax.nn.softplus(-lam.astype(jnp.float32))
    a = jnp.exp(log_a)
    inp = jnp.sqrt(-jnp.expm1(2.0 * log_a)) * (i * xc).astype(jnp.float32)
    return a, inp


def linear_scan(a, b, h0):
    def comb(x, y):
        return (x[0] * y[0], y[0] * x[1] + y[1])
    A, H = lax.associative_scan(comb, (a, b), axis=1)
    h = H + A * h0[:, None, :]
    return h, h[:, -1]


def rglru_mixer(lx, ly, clx, cly, conv_w, conv_b, w_r, b_r, w_i, b_i, lam):
    xl = short_conv(lx, conv_w, conv_b)
    xc = short_conv(clx, conv_w, conv_b)
    outs_l, outs_c = [], []
    for d in range(2):
        al, bl = rglru_gates(xl, w_r[d], b_r[d], w_i[d], b_i[d], lam[d])
        ac, bc = rglru_gates(xc, w_r[d], b_r[d], w_i[d], b_i[d], lam[d])
        if d == 1:
            al, bl, ac, bc = [jnp.flip(t, axis=1) for t in (al, bl, ac, bc)]
        h0 = jnp.zeros((xl.shape[0], LRU_W), jnp.float32)
        hc, s_ctx = linear_scan(ac, bc, h0)
        hl, _ = linear_scan(al, bl, s_ctx)
        if d == 1:
            hl, hc = jnp.flip(hl, axis=1), jnp.flip(hc, axis=1)
        outs_l.append(hl)
        outs_c.append(hc)
    out_l = (outs_l[0] + outs_l[1]) * jax.nn.gelu(ly)
    out_c = (outs_c[0] + outs_c[1]) * jax.nn.gelu(cly)
    return out_l, out_c


def neighbourhood_attention(q, k, v, kc, vc, rpb):
    B, N, H, Dh = q.shape
    rows = N // GRID_W
    kr = min(NAT_KR_MAX, rows)
    ncb = GRID_W // NAT_QC
    scale = Dh ** -0.5
    r = jnp.arange(rows)
    row_idx = jnp.clip(r - kr // 2, 0, rows - kr)[:, None] + jnp.arange(kr)[None, :]
    band0 = jnp.clip(jnp.arange(ncb) * NAT_QC - NAT_KC // 2, 0, GRID_W - NAT_KB)
    col_idx = band0[:, None] + jnp.arange(NAT_KB)[None, :]
    qcol = jnp.arange(ncb)[:, None] * NAT_QC + jnp.arange(NAT_QC)[None, :]
    col0 = jnp.clip(qcol - NAT_KC // 2, 0, GRID_W - NAT_KC)
    kcol = col_idx[:, None, :]
    valid = (kcol >= col0[..., None]) & (kcol < col0[..., None] + NAT_KC)
    dr = row_idx - r[:, None] + NAT_KR_MAX - 1
    dc = jnp.clip(kcol - qcol[..., None], -(NAT_KC - 1), NAT_KC - 1) + NAT_KC - 1
    bias = rpb[:, dr[:, None, None, :, None], dc[None, :, :, None, :]]
    bias = jnp.moveaxis(bias, 0, 3).astype(jnp.float32)
    qg = q.reshape(B, rows, ncb, NAT_QC, H, Dh)
    gi_r = row_idx[:, None, :, None]
    gi_c = col_idx[None, :, None, :]
    kg = k.reshape(B, rows, GRID_W, H, Dh)[:, gi_r, gi_c]
    vg = v.reshape(B, rows, GRID_W, H, Dh)[:, gi_r, gi_c]
    s_loc = jnp.einsum('brnqhd,brnijhd->brnqhij', qg, kg).astype(jnp.float32) * scale + bias
    s_loc = jnp.where(valid[None, None, :, :, None, None, :], s_loc, -jnp.inf)
    s_loc = s_loc.reshape(B, rows, ncb, NAT_QC, H, kr * NAT_KB)
    s_ctx = jnp.einsum('brnqhd,bmhd->brnqhm', qg, kc).astype(jnp.float32) * scale
    p = jax.nn.softmax(jnp.concatenate([s_loc, s_ctx], axis=-1), axis=-1).astype(v.dtype)
    p_loc = p[..., :kr * NAT_KB].reshape(B, rows, ncb, NAT_QC, H, kr, NAT_KB)
    p_ctx = p[..., kr * NAT_KB:]
    o = jnp.einsum('brnqhij,brnijhd->brnqhd', p_loc, vg) + jnp.einsum('brnqhm,bmhd->brnqhd', p_ctx, vc)
    return o.reshape(B, N, H * Dh)


def context_attention(q, k, v):
    B, M, H, Dh = q.shape
    s = jnp.einsum('bmhd,bnhd->bhmn', q, k).astype(jnp.float32) * (Dh ** -0.5)
    p = jax.nn.softmax(s, axis=-1).astype(v.dtype)
    return jnp.einsum('bhmn,bnhd->bmhd', p, v).reshape(B, M, H * Dh)


def token_mixer(h, hc, w_in, gla_w_gate, gla_b_gate, gla_norm_g, gmlp_norm_g, gmlp_w_s, gmlp_b_s,
                lru_conv_w, lru_conv_b, lru_w_r, lru_b_r, lru_w_i, lru_b_i, lru_lambda, nat_rpb, need_ctx):
    B, N, _ = h.shape
    M = hc.shape[1]
    q, k, v, og, glf, glb, gu, gv, lx, ly, nq, nk, nv = jnp.split(h @ w_in, IN_OFFSETS, axis=-1)
    cq, ck, cv, cog, cglf, cglb, cgu, cgv, clx, cly, cnq, cnk, cnv = jnp.split(hc @ w_in, IN_OFFSETS, axis=-1)

    gq = axial_rope(split_heads(q, GLA_HEADS), N) * (GLA_DK ** -0.5)
    gk = axial_rope(split_heads(k, GLA_HEADS), N)
    gv_ = split_heads(v, GLA_HEADS)
    gcq = split_heads(cq, GLA_HEADS) * (GLA_DK ** -0.5)
    gck = split_heads(ck, GLA_HEADS)
    gcv = split_heads(cv, GLA_HEADS)
    o_dirs, oc_dirs = [], []
    for d, (gl, cgl) in enumerate(((glf, cglf), (glb, cglb))):
        la = gla_log_decay(gl, gla_w_gate[d], gla_b_gate[d])
        cla = gla_log_decay(cgl, gla_w_gate[d], gla_b_gate[d])
        o, oc = gla_direction(gq, gk, gv_, la, gcq, gck, gcv, cla, reverse=(d == 1))
        o_dirs.append(o)
        oc_dirs.append(oc)
    gla_out = rms_norm(o_dirs[0] + o_dirs[1], gla_norm_g).reshape(B, N, GLA_W) * jax.nn.silu(og)

    gmlp_out = spatial_gate(gu, gv, gmlp_norm_g, gmlp_w_s, gmlp_b_s)

    lru_out, lru_ctx = rglru_mixer(lx, ly, clx, cly, lru_conv_w, lru_conv_b,
                                   lru_w_r, lru_b_r, lru_w_i, lru_b_i, lru_lambda)

    nck = split_heads(cnk, NAT_HEADS)
    ncv = split_heads(cnv, NAT_HEADS)
    nat_out = neighbourhood_attention(split_heads(nq, NAT_HEADS), split_heads(nk, NAT_HEADS),
                                      split_heads(nv, NAT_HEADS), nck, ncv, nat_rpb)

    mix = jnp.concatenate([gla_out.astype(h.dtype), gmlp_out.astype(h.dtype),
                           lru_out.astype(h.dtype), nat_out.astype(h.dtype)], axis=-1)
    if not need_ctx:
        return mix, None
    gla_c = rms_norm(oc_dirs[0] + oc_dirs[1], gla_norm_g).reshape(B, M, GLA_W) * jax.nn.silu(cog)
    gmlp_c = spatial_gate(cgu, cgv, gmlp_norm_g, gmlp_w_s, gmlp_b_s)
    nat_c = context_attention(split_heads(cnq, NAT_HEADS), nck, ncv)
    mix_c = jnp.concatenate([gla_c.astype(hc.dtype), gmlp_c.astype(hc.dtype),
                             lru_ctx.astype(hc.dtype), nat_c.astype(hc.dtype)], axis=-1)
    return mix, mix_c


def expert_choice_ffn(h, w_router, w_gate, w_up, w_down):
    B, n, _ = h.shape
    cap = max(1, EC_CAPACITY * n // N_EXPERTS)
    aff = jax.nn.softmax(jnp.einsum('bnd,de->bne', h, w_router).astype(jnp.float32), axis=-1)
    g, idx = lax.top_k(jnp.swapaxes(aff, 1, 2), cap)
    bi = jnp.arange(B)[:, None, None]
    xs = h[bi, idx]
    hid = jax.nn.silu(jnp.einsum('becd,edf->becf', xs, w_gate)) * jnp.einsum('becd,edf->becf', xs, w_up)
    y = jnp.einsum('becf,efd->becd', hid, w_down) * g[..., None].astype(h.dtype)
    return jnp.zeros_like(h).at[bi, idx].add(y)


def setup_inputs(seed: int = 0) -> dict:
    key = jax.random.key(seed)
    ks = jax.random.split(key, 32)
    f32 = jnp.float32
    L = DEPTH

    def nrm(k, shape, scale):
        return jax.random.normal(k, shape, f32) * scale

    u = jax.random.uniform(ks[22], (L, 2, LRU_W), f32, 0.9, 0.999)
    a0 = u ** (1.0 / LRU_C)
    lru_lambda = jnp.log(a0) - jnp.log1p(-a0)
    hd = LRU_W // LRU_BLOCKS
    return {
        "x": nrm(ks[0], (BATCH, SEQ, D_MODEL), 1.0),
        "c": nrm(ks[1], (BATCH, D_MODEL), 1.0),
        "ctx": nrm(ks[2], (BATCH, CTX_LEN, D_MODEL), 1.0),
        "c_ctx": nrm(ks[3], (D_MODEL,), 1.0),
        "w_ada": nrm(ks[4], (L, D_MODEL, 6 * D_MODEL), 0.5 * D_MODEL ** -0.5),
        "b_ada": nrm(ks[5], (L, 6 * D_MODEL), 0.02),
        "norm1_g": 1.0 + nrm(ks[6], (L, D_MODEL), 0.02),
        "norm2_g": 1.0 + nrm(ks[7], (L, D_MODEL), 0.02),
        "w_in": nrm(ks[8], (L, D_MODEL, IN_COLS), D_MODEL ** -0.5),
        "w_out": nrm(ks[9], (L, MIX_W, D_MODEL), MIX_W ** -0.5),
        "gla_w_gate": nrm(ks[10], (L, 2, GLA_GATE_RANK, GLA_W), GLA_GATE_RANK ** -0.5),
        "gla_b_gate": nrm(ks[11], (L, 2, GLA_W), 0.1),
        "gla_norm_g": 1.0 + nrm(ks[12], (L, GLA_DK), 0.02),
        "gmlp_norm_g": 1.0 + nrm(ks[13], (L, GMLP_W), 0.02),
        "gmlp_w_s": nrm(ks[14], (L, GMLP_GROUPS, GMLP_CHUNK, GMLP_CHUNK), GMLP_CHUNK ** -0.5),
        "gmlp_b_s": nrm(ks[15], (L, GMLP_GROUPS, GMLP_CHUNK), 0.02),
        "lru_conv_w": nrm(ks[16], (L, CONV_W, LRU_W), CONV_W ** -0.5),
        "lru_conv_b": nrm(ks[17], (L, LRU_W), 0.02),
        "lru_w_r": nrm(ks[18], (L, 2, LRU_BLOCKS, hd, hd), hd ** -0.5),
        "lru_b_r": nrm(ks[19], (L, 2, LRU_W), 0.02),
        "lru_w_i": nrm(ks[20], (L, 2, LRU_BLOCKS, hd, hd), hd ** -0.5),
        "lru_b_i": nrm(ks[21], (L, 2, LRU_W), 0.02),
        "lru_lambda": lru_lambda,
        "nat_rpb": nrm(ks[23], (L, NAT_HEADS, 2 * NAT_KR_MAX - 1, 2 * NAT_KC - 1), 0.1),
        "moe_w_router": nrm(ks[24], (L, D_MODEL, N_EXPERTS), D_MODEL ** -0.5),
        "moe_w_gate": nrm(ks[25], (L, N_EXPERTS, D_MODEL, EXPERT_FF), D_MODEL ** -0.5),
        "moe_w_up": nrm(ks[26], (L, N_EXPERTS, D_MODEL, EXPERT_FF), D_MODEL ** -0.5),
        "moe_w_down": nrm(ks[27], (L, N_EXPERTS, EXPERT_FF, D_MODEL), EXPERT_FF ** -0.5),
        "final_norm_g": 1.0 + nrm(ks[28], (D_MODEL,), 0.02),
    }


def reference(x, c, ctx, c_ctx, w_ada, b_ada, norm1_g, norm2_g, w_in, w_out, gla_w_gate, gla_b_gate,
              gla_norm_g, gmlp_norm_g, gmlp_w_s, gmlp_b_s, lru_conv_w, lru_conv_b, lru_w_r, lru_b_r,
              lru_w_i, lru_b_i, lru_lambda, nat_rpb, moe_w_router, moe_w_gate, moe_w_up, moe_w_down,
              final_norm_g):
    sc = jax.nn.silu(c)
    scc = jax.nn.silu(c_ctx)
    xc = ctx
    for l in range(DEPTH):
        need_ctx = l < DEPTH - 1
        mod = sc @ w_ada[l] + b_ada[l]
        mod_c = scc @ w_ada[l] + b_ada[l]
        sh1, sc1, g1, sh2, sc2, g2 = jnp.split(mod[:, None, :], 6, axis=-1)
        csh1, csc1, cg1, csh2, csc2, cg2 = jnp.split(mod_c, 6, axis=-1)
        h = rms_norm(x, norm1_g[l]) * (1.0 + sc1) + sh1
        hc = rms_norm(xc, norm1_g[l]) * (1.0 + csc1) + csh1
        mix, mix_c = token_mixer(h, hc, w_in[l], gla_w_gate[l], gla_b_gate[l], gla_norm_g[l],
                                 gmlp_norm_g[l], gmlp_w_s[l], gmlp_b_s[l], lru_conv_w[l], lru_conv_b[l],
                                 lru_w_r[l], lru_b_r[l], lru_w_i[l], lru_b_i[l], lru_lambda[l], nat_rpb[l],
                                 need_ctx)
        x = x + g1 * (mix @ w_out[l])
        h2 = rms_norm(x, norm2_g[l]) * (1.0 + sc2) + sh2
        x = x + g2 * expert_choice_ffn(h2, moe_w_router[l], moe_w_gate[l], moe_w_up[l], moe_w_down[l])
        if need_ctx:
            xc = xc + cg1 * (mix_c @ w_out[l])
            hc2 = rms_norm(xc, norm2_g[l]) * (1.0 + csc2) + csh2
            xc = xc + cg2 * expert_choice_ffn(hc2, moe_w_router[l], moe_w_gate[l], moe_w_up[l], moe_w_down[l])
    return rms_norm(x, final_norm_g)
```

```python
import functools

import jax
import jax.numpy as jnp
from jax import lax
from jax.experimental import pallas as pl
from jax.experimental.pallas import tpu as pltpu

D_MODEL = 1024
DEPTH = 4
GRID_W = 64
N_GROUPS = 4
GROUP_W = D_MODEL // N_GROUPS
GLA_HEADS = 4
GLA_DK = GROUP_W // GLA_HEADS
GLA_W = GLA_HEADS * GLA_DK
GLA_GATE_RANK = 16
GLA_TAU = 16.0
GLA_CHUNK = 64
GMLP_GROUPS = 4
GMLP_W = GROUP_W
GMLP_CHUNK = 128
LRU_W = GROUP_W
LRU_BLOCKS = 4
LRU_C = 8.0
CONV_W = 4
NAT_HEADS = 4
NAT_DH = GROUP_W // NAT_HEADS
NAT_W = NAT_HEADS * NAT_DH
NAT_KR_MAX = 8
NAT_KC = 16
NAT_QC = 16
NAT_KB = NAT_QC + NAT_KC
N_EXPERTS = 16
EXPERT_FF = 2 * D_MODEL
EC_CAPACITY = 2
ROPE_BASE = 10000.0
EPS = 1e-6

IN_SIZES = (GLA_W, GLA_W, GLA_W, GLA_W, GLA_GATE_RANK, GLA_GATE_RANK,
            GMLP_W, GMLP_W, LRU_W, LRU_W, NAT_W, NAT_W, NAT_W)
IN_COLS = sum(IN_SIZES)
IN_OFFSETS = tuple(sum(IN_SIZES[:i + 1]) for i in range(len(IN_SIZES) - 1))

V7X_VMEM_LIMIT_BYTES = 56 * 1024 * 1024
F32 = jnp.float32
BF16 = jnp.bfloat16


def _cparams(sem):
    return pltpu.CompilerParams(dimension_semantics=sem, vmem_limit_bytes=V7X_VMEM_LIMIT_BYTES)


def _norm_inproj_kernel(x_ref, g_ref, sc_ref, sh_ref, w_ref, o_ref):
    x = x_ref[0]
    y = x * lax.rsqrt(jnp.mean(x * x, axis=-1, keepdims=True) + EPS)
    h = (y * g_ref[...]) * (1.0 + sc_ref[0]) + sh_ref[0]
    o_ref[0] = jnp.dot(h.astype(BF16), w_ref[...].astype(BF16), preferred_element_type=F32)


def norm_inproj(x, g, scale, shift, w, tm, tn):
    B, n, D = x.shape
    C = w.shape[1]
    return pl.pallas_call(
        _norm_inproj_kernel,
        grid=(B, n // tm, C // tn),
        in_specs=[
            pl.BlockSpec((1, tm, D), lambda b, i, j: (b, i, 0)),
            pl.BlockSpec((1, D), lambda b, i, j: (0, 0)),
            pl.BlockSpec((1, 1, D), lambda b, i, j: (b, 0, 0)),
            pl.BlockSpec((1, 1, D), lambda b, i, j: (b, 0, 0)),
            pl.BlockSpec((D, tn), lambda b, i, j: (0, j)),
        ],
        out_specs=pl.BlockSpec((1, tm, tn), lambda b, i, j: (b, i, j)),
        out_shape=jax.ShapeDtypeStruct((B, n, C), F32),
        compiler_params=_cparams(("parallel", "parallel", "arbitrary")),
        name="norm_inproj",
    )(x, g.reshape(1, D), scale.reshape(B, 1, D), shift.reshape(B, 1, D), w)


def _outproj_kernel(mix_ref, w_ref, x_ref, g1_ref, n2_ref, sc_ref, sh_ref, wr_ref, xo_ref, h2_ref, aff_ref):
    y = jnp.dot(mix_ref[0].astype(BF16), w_ref[...].astype(BF16), preferred_element_type=F32)
    x = x_ref[0] + g1_ref[0] * y
    xo_ref[0] = x
    xn = x * lax.rsqrt(jnp.mean(x * x, axis=-1, keepdims=True) + EPS)
    h2 = (xn * n2_ref[...]) * (1.0 + sc_ref[0]) + sh_ref[0]
    h2b = h2.astype(BF16)
    h2_ref[0] = h2b
    logits = jnp.dot(h2b, wr_ref[...].astype(BF16), preferred_element_type=F32)
    m = jnp.max(logits, axis=-1, keepdims=True)
    e = jnp.exp(logits - m)
    aff_ref[0] = e / jnp.sum(e, axis=-1, keepdims=True)


def outproj_norm_router(mix, w_out, x, g1, n2, scale2, shift2, w_router, tm):
    B, n, D = x.shape
    E = w_router.shape[1]
    row = lambda b, i: (b, i, 0)
    per_b = lambda b, i: (b, 0, 0)
    full = lambda b, i: (0, 0)
    return pl.pallas_call(
        _outproj_kernel,
        grid=(B, n // tm),
        in_specs=[
            pl.BlockSpec((1, tm, D), row),
            pl.BlockSpec((D, D), full),
            pl.BlockSpec((1, tm, D), row),
            pl.BlockSpec((1, 1, D), per_b),
            pl.BlockSpec((1, D), full),
            pl.BlockSpec((1, 1, D), per_b),
            pl.BlockSpec((1, 1, D), per_b),
            pl.BlockSpec((D, E), full),
        ],
        out_specs=[
            pl.BlockSpec((1, tm, D), row),
            pl.BlockSpec((1, tm, D), row),
            pl.BlockSpec((1, tm, E), row),
        ],
        out_shape=[
            jax.ShapeDtypeStruct((B, n, D), F32),
            jax.ShapeDtypeStruct((B, n, D), BF16),
            jax.ShapeDtypeStruct((B, n, E), F32),
        ],
        compiler_params=_cparams(("parallel", "parallel")),
        name="outproj_norm_router",
    )(mix, w_out, x, g1.reshape(B, 1, D), n2.reshape(1, D), scale2.reshape(B, 1, D), shift2.reshape(B, 1, D),
      w_router)


def _moe_ffn_kernel(x_ref, wg_ref, wu_ref, wd_ref, g_ref, o_ref, *, tm, n_chunks):
    f = pl.program_id(1)
    last = pl.num_programs(1) - 1
    wg = wg_ref[0].astype(BF16)
    wu = wu_ref[0].astype(BF16)
    wd = wd_ref[0].astype(BF16)

    def chunk(i, carry):
        r = pl.ds(pl.multiple_of(i * tm, tm), tm)
        x = x_ref[0, r, :]
        hg = jnp.dot(x, wg, preferred_element_type=F32)
        hu = jnp.dot(x, wu, preferred_element_type=F32)
        hid = (hg * jax.nn.sigmoid(hg) * hu).astype(BF16)
        y = jnp.dot(hid, wd, preferred_element_type=F32)

        @pl.when(f == 0)
        def _():
            o_ref[0, r, :] = y

        @pl.when(f > 0)
        def _():
            o_ref[0, r, :] += y

        return carry

    lax.fori_loop(0, n_chunks, chunk, 0)

    @pl.when(f == last)
    def _():
        o_ref[0] = o_ref[0] * g_ref[0]


def moe_ffn(xs, w_gate, w_up, w_down, gates, tf, tm):
    E, T, D = xs.shape
    F = w_gate.shape[2]
    return pl.pallas_call(
        functools.partial(_moe_ffn_kernel, tm=tm, n_chunks=T // tm),
        grid=(E, F // tf),
        in_specs=[
            pl.BlockSpec((1, T, D), lambda e, f: (e, 0, 0)),
            pl.BlockSpec((1, D, tf), lambda e, f: (e, 0, f)),
            pl.BlockSpec((1, D, tf), lambda e, f: (e, 0, f)),
            pl.BlockSpec((1, tf, D), lambda e, f: (e, f, 0)),
            pl.BlockSpec((1, T, 1), lambda e, f: (e, 0, 0)),
        ],
        out_specs=pl.BlockSpec((1, T, D), lambda e, f: (e, 0, 0)),
        out_shape=jax.ShapeDtypeStruct((E, T, D), F32),
        compiler_params=_cparams(("parallel", "arbitrary")),
        name="moe_ffn",
    )(xs, w_gate, w_up, w_down, gates)


def _final_norm_kernel(x_ref, g_ref, o_ref):
    x = x_ref[0]
    o_ref[0] = x * lax.rsqrt(jnp.mean(x * x, axis=-1, keepdims=True) + EPS) * g_ref[...]


def final_norm(x, g, tm):
    B, n, D = x.shape
    return pl.pallas_call(
        _final_norm_kernel,
        grid=(B, n // tm),
        in_specs=[pl.BlockSpec((1, tm, D), lambda b, i: (b, i, 0)), pl.BlockSpec((1, D), lambda b, i: (0, 0))],
        out_specs=pl.BlockSpec((1, tm, D), lambda b, i: (b, i, 0)),
        out_shape=jax.ShapeDtypeStruct((B, n, D), F32),
        compiler_params=_cparams(("parallel", "parallel")),
        name="final_norm",
    )(x, g.reshape(1, D))


def rms_norm(x, g):
    xf = x.astype(F32)
    y = xf * lax.rsqrt(jnp.mean(xf * xf, axis=-1, keepdims=True) + EPS)
    return (y * g.astype(F32)).astype(x.dtype)


def split_heads(t, h):
    return t.reshape(t.shape[0], t.shape[1], h, -1)


def axial_rope(t, n):
    half = t.shape[-1] // 2
    quarter = half // 2
    pos = jnp.arange(n)
    row = (pos // GRID_W).astype(F32)
    col = (pos % GRID_W).astype(F32)
    inv = ROPE_BASE ** (-jnp.arange(quarter, dtype=F32) / quarter)

    def rot(u, p):
        ang = p[:, None] * inv[None, :]
        cos = jnp.cos(ang)[None, :, None, :].astype(u.dtype)
        sin = jnp.sin(ang)[None, :, None, :].astype(u.dtype)
        u1, u2 = u[..., :quarter], u[..., quarter:]
        return jnp.concatenate([u1 * cos - u2 * sin, u1 * sin + u2 * cos], axis=-1)

    return jnp.concatenate([rot(t[..., :half], row), rot(t[..., half:], col)], axis=-1)


def gla_log_decay(z, w, b):
    a = jnp.einsum('blr,rk->blk', z, w) + b
    la = jax.nn.log_sigmoid(a.astype(F32)) / GLA_TAU
    return la.reshape(z.shape[0], z.shape[1], GLA_HEADS, GLA_DK)


def gla_chunk_scan(q, k, v, la, s0):
    B, L, H, _ = q.shape
    nc = L // GLA_CHUNK

    def blk(t):
        return t.astype(F32).reshape(B, nc, GLA_CHUNK, H, t.shape[-1]).transpose(1, 0, 3, 2, 4)

    qf, kf, vf, gf = blk(q), blk(k), blk(v), blk(la)
    b = jnp.cumsum(gf, axis=3)
    b_last = b[:, :, :, -1:, :]
    q_in = qf * jnp.exp(b)
    k_in = kf * jnp.exp(-b)
    k_end = kf * jnp.exp(b_last - b)
    mask = jnp.tril(jnp.ones((GLA_CHUNK, GLA_CHUNK), dtype=bool))
    att = jnp.where(mask, jnp.einsum('nbhcd,nbhsd->nbhcs', q_in, k_in), 0.0)
    o_intra = jnp.einsum('nbhcs,nbhse->nbhce', att, vf)
    dec = jnp.exp(b_last)

    def step(S, inp):
        q_i, k_e, v_i, d_i = inp
        o = jnp.einsum('bhcd,bhde->bhce', q_i, S)
        S = S * d_i[:, :, 0, :, None] + jnp.einsum('bhcd,bhce->bhde', k_e, v_i)
        return S, o

    s_fin, o_inter = lax.scan(step, s0.astype(F32), (q_in, k_end, vf, dec))
    o = (o_intra + o_inter).transpose(1, 0, 3, 2, 4).reshape(B, L, H, -1)
    return o, s_fin


def gla_direction(q, k, v, la, cq, ck, cv, cla, reverse):
    if reverse:
        q, k, v, la, cq, ck, cv, cla = [jnp.flip(t, axis=1) for t in (q, k, v, la, cq, ck, cv, cla)]
    s0 = jnp.zeros((q.shape[0], GLA_HEADS, GLA_DK, GLA_DK), F32)
    oc, s_ctx = gla_chunk_scan(cq, ck, cv, cla, s0)
    o, _ = gla_chunk_scan(q, k, v, la, s_ctx)
    if reverse:
        o, oc = jnp.flip(o, axis=1), jnp.flip(oc, axis=1)
    return o, oc


def spatial_gate(u, v, norm_g, w_s, b_s):
    B, L, _ = v.shape
    vn = rms_norm(v, norm_g).reshape(B, L // GMLP_CHUNK, GMLP_CHUNK, GMLP_GROUPS, GMLP_W // GMLP_GROUPS)
    mixed = jnp.einsum('gij,bnjgd->bnigd', w_s, vn) + b_s.T[None, None, :, :, None]
    return u * mixed.reshape(B, L, GMLP_W)


def short_conv(x, w, b):
    y = lax.conv_general_dilated(x, w[:, None, :].astype(x.dtype), window_strides=(1,),
                                 padding=[(CONV_W // 2, CONV_W - 1 - CONV_W // 2)],
                                 dimension_numbers=('NWC', 'WIO', 'NWC'),
                                 feature_group_count=x.shape[-1])
    return y + b


def rglru_gates(xc, w_r, b_r, w_i, b_i, lam):
    B, L, _ = xc.shape
    xb = xc.reshape(B, L, LRU_BLOCKS, -1)
    r = jax.nn.sigmoid(jnp.einsum('blgi,gij->blgj', xb, w_r).reshape(B, L, LRU_W) + b_r)
    i = jax.nn.sigmoid(jnp.einsum('blgi,gij->blgj', xb, w_i).reshape(B, L, LRU_W) + b_i)
    log_a = -LRU_C * r.astype(F32) * jax.nn.softplus(-lam.astype(F32))
    a = jnp.exp(log_a)
    inp = jnp.sqrt(-jnp.expm1(2.0 * log_a)) * (i * xc).astype(F32)
    return a, inp


def linear_scan(a, b, h0):
    def comb(x, y):
        return (x[0] * y[0], y[0] * x[1] + y[1])
    A, H = lax.associative_scan(comb, (a, b), axis=1)
    h = H + A * h0[:, None, :]
    return h, h[:, -1]


def rglru_mixer(lx, ly, clx, cly, conv_w, conv_b, w_r, b_r, w_i, b_i, lam):
    xl = short_conv(lx, conv_w, conv_b)
    xc = short_conv(clx, conv_w, conv_b)
    outs_l, outs_c = [], []
    for d in range(2):
        al, bl = rglru_gates(xl, w_r[d], b_r[d], w_i[d], b_i[d], lam[d])
        ac, bc = rglru_gates(xc, w_r[d], b_r[d], w_i[d], b_i[d], lam[d])
        if d == 1:
            al, bl, ac, bc = [jnp.flip(t, axis=1) for t in (al, bl, ac, bc)]
        h0 = jnp.zeros((xl.shape[0], LRU_W), F32)
        hc, s_ctx = linear_scan(ac, bc, h0)
        hl, _ = linear_scan(al, bl, s_ctx)
        if d == 1:
            hl, hc = jnp.flip(hl, axis=1), jnp.flip(hc, axis=1)
        outs_l.append(hl)
        outs_c.append(hc)
    out_l = (outs_l[0] + outs_l[1]) * jax.nn.gelu(ly)
    out_c = (outs_c[0] + outs_c[1]) * jax.nn.gelu(cly)
    return out_l, out_c


def neighbourhood_attention(q, k, v, kc, vc, rpb):
    B, N, H, Dh = q.shape
    rows = N // GRID_W
    kr = min(NAT_KR_MAX, rows)
    ncb = GRID_W // NAT_QC
    scale = Dh ** -0.5
    r = jnp.arange(rows)
    row_idx = jnp.clip(r - kr // 2, 0, rows - kr)[:, None] + jnp.arange(kr)[None, :]
    band0 = jnp.clip(jnp.arange(ncb) * NAT_QC - NAT_KC // 2, 0, GRID_W - NAT_KB)
    col_idx = band0[:, None] + jnp.arange(NAT_KB)[None, :]
    qcol = jnp.arange(ncb)[:, None] * NAT_QC + jnp.arange(NAT_QC)[None, :]
    col0 = jnp.clip(qcol - NAT_KC // 2, 0, GRID_W - NAT_KC)
    kcol = col_idx[:, None, :]
    valid = (kcol >= col0[..., None]) & (kcol < col0[..., None] + NAT_KC)
    dr = row_idx - r[:, None] + NAT_KR_MAX - 1
    dc = jnp.clip(kcol - qcol[..., None], -(NAT_KC - 1), NAT_KC - 1) + NAT_KC - 1
    bias = rpb[:, dr[:, None, None, :, None], dc[None, :, :, None, :]]
    bias = jnp.moveaxis(bias, 0, 3).astype(F32)
    qg = q.reshape(B, rows, ncb, NAT_QC, H, Dh)
    gi_r = row_idx[:, None, :, None]
    gi_c = col_idx[None, :, None, :]
    kg = k.reshape(B, rows, GRID_W, H, Dh)[:, gi_r, gi_c]
    vg = v.reshape(B, rows, GRID_W, H, Dh)[:, gi_r, gi_c]
    s_loc = jnp.einsum('brnqhd,brnijhd->brnqhij', qg, kg).astype(F32) * scale + bias
    s_loc = jnp.where(valid[None, None, :, :, None, None, :], s_loc, -jnp.inf)
    s_loc = s_loc.reshape(B, rows, ncb, NAT_QC, H, kr * NAT_KB)
    s_ctx = jnp.einsum('brnqhd,bmhd->brnqhm', qg, kc).astype(F32) * scale
    p = jax.nn.softmax(jnp.concatenate([s_loc, s_ctx], axis=-1), axis=-1).astype(v.dtype)
    p_loc = p[..., :kr * NAT_KB].reshape(B, rows, ncb, NAT_QC, H, kr, NAT_KB)
    p_ctx = p[..., kr * NAT_KB:]
    o = jnp.einsum('brnqhij,brnijhd->brnqhd', p_loc, vg) + jnp.einsum('brnqhm,bmhd->brnqhd', p_ctx, vc)
    return o.reshape(B, N, H * Dh)


def context_attention(q, k, v):
    B, M, H, Dh = q.shape
    s = jnp.einsum('bmhd,bnhd->bhmn', q, k).astype(F32) * (Dh ** -0.5)
    p = jax.nn.softmax(s, axis=-1).astype(v.dtype)
    return jnp.einsum('bhmn,bnhd->bmhd', p, v).reshape(B, M, H * Dh)


def token_mixer(proj, cproj, gla_w_gate, gla_b_gate, gla_norm_g, gmlp_norm_g, gmlp_w_s, gmlp_b_s,
                lru_conv_w, lru_conv_b, lru_w_r, lru_b_r, lru_w_i, lru_b_i, lru_lambda, nat_rpb, need_ctx):
    B, N, _ = proj.shape
    M = cproj.shape[1]
    q, k, v, og, glf, glb, gu, gv, lx, ly, nq, nk, nv = jnp.split(proj, IN_OFFSETS, axis=-1)
    cq, ck, cv, cog, cglf, cglb, cgu, cgv, clx, cly, cnq, cnk, cnv = jnp.split(cproj, IN_OFFSETS, axis=-1)

    gq = axial_rope(split_heads(q, GLA_HEADS), N) * (GLA_DK ** -0.5)
    gk = axial_rope(split_heads(k, GLA_HEADS), N)
    gv_ = split_heads(v, GLA_HEADS)
    gcq = split_heads(cq, GLA_HEADS) * (GLA_DK ** -0.5)
    gck = split_heads(ck, GLA_HEADS)
    gcv = split_heads(cv, GLA_HEADS)
    o_dirs, oc_dirs = [], []
    for d, (gl, cgl) in enumerate(((glf, cglf), (glb, cglb))):
        la = gla_log_decay(gl, gla_w_gate[d], gla_b_gate[d])
        cla = gla_log_decay(cgl, gla_w_gate[d], gla_b_gate[d])
        o, oc = gla_direction(gq, gk, gv_, la, gcq, gck, gcv, cla, reverse=(d == 1))
        o_dirs.append(o)
        oc_dirs.append(oc)
    gla_out = rms_norm(o_dirs[0] + o_dirs[1], gla_norm_g).reshape(B, N, GLA_W) * jax.nn.silu(og)

    gmlp_out = spatial_gate(gu, gv, gmlp_norm_g, gmlp_w_s, gmlp_b_s)

    lru_out, lru_ctx = rglru_mixer(lx, ly, clx, cly, lru_conv_w, lru_conv_b,
                                   lru_w_r, lru_b_r, lru_w_i, lru_b_i, lru_lambda)

    nck = split_heads(cnk, NAT_HEADS)
    ncv = split_heads(cnv, NAT_HEADS)
    nat_out = neighbourhood_attention(split_heads(nq, NAT_HEADS), split_heads(nk, NAT_HEADS),
                                      split_heads(nv, NAT_HEADS), nck, ncv, nat_rpb)

    mix = jnp.concatenate([gla_out, gmlp_out, lru_out, nat_out], axis=-1)
    if not need_ctx:
        return mix, None
    gla_c = rms_norm(oc_dirs[0] + oc_dirs[1], gla_norm_g).reshape(B, M, GLA_W) * jax.nn.silu(cog)
    gmlp_c = spatial_gate(cgu, cgv, gmlp_norm_g, gmlp_w_s, gmlp_b_s)
    nat_c = context_attention(split_heads(cnq, NAT_HEADS), nck, ncv)
    mix_c = jnp.concatenate([gla_c, gmlp_c, lru_ctx, nat_c], axis=-1)
    return mix, mix_c


def route(aff, cap):
    return lax.top_k(jnp.swapaxes(aff, 1, 2), cap)


def kernel(x, c, ctx, c_ctx, w_ada, b_ada, norm1_g, norm2_g, w_in, w_out, gla_w_gate, gla_b_gate, gla_norm_g,
           gmlp_norm_g, gmlp_w_s, gmlp_b_s, lru_conv_w, lru_conv_b, lru_w_r, lru_b_r, lru_w_i, lru_b_i,
           lru_lambda, nat_rpb, moe_w_router, moe_w_gate, moe_w_up, moe_w_down, final_norm_g):
    B, N, D = x.shape
    M = ctx.shape[1]
    E = N_EXPERTS
    cap_l = max(1, EC_CAPACITY * N // E)
    cap_c = max(1, EC_CAPACITY * M // E)
    sc = jax.nn.silu(c)
    scc = jax.nn.silu(c_ctx)
    xc = ctx
    bi = jnp.arange(B)[:, None, None]
    for l in range(DEPTH):
        need_ctx = l < DEPTH - 1
        mod = sc @ w_ada[l] + b_ada[l]
        mod_c = jnp.broadcast_to((scc @ w_ada[l] + b_ada[l])[None], (B, 6 * D))
        sh1, sc1, g1, sh2, sc2, g2 = jnp.split(mod, 6, axis=-1)
        csh1, csc1, cg1, csh2, csc2, cg2 = jnp.split(mod_c, 6, axis=-1)

        proj = norm_inproj(x, norm1_g[l], sc1, sh1, w_in[l], tm=512, tn=IN_COLS)
        cproj = norm_inproj(xc, norm1_g[l], csc1, csh1, w_in[l], tm=M, tn=IN_COLS)
        mix, mix_c = token_mixer(proj, cproj, gla_w_gate[l], gla_b_gate[l], gla_norm_g[l], gmlp_norm_g[l],
                                 gmlp_w_s[l], gmlp_b_s[l], lru_conv_w[l], lru_conv_b[l], lru_w_r[l], lru_b_r[l],
                                 lru_w_i[l], lru_b_i[l], lru_lambda[l], nat_rpb[l], need_ctx)

        x, h2, aff = outproj_norm_router(mix, w_out[l], x, g1, norm2_g[l], sc2, sh2, moe_w_router[l], tm=512)
        gl, il = route(aff, cap_l)
        xs = jnp.swapaxes(h2[bi, il], 0, 1).reshape(E, B * cap_l, D)
        gs = jnp.swapaxes(gl, 0, 1).reshape(E, B * cap_l, 1)
        if need_ctx:
            xc, hc2, caff = outproj_norm_router(mix_c, w_out[l], xc, cg1, norm2_g[l], csc2, csh2,
                                                moe_w_router[l], tm=M)
            gc, ic = route(caff, cap_c)
            xsc = jnp.swapaxes(hc2[bi, ic], 0, 1).reshape(E, B * cap_c, D)
            gsc = jnp.swapaxes(gc, 0, 1).reshape(E, B * cap_c, 1)
            xs = jnp.concatenate([xs, xsc], axis=1)
            gs = jnp.concatenate([gs, gsc], axis=1)
        T = xs.shape[1]
        ys = moe_ffn(xs, moe_w_gate[l], moe_w_up[l], moe_w_down[l], gs, tf=512, tm=T // 4)
        yl = jnp.swapaxes(ys[:, :B * cap_l].reshape(E, B, cap_l, D), 0, 1)
        x = x + g2[:, None, :] * jnp.zeros_like(x).at[bi, il].add(yl)
        if need_ctx:
            ycx = jnp.swapaxes(ys[:, B * cap_l:].reshape(E, B, cap_c, D), 0, 1)
            xc = xc + cg2[:, None, :] * jnp.zeros_like(xc).at[bi, ic].add(ycx)
    return final_norm(x, final_norm_g, tm=512)
```

```python
import functools

import numpy as np
import jax
import jax.numpy as jnp
from jax import lax
from jax.experimental import pallas as pl
from jax.experimental.pallas import tpu as pltpu

D_MODEL = 1024
DEPTH = 4
GRID_W = 64
N_GROUPS = 4
GROUP_W = D_MODEL // N_GROUPS
GLA_HEADS = 4
GLA_DK = GROUP_W // GLA_HEADS
GLA_W = GLA_HEADS * GLA_DK
GLA_GATE_RANK = 16
GLA_TAU = 16.0
GLA_CHUNK = 64
GMLP_GROUPS = 4
GMLP_W = GROUP_W
GMLP_CHUNK = 128
LRU_W = GROUP_W
LRU_BLOCKS = 4
LRU_C = 8.0
CONV_W = 4
NAT_HEADS = 4
NAT_DH = GROUP_W // NAT_HEADS
NAT_W = NAT_HEADS * NAT_DH
NAT_KR_MAX = 8
NAT_KC = 16
N_EXPERTS = 16
EXPERT_FF = 2 * D_MODEL
EC_CAPACITY = 2
ROPE_BASE = 10000.0
EPS = 1e-6

IN_SIZES = (GLA_W, GLA_W, GLA_W, GLA_W, GLA_GATE_RANK, GLA_GATE_RANK,
            GMLP_W, GMLP_W, LRU_W, LRU_W, NAT_W, NAT_W, NAT_W)
IN_COLS = sum(IN_SIZES)

V7X_VMEM_LIMIT_BYTES = 56 * 1024 * 1024
LANES = 128
SUBLANES = 8
F32 = jnp.float32
BF16 = jnp.bfloat16
NEG_BIG = -1e30
GATE_PAD = LANES

NT_DIMS = (((1,), (1,)), ((), ()))
TN_DIMS = (((0,), (0,)), ((), ()))


def _cparams(sem):
    return pltpu.CompilerParams(dimension_semantics=sem, vmem_limit_bytes=V7X_VMEM_LIMIT_BYTES)


def _dot(a, b):
    return jnp.dot(a, b, preferred_element_type=F32)


def _split3(x):
    hi = x.astype(BF16)
    r1 = x - hi.astype(F32)
    mid = r1.astype(BF16)
    lo = (r1 - mid.astype(F32)).astype(BF16)
    return hi, mid, lo


def _head_block_mask(n):
    r = lax.broadcasted_iota(jnp.int32, (n, n), 0) // GLA_DK
    c = lax.broadcasted_iota(jnp.int32, (n, n), 1) // GLA_DK
    return r == c


def _norm_inproj_kernel(x_ref, g_ref, sc_ref, sh_ref, w_ref, gla_ref, gmlp_ref, lru_ref, nat_ref, gate_ref):
    x = x_ref[0]
    y = x * lax.rsqrt(jnp.mean(x * x, axis=-1, keepdims=True) + EPS)
    h = ((y * g_ref[...]) * (1.0 + sc_ref[0]) + sh_ref[0]).astype(BF16)
    o = 0
    for ref in (gla_ref, gmlp_ref, lru_ref, nat_ref, gate_ref):
        w = ref.shape[-1]
        ref[0] = _dot(h, w_ref[:, o:o + w]).astype(ref.dtype)
        o += w


def permute_w_in(w):
    g0 = 4 * GLA_W
    g1 = g0 + 2 * GLA_GATE_RANK
    pad = jnp.zeros((w.shape[0], GATE_PAD - 2 * GLA_GATE_RANK), w.dtype)
    return jnp.concatenate([w[:, :g0], w[:, g1:], w[:, g0:g1], pad], axis=1).astype(BF16)


def norm_inproj(x, g, scale, shift, wp, tm):
    B, n, D = x.shape
    C = wp.shape[1]
    widths = (4 * GLA_W, 2 * GMLP_W, 2 * LRU_W, 3 * NAT_W, GATE_PAD)
    dtypes = (F32, F32, F32, BF16, F32)
    row = lambda b, i: (b, i, 0)
    per_b = lambda b, i: (b, 0, 0)
    full = lambda b, i: (0, 0)
    return pl.pallas_call(
        _norm_inproj_kernel,
        grid=(B, n // tm),
        in_specs=[
            pl.BlockSpec((1, tm, D), row),
            pl.BlockSpec((1, D), full),
            pl.BlockSpec((1, 1, D), per_b),
            pl.BlockSpec((1, 1, D), per_b),
            pl.BlockSpec((D, C), full),
        ],
        out_specs=[pl.BlockSpec((1, tm, w), row) for w in widths],
        out_shape=[jax.ShapeDtypeStruct((B, n, w), dt) for w, dt in zip(widths, dtypes)],
        compiler_params=_cparams(("parallel", "parallel")),
        name="norm_inproj",
    )(x, g.reshape(1, D), scale.reshape(B, 1, D), shift.reshape(B, 1, D), wp)


def _gla_kernel(*refs, tm, reverse, rope, combine):
    refs = list(refs)
    q_ref, k_ref, v_ref, gate_ref, wg_ref, bg_ref, tri_ref, ones_ref, s0_ref = refs[:9]
    refs = refs[9:]
    if rope:
        cos_ref, sin_ref = refs[:2]
        refs = refs[2:]
    if combine:
        ob_ref, og_ref, ng_ref = refs[:3]
        refs = refs[3:]
    o_ref, sfin_ref, st = refs
    i = pl.program_id(1)
    nch = tm // GLA_CHUNK

    @pl.when(i == 0)
    def _():
        st[...] = s0_ref[0]

    q = q_ref[0]
    k = k_ref[0]
    if rope:
        cos = cos_ref[...]
        sin = sin_ref[...]
        first = (lax.broadcasted_iota(jnp.int32, (tm, GLA_W), 1) % (GLA_DK // 2)) < (GLA_DK // 4)

        def rot(u):
            partner = jnp.where(first, pltpu.roll(u, GLA_W - GLA_DK // 4, axis=1), pltpu.roll(u, GLA_DK // 4, axis=1))
            return u * cos + partner * sin

        q = rot(q)
        k = rot(k)
    q = q * (GLA_DK ** -0.5)
    a = _dot(gate_ref[0].astype(BF16), wg_ref[...]) + bg_ref[...]
    la = jax.nn.log_sigmoid(a) / GLA_TAU
    pieces = _split3(la)
    b = sum(_dot(tri_ref[...], p) for p in pieces)
    bl = sum(_dot(ones_ref[...], p) for p in pieces)
    q_in = (q * jnp.exp(b)).astype(BF16)
    k_in = (k * jnp.exp(-b)).astype(BF16)
    k_end = (k * jnp.exp(bl - b)).astype(BF16)
    dec = jnp.exp(bl)
    vb = v_ref[0].astype(BF16)

    head_mask = _head_block_mask(GLA_W)
    cr = lax.broadcasted_iota(jnp.int32, (GLA_CHUNK, GLA_W), 0)
    cs = lax.broadcasted_iota(jnp.int32, (GLA_CHUNK, GLA_W), 1) % GLA_CHUNK
    causal = (cs >= cr) if reverse else (cs <= cr)
    zero = jnp.zeros((), BF16)
    chunks = range(nch - 1, -1, -1) if reverse else range(nch)
    for c in chunks:
        sl = slice(c * GLA_CHUNK, (c + 1) * GLA_CHUNK)
        kbd_t = jnp.where(head_mask, jnp.concatenate([k_in[sl]] * GLA_HEADS, axis=0), zero)
        vbd = jnp.where(head_mask, jnp.concatenate([vb[sl]] * GLA_HEADS, axis=0), zero)
        att = lax.dot_general(q_in[sl], kbd_t, NT_DIMS, preferred_element_type=F32)
        att = jnp.where(causal, att, 0.0).astype(BF16)
        s_t = st[...]
        o = _dot(att, vbd) + lax.dot_general(q_in[sl], s_t.astype(BF16), NT_DIMS, preferred_element_type=F32)
        upd = lax.dot_general(vb[sl], k_end[sl], TN_DIMS, preferred_element_type=F32)
        st[...] = s_t * dec[c * GLA_CHUNK:c * GLA_CHUNK + 1] + jnp.where(head_mask, upd, 0.0)
        o_ref[0, sl, :] = o

    sfin_ref[0] = st[...]

    if combine:
        o = o_ref[0] + ob_ref[0]
        sq_hi, sq_mid, _ = _split3(o * o)
        avg = jnp.where(head_mask, 1.0 / GLA_DK, 0.0).astype(BF16)
        ms = _dot(sq_hi, avg) + _dot(sq_mid, avg)
        og = og_ref[0]
        o_ref[0] = o * lax.rsqrt(ms + EPS) * ng_ref[...] * (og * jax.nn.sigmoid(og))


def _chunk_sum_matrices(tm, reverse):
    r = np.arange(tm)[:, None]
    c = np.arange(tm)[None, :]
    same = (r // GLA_CHUNK) == (c // GLA_CHUNK)
    tri = same & ((c >= r) if reverse else (c <= r))
    return jnp.asarray(tri, BF16), jnp.asarray(same, BF16)


def gla_direction(gla, gate, wg, bg, s0, tm, reverse, tables=None, combine=None):
    B, L, _ = gla.shape
    nt = L // tm
    W = GLA_W
    blk = (lambda i: nt - 1 - i) if reverse else (lambda i: i)
    col = lambda j: (lambda b, i: (b, blk(i), j))
    full = lambda b, i: (0, 0)
    tri, ones = _chunk_sum_matrices(tm, reverse)
    args = [gla, gla, gla, gate, wg, bg, tri, ones, s0]
    specs = [
        pl.BlockSpec((1, tm, W), col(0)), pl.BlockSpec((1, tm, W), col(1)), pl.BlockSpec((1, tm, W), col(2)),
        pl.BlockSpec((1, tm, GATE_PAD), col(0)),
        pl.BlockSpec((GATE_PAD, W), full), pl.BlockSpec((1, W), full),
        pl.BlockSpec((tm, tm), full), pl.BlockSpec((tm, tm), full),
        pl.BlockSpec((1, W, W), lambda b, i: (b, 0, 0)),
    ]
    if tables is not None:
        args += list(tables)
        specs += [pl.BlockSpec((tm, W), lambda b, i: (blk(i), 0))] * 2
    if combine is not None:
        ob, norm_g = combine
        args += [ob, gla, norm_g]
        specs += [pl.BlockSpec((1, tm, W), col(0)), pl.BlockSpec((1, tm, W), col(3)), pl.BlockSpec((1, W), full)]
    return pl.pallas_call(
        functools.partial(_gla_kernel, tm=tm, reverse=reverse, rope=tables is not None, combine=combine is not None),
        grid=(B, nt),
        in_specs=specs,
        out_specs=[pl.BlockSpec((1, tm, W), col(0)), pl.BlockSpec((1, W, W), lambda b, i: (b, 0, 0))],
        out_shape=[jax.ShapeDtypeStruct((B, L, W), F32), jax.ShapeDtypeStruct((B, W, W), F32)],
        scratch_shapes=[pltpu.VMEM((W, W), F32)],
        compiler_params=_cparams(("parallel", "arbitrary")),
        name="gla_bwd" if reverse else "gla_fwd",
    )(*args)


def rope_tables(n):
    quarter = GLA_DK // 4
    pos = jnp.arange(n)
    inv = ROPE_BASE ** (-jnp.arange(quarter, dtype=F32) / quarter)
    row = (pos // GRID_W).astype(F32)
    colp = (pos % GRID_W).astype(F32)
    ar = row[:, None] * inv[None, :]
    ac = colp[:, None] * inv[None, :]
    cos = jnp.concatenate([jnp.cos(ar), jnp.cos(ar), jnp.cos(ac), jnp.cos(ac)], axis=1)
    sin = jnp.concatenate([-jnp.sin(ar), jnp.sin(ar), -jnp.sin(ac), jnp.sin(ac)], axis=1)
    return jnp.tile(cos, (1, GLA_HEADS)), jnp.tile(sin, (1, GLA_HEADS))


def gla_mixer(gla, gate, cgla, cgate, w_gate, b_gate, norm_g, tables, need_ctx, tm):
    B, M, _ = cgla.shape
    zeros = jnp.zeros((B, GLA_W, GLA_W), F32)
    ng = jnp.tile(norm_g, GLA_HEADS).reshape(1, GLA_W)
    wgs, bgs = [], []
    for d in range(2):
        wg = jnp.zeros((GATE_PAD, GLA_W), F32).at[d * GLA_GATE_RANK:(d + 1) * GLA_GATE_RANK].set(w_gate[d])
        wgs.append(wg.astype(BF16))
        bgs.append(b_gate[d].reshape(1, GLA_W))
    ocb, sb = gla_direction(cgla, cgate, wgs[1], bgs[1], zeros, M, True)
    ob, _ = gla_direction(gla, gate, wgs[1], bgs[1], sb, tm, True, tables=tables)
    if need_ctx:
        oc, sf = gla_direction(cgla, cgate, wgs[0], bgs[0], zeros, M, False, combine=(ocb, ng))
    else:
        oc, sf = gla_direction(cgla, cgate, wgs[0], bgs[0], zeros, M, False)
    o, _ = gla_direction(gla, gate, wgs[0], bgs[0], sf, tm, False, tables=tables, combine=(ob, ng))
    return o, oc


def _gmlp_kernel(u_ref, v_ref, g_ref, w_ref, b_ref, o_ref, *, tm):
    v = v_ref[0]
    vn = (v * lax.rsqrt(jnp.mean(v * v, axis=-1, keepdims=True) + EPS) * g_ref[...]).astype(BF16)
    grp = lax.broadcasted_iota(jnp.int32, (GMLP_CHUNK, GMLP_W), 1) // (GMLP_W // GMLP_GROUPS)
    zero = jnp.zeros((), BF16)
    for c in range(tm // GMLP_CHUNK):
        sl = slice(c * GMLP_CHUNK, (c + 1) * GMLP_CHUNK)
        mixed = b_ref[...]
        for g in range(GMLP_GROUPS):
            mixed = mixed + _dot(w_ref[g], jnp.where(grp == g, vn[sl], zero))
        o_ref[0, sl, :] = u_ref[0, sl, :] * mixed


def gmlp_mixer(gm, norm_g, w_s, b_s, tm):
    B, L, _ = gm.shape
    W = GMLP_W
    bias = jnp.repeat(b_s.T, W // GMLP_GROUPS, axis=1)
    return pl.pallas_call(
        functools.partial(_gmlp_kernel, tm=tm),
        grid=(B, L // tm),
        in_specs=[
            pl.BlockSpec((1, tm, W), lambda b, i: (b, i, 0)),
            pl.BlockSpec((1, tm, W), lambda b, i: (b, i, 1)),
            pl.BlockSpec((1, W), lambda b, i: (0, 0)),
            pl.BlockSpec((GMLP_GROUPS, GMLP_CHUNK, GMLP_CHUNK), lambda b, i: (0, 0, 0)),
            pl.BlockSpec((GMLP_CHUNK, W), lambda b, i: (0, 0)),
        ],
        out_specs=pl.BlockSpec((1, tm, W), lambda b, i: (b, i, 0)),
        out_shape=jax.ShapeDtypeStruct((B, L, W), F32),
        compiler_params=_cparams(("parallel", "parallel")),
        name="gmlp",
    )(gm, gm, norm_g.reshape(1, W), w_s.astype(BF16), bias)


LRU_UNROLL = 8


def _lru_kernel(*refs, tm, reverse, combine):
    refs = list(refs)
    x_ref, xp_ref, xn_ref, cw_ref, cb_ref, wr_ref, br_ref, wi_ref, bi_ref, ncs_ref, h0_ref = refs[:11]
    refs = refs[11:]
    if combine:
        hb_ref, ly_ref = refs[:2]
        refs = refs[2:]
    o_ref, ext, a_s, b_s, carry = refs
    W = LRU_W
    i = pl.program_id(1)
    nt = pl.num_programs(1)
    t = (nt - 1 - i) if reverse else i
    H = SUBLANES

    ext[0:H] = jnp.where(t > 0, xp_ref[0], 0.0)
    ext[H:H + tm] = x_ref[0]
    ext[H + tm:2 * H + tm] = jnp.where(t < nt - 1, xn_ref[0], 0.0)
    xc = cb_ref[...]
    for tap in range(CONV_W):
        xc = xc + cw_ref[tap:tap + 1, :] * ext[H - CONV_W // 2 + tap:H - CONV_W // 2 + tap + tm]
    xcb = xc.astype(BF16)
    r = jax.nn.sigmoid(_dot(xcb, wr_ref[...]) + br_ref[...])
    ig = jax.nn.sigmoid(_dot(xcb, wi_ref[...]) + bi_ref[...])
    log_a = ncs_ref[...] * r
    a = jnp.exp(log_a)
    b = jnp.sqrt(-jnp.tanh(log_a) * (a * a + 1.0)) * (ig * xc)

    rowi = lax.broadcasted_iota(jnp.int32, (tm, W), 0) % H
    for s in (1, 2, 4):
        if reverse:
            ok = rowi < H - s
            a_sh = pltpu.roll(a, tm - s, axis=0)
            b_sh = pltpu.roll(b, tm - s, axis=0)
        else:
            ok = rowi >= s
            a_sh = pltpu.roll(a, s, axis=0)
            b_sh = pltpu.roll(b, s, axis=0)
        b = a * jnp.where(ok, b_sh, 0.0) + b
        a = a * jnp.where(ok, a_sh, 1.0)
    a_s[...] = a
    b_s[...] = b

    @pl.when(i == 0)
    def _():
        carry[...] = jnp.broadcast_to(h0_ref[0], (H, W))

    ng = tm // H

    def body(j, h):
        for u in range(LRU_UNROLL):
            g = j * LRU_UNROLL + u
            g = (ng - 1 - g) if reverse else g
            rows = pl.ds(pl.multiple_of(g * H, H), H)
            hg = b_s[rows, :] + a_s[rows, :] * h
            o_ref[0, rows, :] = hg
            h = jnp.broadcast_to(hg[0:1] if reverse else hg[H - 1:H], (H, W))
        return h

    carry[...] = lax.fori_loop(0, ng // LRU_UNROLL, body, carry[...])

    if combine:
        o_ref[0] = (o_ref[0] + hb_ref[0]) * jax.nn.gelu(ly_ref[0])


def lru_direction(lru, conv_w, conv_b, wr, br, wi, bi, ncs, h0, tm, reverse, hb=None):
    B, L, _ = lru.shape
    nt = L // tm
    W = LRU_W
    H = SUBLANES
    nh = L // H
    per = tm // H
    blk = (lambda i: nt - 1 - i) if reverse else (lambda i: i)
    full = lambda b, i: (0, 0)
    args = [lru, lru, lru, conv_w, conv_b, wr, br, wi, bi, ncs, h0]
    specs = [
        pl.BlockSpec((1, tm, W), lambda b, i: (b, blk(i), 0)),
        pl.BlockSpec((1, H, W), lambda b, i: (b, jnp.maximum(blk(i) * per - 1, 0), 0)),
        pl.BlockSpec((1, H, W), lambda b, i: (b, jnp.minimum((blk(i) + 1) * per, nh - 1), 0)),
        pl.BlockSpec((CONV_W, W), full), pl.BlockSpec((1, W), full),
        pl.BlockSpec((W, W), full), pl.BlockSpec((1, W), full),
        pl.BlockSpec((W, W), full), pl.BlockSpec((1, W), full),
        pl.BlockSpec((1, W), full),
        pl.BlockSpec((1, 1, W), lambda b, i: (b, 0, 0)),
    ]
    if hb is not None:
        args += [hb, lru]
        specs += [pl.BlockSpec((1, tm, W), lambda b, i: (b, blk(i), 0)),
                  pl.BlockSpec((1, tm, W), lambda b, i: (b, blk(i), 1))]
    return pl.pallas_call(
        functools.partial(_lru_kernel, tm=tm, reverse=reverse, combine=hb is not None),
        grid=(B, nt),
        in_specs=specs,
        out_specs=pl.BlockSpec((1, tm, W), lambda b, i: (b, blk(i), 0)),
        out_shape=jax.ShapeDtypeStruct((B, L, W), F32),
        scratch_shapes=[pltpu.VMEM((tm + 2 * H, W), F32), pltpu.VMEM((tm, W), F32), pltpu.VMEM((tm, W), F32),
                        pltpu.VMEM((H, W), F32)],
        compiler_params=_cparams(("parallel", "arbitrary")),
        name="lru_bwd" if reverse else "lru_fwd",
    )(*args)


def _block_diag(w):
    G, n, _ = w.shape
    out = jnp.zeros((G * n, G * n), w.dtype)
    for g in range(G):
        out = out.at[g * n:(g + 1) * n, g * n:(g + 1) * n].set(w[g])
    return out


def lru_mixer(lru, clru, conv_w, conv_b, w_r, b_r, w_i, b_i, lam, need_ctx, tm):
    B, M, _ = clru.shape
    W = LRU_W
    cb = conv_b.reshape(1, W)
    ncs = -LRU_C * jax.nn.softplus(-lam.astype(F32))
    p = [(_block_diag(w_r[d]).astype(BF16), b_r[d].reshape(1, W), _block_diag(w_i[d]).astype(BF16),
          b_i[d].reshape(1, W), ncs[d].reshape(1, W)) for d in range(2)]
    zeros = jnp.zeros((B, 1, W), F32)
    hcb = lru_direction(clru, conv_w, cb, *p[1], zeros, M, True)
    hb = lru_direction(lru, conv_w, cb, *p[1], hcb[:, 0:1], tm, True)
    hcf = lru_direction(clru, conv_w, cb, *p[0], zeros, M, False)
    out = lru_direction(lru, conv_w, cb, *p[0], hcf[:, M - 1:M], tm, False, hb=hb)
    out_c = None
    if need_ctx:
        out_c = lru_direction(clru, conv_w, cb, *p[0], zeros, M, False, hb=hcb)
    return out, out_c


NAT_QROWS = 8
NAT_KROWS = 2 * NAT_QROWS
NAT_TQ = NAT_QROWS * GRID_W
NAT_TK = NAT_KROWS * GRID_W


def _softmax_pv(s_loc, s_ctx, v_loc, v_ctx):
    m = jnp.maximum(jnp.max(s_loc, axis=-1, keepdims=True), jnp.max(s_ctx, axis=-1, keepdims=True))
    e_loc = jnp.exp(s_loc - m)
    e_ctx = jnp.exp(s_ctx - m)
    den = jnp.sum(e_loc, axis=-1, keepdims=True) + jnp.sum(e_ctx, axis=-1, keepdims=True)
    o = _dot(e_loc.astype(BF16), v_loc) + _dot(e_ctx.astype(BF16), v_ctx)
    return o / den


def _nat_kernel(q_ref, kp_ref, kc_ref, kn_ref, vp_ref, vc_ref, vn_ref, ck_ref, cv_ref, bias_ref, o_ref, kcat, vcat):
    j = pl.program_id(1)
    last = pl.num_programs(1) - 1
    T = NAT_TQ
    kcat[0:T] = kp_ref[0]
    kcat[T:2 * T] = kc_ref[0]
    kcat[2 * T:3 * T] = kn_ref[0]
    vcat[0:T] = vp_ref[0]
    vcat[T:2 * T] = vc_ref[0]
    vcat[2 * T:3 * T] = vn_ref[0]
    off = jnp.where(j == 0, T, jnp.where(j == last, 0, T // 2))
    off = pl.multiple_of(off, T // 2)
    kw = kcat[pl.ds(off, NAT_TK), :]
    vw = vcat[pl.ds(off, NAT_TK), :]
    q = q_ref[0]
    ck = ck_ref[0]
    cv = cv_ref[0]
    scale = NAT_DH ** -0.5
    outs = []
    for h in range(NAT_HEADS):
        hs = slice(h * NAT_DH, (h + 1) * NAT_DH)
        qh = q[:, hs]
        s_loc = lax.dot_general(qh, kw[:, hs], NT_DIMS, preferred_element_type=F32) * scale + bias_ref[0, h]
        s_ctx = lax.dot_general(qh, ck[:, hs], NT_DIMS, preferred_element_type=F32) * scale
        outs.append(_softmax_pv(s_loc, s_ctx, vw[:, hs], cv[:, hs]))
    o_ref[0] = jnp.concatenate(outs, axis=-1)


def nat_bias_tables(rpb, rows):
    kr = NAT_KR_MAX
    qc = np.arange(GRID_W)
    kcol = np.arange(GRID_W)
    cs = np.clip(qc - NAT_KC // 2, 0, GRID_W - NAT_KC)
    valid_c = (kcol[None, :] >= cs[:, None]) & (kcol[None, :] < cs[:, None] + NAT_KC)
    dc = np.clip(kcol[None, :] - qc[:, None], -(NAT_KC - 1), NAT_KC - 1) + NAT_KC - 1
    tables = []
    for r0, ks in ((0, 0), (NAT_QROWS, NAT_QROWS - kr // 2), (rows - NAT_QROWS, rows - NAT_KROWS)):
        r = r0 + np.arange(NAT_QROWS)
        krow = ks + np.arange(NAT_KROWS)
        rs = np.clip(r - kr // 2, 0, rows - kr)
        valid_r = (krow[None, :] >= rs[:, None]) & (krow[None, :] < rs[:, None] + kr)
        dr = np.clip(krow[None, :] - r[:, None] + NAT_KR_MAX - 1, 0, 2 * NAT_KR_MAX - 2)
        bias = rpb[:, dr[:, None, :, None], dc[None, :, None, :]].astype(F32)
        mask = valid_r[:, None, :, None] & valid_c[None, :, None, :]
        bias = jnp.where(jnp.asarray(mask)[None], bias, NEG_BIG)
        tables.append(bias.reshape(rpb.shape[0], NAT_TQ, NAT_TK))
    return jnp.stack(tables)


def nat_mixer(nat, cnat, bias):
    B, N, _ = nat.shape
    M = cnat.shape[1]
    W = NAT_W
    nb = N // NAT_TQ
    T = NAT_TQ
    cur = lambda c: (lambda b, j: (b, j, c))
    prv = lambda c: (lambda b, j: (b, jnp.maximum(j - 1, 0), c))
    nxt = lambda c: (lambda b, j: (b, jnp.minimum(j + 1, nb - 1), c))
    variant = lambda b, j: (jnp.where(j == 0, 0, jnp.where(j == nb - 1, 2, 1)), 0, 0, 0)
    return pl.pallas_call(
        _nat_kernel,
        grid=(B, nb),
        in_specs=[
            pl.BlockSpec((1, T, W), cur(0)),
            pl.BlockSpec((1, T, W), prv(1)), pl.BlockSpec((1, T, W), cur(1)), pl.BlockSpec((1, T, W), nxt(1)),
            pl.BlockSpec((1, T, W), prv(2)), pl.BlockSpec((1, T, W), cur(2)), pl.BlockSpec((1, T, W), nxt(2)),
            pl.BlockSpec((1, M, W), lambda b, j: (b, 0, 1)),
            pl.BlockSpec((1, M, W), lambda b, j: (b, 0, 2)),
            pl.BlockSpec((1, NAT_HEADS, NAT_TQ, NAT_TK), variant),
        ],
        out_specs=pl.BlockSpec((1, T, W), cur(0)),
        out_shape=jax.ShapeDtypeStruct((B, N, W), F32),
        scratch_shapes=[pltpu.VMEM((3 * T, W), BF16), pltpu.VMEM((3 * T, W), BF16)],
        compiler_params=_cparams(("parallel", "arbitrary")),
        name="nat",
    )(nat, nat, nat, nat, nat, nat, nat, cnat, cnat, bias)


def _ctx_attn_kernel(q_ref, k_ref, v_ref, o_ref):
    q = q_ref[0]
    k = k_ref[0]
    v = v_ref[0]
    scale = NAT_DH ** -0.5
    outs = []
    for h in range(NAT_HEADS):
        hs = slice(h * NAT_DH, (h + 1) * NAT_DH)
        s = lax.dot_general(q[:, hs], k[:, hs], NT_DIMS, preferred_element_type=F32) * scale
        e = jnp.exp(s - jnp.max(s, axis=-1, keepdims=True))
        outs.append(_dot(e.astype(BF16), v[:, hs]) / jnp.sum(e, axis=-1, keepdims=True))
    o_ref[0] = jnp.concatenate(outs, axis=-1)


def ctx_attention(cnat):
    B, M, _ = cnat.shape
    W = NAT_W
    return pl.pallas_call(
        _ctx_attn_kernel,
        grid=(B,),
        in_specs=[pl.BlockSpec((1, M, W), lambda b, c=c: (b, 0, c)) for c in range(3)],
        out_specs=pl.BlockSpec((1, M, W), lambda b: (b, 0, 0)),
        out_shape=jax.ShapeDtypeStruct((B, M, W), F32),
        compiler_params=_cparams(("parallel",)),
        name="ctx_attn",
    )(cnat, cnat, cnat)


def _outproj_kernel(m0_ref, m1_ref, m2_ref, m3_ref, w_ref, x_ref, g1_ref, n2_ref, sc_ref, sh_ref, wr_ref,
                    xo_ref, h2_ref, aff_ref):
    y = 0.0
    for gi, m_ref in enumerate((m0_ref, m1_ref, m2_ref, m3_ref)):
        y = y + _dot(m_ref[0].astype(BF16), w_ref[gi * GROUP_W:(gi + 1) * GROUP_W, :])
    x = x_ref[0] + g1_ref[0] * y
    xo_ref[0] = x
    xn = x * lax.rsqrt(jnp.mean(x * x, axis=-1, keepdims=True) + EPS)
    h2b = ((xn * n2_ref[...]) * (1.0 + sc_ref[0]) + sh_ref[0]).astype(BF16)
    h2_ref[0] = h2b
    logits = _dot(h2b, wr_ref[...])
    e = jnp.exp(logits - jnp.max(logits, axis=-1, keepdims=True))
    aff_ref[0] = e / jnp.sum(e, axis=-1, keepdims=True)


def outproj_norm_router(parts, w_out, x, g1, n2, scale2, shift2, w_router, tm):
    B, n, D = x.shape
    E = w_router.shape[1]
    row = lambda b, i: (b, i, 0)
    per_b = lambda b, i: (b, 0, 0)
    full = lambda b, i: (0, 0)
    return pl.pallas_call(
        _outproj_kernel,
        grid=(B, n // tm),
        in_specs=[pl.BlockSpec((1, tm, GROUP_W), row)] * N_GROUPS + [
            pl.BlockSpec((D, D), full),
            pl.BlockSpec((1, tm, D), row),
            pl.BlockSpec((1, 1, D), per_b),
            pl.BlockSpec((1, D), full),
            pl.BlockSpec((1, 1, D), per_b),
            pl.BlockSpec((1, 1, D), per_b),
            pl.BlockSpec((D, E), full),
        ],
        out_specs=[
            pl.BlockSpec((1, tm, D), row),
            pl.BlockSpec((1, tm, D), row),
            pl.BlockSpec((1, tm, E), row),
        ],
        out_shape=[
            jax.ShapeDtypeStruct((B, n, D), F32),
            jax.ShapeDtypeStruct((B, n, D), BF16),
            jax.ShapeDtypeStruct((B, n, E), F32),
        ],
        compiler_params=_cparams(("parallel", "parallel")),
        name="outproj_norm_router",
    )(*parts, w_out, x, g1.reshape(B, 1, D), n2.reshape(1, D), scale2.reshape(B, 1, D), shift2.reshape(B, 1, D),
      w_router)


def _moe_ffn_kernel(x_ref, wg_ref, wu_ref, wd_ref, g_ref, o_ref, *, tm, n_chunks):
    f = pl.program_id(1)
    last = pl.num_programs(1) - 1
    wg = wg_ref[0].astype(BF16)
    wu = wu_ref[0].astype(BF16)
    wd = wd_ref[0].astype(BF16)

    def chunk(i, carry):
        r = pl.ds(pl.multiple_of(i * tm, tm), tm)
        x = x_ref[0, r, :]
        hg = _dot(x, wg)
        hu = _dot(x, wu)
        hid = (hg * jax.nn.sigmoid(hg) * hu).astype(BF16)
        y = _dot(hid, wd)

        @pl.when(f == 0)
        def _():
            o_ref[0, r, :] = y

        @pl.when(f > 0)
        def _():
            o_ref[0, r, :] += y

        return carry

    lax.fori_loop(0, n_chunks, chunk, 0)

    @pl.when(f == last)
    def _():
        o_ref[0] = o_ref[0] * g_ref[0]


def moe_ffn(xs, w_gate, w_up, w_down, gates, tf, tm):
    E, T, D = xs.shape
    F = w_gate.shape[2]
    return pl.pallas_call(
        functools.partial(_moe_ffn_kernel, tm=tm, n_chunks=T // tm),
        grid=(E, F // tf),
        in_specs=[
            pl.BlockSpec((1, T, D), lambda e, f: (e, 0, 0)),
            pl.BlockSpec((1, D, tf), lambda e, f: (e, 0, f)),
            pl.BlockSpec((1, D, tf), lambda e, f: (e, 0, f)),
            pl.BlockSpec((1, tf, D), lambda e, f: (e, f, 0)),
            pl.BlockSpec((1, T, 1), lambda e, f: (e, 0, 0)),
        ],
        out_specs=pl.BlockSpec((1, T, D), lambda e, f: (e, 0, 0)),
        out_shape=jax.ShapeDtypeStruct((E, T, D), F32),
        compiler_params=_cparams(("parallel", "arbitrary")),
        name="moe_ffn",
    )(xs, w_gate, w_up, w_down, gates)


def _final_norm_kernel(x_ref, g_ref, o_ref):
    x = x_ref[0]
    o_ref[0] = x * lax.rsqrt(jnp.mean(x * x, axis=-1, keepdims=True) + EPS) * g_ref[...]


def final_norm(x, g, tm):
    B, n, D = x.shape
    return pl.pallas_call(
        _final_norm_kernel,
        grid=(B, n // tm),
        in_specs=[pl.BlockSpec((1, tm, D), lambda b, i: (b, i, 0)), pl.BlockSpec((1, D), lambda b, i: (0, 0))],
        out_specs=pl.BlockSpec((1, tm, D), lambda b, i: (b, i, 0)),
        out_shape=jax.ShapeDtypeStruct((B, n, D), F32),
        compiler_params=_cparams(("parallel", "parallel")),
        name="final_norm",
    )(x, g.reshape(1, D))


def route(aff, cap):
    return lax.top_k(jnp.swapaxes(aff, 1, 2), cap)


TOKEN_TILE = 512


def kernel(x, c, ctx, c_ctx, w_ada, b_ada, norm1_g, norm2_g, w_in, w_out, gla_w_gate, gla_b_gate, gla_norm_g,
           gmlp_norm_g, gmlp_w_s, gmlp_b_s, lru_conv_w, lru_conv_b, lru_w_r, lru_b_r, lru_w_i, lru_b_i,
           lru_lambda, nat_rpb, moe_w_router, moe_w_gate, moe_w_up, moe_w_down, final_norm_g):
    B, N, D = x.shape
    M = ctx.shape[1]
    E = N_EXPERTS
    tm = TOKEN_TILE
    cap_l = max(1, EC_CAPACITY * N // E)
    cap_c = max(1, EC_CAPACITY * M // E)
    sc = jax.nn.silu(c)
    scc = jax.nn.silu(c_ctx)
    xc = ctx
    bi = jnp.arange(B)[:, None, None]
    tables = rope_tables(N)
    for l in range(DEPTH):
        need_ctx = l < DEPTH - 1
        mod = sc @ w_ada[l] + b_ada[l]
        mod_c = jnp.broadcast_to((scc @ w_ada[l] + b_ada[l])[None], (B, 6 * D))
        sh1, sc1, g1, sh2, sc2, g2 = jnp.split(mod, 6, axis=-1)
        csh1, csc1, cg1, csh2, csc2, cg2 = jnp.split(mod_c, 6, axis=-1)

        wp = permute_w_in(w_in[l])
        gla, gm, lru, nat, gate = norm_inproj(x, norm1_g[l], sc1, sh1, wp, tm)
        cgla, cgm, clru, cnat, cgate = norm_inproj(xc, norm1_g[l], csc1, csh1, wp, M)

        gla_o, gla_c = gla_mixer(gla, gate, cgla, cgate, gla_w_gate[l], gla_b_gate[l], gla_norm_g[l], tables,
                                 need_ctx, tm)
        gm_o = gmlp_mixer(gm, gmlp_norm_g[l], gmlp_w_s[l], gmlp_b_s[l], tm)
        lru_o, lru_c = lru_mixer(lru, clru, lru_conv_w[l], lru_conv_b[l], lru_w_r[l], lru_b_r[l], lru_w_i[l],
                                 lru_b_i[l], lru_lambda[l], need_ctx, tm)
        nat_o = nat_mixer(nat, cnat, nat_bias_tables(nat_rpb[l], N // GRID_W))

        wo = w_out[l].astype(BF16)
        wr = moe_w_router[l].astype(BF16)
        x, h2, aff = outproj_norm_router((gla_o, gm_o, lru_o, nat_o), wo, x, g1, norm2_g[l], sc2, sh2, wr, tm)
        gl, il = route(aff, cap_l)
        xs = jnp.swapaxes(h2[bi, il], 0, 1).reshape(E, B * cap_l, D)
        gs = jnp.swapaxes(gl, 0, 1).reshape(E, B * cap_l, 1)
        if need_ctx:
            gm_c = gmlp_mixer(cgm, gmlp_norm_g[l], gmlp_w_s[l], gmlp_b_s[l], M)
            nat_c = ctx_attention(cnat)
            xc, hc2, caff = outproj_norm_router((gla_c, gm_c, lru_c, nat_c), wo, xc, cg1, norm2_g[l], csc2, csh2,
                                                wr, M)
            gc, ic = route(caff, cap_c)
            xsc = jnp.swapaxes(hc2[bi, ic], 0, 1).reshape(E, B * cap_c, D)
            gsc = jnp.swapaxes(gc, 0, 1).reshape(E, B * cap_c, 1)
            xs = jnp.concatenate([xs, xsc], axis=1)
            gs = jnp.concatenate([gs, gsc], axis=1)
        T = xs.shape[1]
        ys = moe_ffn(xs, moe_w_gate[l], moe_w_up[l], moe_w_down[l], gs, tf=512, tm=T // 4)
        yl = jnp.swapaxes(ys[:, :B * cap_l].reshape(E, B, cap_l, D), 0, 1)
        x = x + g2[:, None, :] * jnp.zeros_like(x).at[bi, il].add(yl)
        if need_ctx:
            ycx = jnp.swapaxes(ys[:, B * cap_l:].reshape(E, B, cap_c, D), 0, 1)
            xc = xc + cg2[:, None, :] * jnp.zeros_like(xc).at[bi, ic].add(ycx)
    return final_norm(x, final_norm_g, tm)
```

```python
import functools

import numpy as np
import jax
import jax.numpy as jnp
from jax import lax
from jax.experimental import pallas as pl
from jax.experimental.pallas import tpu as pltpu

D_MODEL = 1024
DEPTH = 4
GRID_W = 64
N_GROUPS = 4
GROUP_W = D_MODEL // N_GROUPS
GLA_HEADS = 4
GLA_DK = GROUP_W // GLA_HEADS
GLA_W = GLA_HEADS * GLA_DK
GLA_GATE_RANK = 16
GLA_TAU = 16.0
GLA_CHUNK = 64
GMLP_GROUPS = 4
GMLP_W = GROUP_W
GMLP_CHUNK = 128
LRU_W = GROUP_W
LRU_BLOCKS = 4
LRU_C = 8.0
CONV_W = 4
NAT_HEADS = 4
NAT_DH = GROUP_W // NAT_HEADS
NAT_W = NAT_HEADS * NAT_DH
NAT_KR_MAX = 8
NAT_KC = 16
N_EXPERTS = 16
EXPERT_FF = 2 * D_MODEL
EC_CAPACITY = 2
ROPE_BASE = 10000.0
EPS = 1e-6

IN_SIZES = (GLA_W, GLA_W, GLA_W, GLA_W, GLA_GATE_RANK, GLA_GATE_RANK,
            GMLP_W, GMLP_W, LRU_W, LRU_W, NAT_W, NAT_W, NAT_W)
IN_COLS = sum(IN_SIZES)

V7X_VMEM_LIMIT_BYTES = 56 * 1024 * 1024
LANES = 128
SUBLANES = 8
F32 = jnp.float32
BF16 = jnp.bfloat16
NEG_BIG = -1e30
GATE_PAD = LANES

NT_DIMS = (((1,), (1,)), ((), ()))
TN_DIMS = (((0,), (0,)), ((), ()))


def _cparams(sem):
    return pltpu.CompilerParams(dimension_semantics=sem, vmem_limit_bytes=V7X_VMEM_LIMIT_BYTES)


def _dot(a, b):
    return jnp.dot(a, b, preferred_element_type=F32)


def _split3(x):
    hi = x.astype(BF16)
    r1 = x - hi.astype(F32)
    mid = r1.astype(BF16)
    lo = (r1 - mid.astype(F32)).astype(BF16)
    return hi, mid, lo


def _head_block_mask(n):
    r = lax.broadcasted_iota(jnp.int32, (n, n), 0) // GLA_DK
    c = lax.broadcasted_iota(jnp.int32, (n, n), 1) // GLA_DK
    return r == c


def _norm_inproj_kernel(x_ref, g_ref, sc_ref, sh_ref, w_ref, gla_ref, gmlp_ref, lru_ref, nat_ref, gate_ref):
    x = x_ref[0]
    y = x * lax.rsqrt(jnp.mean(x * x, axis=-1, keepdims=True) + EPS)
    h = ((y * g_ref[...]) * (1.0 + sc_ref[0]) + sh_ref[0]).astype(BF16)
    o = 0
    for ref in (gla_ref, gmlp_ref, lru_ref, nat_ref, gate_ref):
        w = ref.shape[-1]
        ref[0] = _dot(h, w_ref[:, o:o + w]).astype(ref.dtype)
        o += w


def permute_w_in(w):
    g0 = 4 * GLA_W
    g1 = g0 + 2 * GLA_GATE_RANK
    pad = jnp.zeros((w.shape[0], GATE_PAD - 2 * GLA_GATE_RANK), w.dtype)
    return jnp.concatenate([w[:, :g0], w[:, g1:], w[:, g0:g1], pad], axis=1).astype(BF16)


def norm_inproj(x, g, scale, shift, wp, tm):
    B, n, D = x.shape
    C = wp.shape[1]
    widths = (4 * GLA_W, 2 * GMLP_W, 2 * LRU_W, 3 * NAT_W, GATE_PAD)
    dtypes = (F32, F32, F32, BF16, F32)
    row = lambda b, i: (b, i, 0)
    per_b = lambda b, i: (b, 0, 0)
    full = lambda b, i: (0, 0)
    return pl.pallas_call(
        _norm_inproj_kernel,
        grid=(B, n // tm),
        in_specs=[
            pl.BlockSpec((1, tm, D), row),
            pl.BlockSpec((1, D), full),
            pl.BlockSpec((1, 1, D), per_b),
            pl.BlockSpec((1, 1, D), per_b),
            pl.BlockSpec((D, C), full),
        ],
        out_specs=[pl.BlockSpec((1, tm, w), row) for w in widths],
        out_shape=[jax.ShapeDtypeStruct((B, n, w), dt) for w, dt in zip(widths, dtypes)],
        compiler_params=_cparams(("parallel", "parallel")),
        name="norm_inproj",
    )(x, g.reshape(1, D), scale.reshape(B, 1, D), shift.reshape(B, 1, D), wp)


def _gla_kernel(*refs, tm, reverse, rope, combine):
    refs = list(refs)
    q_ref, k_ref, v_ref, gate_ref, wg_ref, bg_ref, tri_ref, ones_ref, s0_ref = refs[:9]
    refs = refs[9:]
    if rope:
        cos_ref, sin_ref = refs[:2]
        refs = refs[2:]
    if combine:
        ob_ref, og_ref, ng_ref = refs[:3]
        refs = refs[3:]
    o_ref, sfin_ref, st = refs
    i = pl.program_id(1)
    nch = tm // GLA_CHUNK

    @pl.when(i == 0)
    def _():
        st[...] = s0_ref[0]

    q = q_ref[0]
    k = k_ref[0]
    if rope:
        cos = cos_ref[...]
        sin = sin_ref[...]
        first = (lax.broadcasted_iota(jnp.int32, (tm, GLA_W), 1) % (GLA_DK // 2)) < (GLA_DK // 4)

        def rot(u):
            partner = jnp.where(first, pltpu.roll(u, GLA_W - GLA_DK // 4, axis=1), pltpu.roll(u, GLA_DK // 4, axis=1))
            return u * cos + partner * sin

        q = rot(q)
        k = rot(k)
    q = q * (GLA_DK ** -0.5)
    a = _dot(gate_ref[0].astype(BF16), wg_ref[...]) + bg_ref[...]
    la = jax.nn.log_sigmoid(a) / GLA_TAU
    pieces = _split3(la)
    b = sum(_dot(tri_ref[...], p) for p in pieces)
    bl = sum(_dot(ones_ref[...], p) for p in pieces)
    q_in = (q * jnp.exp(b)).astype(BF16)
    k_in = (k * jnp.exp(-b)).astype(BF16)
    k_end = (k * jnp.exp(bl - b)).astype(BF16)
    dec = jnp.exp(bl)
    vb = v_ref[0].astype(BF16)

    head_mask = _head_block_mask(GLA_W)
    cr = lax.broadcasted_iota(jnp.int32, (GLA_CHUNK, GLA_W), 0)
    cs = lax.broadcasted_iota(jnp.int32, (GLA_CHUNK, GLA_W), 1) % GLA_CHUNK
    causal = (cs >= cr) if reverse else (cs <= cr)
    zero = jnp.zeros((), BF16)
    chunks = range(nch - 1, -1, -1) if reverse else range(nch)
    for c in chunks:
        sl = slice(c * GLA_CHUNK, (c + 1) * GLA_CHUNK)
        kbd_t = jnp.where(head_mask, jnp.concatenate([k_in[sl]] * GLA_HEADS, axis=0), zero)
        vbd = jnp.where(head_mask, jnp.concatenate([vb[sl]] * GLA_HEADS, axis=0), zero)
        att = lax.dot_general(q_in[sl], kbd_t, NT_DIMS, preferred_element_type=F32)
        att = jnp.where(causal, att, 0.0).astype(BF16)
        s_t = st[...]
        o = _dot(att, vbd) + lax.dot_general(q_in[sl], s_t.astype(BF16), NT_DIMS, preferred_element_type=F32)
        upd = lax.dot_general(vb[sl], k_end[sl], TN_DIMS, preferred_element_type=F32)
        st[...] = s_t * dec[c * GLA_CHUNK:c * GLA_CHUNK + 1] + jnp.where(head_mask, upd, 0.0)
        o_ref[0, sl, :] = o

    sfin_ref[0] = st[...]

    if combine:
        o = o_ref[0] + ob_ref[0]
        sq_hi, sq_mid, _ = _split3(o * o)
        avg = jnp.where(head_mask, 1.0 / GLA_DK, 0.0).astype(BF16)
        ms = _dot(sq_hi, avg) + _dot(sq_mid, avg)
        og = og_ref[0]
        o_ref[0] = o * lax.rsqrt(ms + EPS) * ng_ref[...] * (og * jax.nn.sigmoid(og))


def _chunk_sum_matrices(tm, reverse):
    r = np.arange(tm)[:, None]
    c = np.arange(tm)[None, :]
    same = (r // GLA_CHUNK) == (c // GLA_CHUNK)
    tri = same & ((c >= r) if reverse else (c <= r))
    return jnp.asarray(tri, BF16), jnp.asarray(same, BF16)


def gla_direction(gla, gate, wg, bg, s0, tm, reverse, tables=None, combine=None):
    B, L, _ = gla.shape
    nt = L // tm
    W = GLA_W
    blk = (lambda i: nt - 1 - i) if reverse else (lambda i: i)
    col = lambda j: (lambda b, i: (b, blk(i), j))
    full = lambda b, i: (0, 0)
    tri, ones = _chunk_sum_matrices(tm, reverse)
    args = [gla, gla, gla, gate, wg, bg, tri, ones, s0]
    specs = [
        pl.BlockSpec((1, tm, W), col(0)), pl.BlockSpec((1, tm, W), col(1)), pl.BlockSpec((1, tm, W), col(2)),
        pl.BlockSpec((1, tm, GATE_PAD), col(0)),
        pl.BlockSpec((GATE_PAD, W), full), pl.BlockSpec((1, W), full),
        pl.BlockSpec((tm, tm), full), pl.BlockSpec((tm, tm), full),
        pl.BlockSpec((1, W, W), lambda b, i: (b, 0, 0)),
    ]
    if tables is not None:
        args += list(tables)
        specs += [pl.BlockSpec((tm, W), lambda b, i: (blk(i), 0))] * 2
    if combine is not None:
        ob, norm_g = combine
        args += [ob, gla, norm_g]
        specs += [pl.BlockSpec((1, tm, W), col(0)), pl.BlockSpec((1, tm, W), col(3)), pl.BlockSpec((1, W), full)]
    return pl.pallas_call(
        functools.partial(_gla_kernel, tm=tm, reverse=reverse, rope=tables is not None, combine=combine is not None),
        grid=(B, nt),
        in_specs=specs,
        out_specs=[pl.BlockSpec((1, tm, W), col(0)), pl.BlockSpec((1, W, W), lambda b, i: (b, 0, 0))],
        out_shape=[jax.ShapeDtypeStruct((B, L, W), F32), jax.ShapeDtypeStruct((B, W, W), F32)],
        scratch_shapes=[pltpu.VMEM((W, W), F32)],
        compiler_params=_cparams(("parallel", "arbitrary")),
        name="gla_bwd" if reverse else "gla_fwd",
    )(*args)


def rope_tables(n):
    quarter = GLA_DK // 4
    pos = jnp.arange(n)
    inv = ROPE_BASE ** (-jnp.arange(quarter, dtype=F32) / quarter)
    row = (pos // GRID_W).astype(F32)
    colp = (pos % GRID_W).astype(F32)
    ar = row[:, None] * inv[None, :]
    ac = colp[:, None] * inv[None, :]
    cos = jnp.concatenate([jnp.cos(ar), jnp.cos(ar), jnp.cos(ac), jnp.cos(ac)], axis=1)
    sin = jnp.concatenate([-jnp.sin(ar), jnp.sin(ar), -jnp.sin(ac), jnp.sin(ac)], axis=1)
    return jnp.tile(cos, (1, GLA_HEADS)), jnp.tile(sin, (1, GLA_HEADS))


def gla_mixer(gla, gate, cgla, cgate, w_gate, b_gate, norm_g, tables, need_ctx, tm):
    B, M, _ = cgla.shape
    zeros = jnp.zeros((B, GLA_W, GLA_W), F32)
    ng = jnp.tile(norm_g, GLA_HEADS).reshape(1, GLA_W)
    wgs, bgs = [], []
    for d in range(2):
        wg = jnp.zeros((GATE_PAD, GLA_W), F32).at[d * GLA_GATE_RANK:(d + 1) * GLA_GATE_RANK].set(w_gate[d])
        wgs.append(wg.astype(BF16))
        bgs.append(b_gate[d].reshape(1, GLA_W))
    ocb, sb = gla_direction(cgla, cgate, wgs[1], bgs[1], zeros, M, True)
    ob, _ = gla_direction(gla, gate, wgs[1], bgs[1], sb, tm, True, tables=tables)
    if need_ctx:
        oc, sf = gla_direction(cgla, cgate, wgs[0], bgs[0], zeros, M, False, combine=(ocb, ng))
    else:
        oc, sf = gla_direction(cgla, cgate, wgs[0], bgs[0], zeros, M, False)
    o, _ = gla_direction(gla, gate, wgs[0], bgs[0], sf, tm, False, tables=tables, combine=(ob, ng))
    return o, oc


def _gmlp_kernel(u_ref, v_ref, g_ref, w_ref, b_ref, o_ref, *, tm):
    v = v_ref[0]
    vn = (v * lax.rsqrt(jnp.mean(v * v, axis=-1, keepdims=True) + EPS) * g_ref[...]).astype(BF16)
    grp = lax.broadcasted_iota(jnp.int32, (GMLP_CHUNK, GMLP_W), 1) // (GMLP_W // GMLP_GROUPS)
    zero = jnp.zeros((), BF16)
    for c in range(tm // GMLP_CHUNK):
        sl = slice(c * GMLP_CHUNK, (c + 1) * GMLP_CHUNK)
        mixed = b_ref[...]
        for g in range(GMLP_GROUPS):
            mixed = mixed + _dot(w_ref[g], jnp.where(grp == g, vn[sl], zero))
        o_ref[0, sl, :] = u_ref[0, sl, :] * mixed


def gmlp_mixer(gm, norm_g, w_s, b_s, tm):
    B, L, _ = gm.shape
    W = GMLP_W
    bias = jnp.repeat(b_s.T, W // GMLP_GROUPS, axis=1)
    return pl.pallas_call(
        functools.partial(_gmlp_kernel, tm=tm),
        grid=(B, L // tm),
        in_specs=[
            pl.BlockSpec((1, tm, W), lambda b, i: (b, i, 0)),
            pl.BlockSpec((1, tm, W), lambda b, i: (b, i, 1)),
            pl.BlockSpec((1, W), lambda b, i: (0, 0)),
            pl.BlockSpec((GMLP_GROUPS, GMLP_CHUNK, GMLP_CHUNK), lambda b, i: (0, 0, 0)),
            pl.BlockSpec((GMLP_CHUNK, W), lambda b, i: (0, 0)),
        ],
        out_specs=pl.BlockSpec((1, tm, W), lambda b, i: (b, i, 0)),
        out_shape=jax.ShapeDtypeStruct((B, L, W), F32),
        compiler_params=_cparams(("parallel", "parallel")),
        name="gmlp",
    )(gm, gm, norm_g.reshape(1, W), w_s.astype(BF16), bias)


LRU_UNROLL = 8


def _lru_kernel(*refs, tm, reverse, combine):
    refs = list(refs)
    x_ref, xp_ref, xn_ref, cw_ref, cb_ref, wr_ref, br_ref, wi_ref, bi_ref, ncs_ref, h0_ref = refs[:11]
    refs = refs[11:]
    if combine:
        hb_ref, ly_ref = refs[:2]
        refs = refs[2:]
    o_ref, ext, a_s, b_s, carry = refs
    W = LRU_W
    i = pl.program_id(1)
    nt = pl.num_programs(1)
    t = (nt - 1 - i) if reverse else i
    H = SUBLANES

    ext[0:H] = jnp.where(t > 0, xp_ref[0], 0.0)
    ext[H:H + tm] = x_ref[0]
    ext[H + tm:2 * H + tm] = jnp.where(t < nt - 1, xn_ref[0], 0.0)
    xc = cb_ref[...]
    for tap in range(CONV_W):
        xc = xc + cw_ref[tap:tap + 1, :] * ext[H - CONV_W // 2 + tap:H - CONV_W // 2 + tap + tm]
    xcb = xc.astype(BF16)
    r = jax.nn.sigmoid(_dot(xcb, wr_ref[...]) + br_ref[...])
    ig = jax.nn.sigmoid(_dot(xcb, wi_ref[...]) + bi_ref[...])
    log_a = ncs_ref[...] * r
    a = jnp.exp(log_a)
    b = jnp.sqrt(-jnp.tanh(log_a) * (a * a + 1.0)) * (ig * xc)

    rowi = lax.broadcasted_iota(jnp.int32, (tm, W), 0) % H
    for s in (1, 2, 4):
        if reverse:
            ok = rowi < H - s
            a_sh = pltpu.roll(a, tm - s, axis=0)
            b_sh = pltpu.roll(b, tm - s, axis=0)
        else:
            ok = rowi >= s
            a_sh = pltpu.roll(a, s, axis=0)
            b_sh = pltpu.roll(b, s, axis=0)
        b = a * jnp.where(ok, b_sh, 0.0) + b
        a = a * jnp.where(ok, a_sh, 1.0)
    a_s[...] = a
    b_s[...] = b

    @pl.when(i == 0)
    def _():
        carry[...] = jnp.broadcast_to(h0_ref[0], (H, W))

    ng = tm // H

    def body(j, h):
        for u in range(LRU_UNROLL):
            g = j * LRU_UNROLL + u
            g = (ng - 1 - g) if reverse else g
            rows = pl.ds(pl.multiple_of(g * H, H), H)
            hg = b_s[rows, :] + a_s[rows, :] * h
            o_ref[0, rows, :] = hg
            h = jnp.broadcast_to(hg[0:1] if reverse else hg[H - 1:H], (H, W))
        return h

    carry[...] = lax.fori_loop(0, ng // LRU_UNROLL, body, carry[...])

    if combine:
        o_ref[0] = (o_ref[0] + hb_ref[0]) * jax.nn.gelu(ly_ref[0])


def lru_direction(lru, conv_w, conv_b, wr, br, wi, bi, ncs, h0, tm, reverse, hb=None):
    B, L, _ = lru.shape
    nt = L // tm
    W = LRU_W
    H = SUBLANES
    nh = L // H
    per = tm // H
    blk = (lambda i: nt - 1 - i) if reverse else (lambda i: i)
    full = lambda b, i: (0, 0)
    args = [lru, lru, lru, conv_w, conv_b, wr, br, wi, bi, ncs, h0]
    specs = [
        pl.BlockSpec((1, tm, W), lambda b, i: (b, blk(i), 0)),
        pl.BlockSpec((1, H, W), lambda b, i: (b, jnp.maximum(blk(i) * per - 1, 0), 0)),
        pl.BlockSpec((1, H, W), lambda b, i: (b, jnp.minimum((blk(i) + 1) * per, nh - 1), 0)),
        pl.BlockSpec((CONV_W, W), full), pl.BlockSpec((1, W), full),
        pl.BlockSpec((W, W), full), pl.BlockSpec((1, W), full),
        pl.BlockSpec((W, W), full), pl.BlockSpec((1, W), full),
        pl.BlockSpec((1, W), full),
        pl.BlockSpec((1, 1, W), lambda b, i: (b, 0, 0)),
    ]
    if hb is not None:
        args += [hb, lru]
        specs += [pl.BlockSpec((1, tm, W), lambda b, i: (b, blk(i), 0)),
                  pl.BlockSpec((1, tm, W), lambda b, i: (b, blk(i), 1))]
    return pl.pallas_call(
        functools.partial(_lru_kernel, tm=tm, reverse=reverse, combine=hb is not None),
        grid=(B, nt),
        in_specs=specs,
        out_specs=pl.BlockSpec((1, tm, W), lambda b, i: (b, blk(i), 0)),
        out_shape=jax.ShapeDtypeStruct((B, L, W), F32),
        scratch_shapes=[pltpu.VMEM((tm + 2 * H, W), F32), pltpu.VMEM((tm, W), F32), pltpu.VMEM((tm, W), F32),
                        pltpu.VMEM((H, W), F32)],
        compiler_params=_cparams(("parallel", "arbitrary")),
        name="lru_bwd" if reverse else "lru_fwd",
    )(*args)


def _block_diag(w):
    G, n, _ = w.shape
    out = jnp.zeros((G * n, G * n), w.dtype)
    for g in range(G):
        out = out.at[g * n:(g + 1) * n, g * n:(g + 1) * n].set(w[g])
    return out


def lru_mixer(lru, clru, conv_w, conv_b, w_r, b_r, w_i, b_i, lam, need_ctx, tm):
    B, M, _ = clru.shape
    W = LRU_W
    cb = conv_b.reshape(1, W)
    ncs = -LRU_C * jax.nn.softplus(-lam.astype(F32))
    p = [(_block_diag(w_r[d]).astype(BF16), b_r[d].reshape(1, W), _block_diag(w_i[d]).astype(BF16),
          b_i[d].reshape(1, W), ncs[d].reshape(1, W)) for d in range(2)]
    zeros = jnp.zeros((B, 1, W), F32)
    hcb = lru_direction(clru, conv_w, cb, *p[1], zeros, M, True)
    hb = lru_direction(lru, conv_w, cb, *p[1], hcb[:, 0:1], tm, True)
    hcf = lru_direction(clru, conv_w, cb, *p[0], zeros, M, False)
    out = lru_direction(lru, conv_w, cb, *p[0], hcf[:, M - 1:M], tm, False, hb=hb)
    out_c = None
    if need_ctx:
        out_c = lru_direction(clru, conv_w, cb, *p[0], zeros, M, False, hb=hcb)
    return out, out_c


NAT_QROWS = 8
NAT_KROWS = 2 * NAT_QROWS
NAT_TQ = NAT_QROWS * GRID_W
NAT_TK = NAT_KROWS * GRID_W


def _softmax_pv(s_loc, s_ctx, v_loc, v_ctx):
    m = jnp.maximum(jnp.max(s_loc, axis=-1, keepdims=True), jnp.max(s_ctx, axis=-1, keepdims=True))
    e_loc = jnp.exp(s_loc - m)
    e_ctx = jnp.exp(s_ctx - m)
    den = jnp.sum(e_loc, axis=-1, keepdims=True) + jnp.sum(e_ctx, axis=-1, keepdims=True)
    o = _dot(e_loc.astype(BF16), v_loc) + _dot(e_ctx.astype(BF16), v_ctx)
    return o / den


def _nat_kernel(q_ref, kp_ref, kc_ref, kn_ref, vp_ref, vc_ref, vn_ref, ck_ref, cv_ref, bias_ref, o_ref, kcat, vcat):
    j = pl.program_id(1)
    last = pl.num_programs(1) - 1
    T = NAT_TQ
    kcat[0:T] = kp_ref[0]
    kcat[T:2 * T] = kc_ref[0]
    kcat[2 * T:3 * T] = kn_ref[0]
    vcat[0:T] = vp_ref[0]
    vcat[T:2 * T] = vc_ref[0]
    vcat[2 * T:3 * T] = vn_ref[0]
    off = jnp.where(j == 0, T, jnp.where(j == last, 0, T // 2))
    off = pl.multiple_of(off, T // 2)
    kw = kcat[pl.ds(off, NAT_TK), :]
    vw = vcat[pl.ds(off, NAT_TK), :]
    q = q_ref[0]
    ck = ck_ref[0]
    cv = cv_ref[0]
    scale = NAT_DH ** -0.5
    outs = []
    for h in range(NAT_HEADS):
        hs = slice(h * NAT_DH, (h + 1) * NAT_DH)
        qh = q[:, hs]
        s_loc = lax.dot_general(qh, kw[:, hs], NT_DIMS, preferred_element_type=F32) * scale + bias_ref[0, h]
        s_ctx = lax.dot_general(qh, ck[:, hs], NT_DIMS, preferred_element_type=F32) * scale
        outs.append(_softmax_pv(s_loc, s_ctx, vw[:, hs], cv[:, hs]))
    o_ref[0] = jnp.concatenate(outs, axis=-1)


def nat_bias_tables(rpb, rows):
    kr = NAT_KR_MAX
    qc = np.arange(GRID_W)
    kcol = np.arange(GRID_W)
    cs = np.clip(qc - NAT_KC // 2, 0, GRID_W - NAT_KC)
    valid_c = (kcol[None, :] >= cs[:, None]) & (kcol[None, :] < cs[:, None] + NAT_KC)
    edge = GRID_W - NAT_KC
    padded = jnp.pad(rpb.astype(F32), ((0, 0), (0, 0), (edge, edge)), mode="edge")
    by_col = jnp.stack([padded[:, :, GRID_W - 1 - q:2 * GRID_W - 1 - q] for q in range(GRID_W)], axis=2)
    by_col = jnp.pad(by_col, ((0, 0), (NAT_KROWS, NAT_KROWS), (0, 0), (0, 0)))
    tables = []
    for r0, ks in ((0, 0), (NAT_QROWS, NAT_QROWS - kr // 2), (rows - NAT_QROWS, rows - NAT_KROWS)):
        r = r0 + np.arange(NAT_QROWS)
        krow = ks + np.arange(NAT_KROWS)
        rs = np.clip(r - kr // 2, 0, rows - kr)
        valid_r = (krow[None, :] >= rs[:, None]) & (krow[None, :] < rs[:, None] + kr)
        starts = ks - r + NAT_KR_MAX - 1 + NAT_KROWS
        bias = jnp.stack([by_col[:, int(s):int(s) + NAT_KROWS] for s in starts], axis=1)
        bias = bias.transpose(0, 1, 3, 2, 4)
        mask = valid_r[:, None, :, None] & valid_c[None, :, None, :]
        bias = jnp.where(jnp.asarray(mask)[None], bias, NEG_BIG)
        tables.append(bias.reshape(rpb.shape[0], NAT_TQ, NAT_TK))
    return jnp.stack(tables)


def nat_mixer(nat, cnat, bias):
    B, N, _ = nat.shape
    M = cnat.shape[1]
    W = NAT_W
    nb = N // NAT_TQ
    T = NAT_TQ
    cur = lambda c: (lambda b, j: (b, j, c))
    prv = lambda c: (lambda b, j: (b, jnp.maximum(j - 1, 0), c))
    nxt = lambda c: (lambda b, j: (b, jnp.minimum(j + 1, nb - 1), c))
    variant = lambda b, j: (jnp.where(j == 0, 0, jnp.where(j == nb - 1, 2, 1)), 0, 0, 0)
    return pl.pallas_call(
        _nat_kernel,
        grid=(B, nb),
        in_specs=[
            pl.BlockSpec((1, T, W), cur(0)),
            pl.BlockSpec((1, T, W), prv(1)), pl.BlockSpec((1, T, W), cur(1)), pl.BlockSpec((1, T, W), nxt(1)),
            pl.BlockSpec((1, T, W), prv(2)), pl.BlockSpec((1, T, W), cur(2)), pl.BlockSpec((1, T, W), nxt(2)),
            pl.BlockSpec((1, M, W), lambda b, j: (b, 0, 1)),
            pl.BlockSpec((1, M, W), lambda b, j: (b, 0, 2)),
            pl.BlockSpec((1, NAT_HEADS, NAT_TQ, NAT_TK), variant),
        ],
        out_specs=pl.BlockSpec((1, T, W), cur(0)),
        out_shape=jax.ShapeDtypeStruct((B, N, W), F32),
        scratch_shapes=[pltpu.VMEM((3 * T, W), BF16), pltpu.VMEM((3 * T, W), BF16)],
        compiler_params=_cparams(("parallel", "arbitrary")),
        name="nat",
    )(nat, nat, nat, nat, nat, nat, nat, cnat, cnat, bias)


def _ctx_attn_kernel(q_ref, k_ref, v_ref, o_ref):
    q = q_ref[0]
    k = k_ref[0]
    v = v_ref[0]
    scale = NAT_DH ** -0.5
    outs = []
    for h in range(NAT_HEADS):
        hs = slice(h * NAT_DH, (h + 1) * NAT_DH)
        s = lax.dot_general(q[:, hs], k[:, hs], NT_DIMS, preferred_element_type=F32) * scale
        e = jnp.exp(s - jnp.max(s, axis=-1, keepdims=True))
        outs.append(_dot(e.astype(BF16), v[:, hs]) / jnp.sum(e, axis=-1, keepdims=True))
    o_ref[0] = jnp.concatenate(outs, axis=-1)


def ctx_attention(cnat):
    B, M, _ = cnat.shape
    W = NAT_W
    return pl.pallas_call(
        _ctx_attn_kernel,
        grid=(B,),
        in_specs=[pl.BlockSpec((1, M, W), lambda b, c=c: (b, 0, c)) for c in range(3)],
        out_specs=pl.BlockSpec((1, M, W), lambda b: (b, 0, 0)),
        out_shape=jax.ShapeDtypeStruct((B, M, W), F32),
        compiler_params=_cparams(("parallel",)),
        name="ctx_attn",
    )(cnat, cnat, cnat)


def _outproj_kernel(m0_ref, m1_ref, m2_ref, m3_ref, w_ref, x_ref, g1_ref, n2_ref, sc_ref, sh_ref, wr_ref,
                    xo_ref, h2_ref, aff_ref):
    y = 0.0
    for gi, m_ref in enumerate((m0_ref, m1_ref, m2_ref, m3_ref)):
        y = y + _dot(m_ref[0].astype(BF16), w_ref[gi * GROUP_W:(gi + 1) * GROUP_W, :])
    x = x_ref[0] + g1_ref[0] * y
    xo_ref[0] = x
    xn = x * lax.rsqrt(jnp.mean(x * x, axis=-1, keepdims=True) + EPS)
    h2 = (xn * n2_ref[...]) * (1.0 + sc_ref[0]) + sh_ref[0]
    h2_ref[0] = h2
    logits = _dot(h2.astype(BF16), wr_ref[...])
    e = jnp.exp(logits - jnp.max(logits, axis=-1, keepdims=True))
    aff_ref[0] = e / jnp.sum(e, axis=-1, keepdims=True)


def outproj_norm_router(parts, w_out, x, g1, n2, scale2, shift2, w_router, tm):
    B, n, D = x.shape
    E = w_router.shape[1]
    row = lambda b, i: (b, i, 0)
    per_b = lambda b, i: (b, 0, 0)
    full = lambda b, i: (0, 0)
    return pl.pallas_call(
        _outproj_kernel,
        grid=(B, n // tm),
        in_specs=[pl.BlockSpec((1, tm, GROUP_W), row)] * N_GROUPS + [
            pl.BlockSpec((D, D), full),
            pl.BlockSpec((1, tm, D), row),
            pl.BlockSpec((1, 1, D), per_b),
            pl.BlockSpec((1, D), full),
            pl.BlockSpec((1, 1, D), per_b),
            pl.BlockSpec((1, 1, D), per_b),
            pl.BlockSpec((D, E), full),
        ],
        out_specs=[
            pl.BlockSpec((1, tm, D), row),
            pl.BlockSpec((1, tm, D), row),
            pl.BlockSpec((1, tm, E), row),
        ],
        out_shape=[
            jax.ShapeDtypeStruct((B, n, D), F32),
            jax.ShapeDtypeStruct((B, n, D), F32),
            jax.ShapeDtypeStruct((B, n, E), F32),
        ],
        compiler_params=_cparams(("parallel", "parallel")),
        name="outproj_norm_router",
    )(*parts, w_out, x, g1.reshape(B, 1, D), n2.reshape(1, D), scale2.reshape(B, 1, D), shift2.reshape(B, 1, D),
      w_router)


MOE_ROW_CHUNK = 512


def _moe_ffn_kernel(*refs, n_lat, n_ctx, n_batch):
    refs = list(refs)
    idx_ref, h_hbm = refs[:2]
    refs = refs[2:]
    if n_ctx:
        hc_hbm = refs.pop(0)
    wg_ref, wu_ref, wd_ref, gate_ref, g2_ref = refs[:5]
    refs = refs[5:]
    if n_ctx:
        cg2_ref = refs.pop(0)
    yl_ref = refs.pop(0)
    if n_ctx:
        yc_ref = refs.pop(0)
    xf, xb, sem = refs
    e = pl.program_id(0)
    f = pl.program_id(1)
    n_exp = pl.num_programs(0)
    last = pl.num_programs(1) - 1
    T = n_lat + n_ctx

    def row_copy(src, expert, s):
        r = idx_ref[expert * T + s]
        return pltpu.make_async_copy(src.at[pl.ds(r, 1)], xf.at[pl.ds(s, 1)], sem.at[0])

    def gather(expert, wait):
        def lat(s, carry):
            cp = row_copy(h_hbm, expert, s)
            cp.wait() if wait else cp.start()
            return carry

        lax.fori_loop(0, n_lat, lat, 0)
        if n_ctx:
            def ctx(s, carry):
                cp = row_copy(hc_hbm, expert, n_lat + s)
                cp.wait() if wait else cp.start()
                return carry

            lax.fori_loop(0, n_ctx, ctx, 0)

    @pl.when(f == 0)
    def _():
        @pl.when(e == 0)
        def _():
            gather(e, wait=False)

        gather(e, wait=True)
        xb[...] = xf[...].astype(BF16)

        @pl.when(e + 1 < n_exp)
        def _():
            gather(e + 1, wait=False)

    wg = wg_ref[0].astype(BF16)
    wu = wu_ref[0].astype(BF16)
    wd = wd_ref[0].astype(BF16)

    def ffn(x):
        hg = _dot(x, wg)
        hu = _dot(x, wu)
        return _dot((hg * jax.nn.sigmoid(hg) * hu).astype(BF16), wd)

    def accumulate(o_ref, rows, y):
        @pl.when(f == 0)
        def _():
            o_ref[0, rows, :] = y

        @pl.when(f > 0)
        def _():
            o_ref[0, rows, :] += y

    def chunk(i, carry):
        rows = pl.ds(pl.multiple_of(i * MOE_ROW_CHUNK, MOE_ROW_CHUNK), MOE_ROW_CHUNK)
        accumulate(yl_ref, rows, ffn(xb[rows, :]))
        return carry

    lax.fori_loop(0, n_lat // MOE_ROW_CHUNK, chunk, 0)
    if n_ctx:
        accumulate(yc_ref, slice(None), ffn(xb[n_lat:T, :]))

    @pl.when(f == last)
    def _():
        per_l = n_lat // n_batch
        for b in range(n_batch):
            rows = slice(b * per_l, (b + 1) * per_l)
            yl_ref[0, rows, :] = yl_ref[0, rows, :] * gate_ref[0, rows, :] * g2_ref[b:b + 1, :]
        if n_ctx:
            per_c = n_ctx // n_batch
            for b in range(n_batch):
                rows = slice(b * per_c, (b + 1) * per_c)
                yc_ref[0, rows, :] = (yc_ref[0, rows, :] * gate_ref[0, n_lat + b * per_c:n_lat + (b + 1) * per_c, :]
                                      * cg2_ref[b:b + 1, :])


def moe_ffn(idx, h2, hc2, w_gate, w_up, w_down, gates, g2, cg2, n_lat, tf):
    E, T, _ = gates.shape
    n_ctx = T - n_lat
    D = h2.shape[1]
    F = w_gate.shape[2]
    B = g2.shape[0]
    any_spec = pl.BlockSpec(memory_space=pl.ANY)
    per_e = lambda e, f, idx_ref: (e, 0, 0)
    full = lambda e, f, idx_ref: (0, 0)
    args = [h2] + ([hc2] if n_ctx else []) + [w_gate, w_up, w_down, gates, g2] + ([cg2] if n_ctx else [])
    specs = [any_spec] + ([any_spec] if n_ctx else []) + [
        pl.BlockSpec((1, D, tf), lambda e, f, idx_ref: (e, 0, f)),
        pl.BlockSpec((1, D, tf), lambda e, f, idx_ref: (e, 0, f)),
        pl.BlockSpec((1, tf, D), lambda e, f, idx_ref: (e, f, 0)),
        pl.BlockSpec((1, T, 1), per_e),
        pl.BlockSpec((B, D), full),
    ] + ([pl.BlockSpec((B, D), full)] if n_ctx else [])
    out_specs = [pl.BlockSpec((1, n_lat, D), per_e)] + ([pl.BlockSpec((1, n_ctx, D), per_e)] if n_ctx else [])
    out_shape = [jax.ShapeDtypeStruct((E, n_lat, D), F32)] + (
        [jax.ShapeDtypeStruct((E, n_ctx, D), F32)] if n_ctx else [])
    return pl.pallas_call(
        functools.partial(_moe_ffn_kernel, n_lat=n_lat, n_ctx=n_ctx, n_batch=B),
        grid_spec=pltpu.PrefetchScalarGridSpec(
            num_scalar_prefetch=1,
            grid=(E, F // tf),
            in_specs=specs,
            out_specs=out_specs,
            scratch_shapes=[pltpu.VMEM((T, D), F32), pltpu.VMEM((T, D), BF16), pltpu.SemaphoreType.DMA((1,))],
        ),
        out_shape=out_shape,
        compiler_params=_cparams(("arbitrary", "arbitrary")),
        name="moe_ffn",
    )(idx, *args)


def _final_norm_kernel(x_ref, g_ref, o_ref):
    x = x_ref[0]
    o_ref[0] = x * lax.rsqrt(jnp.mean(x * x, axis=-1, keepdims=True) + EPS) * g_ref[...]


def final_norm(x, g, tm):
    B, n, D = x.shape
    return pl.pallas_call(
        _final_norm_kernel,
        grid=(B, n // tm),
        in_specs=[pl.BlockSpec((1, tm, D), lambda b, i: (b, i, 0)), pl.BlockSpec((1, D), lambda b, i: (0, 0))],
        out_specs=pl.BlockSpec((1, tm, D), lambda b, i: (b, i, 0)),
        out_shape=jax.ShapeDtypeStruct((B, n, D), F32),
        compiler_params=_cparams(("parallel", "parallel")),
        name="final_norm",
    )(x, g.reshape(1, D))


def route(aff, cap):
    return lax.top_k(jnp.swapaxes(aff, 1, 2), cap)


TOKEN_TILE = 512


def kernel(x, c, ctx, c_ctx, w_ada, b_ada, norm1_g, norm2_g, w_in, w_out, gla_w_gate, gla_b_gate, gla_norm_g,
           gmlp_norm_g, gmlp_w_s, gmlp_b_s, lru_conv_w, lru_conv_b, lru_w_r, lru_b_r, lru_w_i, lru_b_i,
           lru_lambda, nat_rpb, moe_w_router, moe_w_gate, moe_w_up, moe_w_down, final_norm_g):
    B, N, D = x.shape
    M = ctx.shape[1]
    E = N_EXPERTS
    tm = TOKEN_TILE
    cap_l = max(1, EC_CAPACITY * N // E)
    cap_c = max(1, EC_CAPACITY * M // E)
    sc = jax.nn.silu(c)
    scc = jax.nn.silu(c_ctx)
    xc = ctx
    batch_ids = jnp.arange(B, dtype=jnp.int32)[:, None, None]
    tables = rope_tables(N)
    for l in range(DEPTH):
        need_ctx = l < DEPTH - 1
        mod = sc @ w_ada[l] + b_ada[l]
        mod_c = jnp.broadcast_to((scc @ w_ada[l] + b_ada[l])[None], (B, 6 * D))
        sh1, sc1, g1, sh2, sc2, g2 = jnp.split(mod, 6, axis=-1)
        csh1, csc1, cg1, csh2, csc2, cg2 = jnp.split(mod_c, 6, axis=-1)

        wp = permute_w_in(w_in[l])
        gla, gm, lru, nat, gate = norm_inproj(x, norm1_g[l], sc1, sh1, wp, tm)
        cgla, cgm, clru, cnat, cgate = norm_inproj(xc, norm1_g[l], csc1, csh1, wp, M)

        gla_o, gla_c = gla_mixer(gla, gate, cgla, cgate, gla_w_gate[l], gla_b_gate[l], gla_norm_g[l], tables,
                                 need_ctx, tm)
        gm_o = gmlp_mixer(gm, gmlp_norm_g[l], gmlp_w_s[l], gmlp_b_s[l], tm)
        lru_o, lru_c = lru_mixer(lru, clru, lru_conv_w[l], lru_conv_b[l], lru_w_r[l], lru_b_r[l], lru_w_i[l],
                                 lru_b_i[l], lru_lambda[l], need_ctx, tm)
        nat_o = nat_mixer(nat, cnat, nat_bias_tables(nat_rpb[l], N // GRID_W))

        wo = w_out[l].astype(BF16)
        wr = moe_w_router[l].astype(BF16)
        x, h2, aff = outproj_norm_router((gla_o, gm_o, lru_o, nat_o), wo, x, g1, norm2_g[l], sc2, sh2, wr, tm)
        gl, il = route(aff, cap_l)
        rows_l = jnp.swapaxes(il + batch_ids * N, 0, 1).reshape(E, B * cap_l)
        idx = rows_l
        gs = jnp.swapaxes(gl, 0, 1).reshape(E, B * cap_l)
        hc2 = None
        if need_ctx:
            gm_c = gmlp_mixer(cgm, gmlp_norm_g[l], gmlp_w_s[l], gmlp_b_s[l], M)
            nat_c = ctx_attention(cnat)
            xc, hc2, caff = outproj_norm_router((gla_c, gm_c, lru_c, nat_c), wo, xc, cg1, norm2_g[l], csc2, csh2,
                                                wr, M)
            gc, ic = route(caff, cap_c)
            rows_c = jnp.swapaxes(ic + batch_ids * M, 0, 1).reshape(E, B * cap_c)
            idx = jnp.concatenate([rows_l, rows_c], axis=1)
            gs = jnp.concatenate([gs, jnp.swapaxes(gc, 0, 1).reshape(E, B * cap_c)], axis=1)
            hc2 = hc2.reshape(B * M, D)
        ys = moe_ffn(idx.reshape(-1).astype(jnp.int32), h2.reshape(B * N, D), hc2, moe_w_gate[l], moe_w_up[l],
                     moe_w_down[l], gs[..., None], g2, cg2, B * cap_l, tf=512)
        x = x.reshape(B * N, D).at[rows_l.reshape(-1)].add(ys[0].reshape(-1, D)).reshape(B, N, D)
        if need_ctx:
            xc = xc.reshape(B * M, D).at[rows_c.reshape(-1)].add(ys[1].reshape(-1, D)).reshape(B, M, D)
    return final_norm(x, final_norm_g, tm)
```

```python
import functools

import numpy as np
import jax
import jax.numpy as jnp
from jax import lax
from jax.experimental import pallas as pl
from jax.experimental.pallas import tpu as pltpu

D_MODEL = 1024
DEPTH = 4
GRID_W = 64
N_GROUPS = 4
GROUP_W = D_MODEL // N_GROUPS
GLA_HEADS = 4
GLA_DK = GROUP_W // GLA_HEADS
GLA_W = GLA_HEADS * GLA_DK
GLA_GATE_RANK = 16
GLA_TAU = 16.0
GLA_CHUNK = 64
GMLP_GROUPS = 4
GMLP_W = GROUP_W
GMLP_CHUNK = 128
LRU_W = GROUP_W
LRU_BLOCKS = 4
LRU_C = 8.0
CONV_W = 4
NAT_HEADS = 4
NAT_DH = GROUP_W // NAT_HEADS
NAT_W = NAT_HEADS * NAT_DH
NAT_KR_MAX = 8
NAT_KC = 16
N_EXPERTS = 16
EXPERT_FF = 2 * D_MODEL
EC_CAPACITY = 2
ROPE_BASE = 10000.0
EPS = 1e-6

IN_SIZES = (GLA_W, GLA_W, GLA_W, GLA_W, GLA_GATE_RANK, GLA_GATE_RANK,
            GMLP_W, GMLP_W, LRU_W, LRU_W, NAT_W, NAT_W, NAT_W)
IN_COLS = sum(IN_SIZES)

V7X_VMEM_LIMIT_BYTES = 56 * 1024 * 1024
LANES = 128
SUBLANES = 8
F32 = jnp.float32
BF16 = jnp.bfloat16
NEG_BIG = -1e30
GATE_PAD = LANES

NT_DIMS = (((1,), (1,)), ((), ()))
TN_DIMS = (((0,), (0,)), ((), ()))


def _cparams(sem):
    return pltpu.CompilerParams(dimension_semantics=sem, vmem_limit_bytes=V7X_VMEM_LIMIT_BYTES)


def _dot(a, b):
    return jnp.dot(a, b, preferred_element_type=F32)


def _split3(x):
    hi = x.astype(BF16)
    r1 = x - hi.astype(F32)
    mid = r1.astype(BF16)
    lo = (r1 - mid.astype(F32)).astype(BF16)
    return hi, mid, lo


def _head_block_mask(n):
    r = lax.broadcasted_iota(jnp.int32, (n, n), 0) // GLA_DK
    c = lax.broadcasted_iota(jnp.int32, (n, n), 1) // GLA_DK
    return r == c


def _norm_inproj_kernel(x_ref, g_ref, sc_ref, sh_ref, w_ref, gla_ref, gmlp_ref, lru_ref, nat_ref, gate_ref):
    x = x_ref[0]
    y = x * lax.rsqrt(jnp.mean(x * x, axis=-1, keepdims=True) + EPS)
    h = ((y * g_ref[...]) * (1.0 + sc_ref[0]) + sh_ref[0]).astype(BF16)
    o = 0
    for ref in (gla_ref, gmlp_ref, lru_ref, nat_ref, gate_ref):
        w = ref.shape[-1]
        ref[0] = _dot(h, w_ref[:, o:o + w]).astype(ref.dtype)
        o += w


def permute_w_in(w):
    g0 = 4 * GLA_W
    g1 = g0 + 2 * GLA_GATE_RANK
    pad = jnp.zeros((w.shape[0], GATE_PAD - 2 * GLA_GATE_RANK), w.dtype)
    return jnp.concatenate([w[:, :g0], w[:, g1:], w[:, g0:g1], pad], axis=1).astype(BF16)


def norm_inproj(x, g, scale, shift, wp, tm):
    B, n, D = x.shape
    C = wp.shape[1]
    widths = (4 * GLA_W, 2 * GMLP_W, 2 * LRU_W, 3 * NAT_W, GATE_PAD)
    dtypes = (F32, F32, F32, BF16, F32)
    row = lambda b, i: (b, i, 0)
    per_b = lambda b, i: (b, 0, 0)
    full = lambda b, i: (0, 0)
    return pl.pallas_call(
        _norm_inproj_kernel,
        grid=(B, n // tm),
        in_specs=[
            pl.BlockSpec((1, tm, D), row),
            pl.BlockSpec((1, D), full),
            pl.BlockSpec((1, 1, D), per_b),
            pl.BlockSpec((1, 1, D), per_b),
            pl.BlockSpec((D, C), full),
        ],
        out_specs=[pl.BlockSpec((1, tm, w), row) for w in widths],
        out_shape=[jax.ShapeDtypeStruct((B, n, w), dt) for w, dt in zip(widths, dtypes)],
        compiler_params=_cparams(("parallel", "parallel")),
        name="norm_inproj",
    )(x, g.reshape(1, D), scale.reshape(B, 1, D), shift.reshape(B, 1, D), wp)


def _gla_kernel(*refs, tm, reverse, rope, combine):
    refs = list(refs)
    q_ref, k_ref, v_ref, gate_ref, wg_ref, bg_ref, tri_ref, ones_ref, s0_ref = refs[:9]
    refs = refs[9:]
    if rope:
        cos_ref, sin_ref = refs[:2]
        refs = refs[2:]
    if combine:
        ob_ref, og_ref, ng_ref = refs[:3]
        refs = refs[3:]
    o_ref, sfin_ref, st = refs
    i = pl.program_id(1)
    nch = tm // GLA_CHUNK

    @pl.when(i == 0)
    def _():
        st[...] = s0_ref[0]

    q = q_ref[0]
    k = k_ref[0]
    if rope:
        cos = cos_ref[...]
        sin = sin_ref[...]
        first = (lax.broadcasted_iota(jnp.int32, (tm, GLA_W), 1) % (GLA_DK // 2)) < (GLA_DK // 4)

        def rot(u):
            partner = jnp.where(first, pltpu.roll(u, GLA_W - GLA_DK // 4, axis=1), pltpu.roll(u, GLA_DK // 4, axis=1))
            return u * cos + partner * sin

        q = rot(q)
        k = rot(k)
    q = q * (GLA_DK ** -0.5)
    a = _dot(gate_ref[0].astype(BF16), wg_ref[...]) + bg_ref[...]
    la = jax.nn.log_sigmoid(a) / GLA_TAU
    pieces = _split3(la)
    b = sum(_dot(tri_ref[...], p) for p in pieces)
    bl = sum(_dot(ones_ref[...], p) for p in pieces)
    q_in = (q * jnp.exp(b)).astype(BF16)
    k_in = (k * jnp.exp(-b)).astype(BF16)
    k_end = (k * jnp.exp(bl - b)).astype(BF16)
    dec = jnp.exp(bl)
    vb = v_ref[0].astype(BF16)

    head_mask = _head_block_mask(GLA_W)
    cr = lax.broadcasted_iota(jnp.int32, (GLA_CHUNK, GLA_W), 0)
    cs = lax.broadcasted_iota(jnp.int32, (GLA_CHUNK, GLA_W), 1) % GLA_CHUNK
    causal = (cs >= cr) if reverse else (cs <= cr)
    zero = jnp.zeros((), BF16)
    chunks = range(nch - 1, -1, -1) if reverse else range(nch)
    for c in chunks:
        sl = slice(c * GLA_CHUNK, (c + 1) * GLA_CHUNK)
        kbd_t = jnp.where(head_mask, jnp.concatenate([k_in[sl]] * GLA_HEADS, axis=0), zero)
        vbd = jnp.where(head_mask, jnp.concatenate([vb[sl]] * GLA_HEADS, axis=0), zero)
        att = lax.dot_general(q_in[sl], kbd_t, NT_DIMS, preferred_element_type=F32)
        att = jnp.where(causal, att, 0.0).astype(BF16)
        s_t = st[...]
        o = _dot(att, vbd) + lax.dot_general(q_in[sl], s_t.astype(BF16), NT_DIMS, preferred_element_type=F32)
        upd = lax.dot_general(vb[sl], k_end[sl], TN_DIMS, preferred_element_type=F32)
        st[...] = s_t * dec[c * GLA_CHUNK:c * GLA_CHUNK + 1] + jnp.where(head_mask, upd, 0.0)
        o_ref[0, sl, :] = o

    sfin_ref[0] = st[...]

    if combine:
        o = o_ref[0] + ob_ref[0]
        sq_hi, sq_mid, _ = _split3(o * o)
        avg = jnp.where(head_mask, 1.0 / GLA_DK, 0.0).astype(BF16)
        ms = _dot(sq_hi, avg) + _dot(sq_mid, avg)
        og = og_ref[0]
        o_ref[0] = o * lax.rsqrt(ms + EPS) * ng_ref[...] * (og * jax.nn.sigmoid(og))


def _chunk_sum_matrices(tm, reverse):
    r = np.arange(tm)[:, None]
    c = np.arange(tm)[None, :]
    same = (r // GLA_CHUNK) == (c // GLA_CHUNK)
    tri = same & ((c >= r) if reverse else (c <= r))
    return jnp.asarray(tri, BF16), jnp.asarray(same, BF16)


def gla_direction(gla, gate, wg, bg, s0, tm, reverse, tables=None, combine=None):
    B, L, _ = gla.shape
    nt = L // tm
    W = GLA_W
    blk = (lambda i: nt - 1 - i) if reverse else (lambda i: i)
    col = lambda j: (lambda b, i: (b, blk(i), j))
    full = lambda b, i: (0, 0)
    tri, ones = _chunk_sum_matrices(tm, reverse)
    args = [gla, gla, gla, gate, wg, bg, tri, ones, s0]
    specs = [
        pl.BlockSpec((1, tm, W), col(0)), pl.BlockSpec((1, tm, W), col(1)), pl.BlockSpec((1, tm, W), col(2)),
        pl.BlockSpec((1, tm, GATE_PAD), col(0)),
        pl.BlockSpec((GATE_PAD, W), full), pl.BlockSpec((1, W), full),
        pl.BlockSpec((tm, tm), full), pl.BlockSpec((tm, tm), full),
        pl.BlockSpec((1, W, W), lambda b, i: (b, 0, 0)),
    ]
    if tables is not None:
        args += list(tables)
        specs += [pl.BlockSpec((tm, W), lambda b, i: (blk(i), 0))] * 2
    if combine is not None:
        ob, norm_g = combine
        args += [ob, gla, norm_g]
        specs += [pl.BlockSpec((1, tm, W), col(0)), pl.BlockSpec((1, tm, W), col(3)), pl.BlockSpec((1, W), full)]
    return pl.pallas_call(
        functools.partial(_gla_kernel, tm=tm, reverse=reverse, rope=tables is not None, combine=combine is not None),
        grid=(B, nt),
        in_specs=specs,
        out_specs=[pl.BlockSpec((1, tm, W), col(0)), pl.BlockSpec((1, W, W), lambda b, i: (b, 0, 0))],
        out_shape=[jax.ShapeDtypeStruct((B, L, W), F32), jax.ShapeDtypeStruct((B, W, W), F32)],
        scratch_shapes=[pltpu.VMEM((W, W), F32)],
        compiler_params=_cparams(("parallel", "arbitrary")),
        name="gla_bwd" if reverse else "gla_fwd",
    )(*args)


def rope_tables(n):
    quarter = GLA_DK // 4
    pos = jnp.arange(n)
    inv = ROPE_BASE ** (-jnp.arange(quarter, dtype=F32) / quarter)
    row = (pos // GRID_W).astype(F32)
    colp = (pos % GRID_W).astype(F32)
    ar = row[:, None] * inv[None, :]
    ac = colp[:, None] * inv[None, :]
    cos = jnp.concatenate([jnp.cos(ar), jnp.cos(ar), jnp.cos(ac), jnp.cos(ac)], axis=1)
    sin = jnp.concatenate([-jnp.sin(ar), jnp.sin(ar), -jnp.sin(ac), jnp.sin(ac)], axis=1)
    return jnp.tile(cos, (1, GLA_HEADS)), jnp.tile(sin, (1, GLA_HEADS))


def gla_mixer(gla, gate, cgla, cgate, w_gate, b_gate, norm_g, tables, need_ctx, tm):
    B, M, _ = cgla.shape
    zeros = jnp.zeros((B, GLA_W, GLA_W), F32)
    ng = jnp.tile(norm_g, GLA_HEADS).reshape(1, GLA_W)
    wgs, bgs = [], []
    for d in range(2):
        wg = jnp.zeros((GATE_PAD, GLA_W), F32).at[d * GLA_GATE_RANK:(d + 1) * GLA_GATE_RANK].set(w_gate[d])
        wgs.append(wg.astype(BF16))
        bgs.append(b_gate[d].reshape(1, GLA_W))
    ocb, sb = gla_direction(cgla, cgate, wgs[1], bgs[1], zeros, M, True)
    ob, _ = gla_direction(gla, gate, wgs[1], bgs[1], sb, tm, True, tables=tables)
    if need_ctx:
        oc, sf = gla_direction(cgla, cgate, wgs[0], bgs[0], zeros, M, False, combine=(ocb, ng))
    else:
        oc, sf = gla_direction(cgla, cgate, wgs[0], bgs[0], zeros, M, False)
    o, _ = gla_direction(gla, gate, wgs[0], bgs[0], sf, tm, False, tables=tables, combine=(ob, ng))
    return o, oc


def _gmlp_kernel(u_ref, v_ref, g_ref, w_ref, b_ref, o_ref, *, tm):
    v = v_ref[0]
    vn = (v * lax.rsqrt(jnp.mean(v * v, axis=-1, keepdims=True) + EPS) * g_ref[...]).astype(BF16)
    grp = lax.broadcasted_iota(jnp.int32, (GMLP_CHUNK, GMLP_W), 1) // (GMLP_W // GMLP_GROUPS)
    zero = jnp.zeros((), BF16)
    for c in range(tm // GMLP_CHUNK):
        sl = slice(c * GMLP_CHUNK, (c + 1) * GMLP_CHUNK)
        mixed = b_ref[...]
        for g in range(GMLP_GROUPS):
            mixed = mixed + _dot(w_ref[g], jnp.where(grp == g, vn[sl], zero))
        o_ref[0, sl, :] = u_ref[0, sl, :] * mixed


def gmlp_mixer(gm, norm_g, w_s, b_s, tm):
    B, L, _ = gm.shape
    W = GMLP_W
    bias = jnp.repeat(b_s.T, W // GMLP_GROUPS, axis=1)
    return pl.pallas_call(
        functools.partial(_gmlp_kernel, tm=tm),
        grid=(B, L // tm),
        in_specs=[
            pl.BlockSpec((1, tm, W), lambda b, i: (b, i, 0)),
            pl.BlockSpec((1, tm, W), lambda b, i: (b, i, 1)),
            pl.BlockSpec((1, W), lambda b, i: (0, 0)),
            pl.BlockSpec((GMLP_GROUPS, GMLP_CHUNK, GMLP_CHUNK), lambda b, i: (0, 0, 0)),
            pl.BlockSpec((GMLP_CHUNK, W), lambda b, i: (0, 0)),
        ],
        out_specs=pl.BlockSpec((1, tm, W), lambda b, i: (b, i, 0)),
        out_shape=jax.ShapeDtypeStruct((B, L, W), F32),
        compiler_params=_cparams(("parallel", "parallel")),
        name="gmlp",
    )(gm, gm, norm_g.reshape(1, W), w_s.astype(BF16), bias)


LRU_UNROLL = 8


def _lru_kernel(*refs, tm, reverse, combine):
    refs = list(refs)
    x_ref, xp_ref, xn_ref, cw_ref, cb_ref, wr_ref, br_ref, wi_ref, bi_ref, ncs_ref, h0_ref = refs[:11]
    refs = refs[11:]
    if combine:
        hb_ref, ly_ref = refs[:2]
        refs = refs[2:]
    o_ref, ext, a_s, b_s, carry = refs
    W = LRU_W
    i = pl.program_id(1)
    nt = pl.num_programs(1)
    t = (nt - 1 - i) if reverse else i
    H = SUBLANES

    ext[0:H] = jnp.where(t > 0, xp_ref[0], 0.0)
    ext[H:H + tm] = x_ref[0]
    ext[H + tm:2 * H + tm] = jnp.where(t < nt - 1, xn_ref[0], 0.0)
    xc = cb_ref[...]
    for tap in range(CONV_W):
        xc = xc + cw_ref[tap:tap + 1, :] * ext[H - CONV_W // 2 + tap:H - CONV_W // 2 + tap + tm]
    xcb = xc.astype(BF16)
    r = jax.nn.sigmoid(_dot(xcb, wr_ref[...]) + br_ref[...])
    ig = jax.nn.sigmoid(_dot(xcb, wi_ref[...]) + bi_ref[...])
    log_a = ncs_ref[...] * r
    a = jnp.exp(log_a)
    b = jnp.sqrt(-jnp.tanh(log_a) * (a * a + 1.0)) * (ig * xc)

    rowi = lax.broadcasted_iota(jnp.int32, (tm, W), 0) % H
    for s in (1, 2, 4):
        if reverse:
            ok = rowi < H - s
            a_sh = pltpu.roll(a, tm - s, axis=0)
            b_sh = pltpu.roll(b, tm - s, axis=0)
        else:
            ok = rowi >= s
            a_sh = pltpu.roll(a, s, axis=0)
            b_sh = pltpu.roll(b, s, axis=0)
        b = a * jnp.where(ok, b_sh, 0.0) + b
        a = a * jnp.where(ok, a_sh, 1.0)
    a_s[...] = a
    b_s[...] = b

    @pl.when(i == 0)
    def _():
        carry[...] = jnp.broadcast_to(h0_ref[0], (H, W))

    ng = tm // H

    def body(j, h):
        for u in range(LRU_UNROLL):
            g = j * LRU_UNROLL + u
            g = (ng - 1 - g) if reverse else g
            rows = pl.ds(pl.multiple_of(g * H, H), H)
            hg = b_s[rows, :] + a_s[rows, :] * h
            o_ref[0, rows, :] = hg
            h = jnp.broadcast_to(hg[0:1] if reverse else hg[H - 1:H], (H, W))
        return h

    carry[...] = lax.fori_loop(0, ng // LRU_UNROLL, body, carry[...])

    if combine:
        o_ref[0] = (o_ref[0] + hb_ref[0]) * jax.nn.gelu(ly_ref[0])


def lru_direction(lru, conv_w, conv_b, wr, br, wi, bi, ncs, h0, tm, reverse, hb=None):
    B, L, _ = lru.shape
    nt = L // tm
    W = LRU_W
    H = SUBLANES
    nh = L // H
    per = tm // H
    blk = (lambda i: nt - 1 - i) if reverse else (lambda i: i)
    full = lambda b, i: (0, 0)
    args = [lru, lru, lru, conv_w, conv_b, wr, br, wi, bi, ncs, h0]
    specs = [
        pl.BlockSpec((1, tm, W), lambda b, i: (b, blk(i), 0)),
        pl.BlockSpec((1, H, W), lambda b, i: (b, jnp.maximum(blk(i) * per - 1, 0), 0)),
        pl.BlockSpec((1, H, W), lambda b, i: (b, jnp.minimum((blk(i) + 1) * per, nh - 1), 0)),
        pl.BlockSpec((CONV_W, W), full), pl.BlockSpec((1, W), full),
        pl.BlockSpec((W, W), full), pl.BlockSpec((1, W), full),
        pl.BlockSpec((W, W), full), pl.BlockSpec((1, W), full),
        pl.BlockSpec((1, W), full),
        pl.BlockSpec((1, 1, W), lambda b, i: (b, 0, 0)),
    ]
    if hb is not None:
        args += [hb, lru]
        specs += [pl.BlockSpec((1, tm, W), lambda b, i: (b, blk(i), 0)),
                  pl.BlockSpec((1, tm, W), lambda b, i: (b, blk(i), 1))]
    return pl.pallas_call(
        functools.partial(_lru_kernel, tm=tm, reverse=reverse, combine=hb is not None),
        grid=(B, nt),
        in_specs=specs,
        out_specs=pl.BlockSpec((1, tm, W), lambda b, i: (b, blk(i), 0)),
        out_shape=jax.ShapeDtypeStruct((B, L, W), F32),
        scratch_shapes=[pltpu.VMEM((tm + 2 * H, W), F32), pltpu.VMEM((tm, W), F32), pltpu.VMEM((tm, W), F32),
                        pltpu.VMEM((H, W), F32)],
        compiler_params=_cparams(("parallel", "arbitrary")),
        name="lru_bwd" if reverse else "lru_fwd",
    )(*args)


def _block_diag(w):
    G, n, _ = w.shape
    out = jnp.zeros((G * n, G * n), w.dtype)
    for g in range(G):
        out = out.at[g * n:(g + 1) * n, g * n:(g + 1) * n].set(w[g])
    return out


def lru_mixer(lru, clru, conv_w, conv_b, w_r, b_r, w_i, b_i, lam, need_ctx, tm):
    B, M, _ = clru.shape
    W = LRU_W
    cb = conv_b.reshape(1, W)
    ncs = -LRU_C * jax.nn.softplus(-lam.astype(F32))
    p = [(_block_diag(w_r[d]).astype(BF16), b_r[d].reshape(1, W), _block_diag(w_i[d]).astype(BF16),
          b_i[d].reshape(1, W), ncs[d].reshape(1, W)) for d in range(2)]
    zeros = jnp.zeros((B, 1, W), F32)
    hcb = lru_direction(clru, conv_w, cb, *p[1], zeros, M, True)
    hb = lru_direction(lru, conv_w, cb, *p[1], hcb[:, 0:1], tm, True)
    hcf = lru_direction(clru, conv_w, cb, *p[0], zeros, M, False)
    out = lru_direction(lru, conv_w, cb, *p[0], hcf[:, M - 1:M], tm, False, hb=hb)
    out_c = None
    if need_ctx:
        out_c = lru_direction(clru, conv_w, cb, *p[0], zeros, M, False, hb=hcb)
    return out, out_c


NAT_QROWS = 8
NAT_KROWS = 2 * NAT_QROWS
NAT_TQ = NAT_QROWS * GRID_W
NAT_TK = NAT_KROWS * GRID_W


def _softmax_pv(s_loc, s_ctx, v_loc, v_ctx):
    m = jnp.maximum(jnp.max(s_loc, axis=-1, keepdims=True), jnp.max(s_ctx, axis=-1, keepdims=True))
    e_loc = jnp.exp(s_loc - m)
    e_ctx = jnp.exp(s_ctx - m)
    den = jnp.sum(e_loc, axis=-1, keepdims=True) + jnp.sum(e_ctx, axis=-1, keepdims=True)
    o = _dot(e_loc.astype(BF16), v_loc) + _dot(e_ctx.astype(BF16), v_ctx)
    return o / den


def _nat_kernel(q_ref, kp_ref, kc_ref, kn_ref, vp_ref, vc_ref, vn_ref, ck_ref, cv_ref, bias_ref, o_ref, kcat, vcat):
    j = pl.program_id(1)
    last = pl.num_programs(1) - 1
    T = NAT_TQ
    kcat[0:T] = kp_ref[0]
    kcat[T:2 * T] = kc_ref[0]
    kcat[2 * T:3 * T] = kn_ref[0]
    vcat[0:T] = vp_ref[0]
    vcat[T:2 * T] = vc_ref[0]
    vcat[2 * T:3 * T] = vn_ref[0]
    off = jnp.where(j == 0, T, jnp.where(j == last, 0, T // 2))
    off = pl.multiple_of(off, T // 2)
    kw = kcat[pl.ds(off, NAT_TK), :]
    vw = vcat[pl.ds(off, NAT_TK), :]
    q = q_ref[0]
    ck = ck_ref[0]
    cv = cv_ref[0]
    scale = NAT_DH ** -0.5
    outs = []
    for h in range(NAT_HEADS):
        hs = slice(h * NAT_DH, (h + 1) * NAT_DH)
        qh = q[:, hs]
        s_loc = lax.dot_general(qh, kw[:, hs], NT_DIMS, preferred_element_type=F32) * scale + bias_ref[0, h]
        s_ctx = lax.dot_general(qh, ck[:, hs], NT_DIMS, preferred_element_type=F32) * scale
        outs.append(_softmax_pv(s_loc, s_ctx, vw[:, hs], cv[:, hs]))
    o_ref[0] = jnp.concatenate(outs, axis=-1)


def nat_bias_tables(rpb, rows):
    kr = NAT_KR_MAX
    qc = np.arange(GRID_W)
    kcol = np.arange(GRID_W)
    cs = np.clip(qc - NAT_KC // 2, 0, GRID_W - NAT_KC)
    valid_c = (kcol[None, :] >= cs[:, None]) & (kcol[None, :] < cs[:, None] + NAT_KC)
    edge = GRID_W - NAT_KC
    padded = jnp.pad(rpb.astype(F32), ((0, 0), (0, 0), (edge, edge)), mode="edge")
    by_col = jnp.stack([padded[:, :, GRID_W - 1 - q:2 * GRID_W - 1 - q] for q in range(GRID_W)], axis=2)
    by_col = jnp.pad(by_col, ((0, 0), (NAT_KROWS, NAT_KROWS), (0, 0), (0, 0)))
    tables = []
    for r0, ks in ((0, 0), (NAT_QROWS, NAT_QROWS - kr // 2), (rows - NAT_QROWS, rows - NAT_KROWS)):
        r = r0 + np.arange(NAT_QROWS)
        krow = ks + np.arange(NAT_KROWS)
        rs = np.clip(r - kr // 2, 0, rows - kr)
        valid_r = (krow[None, :] >= rs[:, None]) & (krow[None, :] < rs[:, None] + kr)
        starts = ks - r + NAT_KR_MAX - 1 + NAT_KROWS
        bias = jnp.stack([by_col[:, int(s):int(s) + NAT_KROWS] for s in starts], axis=1)
        bias = bias.transpose(0, 1, 3, 2, 4)
        mask = valid_r[:, None, :, None] & valid_c[None, :, None, :]
        bias = jnp.where(jnp.asarray(mask)[None], bias, NEG_BIG)
        tables.append(bias.reshape(rpb.shape[0], NAT_TQ, NAT_TK))
    return jnp.stack(tables)


def nat_mixer(nat, cnat, bias):
    B, N, _ = nat.shape
    M = cnat.shape[1]
    W = NAT_W
    nb = N // NAT_TQ
    T = NAT_TQ
    cur = lambda c: (lambda b, j: (b, j, c))
    prv = lambda c: (lambda b, j: (b, jnp.maximum(j - 1, 0), c))
    nxt = lambda c: (lambda b, j: (b, jnp.minimum(j + 1, nb - 1), c))
    variant = lambda b, j: (jnp.where(j == 0, 0, jnp.where(j == nb - 1, 2, 1)), 0, 0, 0)
    return pl.pallas_call(
        _nat_kernel,
        grid=(B, nb),
        in_specs=[
            pl.BlockSpec((1, T, W), cur(0)),
            pl.BlockSpec((1, T, W), prv(1)), pl.BlockSpec((1, T, W), cur(1)), pl.BlockSpec((1, T, W), nxt(1)),
            pl.BlockSpec((1, T, W), prv(2)), pl.BlockSpec((1, T, W), cur(2)), pl.BlockSpec((1, T, W), nxt(2)),
            pl.BlockSpec((1, M, W), lambda b, j: (b, 0, 1)),
            pl.BlockSpec((1, M, W), lambda b, j: (b, 0, 2)),
            pl.BlockSpec((1, NAT_HEADS, NAT_TQ, NAT_TK), variant),
        ],
        out_specs=pl.BlockSpec((1, T, W), cur(0)),
        out_shape=jax.ShapeDtypeStruct((B, N, W), F32),
        scratch_shapes=[pltpu.VMEM((3 * T, W), BF16), pltpu.VMEM((3 * T, W), BF16)],
        compiler_params=_cparams(("parallel", "arbitrary")),
        name="nat",
    )(nat, nat, nat, nat, nat, nat, nat, cnat, cnat, bias)


def _ctx_attn_kernel(q_ref, k_ref, v_ref, o_ref):
    q = q_ref[0]
    k = k_ref[0]
    v = v_ref[0]
    scale = NAT_DH ** -0.5
    outs = []
    for h in range(NAT_HEADS):
        hs = slice(h * NAT_DH, (h + 1) * NAT_DH)
        s = lax.dot_general(q[:, hs], k[:, hs], NT_DIMS, preferred_element_type=F32) * scale
        e = jnp.exp(s - jnp.max(s, axis=-1, keepdims=True))
        outs.append(_dot(e.astype(BF16), v[:, hs]) / jnp.sum(e, axis=-1, keepdims=True))
    o_ref[0] = jnp.concatenate(outs, axis=-1)


def ctx_attention(cnat):
    B, M, _ = cnat.shape
    W = NAT_W
    return pl.pallas_call(
        _ctx_attn_kernel,
        grid=(B,),
        in_specs=[pl.BlockSpec((1, M, W), lambda b, c=c: (b, 0, c)) for c in range(3)],
        out_specs=pl.BlockSpec((1, M, W), lambda b: (b, 0, 0)),
        out_shape=jax.ShapeDtypeStruct((B, M, W), F32),
        compiler_params=_cparams(("parallel",)),
        name="ctx_attn",
    )(cnat, cnat, cnat)


def _outproj_kernel(m0_ref, m1_ref, m2_ref, m3_ref, w_ref, x_ref, g1_ref, n2_ref, sc_ref, sh_ref, wr_ref,
                    xo_ref, h2_ref, aff_ref):
    y = 0.0
    for gi, m_ref in enumerate((m0_ref, m1_ref, m2_ref, m3_ref)):
        y = y + _dot(m_ref[0].astype(BF16), w_ref[gi * GROUP_W:(gi + 1) * GROUP_W, :])
    x = x_ref[0] + g1_ref[0] * y
    xo_ref[0] = x
    xn = x * lax.rsqrt(jnp.mean(x * x, axis=-1, keepdims=True) + EPS)
    h2 = (xn * n2_ref[...]) * (1.0 + sc_ref[0]) + sh_ref[0]
    h2_ref[0] = h2
    logits = _dot(h2.astype(BF16), wr_ref[...])
    e = jnp.exp(logits - jnp.max(logits, axis=-1, keepdims=True))
    aff_ref[0] = e / jnp.sum(e, axis=-1, keepdims=True)


def outproj_norm_router(parts, w_out, x, g1, n2, scale2, shift2, w_router, tm):
    B, n, D = x.shape
    E = w_router.shape[1]
    row = lambda b, i: (b, i, 0)
    per_b = lambda b, i: (b, 0, 0)
    full = lambda b, i: (0, 0)
    return pl.pallas_call(
        _outproj_kernel,
        grid=(B, n // tm),
        in_specs=[pl.BlockSpec((1, tm, GROUP_W), row)] * N_GROUPS + [
            pl.BlockSpec((D, D), full),
            pl.BlockSpec((1, tm, D), row),
            pl.BlockSpec((1, 1, D), per_b),
            pl.BlockSpec((1, D), full),
            pl.BlockSpec((1, 1, D), per_b),
            pl.BlockSpec((1, 1, D), per_b),
            pl.BlockSpec((D, E), full),
        ],
        out_specs=[
            pl.BlockSpec((1, tm, D), row),
            pl.BlockSpec((1, tm, D), row),
            pl.BlockSpec((1, tm, E), row),
        ],
        out_shape=[
            jax.ShapeDtypeStruct((B, n, D), F32),
            jax.ShapeDtypeStruct((B, n, D), F32),
            jax.ShapeDtypeStruct((B, n, E), F32),
        ],
        compiler_params=_cparams(("parallel", "parallel")),
        name="outproj_norm_router",
    )(*parts, w_out, x, g1.reshape(B, 1, D), n2.reshape(1, D), scale2.reshape(B, 1, D), shift2.reshape(B, 1, D),
      w_router)


MOE_ROW_CHUNK = 512
MOE_ISSUE_UNROLL = 8


def _moe_ffn_kernel(*refs, n_lat, n_ctx, n_batch):
    refs = list(refs)
    idx_ref, h_hbm = refs[:2]
    refs = refs[2:]
    if n_ctx:
        hc_hbm = refs.pop(0)
    wg_ref, wu_ref, wd_ref, gate_ref, g2_ref = refs[:5]
    refs = refs[5:]
    if n_ctx:
        cg2_ref = refs.pop(0)
    yl_ref = refs.pop(0)
    if n_ctx:
        yc_ref = refs.pop(0)
    xf, xb, sem = refs
    e = pl.program_id(0)
    f = pl.program_id(1)
    n_exp = pl.num_programs(0)
    last = pl.num_programs(1) - 1
    T = n_lat + n_ctx

    R = SUBLANES

    def start_rows(src, expert, first, count):
        def body(i, carry):
            for u in range(MOE_ISSUE_UNROLL):
                s = first + i * MOE_ISSUE_UNROLL + u
                r = pl.multiple_of(idx_ref[expert * T + s], R)
                pltpu.make_async_copy(src.at[pl.ds(r, R)], xf.at[pl.ds(pl.multiple_of(s * R, R), R)],
                                      sem.at[0]).start()
            return carry

        lax.fori_loop(0, count // MOE_ISSUE_UNROLL, body, 0)

    def start_gather(expert):
        start_rows(h_hbm, expert, 0, n_lat)
        if n_ctx:
            start_rows(hc_hbm, expert, n_lat, n_ctx)

    @pl.when(f == 0)
    def _():
        @pl.when(e == 0)
        def _():
            start_gather(e)

        pltpu.make_async_copy(h_hbm.at[pl.ds(0, T * R)], xf, sem.at[0]).wait()
        for j in range(R):
            xb[:, j * LANES:(j + 1) * LANES] = xf[pl.ds(j, T, stride=R), :].astype(BF16)

        @pl.when(e + 1 < n_exp)
        def _():
            start_gather(e + 1)

    wg = wg_ref[0, 0].astype(BF16)
    wu = wu_ref[0, 0].astype(BF16)
    wd = wd_ref[0, 0].astype(BF16)

    def ffn(x):
        hg = _dot(x, wg)
        hu = _dot(x, wu)
        return _dot((hg * jax.nn.sigmoid(hg) * hu).astype(BF16), wd)

    def accumulate(o_ref, rows, y):
        @pl.when(f == 0)
        def _():
            o_ref[0, rows, :] = y

        @pl.when(f > 0)
        def _():
            o_ref[0, rows, :] += y

    def chunk(i, carry):
        rows = pl.ds(pl.multiple_of(i * MOE_ROW_CHUNK, MOE_ROW_CHUNK), MOE_ROW_CHUNK)
        accumulate(yl_ref, rows, ffn(xb[rows, :]))
        return carry

    lax.fori_loop(0, n_lat // MOE_ROW_CHUNK, chunk, 0)
    if n_ctx:
        accumulate(yc_ref, slice(None), ffn(xb[n_lat:T, :]))

    @pl.when(f == last)
    def _():
        per_l = n_lat // n_batch
        for b in range(n_batch):
            rows = slice(b * per_l, (b + 1) * per_l)
            yl_ref[0, rows, :] = yl_ref[0, rows, :] * gate_ref[0, rows, :] * g2_ref[b:b + 1, :]
        if n_ctx:
            per_c = n_ctx // n_batch
            for b in range(n_batch):
                rows = slice(b * per_c, (b + 1) * per_c)
                yc_ref[0, rows, :] = (yc_ref[0, rows, :] * gate_ref[0, n_lat + b * per_c:n_lat + (b + 1) * per_c, :]
                                      * cg2_ref[b:b + 1, :])


def moe_ffn(idx, h2, hc2, w_gate, w_up, w_down, layer, gates, g2, cg2, n_lat, tf):
    E, T, _ = gates.shape
    n_ctx = T - n_lat
    D = SUBLANES * LANES
    F = w_gate.shape[3]
    B = g2.shape[0]
    any_spec = pl.BlockSpec(memory_space=pl.ANY)
    per_e = lambda e, f, idx_ref: (e, 0, 0)
    full = lambda e, f, idx_ref: (0, 0)
    args = [h2] + ([hc2] if n_ctx else []) + [w_gate, w_up, w_down, gates, g2] + ([cg2] if n_ctx else [])
    specs = [any_spec] + ([any_spec] if n_ctx else []) + [
        pl.BlockSpec((1, 1, D, tf), lambda e, f, idx_ref: (layer, e, 0, f)),
        pl.BlockSpec((1, 1, D, tf), lambda e, f, idx_ref: (layer, e, 0, f)),
        pl.BlockSpec((1, 1, tf, D), lambda e, f, idx_ref: (layer, e, f, 0)),
        pl.BlockSpec((1, T, 1), per_e),
        pl.BlockSpec((B, D), full),
    ] + ([pl.BlockSpec((B, D), full)] if n_ctx else [])
    out_specs = [pl.BlockSpec((1, n_lat, D), per_e)] + ([pl.BlockSpec((1, n_ctx, D), per_e)] if n_ctx else [])
    out_shape = [jax.ShapeDtypeStruct((E, n_lat, D), F32)] + (
        [jax.ShapeDtypeStruct((E, n_ctx, D), F32)] if n_ctx else [])
    return pl.pallas_call(
        functools.partial(_moe_ffn_kernel, n_lat=n_lat, n_ctx=n_ctx, n_batch=B),
        grid_spec=pltpu.PrefetchScalarGridSpec(
            num_scalar_prefetch=1,
            grid=(E, F // tf),
            in_specs=specs,
            out_specs=out_specs,
            scratch_shapes=[pltpu.VMEM((T * SUBLANES, LANES), F32), pltpu.VMEM((T, D), BF16),
                            pltpu.SemaphoreType.DMA((1,))],
        ),
        out_shape=out_shape,
        compiler_params=_cparams(("arbitrary", "arbitrary")),
        name="moe_ffn",
    )(idx, *args)


def _final_norm_kernel(x_ref, g_ref, o_ref):
    x = x_ref[0]
    o_ref[0] = x * lax.rsqrt(jnp.mean(x * x, axis=-1, keepdims=True) + EPS) * g_ref[...]


def final_norm(x, g, tm):
    B, n, D = x.shape
    return pl.pallas_call(
        _final_norm_kernel,
        grid=(B, n // tm),
        in_specs=[pl.BlockSpec((1, tm, D), lambda b, i: (b, i, 0)), pl.BlockSpec((1, D), lambda b, i: (0, 0))],
        out_specs=pl.BlockSpec((1, tm, D), lambda b, i: (b, i, 0)),
        out_shape=jax.ShapeDtypeStruct((B, n, D), F32),
        compiler_params=_cparams(("parallel", "parallel")),
        name="final_norm",
    )(x, g.reshape(1, D))


def route(aff, cap):
    return lax.top_k(jnp.swapaxes(aff, 1, 2), cap)


TOKEN_TILE = 512


def kernel(x, c, ctx, c_ctx, w_ada, b_ada, norm1_g, norm2_g, w_in, w_out, gla_w_gate, gla_b_gate, gla_norm_g,
           gmlp_norm_g, gmlp_w_s, gmlp_b_s, lru_conv_w, lru_conv_b, lru_w_r, lru_b_r, lru_w_i, lru_b_i,
           lru_lambda, nat_rpb, moe_w_router, moe_w_gate, moe_w_up, moe_w_down, final_norm_g):
    B, N, D = x.shape
    M = ctx.shape[1]
    E = N_EXPERTS
    tm = TOKEN_TILE
    cap_l = max(1, EC_CAPACITY * N // E)
    cap_c = max(1, EC_CAPACITY * M // E)
    sc = jax.nn.silu(c)
    scc = jax.nn.silu(c_ctx)
    xc = ctx
    batch_ids = jnp.arange(B, dtype=jnp.int32)[:, None, None]
    tables = rope_tables(N)
    for l in range(DEPTH):
        need_ctx = l < DEPTH - 1
        mod = sc @ w_ada[l] + b_ada[l]
        mod_c = jnp.broadcast_to((scc @ w_ada[l] + b_ada[l])[None], (B, 6 * D))
        sh1, sc1, g1, sh2, sc2, g2 = jnp.split(mod, 6, axis=-1)
        csh1, csc1, cg1, csh2, csc2, cg2 = jnp.split(mod_c, 6, axis=-1)

        wp = permute_w_in(w_in[l])
        gla, gm, lru, nat, gate = norm_inproj(x, norm1_g[l], sc1, sh1, wp, tm)
        cgla, cgm, clru, cnat, cgate = norm_inproj(xc, norm1_g[l], csc1, csh1, wp, M)

        gla_o, gla_c = gla_mixer(gla, gate, cgla, cgate, gla_w_gate[l], gla_b_gate[l], gla_norm_g[l], tables,
                                 need_ctx, tm)
        gm_o = gmlp_mixer(gm, gmlp_norm_g[l], gmlp_w_s[l], gmlp_b_s[l], tm)
        lru_o, lru_c = lru_mixer(lru, clru, lru_conv_w[l], lru_conv_b[l], lru_w_r[l], lru_b_r[l], lru_w_i[l],
                                 lru_b_i[l], lru_lambda[l], need_ctx, tm)
        nat_o = nat_mixer(nat, cnat, nat_bias_tables(nat_rpb[l], N // GRID_W))

        wo = w_out[l].astype(BF16)
        wr = moe_w_router[l].astype(BF16)
        x, h2, aff = outproj_norm_router((gla_o, gm_o, lru_o, nat_o), wo, x, g1, norm2_g[l], sc2, sh2, wr, tm)
        gl, il = route(aff, cap_l)
        rows_l = jnp.swapaxes(il + batch_ids * N, 0, 1).reshape(E, B * cap_l)
        idx = rows_l
        gs = jnp.swapaxes(gl, 0, 1).reshape(E, B * cap_l)
        hc2 = None
        if need_ctx:
            gm_c = gmlp_mixer(cgm, gmlp_norm_g[l], gmlp_w_s[l], gmlp_b_s[l], M)
            nat_c = ctx_attention(cnat)
            xc, hc2, caff = outproj_norm_router((gla_c, gm_c, lru_c, nat_c), wo, xc, cg1, norm2_g[l], csc2, csh2,
                                                wr, M)
            gc, ic = route(caff, cap_c)
            rows_c = jnp.swapaxes(ic + batch_ids * M, 0, 1).reshape(E, B * cap_c)
            idx = jnp.concatenate([rows_l, rows_c], axis=1)
            gs = jnp.concatenate([gs, jnp.swapaxes(gc, 0, 1).reshape(E, B * cap_c)], axis=1)
            hc2 = hc2.reshape(B * M * SUBLANES, LANES)
        ys = moe_ffn((idx * SUBLANES).reshape(-1).astype(jnp.int32), h2.reshape(B * N * SUBLANES, LANES), hc2,
                     moe_w_gate, moe_w_up, moe_w_down, l, gs[..., None], g2, cg2, B * cap_l, tf=512)
        x = x.reshape(B * N, D).at[rows_l.reshape(-1)].add(ys[0].reshape(-1, D)).reshape(B, N, D)
        if need_ctx:
            xc = xc.reshape(B * M, D).at[rows_c.reshape(-1)].add(ys[1].reshape(-1, D)).reshape(B, M, D)
    return final_norm(x, final_norm_g, tm)
```

```python
import functools

import numpy as np
import jax
import jax.numpy as jnp
from jax import lax
from jax.experimental import pallas as pl
from jax.experimental.pallas import tpu as pltpu

D_MODEL = 1024
DEPTH = 4
GRID_W = 64
N_GROUPS = 4
GROUP_W = D_MODEL // N_GROUPS
GLA_HEADS = 4
GLA_DK = GROUP_W // GLA_HEADS
GLA_W = GLA_HEADS * GLA_DK
GLA_GATE_RANK = 16
GLA_TAU = 16.0
GLA_CHUNK = 64
GMLP_GROUPS = 4
GMLP_W = GROUP_W
GMLP_CHUNK = 128
LRU_W = GROUP_W
LRU_BLOCKS = 4
LRU_C = 8.0
CONV_W = 4
NAT_HEADS = 4
NAT_DH = GROUP_W // NAT_HEADS
NAT_W = NAT_HEADS * NAT_DH
NAT_KR_MAX = 8
NAT_KC = 16
N_EXPERTS = 16
EXPERT_FF = 2 * D_MODEL
EC_CAPACITY = 2
ROPE_BASE = 10000.0
EPS = 1e-6

IN_SIZES = (GLA_W, GLA_W, GLA_W, GLA_W, GLA_GATE_RANK, GLA_GATE_RANK,
            GMLP_W, GMLP_W, LRU_W, LRU_W, NAT_W, NAT_W, NAT_W)
IN_COLS = sum(IN_SIZES)

V7X_VMEM_LIMIT_BYTES = 56 * 1024 * 1024
LANES = 128
SUBLANES = 8
F32 = jnp.float32
BF16 = jnp.bfloat16
NEG_BIG = -1e30
GATE_PAD = LANES

NT_DIMS = (((1,), (1,)), ((), ()))
TN_DIMS = (((0,), (0,)), ((), ()))


def _cparams(sem):
    return pltpu.CompilerParams(dimension_semantics=sem, vmem_limit_bytes=V7X_VMEM_LIMIT_BYTES)


def _dot(a, b):
    return jnp.dot(a, b, preferred_element_type=F32)


def _split3(x):
    hi = x.astype(BF16)
    r1 = x - hi.astype(F32)
    mid = r1.astype(BF16)
    lo = (r1 - mid.astype(F32)).astype(BF16)
    return hi, mid, lo


def _head_block_mask(n):
    r = lax.broadcasted_iota(jnp.int32, (n, n), 0) // GLA_DK
    c = lax.broadcasted_iota(jnp.int32, (n, n), 1) // GLA_DK
    return r == c


def _norm_inproj_kernel(x_ref, g_ref, sc_ref, sh_ref, w_ref, gla_ref, gmlp_ref, lru_ref, nat_ref, gate_ref):
    x = x_ref[0]
    y = x * lax.rsqrt(jnp.mean(x * x, axis=-1, keepdims=True) + EPS)
    h = ((y * g_ref[...]) * (1.0 + sc_ref[0]) + sh_ref[0]).astype(BF16)
    o = 0
    for ref in (gla_ref, gmlp_ref, lru_ref, nat_ref, gate_ref):
        w = ref.shape[-1]
        ref[0] = _dot(h, w_ref[:, o:o + w]).astype(ref.dtype)
        o += w


def permute_w_in(w):
    g0 = 4 * GLA_W
    g1 = g0 + 2 * GLA_GATE_RANK
    pad = jnp.zeros((w.shape[0], GATE_PAD - 2 * GLA_GATE_RANK), w.dtype)
    return jnp.concatenate([w[:, :g0], w[:, g1:], w[:, g0:g1], pad], axis=1).astype(BF16)


def norm_inproj(x, g, scale, shift, wp, tm):
    B, n, D = x.shape
    C = wp.shape[1]
    widths = (4 * GLA_W, 2 * GMLP_W, 2 * LRU_W, 3 * NAT_W, GATE_PAD)
    dtypes = (F32, F32, F32, BF16, F32)
    row = lambda b, i: (b, i, 0)
    per_b = lambda b, i: (b, 0, 0)
    full = lambda b, i: (0, 0)
    return pl.pallas_call(
        _norm_inproj_kernel,
        grid=(B, n // tm),
        in_specs=[
            pl.BlockSpec((1, tm, D), row),
            pl.BlockSpec((1, D), full),
            pl.BlockSpec((1, 1, D), per_b),
            pl.BlockSpec((1, 1, D), per_b),
            pl.BlockSpec((D, C), full),
        ],
        out_specs=[pl.BlockSpec((1, tm, w), row) for w in widths],
        out_shape=[jax.ShapeDtypeStruct((B, n, w), dt) for w, dt in zip(widths, dtypes)],
        compiler_params=_cparams(("parallel", "parallel")),
        name="norm_inproj",
    )(x, g.reshape(1, D), scale.reshape(B, 1, D), shift.reshape(B, 1, D), wp)


def _gla_kernel(*refs, tm, reverse, rope, combine):
    refs = list(refs)
    q_ref, k_ref, v_ref, gate_ref, wg_ref, bg_ref, tri_ref, ones_ref, s0_ref = refs[:9]
    refs = refs[9:]
    if rope:
        cos_ref, sin_ref = refs[:2]
        refs = refs[2:]
    if combine:
        ob_ref, og_ref, ng_ref = refs[:3]
        refs = refs[3:]
    o_ref, sfin_ref, st = refs
    i = pl.program_id(1)
    nch = tm // GLA_CHUNK

    @pl.when(i == 0)
    def _():
        st[...] = s0_ref[0]

    q = q_ref[0]
    k = k_ref[0]
    if rope:
        cos = cos_ref[...]
        sin = sin_ref[...]
        first = (lax.broadcasted_iota(jnp.int32, (tm, GLA_W), 1) % (GLA_DK // 2)) < (GLA_DK // 4)

        def rot(u):
            partner = jnp.where(first, pltpu.roll(u, GLA_W - GLA_DK // 4, axis=1), pltpu.roll(u, GLA_DK // 4, axis=1))
            return u * cos + partner * sin

        q = rot(q)
        k = rot(k)
    q = q * (GLA_DK ** -0.5)
    a = _dot(gate_ref[0].astype(BF16), wg_ref[...]) + bg_ref[...]
    la = jax.nn.log_sigmoid(a) / GLA_TAU
    pieces = _split3(la)
    b = sum(_dot(tri_ref[...], p) for p in pieces)
    bl = sum(_dot(ones_ref[...], p) for p in pieces)
    q_in = (q * jnp.exp(b)).astype(BF16)
    k_in = (k * jnp.exp(-b)).astype(BF16)
    k_end = (k * jnp.exp(bl - b)).astype(BF16)
    dec = jnp.exp(bl)
    vb = v_ref[0].astype(BF16)

    head_mask = _head_block_mask(GLA_W)
    cr = lax.broadcasted_iota(jnp.int32, (GLA_CHUNK, GLA_W), 0)
    cs = lax.broadcasted_iota(jnp.int32, (GLA_CHUNK, GLA_W), 1) % GLA_CHUNK
    causal = (cs >= cr) if reverse else (cs <= cr)
    zero = jnp.zeros((), BF16)
    chunks = range(nch - 1, -1, -1) if reverse else range(nch)
    for c in chunks:
        sl = slice(c * GLA_CHUNK, (c + 1) * GLA_CHUNK)
        kbd_t = jnp.where(head_mask, jnp.concatenate([k_in[sl]] * GLA_HEADS, axis=0), zero)
        vbd = jnp.where(head_mask, jnp.concatenate([vb[sl]] * GLA_HEADS, axis=0), zero)
        att = lax.dot_general(q_in[sl], kbd_t, NT_DIMS, preferred_element_type=F32)
        att = jnp.where(causal, att, 0.0).astype(BF16)
        s_t = st[...]
        o = _dot(att, vbd) + lax.dot_general(q_in[sl], s_t.astype(BF16), NT_DIMS, preferred_element_type=F32)
        upd = lax.dot_general(vb[sl], k_end[sl], TN_DIMS, preferred_element_type=F32)
        st[...] = s_t * dec[c * GLA_CHUNK:c * GLA_CHUNK + 1] + jnp.where(head_mask, upd, 0.0)
        o_ref[0, sl, :] = o

    sfin_ref[0] = st[...]

    if combine:
        o = o_ref[0] + ob_ref[0]
        sq_hi, sq_mid, _ = _split3(o * o)
        avg = jnp.where(head_mask, 1.0 / GLA_DK, 0.0).astype(BF16)
        ms = _dot(sq_hi, avg) + _dot(sq_mid, avg)
        og = og_ref[0]
        o_ref[0] = o * lax.rsqrt(ms + EPS) * ng_ref[...] * (og * jax.nn.sigmoid(og))


def _chunk_sum_matrices(tm, reverse):
    r = np.arange(tm)[:, None]
    c = np.arange(tm)[None, :]
    same = (r // GLA_CHUNK) == (c // GLA_CHUNK)
    tri = same & ((c >= r) if reverse else (c <= r))
    return jnp.asarray(tri, BF16), jnp.asarray(same, BF16)


def gla_direction(gla, gate, wg, bg, s0, tm, reverse, tables=None, combine=None):
    B, L, _ = gla.shape
    nt = L // tm
    W = GLA_W
    blk = (lambda i: nt - 1 - i) if reverse else (lambda i: i)
    col = lambda j: (lambda b, i: (b, blk(i), j))
    full = lambda b, i: (0, 0)
    tri, ones = _chunk_sum_matrices(tm, reverse)
    args = [gla, gla, gla, gate, wg, bg, tri, ones, s0]
    specs = [
        pl.BlockSpec((1, tm, W), col(0)), pl.BlockSpec((1, tm, W), col(1)), pl.BlockSpec((1, tm, W), col(2)),
        pl.BlockSpec((1, tm, GATE_PAD), col(0)),
        pl.BlockSpec((GATE_PAD, W), full), pl.BlockSpec((1, W), full),
        pl.BlockSpec((tm, tm), full), pl.BlockSpec((tm, tm), full),
        pl.BlockSpec((1, W, W), lambda b, i: (b, 0, 0)),
    ]
    if tables is not None:
        args += list(tables)
        specs += [pl.BlockSpec((tm, W), lambda b, i: (blk(i), 0))] * 2
    if combine is not None:
        ob, norm_g = combine
        args += [ob, gla, norm_g]
        specs += [pl.BlockSpec((1, tm, W), col(0)), pl.BlockSpec((1, tm, W), col(3)), pl.BlockSpec((1, W), full)]
    return pl.pallas_call(
        functools.partial(_gla_kernel, tm=tm, reverse=reverse, rope=tables is not None, combine=combine is not None),
        grid=(B, nt),
        in_specs=specs,
        out_specs=[pl.BlockSpec((1, tm, W), col(0)), pl.BlockSpec((1, W, W), lambda b, i: (b, 0, 0))],
        out_shape=[jax.ShapeDtypeStruct((B, L, W), F32), jax.ShapeDtypeStruct((B, W, W), F32)],
        scratch_shapes=[pltpu.VMEM((W, W), F32)],
        compiler_params=_cparams(("parallel", "arbitrary")),
        name="gla_bwd" if reverse else "gla_fwd",
    )(*args)


def rope_tables(n):
    quarter = GLA_DK // 4
    pos = jnp.arange(n)
    inv = ROPE_BASE ** (-jnp.arange(quarter, dtype=F32) / quarter)
    row = (pos // GRID_W).astype(F32)
    colp = (pos % GRID_W).astype(F32)
    ar = row[:, None] * inv[None, :]
    ac = colp[:, None] * inv[None, :]
    cos = jnp.concatenate([jnp.cos(ar), jnp.cos(ar), jnp.cos(ac), jnp.cos(ac)], axis=1)
    sin = jnp.concatenate([-jnp.sin(ar), jnp.sin(ar), -jnp.sin(ac), jnp.sin(ac)], axis=1)
    return jnp.tile(cos, (1, GLA_HEADS)), jnp.tile(sin, (1, GLA_HEADS))


def gla_mixer(gla, gate, cgla, cgate, w_gate, b_gate, norm_g, tables, need_ctx, tm):
    B, M, _ = cgla.shape
    zeros = jnp.zeros((B, GLA_W, GLA_W), F32)
    ng = jnp.tile(norm_g, GLA_HEADS).reshape(1, GLA_W)
    wgs, bgs = [], []
    for d in range(2):
        wg = jnp.zeros((GATE_PAD, GLA_W), F32).at[d * GLA_GATE_RANK:(d + 1) * GLA_GATE_RANK].set(w_gate[d])
        wgs.append(wg.astype(BF16))
        bgs.append(b_gate[d].reshape(1, GLA_W))
    ocb, sb = gla_direction(cgla, cgate, wgs[1], bgs[1], zeros, M, True)
    ob, _ = gla_direction(gla, gate, wgs[1], bgs[1], sb, tm, True, tables=tables)
    if need_ctx:
        oc, sf = gla_direction(cgla, cgate, wgs[0], bgs[0], zeros, M, False, combine=(ocb, ng))
    else:
        oc, sf = gla_direction(cgla, cgate, wgs[0], bgs[0], zeros, M, False)
    o, _ = gla_direction(gla, gate, wgs[0], bgs[0], sf, tm, False, tables=tables, combine=(ob, ng))
    return o, oc


def _gmlp_kernel(u_ref, v_ref, g_ref, w_ref, b_ref, o_ref, *, tm):
    v = v_ref[0]
    vn = (v * lax.rsqrt(jnp.mean(v * v, axis=-1, keepdims=True) + EPS) * g_ref[...]).astype(BF16)
    grp = lax.broadcasted_iota(jnp.int32, (GMLP_CHUNK, GMLP_W), 1) // (GMLP_W // GMLP_GROUPS)
    zero = jnp.zeros((), BF16)
    for c in range(tm // GMLP_CHUNK):
        sl = slice(c * GMLP_CHUNK, (c + 1) * GMLP_CHUNK)
        mixed = b_ref[...]
        for g in range(GMLP_GROUPS):
            mixed = mixed + _dot(w_ref[g], jnp.where(grp == g, vn[sl], zero))
        o_ref[0, sl, :] = u_ref[0, sl, :] * mixed


def gmlp_mixer(gm, norm_g, w_s, b_s, tm):
    B, L, _ = gm.shape
    W = GMLP_W
    bias = jnp.repeat(b_s.T, W // GMLP_GROUPS, axis=1)
    return pl.pallas_call(
        functools.partial(_gmlp_kernel, tm=tm),
        grid=(B, L // tm),
        in_specs=[
            pl.BlockSpec((1, tm, W), lambda b, i: (b, i, 0)),
            pl.BlockSpec((1, tm, W), lambda b, i: (b, i, 1)),
            pl.BlockSpec((1, W), lambda b, i: (0, 0)),
            pl.BlockSpec((GMLP_GROUPS, GMLP_CHUNK, GMLP_CHUNK), lambda b, i: (0, 0, 0)),
            pl.BlockSpec((GMLP_CHUNK, W), lambda b, i: (0, 0)),
        ],
        out_specs=pl.BlockSpec((1, tm, W), lambda b, i: (b, i, 0)),
        out_shape=jax.ShapeDtypeStruct((B, L, W), F32),
        compiler_params=_cparams(("parallel", "parallel")),
        name="gmlp",
    )(gm, gm, norm_g.reshape(1, W), w_s.astype(BF16), bias)


LRU_UNROLL = 8


def _lru_kernel(*refs, tm, reverse, combine):
    refs = list(refs)
    x_ref, xp_ref, xn_ref, cw_ref, cb_ref, wr_ref, br_ref, wi_ref, bi_ref, ncs_ref, h0_ref = refs[:11]
    refs = refs[11:]
    if combine:
        hb_ref, ly_ref = refs[:2]
        refs = refs[2:]
    o_ref, ext, a_s, b_s, carry = refs
    W = LRU_W
    i = pl.program_id(1)
    nt = pl.num_programs(1)
    t = (nt - 1 - i) if reverse else i
    H = SUBLANES

    ext[0:H] = jnp.where(t > 0, xp_ref[0], 0.0)
    ext[H:H + tm] = x_ref[0]
    ext[H + tm:2 * H + tm] = jnp.where(t < nt - 1, xn_ref[0], 0.0)
    xc = cb_ref[...]
    for tap in range(CONV_W):
        xc = xc + cw_ref[tap:tap + 1, :] * ext[H - CONV_W // 2 + tap:H - CONV_W // 2 + tap + tm]
    xcb = xc.astype(BF16)
    r = jax.nn.sigmoid(_dot(xcb, wr_ref[...]) + br_ref[...])
    ig = jax.nn.sigmoid(_dot(xcb, wi_ref[...]) + bi_ref[...])
    log_a = ncs_ref[...] * r
    a = jnp.exp(log_a)
    b = jnp.sqrt(-jnp.tanh(log_a) * (a * a + 1.0)) * (ig * xc)

    rowi = lax.broadcasted_iota(jnp.int32, (tm, W), 0) % H
    for s in (1, 2, 4):
        if reverse:
            ok = rowi < H - s
            a_sh = pltpu.roll(a, tm - s, axis=0)
            b_sh = pltpu.roll(b, tm - s, axis=0)
        else:
            ok = rowi >= s
            a_sh = pltpu.roll(a, s, axis=0)
            b_sh = pltpu.roll(b, s, axis=0)
        b = a * jnp.where(ok, b_sh, 0.0) + b
        a = a * jnp.where(ok, a_sh, 1.0)
    a_s[...] = a
    b_s[...] = b

    @pl.when(i == 0)
    def _():
        carry[...] = jnp.broadcast_to(h0_ref[0], (H, W))

    ng = tm // H

    def body(j, h):
        for u in range(LRU_UNROLL):
            g = j * LRU_UNROLL + u
            g = (ng - 1 - g) if reverse else g
            rows = pl.ds(pl.multiple_of(g * H, H), H)
            hg = b_s[rows, :] + a_s[rows, :] * h
            o_ref[0, rows, :] = hg
            h = jnp.broadcast_to(hg[0:1] if reverse else hg[H - 1:H], (H, W))
        return h

    carry[...] = lax.fori_loop(0, ng // LRU_UNROLL, body, carry[...])

    if combine:
        o_ref[0] = (o_ref[0] + hb_ref[0]) * jax.nn.gelu(ly_ref[0])


def lru_direction(lru, conv_w, conv_b, wr, br, wi, bi, ncs, h0, tm, reverse, hb=None):
    B, L, _ = lru.shape
    nt = L // tm
    W = LRU_W
    H = SUBLANES
    nh = L // H
    per = tm // H
    blk = (lambda i: nt - 1 - i) if reverse else (lambda i: i)
    full = lambda b, i: (0, 0)
    args = [lru, lru, lru, conv_w, conv_b, wr, br, wi, bi, ncs, h0]
    specs = [
        pl.BlockSpec((1, tm, W), lambda b, i: (b, blk(i), 0)),
        pl.BlockSpec((1, H, W), lambda b, i: (b, jnp.maximum(blk(i) * per - 1, 0), 0)),
        pl.BlockSpec((1, H, W), lambda b, i: (b, jnp.minimum((blk(i) + 1) * per, nh - 1), 0)),
        pl.BlockSpec((CONV_W, W), full), pl.BlockSpec((1, W), full),
        pl.BlockSpec((W, W), full), pl.BlockSpec((1, W), full),
        pl.BlockSpec((W, W), full), pl.BlockSpec((1, W), full),
        pl.BlockSpec((1, W), full),
        pl.BlockSpec((1, 1, W), lambda b, i: (b, 0, 0)),
    ]
    if hb is not None:
        args += [hb, lru]
        specs += [pl.BlockSpec((1, tm, W), lambda b, i: (b, blk(i), 0)),
                  pl.BlockSpec((1, tm, W), lambda b, i: (b, blk(i), 1))]
    return pl.pallas_call(
        functools.partial(_lru_kernel, tm=tm, reverse=reverse, combine=hb is not None),
        grid=(B, nt),
        in_specs=specs,
        out_specs=pl.BlockSpec((1, tm, W), lambda b, i: (b, blk(i), 0)),
        out_shape=jax.ShapeDtypeStruct((B, L, W), F32),
        scratch_shapes=[pltpu.VMEM((tm + 2 * H, W), F32), pltpu.VMEM((tm, W), F32), pltpu.VMEM((tm, W), F32),
                        pltpu.VMEM((H, W), F32)],
        compiler_params=_cparams(("parallel", "arbitrary")),
        name="lru_bwd" if reverse else "lru_fwd",
    )(*args)


def _block_diag(w):
    G, n, _ = w.shape
    out = jnp.zeros((G * n, G * n), w.dtype)
    for g in range(G):
        out = out.at[g * n:(g + 1) * n, g * n:(g + 1) * n].set(w[g])
    return out


def lru_mixer(lru, clru, conv_w, conv_b, w_r, b_r, w_i, b_i, lam, need_ctx, tm):
    B, M, _ = clru.shape
    W = LRU_W
    cb = conv_b.reshape(1, W)
    ncs = -LRU_C * jax.nn.softplus(-lam.astype(F32))
    p = [(_block_diag(w_r[d]).astype(BF16), b_r[d].reshape(1, W), _block_diag(w_i[d]).astype(BF16),
          b_i[d].reshape(1, W), ncs[d].reshape(1, W)) for d in range(2)]
    zeros = jnp.zeros((B, 1, W), F32)
    hcb = lru_direction(clru, conv_w, cb, *p[1], zeros, M, True)
    hb = lru_direction(lru, conv_w, cb, *p[1], hcb[:, 0:1], tm, True)
    hcf = lru_direction(clru, conv_w, cb, *p[0], zeros, M, False)
    out = lru_direction(lru, conv_w, cb, *p[0], hcf[:, M - 1:M], tm, False, hb=hb)
    out_c = None
    if need_ctx:
        out_c = lru_direction(clru, conv_w, cb, *p[0], zeros, M, False, hb=hcb)
    return out, out_c


NAT_QROWS = 8
NAT_KROWS = 2 * NAT_QROWS
NAT_TQ = NAT_QROWS * GRID_W
NAT_TK = NAT_KROWS * GRID_W


def _softmax_pv(s_loc, s_ctx, v_loc, v_ctx):
    m = jnp.maximum(jnp.max(s_loc, axis=-1, keepdims=True), jnp.max(s_ctx, axis=-1, keepdims=True))
    e_loc = jnp.exp(s_loc - m)
    e_ctx = jnp.exp(s_ctx - m)
    den = jnp.sum(e_loc, axis=-1, keepdims=True) + jnp.sum(e_ctx, axis=-1, keepdims=True)
    o = _dot(e_loc.astype(BF16), v_loc) + _dot(e_ctx.astype(BF16), v_ctx)
    return o / den


def _nat_kernel(q_ref, kp_ref, kc_ref, kn_ref, vp_ref, vc_ref, vn_ref, ck_ref, cv_ref, bias_ref, o_ref, kcat, vcat):
    j = pl.program_id(1)
    last = pl.num_programs(1) - 1
    T = NAT_TQ
    kcat[0:T] = kp_ref[0]
    kcat[T:2 * T] = kc_ref[0]
    kcat[2 * T:3 * T] = kn_ref[0]
    vcat[0:T] = vp_ref[0]
    vcat[T:2 * T] = vc_ref[0]
    vcat[2 * T:3 * T] = vn_ref[0]
    off = jnp.where(j == 0, T, jnp.where(j == last, 0, T // 2))
    off = pl.multiple_of(off, T // 2)
    kw = kcat[pl.ds(off, NAT_TK), :]
    vw = vcat[pl.ds(off, NAT_TK), :]
    q = q_ref[0]
    ck = ck_ref[0]
    cv = cv_ref[0]
    scale = NAT_DH ** -0.5
    outs = []
    for h in range(NAT_HEADS):
        hs = slice(h * NAT_DH, (h + 1) * NAT_DH)
        qh = q[:, hs]
        s_loc = lax.dot_general(qh, kw[:, hs], NT_DIMS, preferred_element_type=F32) * scale + bias_ref[0, h]
        s_ctx = lax.dot_general(qh, ck[:, hs], NT_DIMS, preferred_element_type=F32) * scale
        outs.append(_softmax_pv(s_loc, s_ctx, vw[:, hs], cv[:, hs]))
    o_ref[0] = jnp.concatenate(outs, axis=-1)


def nat_bias_tables(rpb, rows):
    kr = NAT_KR_MAX
    qc = np.arange(GRID_W)
    kcol = np.arange(GRID_W)
    cs = np.clip(qc - NAT_KC // 2, 0, GRID_W - NAT_KC)
    valid_c = (kcol[None, :] >= cs[:, None]) & (kcol[None, :] < cs[:, None] + NAT_KC)
    edge = GRID_W - NAT_KC
    padded = jnp.pad(rpb.astype(F32), ((0, 0), (0, 0), (edge, edge)), mode="edge")
    by_col = jnp.stack([padded[:, :, GRID_W - 1 - q:2 * GRID_W - 1 - q] for q in range(GRID_W)], axis=2)
    by_col = jnp.pad(by_col, ((0, 0), (NAT_KROWS, NAT_KROWS), (0, 0), (0, 0)))
    tables = []
    for r0, ks in ((0, 0), (NAT_QROWS, NAT_QROWS - kr // 2), (rows - NAT_QROWS, rows - NAT_KROWS)):
        r = r0 + np.arange(NAT_QROWS)
        krow = ks + np.arange(NAT_KROWS)
        rs = np.clip(r - kr // 2, 0, rows - kr)
        valid_r = (krow[None, :] >= rs[:, None]) & (krow[None, :] < rs[:, None] + kr)
        starts = ks - r + NAT_KR_MAX - 1 + NAT_KROWS
        bias = jnp.stack([by_col[:, int(s):int(s) + NAT_KROWS] for s in starts], axis=1)
        bias = bias.transpose(0, 1, 3, 2, 4)
        mask = valid_r[:, None, :, None] & valid_c[None, :, None, :]
        bias = jnp.where(jnp.asarray(mask)[None], bias, NEG_BIG)
        tables.append(bias.reshape(rpb.shape[0], NAT_TQ, NAT_TK))
    return jnp.stack(tables)


def nat_mixer(nat, cnat, bias):
    B, N, _ = nat.shape
    M = cnat.shape[1]
    W = NAT_W
    nb = N // NAT_TQ
    T = NAT_TQ
    cur = lambda c: (lambda b, j: (b, j, c))
    prv = lambda c: (lambda b, j: (b, jnp.maximum(j - 1, 0), c))
    nxt = lambda c: (lambda b, j: (b, jnp.minimum(j + 1, nb - 1), c))
    variant = lambda b, j: (jnp.where(j == 0, 0, jnp.where(j == nb - 1, 2, 1)), 0, 0, 0)
    return pl.pallas_call(
        _nat_kernel,
        grid=(B, nb),
        in_specs=[
            pl.BlockSpec((1, T, W), cur(0)),
            pl.BlockSpec((1, T, W), prv(1)), pl.BlockSpec((1, T, W), cur(1)), pl.BlockSpec((1, T, W), nxt(1)),
            pl.BlockSpec((1, T, W), prv(2)), pl.BlockSpec((1, T, W), cur(2)), pl.BlockSpec((1, T, W), nxt(2)),
            pl.BlockSpec((1, M, W), lambda b, j: (b, 0, 1)),
            pl.BlockSpec((1, M, W), lambda b, j: (b, 0, 2)),
            pl.BlockSpec((1, NAT_HEADS, NAT_TQ, NAT_TK), variant),
        ],
        out_specs=pl.BlockSpec((1, T, W), cur(0)),
        out_shape=jax.ShapeDtypeStruct((B, N, W), F32),
        scratch_shapes=[pltpu.VMEM((3 * T, W), BF16), pltpu.VMEM((3 * T, W), BF16)],
        compiler_params=_cparams(("parallel", "arbitrary")),
        name="nat",
    )(nat, nat, nat, nat, nat, nat, nat, cnat, cnat, bias)


def _ctx_attn_kernel(q_ref, k_ref, v_ref, o_ref):
    q = q_ref[0]
    k = k_ref[0]
    v = v_ref[0]
    scale = NAT_DH ** -0.5
    outs = []
    for h in range(NAT_HEADS):
        hs = slice(h * NAT_DH, (h + 1) * NAT_DH)
        s = lax.dot_general(q[:, hs], k[:, hs], NT_DIMS, preferred_element_type=F32) * scale
        e = jnp.exp(s - jnp.max(s, axis=-1, keepdims=True))
        outs.append(_dot(e.astype(BF16), v[:, hs]) / jnp.sum(e, axis=-1, keepdims=True))
    o_ref[0] = jnp.concatenate(outs, axis=-1)


def ctx_attention(cnat):
    B, M, _ = cnat.shape
    W = NAT_W
    return pl.pallas_call(
        _ctx_attn_kernel,
        grid=(B,),
        in_specs=[pl.BlockSpec((1, M, W), lambda b, c=c: (b, 0, c)) for c in range(3)],
        out_specs=pl.BlockSpec((1, M, W), lambda b: (b, 0, 0)),
        out_shape=jax.ShapeDtypeStruct((B, M, W), F32),
        compiler_params=_cparams(("parallel",)),
        name="ctx_attn",
    )(cnat, cnat, cnat)


def _outproj_kernel(m0_ref, m1_ref, m2_ref, m3_ref, w_ref, x_ref, g1_ref, n2_ref, sc_ref, sh_ref, wr_ref,
                    xo_ref, h2_ref, aff_ref):
    y = 0.0
    for gi, m_ref in enumerate((m0_ref, m1_ref, m2_ref, m3_ref)):
        y = y + _dot(m_ref[0].astype(BF16), w_ref[gi * GROUP_W:(gi + 1) * GROUP_W, :])
    x = x_ref[0] + g1_ref[0] * y
    xo_ref[0] = x
    xn = x * lax.rsqrt(jnp.mean(x * x, axis=-1, keepdims=True) + EPS)
    h2 = (xn * n2_ref[...]) * (1.0 + sc_ref[0]) + sh_ref[0]
    tm = h2.shape[0]
    for j in range(SUBLANES):
        h2_ref[pl.ds(j, tm, stride=SUBLANES), :] = h2[:, j * LANES:(j + 1) * LANES]
    logits = _dot(h2.astype(BF16), wr_ref[...])
    e = jnp.exp(logits - jnp.max(logits, axis=-1, keepdims=True))
    aff_ref[0] = e / jnp.sum(e, axis=-1, keepdims=True)


def outproj_norm_router(parts, w_out, x, g1, n2, scale2, shift2, w_router, tm):
    B, n, D = x.shape
    E = w_router.shape[1]
    row = lambda b, i: (b, i, 0)
    per_b = lambda b, i: (b, 0, 0)
    full = lambda b, i: (0, 0)
    return pl.pallas_call(
        _outproj_kernel,
        grid=(B, n // tm),
        in_specs=[pl.BlockSpec((1, tm, GROUP_W), row)] * N_GROUPS + [
            pl.BlockSpec((D, D), full),
            pl.BlockSpec((1, tm, D), row),
            pl.BlockSpec((1, 1, D), per_b),
            pl.BlockSpec((1, D), full),
            pl.BlockSpec((1, 1, D), per_b),
            pl.BlockSpec((1, 1, D), per_b),
            pl.BlockSpec((D, E), full),
        ],
        out_specs=[
            pl.BlockSpec((1, tm, D), row),
            pl.BlockSpec((tm * SUBLANES, LANES), lambda b, i: (b * (n // tm) + i, 0)),
            pl.BlockSpec((1, tm, E), row),
        ],
        out_shape=[
            jax.ShapeDtypeStruct((B, n, D), F32),
            jax.ShapeDtypeStruct((B * n * SUBLANES, LANES), F32),
            jax.ShapeDtypeStruct((B, n, E), F32),
        ],
        compiler_params=_cparams(("parallel", "parallel")),
        name="outproj_norm_router",
    )(*parts, w_out, x, g1.reshape(B, 1, D), n2.reshape(1, D), scale2.reshape(B, 1, D), shift2.reshape(B, 1, D),
      w_router)


MOE_ROW_CHUNK = 512
MOE_ISSUE_UNROLL = 8


def _moe_ffn_kernel(*refs, n_lat, n_ctx, n_batch):
    refs = list(refs)
    idx_ref, h_hbm = refs[:2]
    refs = refs[2:]
    if n_ctx:
        hc_hbm = refs.pop(0)
    wg_ref, wu_ref, wd_ref, gate_ref, g2_ref = refs[:5]
    refs = refs[5:]
    if n_ctx:
        cg2_ref = refs.pop(0)
    yl_ref = refs.pop(0)
    if n_ctx:
        yc_ref = refs.pop(0)
    xf, xb, sem = refs
    e = pl.program_id(0)
    f = pl.program_id(1)
    n_exp = pl.num_programs(0)
    last = pl.num_programs(1) - 1
    T = n_lat + n_ctx

    R = SUBLANES

    def start_rows(src, expert, first, count):
        def body(i, carry):
            for u in range(MOE_ISSUE_UNROLL):
                s = first + i * MOE_ISSUE_UNROLL + u
                r = pl.multiple_of(idx_ref[expert * T + s], R)
                pltpu.make_async_copy(src.at[pl.ds(r, R)], xf.at[pl.ds(pl.multiple_of(s * R, R), R)],
                                      sem.at[0]).start()
            return carry

        lax.fori_loop(0, count // MOE_ISSUE_UNROLL, body, 0)

    def start_gather(expert):
        start_rows(h_hbm, expert, 0, n_lat)
        if n_ctx:
            start_rows(hc_hbm, expert, n_lat, n_ctx)

    @pl.when(f == 0)
    def _():
        @pl.when(e == 0)
        def _():
            start_gather(e)

        pltpu.make_async_copy(h_hbm.at[pl.ds(0, T * R)], xf, sem.at[0]).wait()
        for j in range(R):
            xb[:, j * LANES:(j + 1) * LANES] = xf[pl.ds(j, T, stride=R), :].astype(BF16)

        @pl.when(e + 1 < n_exp)
        def _():
            start_gather(e + 1)

        yl_ref[...] = jnp.zeros_like(yl_ref)
        if n_ctx:
            yc_ref[...] = jnp.zeros_like(yc_ref)

    wg = wg_ref[0, 0].astype(BF16)
    wu = wu_ref[0, 0].astype(BF16)
    wd = wd_ref[0, 0].astype(BF16)

    def ffn(x):
        hg = _dot(x, wg)
        hu = _dot(x, wu)
        return _dot((hg * jax.nn.sigmoid(hg) * hu).astype(BF16), wd)

    for i in range(n_lat // MOE_ROW_CHUNK):
        rows = slice(i * MOE_ROW_CHUNK, (i + 1) * MOE_ROW_CHUNK)
        yl_ref[0, rows, :] += ffn(xb[rows, :])
    if n_ctx:
        yc_ref[0] += ffn(xb[n_lat:T, :])

    @pl.when(f == last)
    def _():
        per_l = n_lat // n_batch
        for b in range(n_batch):
            rows = slice(b * per_l, (b + 1) * per_l)
            yl_ref[0, rows, :] = yl_ref[0, rows, :] * gate_ref[0, rows, :] * g2_ref[b:b + 1, :]
        if n_ctx:
            per_c = n_ctx // n_batch
            for b in range(n_batch):
                rows = slice(b * per_c, (b + 1) * per_c)
                yc_ref[0, rows, :] = (yc_ref[0, rows, :] * gate_ref[0, n_lat + b * per_c:n_lat + (b + 1) * per_c, :]
                                      * cg2_ref[b:b + 1, :])


def moe_ffn(idx, h2, hc2, w_gate, w_up, w_down, layer, gates, g2, cg2, n_lat, tf):
    E, T, _ = gates.shape
    n_ctx = T - n_lat
    D = SUBLANES * LANES
    F = w_gate.shape[3]
    B = g2.shape[0]
    any_spec = pl.BlockSpec(memory_space=pl.ANY)
    per_e = lambda e, f, idx_ref: (e, 0, 0)
    full = lambda e, f, idx_ref: (0, 0)
    args = [h2] + ([hc2] if n_ctx else []) + [w_gate, w_up, w_down, gates, g2] + ([cg2] if n_ctx else [])
    specs = [any_spec] + ([any_spec] if n_ctx else []) + [
        pl.BlockSpec((1, 1, D, tf), lambda e, f, idx_ref: (layer, e, 0, f)),
        pl.BlockSpec((1, 1, D, tf), lambda e, f, idx_ref: (layer, e, 0, f)),
        pl.BlockSpec((1, 1, tf, D), lambda e, f, idx_ref: (layer, e, f, 0)),
        pl.BlockSpec((1, T, 1), per_e),
        pl.BlockSpec((B, D), full),
    ] + ([pl.BlockSpec((B, D), full)] if n_ctx else [])
    out_specs = [pl.BlockSpec((1, n_lat, D), per_e)] + ([pl.BlockSpec((1, n_ctx, D), per_e)] if n_ctx else [])
    out_shape = [jax.ShapeDtypeStruct((E, n_lat, D), F32)] + (
        [jax.ShapeDtypeStruct((E, n_ctx, D), F32)] if n_ctx else [])
    return pl.pallas_call(
        functools.partial(_moe_ffn_kernel, n_lat=n_lat, n_ctx=n_ctx, n_batch=B),
        grid_spec=pltpu.PrefetchScalarGridSpec(
            num_scalar_prefetch=1,
            grid=(E, F // tf),
            in_specs=specs,
            out_specs=out_specs,
            scratch_shapes=[pltpu.VMEM((T * SUBLANES, LANES), F32), pltpu.VMEM((T, D), BF16),
                            pltpu.SemaphoreType.DMA((1,))],
        ),
        out_shape=out_shape,
        compiler_params=_cparams(("arbitrary", "arbitrary")),
        name="moe_ffn",
    )(idx, *args)


def _final_norm_kernel(x_ref, g_ref, o_ref):
    x = x_ref[0]
    o_ref[0] = x * lax.rsqrt(jnp.mean(x * x, axis=-1, keepdims=True) + EPS) * g_ref[...]


def final_norm(x, g, tm):
    B, n, D = x.shape
    return pl.pallas_call(
        _final_norm_kernel,
        grid=(B, n // tm),
        in_specs=[pl.BlockSpec((1, tm, D), lambda b, i: (b, i, 0)), pl.BlockSpec((1, D), lambda b, i: (0, 0))],
        out_specs=pl.BlockSpec((1, tm, D), lambda b, i: (b, i, 0)),
        out_shape=jax.ShapeDtypeStruct((B, n, D), F32),
        compiler_params=_cparams(("parallel", "parallel")),
        name="final_norm",
    )(x, g.reshape(1, D))


def route(aff, cap):
    return lax.top_k(jnp.swapaxes(aff, 1, 2), cap)


TOKEN_TILE = 512


def kernel(x, c, ctx, c_ctx, w_ada, b_ada, norm1_g, norm2_g, w_in, w_out, gla_w_gate, gla_b_gate, gla_norm_g,
           gmlp_norm_g, gmlp_w_s, gmlp_b_s, lru_conv_w, lru_conv_b, lru_w_r, lru_b_r, lru_w_i, lru_b_i,
           lru_lambda, nat_rpb, moe_w_router, moe_w_gate, moe_w_up, moe_w_down, final_norm_g):
    B, N, D = x.shape
    M = ctx.shape[1]
    E = N_EXPERTS
    tm = TOKEN_TILE
    cap_l = max(1, EC_CAPACITY * N // E)
    cap_c = max(1, EC_CAPACITY * M // E)
    sc = jax.nn.silu(c)
    scc = jax.nn.silu(c_ctx)
    xc = ctx
    batch_ids = jnp.arange(B, dtype=jnp.int32)[:, None, None]
    tables = rope_tables(N)
    for l in range(DEPTH):
        need_ctx = l < DEPTH - 1
        mod = sc @ w_ada[l] + b_ada[l]
        mod_c = jnp.broadcast_to((scc @ w_ada[l] + b_ada[l])[None], (B, 6 * D))
        sh1, sc1, g1, sh2, sc2, g2 = jnp.split(mod, 6, axis=-1)
        csh1, csc1, cg1, csh2, csc2, cg2 = jnp.split(mod_c, 6, axis=-1)

        wp = permute_w_in(w_in[l])
        gla, gm, lru, nat, gate = norm_inproj(x, norm1_g[l], sc1, sh1, wp, tm)
        cgla, cgm, clru, cnat, cgate = norm_inproj(xc, norm1_g[l], csc1, csh1, wp, M)

        gla_o, gla_c = gla_mixer(gla, gate, cgla, cgate, gla_w_gate[l], gla_b_gate[l], gla_norm_g[l], tables,
                                 need_ctx, tm)
        gm_o = gmlp_mixer(gm, gmlp_norm_g[l], gmlp_w_s[l], gmlp_b_s[l], tm)
        lru_o, lru_c = lru_mixer(lru, clru, lru_conv_w[l], lru_conv_b[l], lru_w_r[l], lru_b_r[l], lru_w_i[l],
                                 lru_b_i[l], lru_lambda[l], need_ctx, tm)
        nat_o = nat_mixer(nat, cnat, nat_bias_tables(nat_rpb[l], N // GRID_W))

        wo = w_out[l].astype(BF16)
        wr = moe_w_router[l].astype(BF16)
        x, h2, aff = outproj_norm_router((gla_o, gm_o, lru_o, nat_o), wo, x, g1, norm2_g[l], sc2, sh2, wr, tm)
        gl, il = route(aff, cap_l)
        rows_l = jnp.swapaxes(il + batch_ids * N, 0, 1).reshape(E, B * cap_l)
        idx = rows_l
        gs = jnp.swapaxes(gl, 0, 1).reshape(E, B * cap_l)
        hc2 = None
        if need_ctx:
            gm_c = gmlp_mixer(cgm, gmlp_norm_g[l], gmlp_w_s[l], gmlp_b_s[l], M)
            nat_c = ctx_attention(cnat)
            xc, hc2, caff = outproj_norm_router((gla_c, gm_c, lru_c, nat_c), wo, xc, cg1, norm2_g[l], csc2, csh2,
                                                wr, M)
            gc, ic = route(caff, cap_c)
            rows_c = jnp.swapaxes(ic + batch_ids * M, 0, 1).reshape(E, B * cap_c)
            idx = jnp.concatenate([rows_l, rows_c], axis=1)
            gs = jnp.concatenate([gs, jnp.swapaxes(gc, 0, 1).reshape(E, B * cap_c)], axis=1)
        ys = moe_ffn((idx * SUBLANES).reshape(-1).astype(jnp.int32), h2, hc2, moe_w_gate, moe_w_up, moe_w_down, l,
                     gs[..., None], g2, cg2, B * cap_l, tf=512)
        x = x.reshape(B * N, D).at[rows_l.reshape(-1)].add(ys[0].reshape(-1, D)).reshape(B, N, D)
        if need_ctx:
            xc = xc.reshape(B * M, D).at[rows_c.reshape(-1)].add(ys[1].reshape(-1, D)).reshape(B, M, D)
    return final_norm(x, final_norm_g, tm)
```

```python
import functools

import numpy as np
import jax
import jax.numpy as jnp
from jax import lax
from jax.experimental import pallas as pl
from jax.experimental.pallas import tpu as pltpu

D_MODEL = 1024
DEPTH = 4
GRID_W = 64
N_GROUPS = 4
GROUP_W = D_MODEL // N_GROUPS
GLA_HEADS = 4
GLA_DK = GROUP_W // GLA_HEADS
GLA_W = GLA_HEADS * GLA_DK
GLA_GATE_RANK = 16
GLA_TAU = 16.0
GLA_CHUNK = 64
GMLP_GROUPS = 4
GMLP_W = GROUP_W
GMLP_CHUNK = 128
LRU_W = GROUP_W
LRU_BLOCKS = 4
LRU_C = 8.0
CONV_W = 4
NAT_HEADS = 4
NAT_DH = GROUP_W // NAT_HEADS
NAT_W = NAT_HEADS * NAT_DH
NAT_KR_MAX = 8
NAT_KC = 16
N_EXPERTS = 16
EXPERT_FF = 2 * D_MODEL
EC_CAPACITY = 2
ROPE_BASE = 10000.0
EPS = 1e-6

IN_SIZES = (GLA_W, GLA_W, GLA_W, GLA_W, GLA_GATE_RANK, GLA_GATE_RANK,
            GMLP_W, GMLP_W, LRU_W, LRU_W, NAT_W, NAT_W, NAT_W)
IN_COLS = sum(IN_SIZES)

V7X_VMEM_LIMIT_BYTES = 56 * 1024 * 1024
LANES = 128
SUBLANES = 8
F32 = jnp.float32
BF16 = jnp.bfloat16
NEG_BIG = -1e30
GATE_PAD = LANES

NT_DIMS = (((1,), (1,)), ((), ()))
TN_DIMS = (((0,), (0,)), ((), ()))


def _cparams(sem):
    return pltpu.CompilerParams(dimension_semantics=sem, vmem_limit_bytes=V7X_VMEM_LIMIT_BYTES)


def _dot(a, b):
    return jnp.dot(a, b, preferred_element_type=F32)


def _split3(x):
    hi = x.astype(BF16)
    r1 = x - hi.astype(F32)
    mid = r1.astype(BF16)
    lo = (r1 - mid.astype(F32)).astype(BF16)
    return hi, mid, lo


def _head_block_mask(n):
    r = lax.broadcasted_iota(jnp.int32, (n, n), 0) // GLA_DK
    c = lax.broadcasted_iota(jnp.int32, (n, n), 1) // GLA_DK
    return r == c


def _norm_inproj_kernel(x_ref, g_ref, sc_ref, sh_ref, w_ref, gla_ref, gmlp_ref, lru_ref, nat_ref, gate_ref):
    x = x_ref[0]
    y = x * lax.rsqrt(jnp.mean(x * x, axis=-1, keepdims=True) + EPS)
    h = ((y * g_ref[...]) * (1.0 + sc_ref[0]) + sh_ref[0]).astype(BF16)
    o = 0
    for ref in (gla_ref, gmlp_ref, lru_ref, nat_ref, gate_ref):
        w = ref.shape[-1]
        ref[0] = _dot(h, w_ref[:, o:o + w]).astype(ref.dtype)
        o += w


def permute_w_in(w):
    g0 = 4 * GLA_W
    g1 = g0 + 2 * GLA_GATE_RANK
    pad = jnp.zeros((w.shape[0], GATE_PAD - 2 * GLA_GATE_RANK), w.dtype)
    return jnp.concatenate([w[:, :g0], w[:, g1:], w[:, g0:g1], pad], axis=1).astype(BF16)


def norm_inproj(x, g, scale, shift, wp, tm):
    B, n, D = x.shape
    C = wp.shape[1]
    widths = (4 * GLA_W, 2 * GMLP_W, 2 * LRU_W, 3 * NAT_W, GATE_PAD)
    dtypes = (F32, F32, F32, BF16, F32)
    row = lambda b, i: (b, i, 0)
    per_b = lambda b, i: (b, 0, 0)
    full = lambda b, i: (0, 0)
    return pl.pallas_call(
        _norm_inproj_kernel,
        grid=(B, n // tm),
        in_specs=[
            pl.BlockSpec((1, tm, D), row),
            pl.BlockSpec((1, D), full),
            pl.BlockSpec((1, 1, D), per_b),
            pl.BlockSpec((1, 1, D), per_b),
            pl.BlockSpec((D, C), full),
        ],
        out_specs=[pl.BlockSpec((1, tm, w), row) for w in widths],
        out_shape=[jax.ShapeDtypeStruct((B, n, w), dt) for w, dt in zip(widths, dtypes)],
        compiler_params=_cparams(("parallel", "parallel")),
        name="norm_inproj",
    )(x, g.reshape(1, D), scale.reshape(B, 1, D), shift.reshape(B, 1, D), wp)


def _gla_kernel(*refs, tm, reverse, rope, combine):
    refs = list(refs)
    q_ref, k_ref, v_ref, gate_ref, wg_ref, bg_ref, tri_ref, ones_ref, s0_ref = refs[:9]
    refs = refs[9:]
    if rope:
        cos_ref, sin_ref = refs[:2]
        refs = refs[2:]
    if combine:
        ob_ref, og_ref, ng_ref = refs[:3]
        refs = refs[3:]
    o_ref, sfin_ref, st = refs
    i = pl.program_id(1)
    nch = tm // GLA_CHUNK

    @pl.when(i == 0)
    def _():
        st[...] = s0_ref[0]

    q = q_ref[0]
    k = k_ref[0]
    if rope:
        cos = cos_ref[...]
        sin = sin_ref[...]
        first = (lax.broadcasted_iota(jnp.int32, (tm, GLA_W), 1) % (GLA_DK // 2)) < (GLA_DK // 4)

        def rot(u):
            partner = jnp.where(first, pltpu.roll(u, GLA_W - GLA_DK // 4, axis=1), pltpu.roll(u, GLA_DK // 4, axis=1))
            return u * cos + partner * sin

        q = rot(q)
        k = rot(k)
    q = q * (GLA_DK ** -0.5)
    a = _dot(gate_ref[0].astype(BF16), wg_ref[...]) + bg_ref[...]
    la = jax.nn.log_sigmoid(a) / GLA_TAU
    pieces = _split3(la)
    b = sum(_dot(tri_ref[...], p) for p in pieces)
    bl = sum(_dot(ones_ref[...], p) for p in pieces)
    q_in = (q * jnp.exp(b)).astype(BF16)
    k_in = (k * jnp.exp(-b)).astype(BF16)
    k_end = (k * jnp.exp(bl - b)).astype(BF16)
    dec = jnp.exp(bl)
    vb = v_ref[0].astype(BF16)

    head_mask = _head_block_mask(GLA_W)
    cr = lax.broadcasted_iota(jnp.int32, (GLA_CHUNK, GLA_W), 0)
    cs = lax.broadcasted_iota(jnp.int32, (GLA_CHUNK, GLA_W), 1) % GLA_CHUNK
    causal = (cs >= cr) if reverse else (cs <= cr)
    zero = jnp.zeros((), BF16)
    chunks = range(nch - 1, -1, -1) if reverse else range(nch)
    for c in chunks:
        sl = slice(c * GLA_CHUNK, (c + 1) * GLA_CHUNK)
        kbd_t = jnp.where(head_mask, jnp.concatenate([k_in[sl]] * GLA_HEADS, axis=0), zero)
        vbd = jnp.where(head_mask, jnp.concatenate([vb[sl]] * GLA_HEADS, axis=0), zero)
        att = lax.dot_general(q_in[sl], kbd_t, NT_DIMS, preferred_element_type=F32)
        att = jnp.where(causal, att, 0.0).astype(BF16)
        s_t = st[...]
        o = _dot(att, vbd) + lax.dot_general(q_in[sl], s_t.astype(BF16), NT_DIMS, preferred_element_type=F32)
        upd = lax.dot_general(vb[sl], k_end[sl], TN_DIMS, preferred_element_type=F32)
        st[...] = s_t * dec[c * GLA_CHUNK:c * GLA_CHUNK + 1] + jnp.where(head_mask, upd, 0.0)
        o_ref[0, sl, :] = o

    sfin_ref[0] = st[...]

    if combine:
        o = o_ref[0] + ob_ref[0]
        sq_hi, sq_mid, _ = _split3(o * o)
        avg = jnp.where(head_mask, 1.0 / GLA_DK, 0.0).astype(BF16)
        ms = _dot(sq_hi, avg) + _dot(sq_mid, avg)
        og = og_ref[0]
        o_ref[0] = o * lax.rsqrt(ms + EPS) * ng_ref[...] * (og * jax.nn.sigmoid(og))


def _chunk_sum_matrices(tm, reverse):
    r = np.arange(tm)[:, None]
    c = np.arange(tm)[None, :]
    same = (r // GLA_CHUNK) == (c // GLA_CHUNK)
    tri = same & ((c >= r) if reverse else (c <= r))
    return jnp.asarray(tri, BF16), jnp.asarray(same, BF16)


def gla_direction(gla, gate, wg, bg, s0, tm, reverse, tables=None, combine=None):
    B, L, _ = gla.shape
    nt = L // tm
    W = GLA_W
    blk = (lambda i: nt - 1 - i) if reverse else (lambda i: i)
    col = lambda j: (lambda b, i: (b, blk(i), j))
    full = lambda b, i: (0, 0)
    tri, ones = _chunk_sum_matrices(tm, reverse)
    args = [gla, gla, gla, gate, wg, bg, tri, ones, s0]
    specs = [
        pl.BlockSpec((1, tm, W), col(0)), pl.BlockSpec((1, tm, W), col(1)), pl.BlockSpec((1, tm, W), col(2)),
        pl.BlockSpec((1, tm, GATE_PAD), col(0)),
        pl.BlockSpec((GATE_PAD, W), full), pl.BlockSpec((1, W), full),
        pl.BlockSpec((tm, tm), full), pl.BlockSpec((tm, tm), full),
        pl.BlockSpec((1, W, W), lambda b, i: (b, 0, 0)),
    ]
    if tables is not None:
        args += list(tables)
        specs += [pl.BlockSpec((tm, W), lambda b, i: (blk(i), 0))] * 2
    if combine is not None:
        ob, norm_g = combine
        args += [ob, gla, norm_g]
        specs += [pl.BlockSpec((1, tm, W), col(0)), pl.BlockSpec((1, tm, W), col(3)), pl.BlockSpec((1, W), full)]
    return pl.pallas_call(
        functools.partial(_gla_kernel, tm=tm, reverse=reverse, rope=tables is not None, combine=combine is not None),
        grid=(B, nt),
        in_specs=specs,
        out_specs=[pl.BlockSpec((1, tm, W), col(0)), pl.BlockSpec((1, W, W), lambda b, i: (b, 0, 0))],
        out_shape=[jax.ShapeDtypeStruct((B, L, W), F32), jax.ShapeDtypeStruct((B, W, W), F32)],
        scratch_shapes=[pltpu.VMEM((W, W), F32)],
        compiler_params=_cparams(("parallel", "arbitrary")),
        name="gla_bwd" if reverse else "gla_fwd",
    )(*args)


def rope_tables(n):
    quarter = GLA_DK // 4
    pos = jnp.arange(n)
    inv = ROPE_BASE ** (-jnp.arange(quarter, dtype=F32) / quarter)
    row = (pos // GRID_W).astype(F32)
    colp = (pos % GRID_W).astype(F32)
    ar = row[:, None] * inv[None, :]
    ac = colp[:, None] * inv[None, :]
    cos = jnp.concatenate([jnp.cos(ar), jnp.cos(ar), jnp.cos(ac), jnp.cos(ac)], axis=1)
    sin = jnp.concatenate([-jnp.sin(ar), jnp.sin(ar), -jnp.sin(ac), jnp.sin(ac)], axis=1)
    return jnp.tile(cos, (1, GLA_HEADS)), jnp.tile(sin, (1, GLA_HEADS))


def gla_mixer(gla, gate, cgla, cgate, w_gate, b_gate, norm_g, tables, need_ctx, tm):
    B, M, _ = cgla.shape
    zeros = jnp.zeros((B, GLA_W, GLA_W), F32)
    ng = jnp.tile(norm_g, GLA_HEADS).reshape(1, GLA_W)
    wgs, bgs = [], []
    for d in range(2):
        wg = jnp.zeros((GATE_PAD, GLA_W), F32).at[d * GLA_GATE_RANK:(d + 1) * GLA_GATE_RANK].set(w_gate[d])
        wgs.append(wg.astype(BF16))
        bgs.append(b_gate[d].reshape(1, GLA_W))
    ocb, sb = gla_direction(cgla, cgate, wgs[1], bgs[1], zeros, M, True)
    ob, _ = gla_direction(gla, gate, wgs[1], bgs[1], sb, tm, True, tables=tables)
    if need_ctx:
        oc, sf = gla_direction(cgla, cgate, wgs[0], bgs[0], zeros, M, False, combine=(ocb, ng))
    else:
        oc, sf = gla_direction(cgla, cgate, wgs[0], bgs[0], zeros, M, False)
    o, _ = gla_direction(gla, gate, wgs[0], bgs[0], sf, tm, False, tables=tables, combine=(ob, ng))
    return o, oc


def _gmlp_kernel(u_ref, v_ref, g_ref, w_ref, b_ref, o_ref, *, tm):
    v = v_ref[0]
    vn = (v * lax.rsqrt(jnp.mean(v * v, axis=-1, keepdims=True) + EPS) * g_ref[...]).astype(BF16)
    grp = lax.broadcasted_iota(jnp.int32, (GMLP_CHUNK, GMLP_W), 1) // (GMLP_W // GMLP_GROUPS)
    zero = jnp.zeros((), BF16)
    for c in range(tm // GMLP_CHUNK):
        sl = slice(c * GMLP_CHUNK, (c + 1) * GMLP_CHUNK)
        mixed = b_ref[...]
        for g in range(GMLP_GROUPS):
            mixed = mixed + _dot(w_ref[g], jnp.where(grp == g, vn[sl], zero))
        o_ref[0, sl, :] = u_ref[0, sl, :] * mixed


def gmlp_mixer(gm, norm_g, w_s, b_s, tm):
    B, L, _ = gm.shape
    W = GMLP_W
    bias = jnp.repeat(b_s.T, W // GMLP_GROUPS, axis=1)
    return pl.pallas_call(
        functools.partial(_gmlp_kernel, tm=tm),
        grid=(B, L // tm),
        in_specs=[
            pl.BlockSpec((1, tm, W), lambda b, i: (b, i, 0)),
            pl.BlockSpec((1, tm, W), lambda b, i: (b, i, 1)),
            pl.BlockSpec((1, W), lambda b, i: (0, 0)),
            pl.BlockSpec((GMLP_GROUPS, GMLP_CHUNK, GMLP_CHUNK), lambda b, i: (0, 0, 0)),
            pl.BlockSpec((GMLP_CHUNK, W), lambda b, i: (0, 0)),
        ],
        out_specs=pl.BlockSpec((1, tm, W), lambda b, i: (b, i, 0)),
        out_shape=jax.ShapeDtypeStruct((B, L, W), F32),
        compiler_params=_cparams(("parallel", "parallel")),
        name="gmlp",
    )(gm, gm, norm_g.reshape(1, W), w_s.astype(BF16), bias)


LRU_UNROLL = 8


def _lru_kernel(*refs, tm, reverse, combine):
    refs = list(refs)
    x_ref, xp_ref, xn_ref, cw_ref, cb_ref, wr_ref, br_ref, wi_ref, bi_ref, ncs_ref, h0_ref = refs[:11]
    refs = refs[11:]
    if combine:
        hb_ref, ly_ref = refs[:2]
        refs = refs[2:]
    o_ref, ext, a_s, b_s, carry = refs
    W = LRU_W
    i = pl.program_id(1)
    nt = pl.num_programs(1)
    t = (nt - 1 - i) if reverse else i
    H = SUBLANES

    ext[0:H] = jnp.where(t > 0, xp_ref[0], 0.0)
    ext[H:H + tm] = x_ref[0]
    ext[H + tm:2 * H + tm] = jnp.where(t < nt - 1, xn_ref[0], 0.0)
    xc = cb_ref[...]
    for tap in range(CONV_W):
        xc = xc + cw_ref[tap:tap + 1, :] * ext[H - CONV_W // 2 + tap:H - CONV_W // 2 + tap + tm]
    xcb = xc.astype(BF16)
    r = jax.nn.sigmoid(_dot(xcb, wr_ref[...]) + br_ref[...])
    ig = jax.nn.sigmoid(_dot(xcb, wi_ref[...]) + bi_ref[...])
    log_a = ncs_ref[...] * r
    a = jnp.exp(log_a)
    b = jnp.sqrt(-jnp.tanh(log_a) * (a * a + 1.0)) * (ig * xc)

    rowi = lax.broadcasted_iota(jnp.int32, (tm, W), 0) % H
    for s in (1, 2, 4):
        if reverse:
            ok = rowi < H - s
            a_sh = pltpu.roll(a, tm - s, axis=0)
            b_sh = pltpu.roll(b, tm - s, axis=0)
        else:
            ok = rowi >= s
            a_sh = pltpu.roll(a, s, axis=0)
            b_sh = pltpu.roll(b, s, axis=0)
        b = a * jnp.where(ok, b_sh, 0.0) + b
        a = a * jnp.where(ok, a_sh, 1.0)
    a_s[...] = a
    b_s[...] = b

    @pl.when(i == 0)
    def _():
        carry[...] = jnp.broadcast_to(h0_ref[0], (H, W))

    ng = tm // H

    def body(j, h):
        for u in range(LRU_UNROLL):
            g = j * LRU_UNROLL + u
            g = (ng - 1 - g) if reverse else g
            rows = pl.ds(pl.multiple_of(g * H, H), H)
            hg = b_s[rows, :] + a_s[rows, :] * h
            o_ref[0, rows, :] = hg
            h = jnp.broadcast_to(hg[0:1] if reverse else hg[H - 1:H], (H, W))
        return h

    carry[...] = lax.fori_loop(0, ng // LRU_UNROLL, body, carry[...])

    if combine:
        o_ref[0] = (o_ref[0] + hb_ref[0]) * jax.nn.gelu(ly_ref[0])


def lru_direction(lru, conv_w, conv_b, wr, br, wi, bi, ncs, h0, tm, reverse, hb=None):
    B, L, _ = lru.shape
    nt = L // tm
    W = LRU_W
    H = SUBLANES
    nh = L // H
    per = tm // H
    blk = (lambda i: nt - 1 - i) if reverse else (lambda i: i)
    full = lambda b, i: (0, 0)
    args = [lru, lru, lru, conv_w, conv_b, wr, br, wi, bi, ncs, h0]
    specs = [
        pl.BlockSpec((1, tm, W), lambda b, i: (b, blk(i), 0)),
        pl.BlockSpec((1, H, W), lambda b, i: (b, jnp.maximum(blk(i) * per - 1, 0), 0)),
        pl.BlockSpec((1, H, W), lambda b, i: (b, jnp.minimum((blk(i) + 1) * per, nh - 1), 0)),
        pl.BlockSpec((CONV_W, W), full), pl.BlockSpec((1, W), full),
        pl.BlockSpec((W, W), full), pl.BlockSpec((1, W), full),
        pl.BlockSpec((W, W), full), pl.BlockSpec((1, W), full),
        pl.BlockSpec((1, W), full),
        pl.BlockSpec((1, 1, W), lambda b, i: (b, 0, 0)),
    ]
    if hb is not None:
        args += [hb, lru]
        specs += [pl.BlockSpec((1, tm, W), lambda b, i: (b, blk(i), 0)),
                  pl.BlockSpec((1, tm, W), lambda b, i: (b, blk(i), 1))]
    return pl.pallas_call(
        functools.partial(_lru_kernel, tm=tm, reverse=reverse, combine=hb is not None),
        grid=(B, nt),
        in_specs=specs,
        out_specs=pl.BlockSpec((1, tm, W), lambda b, i: (b, blk(i), 0)),
        out_shape=jax.ShapeDtypeStruct((B, L, W), F32),
        scratch_shapes=[pltpu.VMEM((tm + 2 * H, W), F32), pltpu.VMEM((tm, W), F32), pltpu.VMEM((tm, W), F32),
                        pltpu.VMEM((H, W), F32)],
        compiler_params=_cparams(("parallel", "arbitrary")),
        name="lru_bwd" if reverse else "lru_fwd",
    )(*args)


def _block_diag(w):
    G, n, _ = w.shape
    out = jnp.zeros((G * n, G * n), w.dtype)
    for g in range(G):
        out = out.at[g * n:(g + 1) * n, g * n:(g + 1) * n].set(w[g])
    return out


def lru_mixer(lru, clru, conv_w, conv_b, w_r, b_r, w_i, b_i, lam, need_ctx, tm):
    B, M, _ = clru.shape
    W = LRU_W
    cb = conv_b.reshape(1, W)
    ncs = -LRU_C * jax.nn.softplus(-lam.astype(F32))
    p = [(_block_diag(w_r[d]).astype(BF16), b_r[d].reshape(1, W), _block_diag(w_i[d]).astype(BF16),
          b_i[d].reshape(1, W), ncs[d].reshape(1, W)) for d in range(2)]
    zeros = jnp.zeros((B, 1, W), F32)
    hcb = lru_direction(clru, conv_w, cb, *p[1], zeros, M, True)
    hb = lru_direction(lru, conv_w, cb, *p[1], hcb[:, 0:1], tm, True)
    hcf = lru_direction(clru, conv_w, cb, *p[0], zeros, M, False)
    out = lru_direction(lru, conv_w, cb, *p[0], hcf[:, M - 1:M], tm, False, hb=hb)
    out_c = None
    if need_ctx:
        out_c = lru_direction(clru, conv_w, cb, *p[0], zeros, M, False, hb=hcb)
    return out, out_c


NAT_QROWS = 8
NAT_KROWS = 2 * NAT_QROWS
NAT_TQ = NAT_QROWS * GRID_W
NAT_TK = NAT_KROWS * GRID_W


def _softmax_pv(s_loc, s_ctx, v_loc, v_ctx):
    m = jnp.maximum(jnp.max(s_loc, axis=-1, keepdims=True), jnp.max(s_ctx, axis=-1, keepdims=True))
    e_loc = jnp.exp(s_loc - m)
    e_ctx = jnp.exp(s_ctx - m)
    den = jnp.sum(e_loc, axis=-1, keepdims=True) + jnp.sum(e_ctx, axis=-1, keepdims=True)
    o = _dot(e_loc.astype(BF16), v_loc) + _dot(e_ctx.astype(BF16), v_ctx)
    return o / den


def _nat_kernel(q_ref, kp_ref, kc_ref, kn_ref, vp_ref, vc_ref, vn_ref, ck_ref, cv_ref, bias_ref, o_ref, kcat, vcat):
    j = pl.program_id(1)
    last = pl.num_programs(1) - 1
    T = NAT_TQ
    kcat[0:T] = kp_ref[0]
    kcat[T:2 * T] = kc_ref[0]
    kcat[2 * T:3 * T] = kn_ref[0]
    vcat[0:T] = vp_ref[0]
    vcat[T:2 * T] = vc_ref[0]
    vcat[2 * T:3 * T] = vn_ref[0]
    off = jnp.where(j == 0, T, jnp.where(j == last, 0, T // 2))
    off = pl.multiple_of(off, T // 2)
    kw = kcat[pl.ds(off, NAT_TK), :]
    vw = vcat[pl.ds(off, NAT_TK), :]
    q = q_ref[0]
    ck = ck_ref[0]
    cv = cv_ref[0]
    scale = NAT_DH ** -0.5
    outs = []
    for h in range(NAT_HEADS):
        hs = slice(h * NAT_DH, (h + 1) * NAT_DH)
        qh = q[:, hs]
        s_loc = lax.dot_general(qh, kw[:, hs], NT_DIMS, preferred_element_type=F32) * scale + bias_ref[0, h]
        s_ctx = lax.dot_general(qh, ck[:, hs], NT_DIMS, preferred_element_type=F32) * scale
        outs.append(_softmax_pv(s_loc, s_ctx, vw[:, hs], cv[:, hs]))
    o_ref[0] = jnp.concatenate(outs, axis=-1)


def nat_bias_tables(rpb, rows):
    kr = NAT_KR_MAX
    qc = np.arange(GRID_W)
    kcol = np.arange(GRID_W)
    cs = np.clip(qc - NAT_KC // 2, 0, GRID_W - NAT_KC)
    valid_c = (kcol[None, :] >= cs[:, None]) & (kcol[None, :] < cs[:, None] + NAT_KC)
    edge = GRID_W - NAT_KC
    padded = jnp.pad(rpb.astype(F32), ((0, 0), (0, 0), (edge, edge)), mode="edge")
    by_col = jnp.stack([padded[:, :, GRID_W - 1 - q:2 * GRID_W - 1 - q] for q in range(GRID_W)], axis=2)
    by_col = jnp.pad(by_col, ((0, 0), (NAT_KROWS, NAT_KROWS), (0, 0), (0, 0)))
    tables = []
    for r0, ks in ((0, 0), (NAT_QROWS, NAT_QROWS - kr // 2), (rows - NAT_QROWS, rows - NAT_KROWS)):
        r = r0 + np.arange(NAT_QROWS)
        krow = ks + np.arange(NAT_KROWS)
        rs = np.clip(r - kr // 2, 0, rows - kr)
        valid_r = (krow[None, :] >= rs[:, None]) & (krow[None, :] < rs[:, None] + kr)
        starts = ks - r + NAT_KR_MAX - 1 + NAT_KROWS
        bias = jnp.stack([by_col[:, int(s):int(s) + NAT_KROWS] for s in starts], axis=1)
        bias = bias.transpose(0, 1, 3, 2, 4)
        mask = valid_r[:, None, :, None] & valid_c[None, :, None, :]
        bias = jnp.where(jnp.asarray(mask)[None], bias, NEG_BIG)
        tables.append(bias.reshape(rpb.shape[0], NAT_TQ, NAT_TK))
    return jnp.stack(tables)


def nat_mixer(nat, cnat, bias):
    B, N, _ = nat.shape
    M = cnat.shape[1]
    W = NAT_W
    nb = N // NAT_TQ
    T = NAT_TQ
    cur = lambda c: (lambda b, j: (b, j, c))
    prv = lambda c: (lambda b, j: (b, jnp.maximum(j - 1, 0), c))
    nxt = lambda c: (lambda b, j: (b, jnp.minimum(j + 1, nb - 1), c))
    variant = lambda b, j: (jnp.where(j == 0, 0, jnp.where(j == nb - 1, 2, 1)), 0, 0, 0)
    return pl.pallas_call(
        _nat_kernel,
        grid=(B, nb),
        in_specs=[
            pl.BlockSpec((1, T, W), cur(0)),
            pl.BlockSpec((1, T, W), prv(1)), pl.BlockSpec((1, T, W), cur(1)), pl.BlockSpec((1, T, W), nxt(1)),
            pl.BlockSpec((1, T, W), prv(2)), pl.BlockSpec((1, T, W), cur(2)), pl.BlockSpec((1, T, W), nxt(2)),
            pl.BlockSpec((1, M, W), lambda b, j: (b, 0, 1)),
            pl.BlockSpec((1, M, W), lambda b, j: (b, 0, 2)),
            pl.BlockSpec((1, NAT_HEADS, NAT_TQ, NAT_TK), variant),
        ],
        out_specs=pl.BlockSpec((1, T, W), cur(0)),
        out_shape=jax.ShapeDtypeStruct((B, N, W), F32),
        scratch_shapes=[pltpu.VMEM((3 * T, W), BF16), pltpu.VMEM((3 * T, W), BF16)],
        compiler_params=_cparams(("parallel", "arbitrary")),
        name="nat",
    )(nat, nat, nat, nat, nat, nat, nat, cnat, cnat, bias)


def _ctx_attn_kernel(q_ref, k_ref, v_ref, o_ref):
    q = q_ref[0]
    k = k_ref[0]
    v = v_ref[0]
    scale = NAT_DH ** -0.5
    outs = []
    for h in range(NAT_HEADS):
        hs = slice(h * NAT_DH, (h + 1) * NAT_DH)
        s = lax.dot_general(q[:, hs], k[:, hs], NT_DIMS, preferred_element_type=F32) * scale
        e = jnp.exp(s - jnp.max(s, axis=-1, keepdims=True))
        outs.append(_dot(e.astype(BF16), v[:, hs]) / jnp.sum(e, axis=-1, keepdims=True))
    o_ref[0] = jnp.concatenate(outs, axis=-1)


def ctx_attention(cnat):
    B, M, _ = cnat.shape
    W = NAT_W
    return pl.pallas_call(
        _ctx_attn_kernel,
        grid=(B,),
        in_specs=[pl.BlockSpec((1, M, W), lambda b, c=c: (b, 0, c)) for c in range(3)],
        out_specs=pl.BlockSpec((1, M, W), lambda b: (b, 0, 0)),
        out_shape=jax.ShapeDtypeStruct((B, M, W), F32),
        compiler_params=_cparams(("parallel",)),
        name="ctx_attn",
    )(cnat, cnat, cnat)


def _outproj_kernel(m0_ref, m1_ref, m2_ref, m3_ref, w_ref, x_ref, g1_ref, n2_ref, sc_ref, sh_ref, wr_ref,
                    xo_ref, h2_ref, aff_ref):
    y = 0.0
    for gi, m_ref in enumerate((m0_ref, m1_ref, m2_ref, m3_ref)):
        y = y + _dot(m_ref[0].astype(BF16), w_ref[gi * GROUP_W:(gi + 1) * GROUP_W, :])
    x = x_ref[0] + g1_ref[0] * y
    xo_ref[0] = x
    xn = x * lax.rsqrt(jnp.mean(x * x, axis=-1, keepdims=True) + EPS)
    h2 = (xn * n2_ref[...]) * (1.0 + sc_ref[0]) + sh_ref[0]
    tm = h2.shape[0]
    for j in range(SUBLANES):
        h2_ref[pl.ds(j, tm, stride=SUBLANES), :] = h2[:, j * LANES:(j + 1) * LANES]
    logits = _dot(h2.astype(BF16), wr_ref[...])
    e = jnp.exp(logits - jnp.max(logits, axis=-1, keepdims=True))
    aff_ref[0] = e / jnp.sum(e, axis=-1, keepdims=True)


def outproj_norm_router(parts, w_out, x, g1, n2, scale2, shift2, w_router, tm):
    B, n, D = x.shape
    E = w_router.shape[1]
    row = lambda b, i: (b, i, 0)
    per_b = lambda b, i: (b, 0, 0)
    full = lambda b, i: (0, 0)
    return pl.pallas_call(
        _outproj_kernel,
        grid=(B, n // tm),
        in_specs=[pl.BlockSpec((1, tm, GROUP_W), row)] * N_GROUPS + [
            pl.BlockSpec((D, D), full),
            pl.BlockSpec((1, tm, D), row),
            pl.BlockSpec((1, 1, D), per_b),
            pl.BlockSpec((1, D), full),
            pl.BlockSpec((1, 1, D), per_b),
            pl.BlockSpec((1, 1, D), per_b),
            pl.BlockSpec((D, E), full),
        ],
        out_specs=[
            pl.BlockSpec((1, tm, D), row),
            pl.BlockSpec((tm * SUBLANES, LANES), lambda b, i: (b * (n // tm) + i, 0)),
            pl.BlockSpec((1, tm, E), row),
        ],
        out_shape=[
            jax.ShapeDtypeStruct((B, n, D), F32),
            jax.ShapeDtypeStruct((B * n * SUBLANES, LANES), F32),
            jax.ShapeDtypeStruct((B, n, E), F32),
        ],
        compiler_params=_cparams(("parallel", "parallel")),
        name="outproj_norm_router",
    )(*parts, w_out, x, g1.reshape(B, 1, D), n2.reshape(1, D), scale2.reshape(B, 1, D), shift2.reshape(B, 1, D),
      w_router)


MOE_ROW_CHUNK = 512
MOE_ISSUE_UNROLL = 8


def _moe_ffn_kernel(*refs, n_lat, n_ctx, n_batch):
    refs = list(refs)
    idx_ref, h_hbm = refs[:2]
    refs = refs[2:]
    if n_ctx:
        hc_hbm = refs.pop(0)
    wg_ref, wu_ref, wd_ref, gate_ref, g2_ref = refs[:5]
    refs = refs[5:]
    if n_ctx:
        cg2_ref = refs.pop(0)
    yl_ref = refs.pop(0)
    if n_ctx:
        yc_ref = refs.pop(0)
    xf, xb, sem = refs
    e = pl.program_id(0)
    f = pl.program_id(1)
    n_exp = pl.num_programs(0)
    last = pl.num_programs(1) - 1
    T = n_lat + n_ctx

    R = SUBLANES

    def start_rows(src, expert, first, count):
        def body(i, carry):
            for u in range(MOE_ISSUE_UNROLL):
                s = first + i * MOE_ISSUE_UNROLL + u
                r = pl.multiple_of(idx_ref[expert * T + s], R)
                pltpu.make_async_copy(src.at[pl.ds(r, R)], xf.at[pl.ds(pl.multiple_of(s * R, R), R)],
                                      sem.at[0]).start()
            return carry

        lax.fori_loop(0, count // MOE_ISSUE_UNROLL, body, 0)

    def start_gather(expert):
        start_rows(h_hbm, expert, 0, n_lat)
        if n_ctx:
            start_rows(hc_hbm, expert, n_lat, n_ctx)

    @pl.when(f == 0)
    def _():
        @pl.when(e == 0)
        def _():
            start_gather(e)

        pltpu.make_async_copy(h_hbm.at[pl.ds(0, T * R)], xf, sem.at[0]).wait()
        for j in range(R):
            xb[:, j * LANES:(j + 1) * LANES] = xf[pl.ds(j, T, stride=R), :].astype(BF16)

        @pl.when(e + 1 < n_exp)
        def _():
            start_gather(e + 1)

        yl_ref[...] = jnp.zeros_like(yl_ref)
        if n_ctx:
            yc_ref[...] = jnp.zeros_like(yc_ref)

    wg = wg_ref[0, 0].astype(BF16)
    wu = wu_ref[0, 0].astype(BF16)
    wd = wd_ref[0, 0].astype(BF16)

    def ffn(x):
        hg = _dot(x, wg)
        hu = _dot(x, wu)
        return _dot((hg * jax.nn.sigmoid(hg) * hu).astype(BF16), wd)

    def lat_rows(first, count, j):
        return pl.ds(first * R + j, count, stride=R)

    for i in range(n_lat // MOE_ROW_CHUNK):
        y = ffn(xb[i * MOE_ROW_CHUNK:(i + 1) * MOE_ROW_CHUNK, :])
        for j in range(R):
            yl_ref[lat_rows(i * MOE_ROW_CHUNK, MOE_ROW_CHUNK, j), :] += y[:, j * LANES:(j + 1) * LANES]
    if n_ctx:
        yc_ref[0] += ffn(xb[n_lat:T, :])

    @pl.when(f == last)
    def _():
        per_l = n_lat // n_batch
        for b in range(n_batch):
            gate = gate_ref[0, b * per_l:(b + 1) * per_l, :]
            for j in range(R):
                rows = lat_rows(b * per_l, per_l, j)
                yl_ref[rows, :] = yl_ref[rows, :] * gate * g2_ref[b:b + 1, j * LANES:(j + 1) * LANES]
        if n_ctx:
            per_c = n_ctx // n_batch
            for b in range(n_batch):
                rows = slice(b * per_c, (b + 1) * per_c)
                yc_ref[0, rows, :] = (yc_ref[0, rows, :] * gate_ref[0, n_lat + b * per_c:n_lat + (b + 1) * per_c, :]
                                      * cg2_ref[b:b + 1, :])


def moe_ffn(idx, h2, hc2, w_gate, w_up, w_down, layer, gates, g2, cg2, n_lat, tf):
    E, T, _ = gates.shape
    n_ctx = T - n_lat
    D = SUBLANES * LANES
    F = w_gate.shape[3]
    B = g2.shape[0]
    any_spec = pl.BlockSpec(memory_space=pl.ANY)
    per_e = lambda e, f, idx_ref: (e, 0, 0)
    full = lambda e, f, idx_ref: (0, 0)
    args = [h2] + ([hc2] if n_ctx else []) + [w_gate, w_up, w_down, gates, g2] + ([cg2] if n_ctx else [])
    specs = [any_spec] + ([any_spec] if n_ctx else []) + [
        pl.BlockSpec((1, 1, D, tf), lambda e, f, idx_ref: (layer, e, 0, f)),
        pl.BlockSpec((1, 1, D, tf), lambda e, f, idx_ref: (layer, e, 0, f)),
        pl.BlockSpec((1, 1, tf, D), lambda e, f, idx_ref: (layer, e, f, 0)),
        pl.BlockSpec((1, T, 1), per_e),
        pl.BlockSpec((B, D), full),
    ] + ([pl.BlockSpec((B, D), full)] if n_ctx else [])
    out_specs = [pl.BlockSpec((n_lat * SUBLANES, LANES), lambda e, f, idx_ref: (e, 0))] + (
        [pl.BlockSpec((1, n_ctx, D), per_e)] if n_ctx else [])
    out_shape = [jax.ShapeDtypeStruct((E * n_lat * SUBLANES, LANES), F32)] + (
        [jax.ShapeDtypeStruct((E, n_ctx, D), F32)] if n_ctx else [])
    return pl.pallas_call(
        functools.partial(_moe_ffn_kernel, n_lat=n_lat, n_ctx=n_ctx, n_batch=B),
        grid_spec=pltpu.PrefetchScalarGridSpec(
            num_scalar_prefetch=1,
            grid=(E, F // tf),
            in_specs=specs,
            out_specs=out_specs,
            scratch_shapes=[pltpu.VMEM((T * SUBLANES, LANES), F32), pltpu.VMEM((T, D), BF16),
                            pltpu.SemaphoreType.DMA((1,))],
        ),
        out_shape=out_shape,
        compiler_params=_cparams(("arbitrary", "arbitrary")),
        name="moe_ffn",
    )(idx, *args)


COMBINE_TILE = 128
COMBINE_ISSUE_UNROLL = 8


def _combine_kernel(src_ref, dst_ref, bnd_ref, kmax_ref, y_hbm, x_ref, o_ref, planes, sem, *, tm):
    i = pl.program_id(0)
    nt = pl.num_programs(0)
    R = SUBLANES
    U = COMBINE_ISSUE_UNROLL
    slab = tm * R

    def groups(t):
        return (bnd_ref[t + 1] - bnd_ref[t] + U - 1) // U

    def prepare(t, buf):
        def zero(k, carry):
            planes[buf, pl.ds(pl.multiple_of(k * slab, slab), slab), :] = jnp.zeros((slab, LANES), F32)
            return carry

        lax.fori_loop(0, kmax_ref[t], zero, 0)
        first = bnd_ref[t]
        final = bnd_ref[t + 1] - 1

        def issue(g, carry):
            for u in range(U):
                a = first + g * U + u
                live = a <= final
                a = jnp.minimum(a, final)
                s = pl.multiple_of(src_ref[a], R)
                d = pl.multiple_of(jnp.where(live, dst_ref[a], N_EXPERTS * slab + u * R), R)
                pltpu.make_async_copy(y_hbm.at[pl.ds(s, R)], planes.at[buf, pl.ds(d, R)], sem.at[buf]).start()
            return carry

        lax.fori_loop(0, groups(t), issue, 0)

    @pl.when(i == 0)
    def _():
        prepare(0, 0)

    @pl.when(i + 1 < nt)
    def _():
        prepare(i + 1, (i + 1) % 2)

    buf = i % 2
    n_groups = groups(i)
    p = 1
    while p * U <= tm * N_EXPERTS:
        @pl.when((n_groups & p) != 0)
        def _(p=p):
            pltpu.make_async_copy(y_hbm.at[pl.ds(0, p * U * R)], planes.at[buf, pl.ds(0, p * U * R)],
                                  sem.at[buf]).wait()
        p *= 2

    kmax = kmax_ref[i]

    def add(k, carry):
        planes[buf, 0:slab, :] += planes[buf, pl.ds(pl.multiple_of(k * slab, slab), slab), :]
        return carry

    lax.fori_loop(1, kmax, add, 0)

    @pl.when(kmax > 0)
    def _():
        for j in range(R):
            o_ref[:, j * LANES:(j + 1) * LANES] = (x_ref[:, j * LANES:(j + 1) * LANES]
                                                   + planes[buf, pl.ds(j, tm, stride=R), :])

    @pl.when(kmax == 0)
    def _():
        o_ref[...] = x_ref[...]


def combine_expert_outputs(x, y_tiles, token_rows):
    n, D = x.shape
    A = token_rows.shape[0]
    tm = COMBINE_TILE
    nt = n // tm
    tok, src = lax.sort_key_val(token_rows, jnp.arange(A, dtype=jnp.int32))
    pos = jnp.arange(A, dtype=jnp.int32)
    is_start = jnp.concatenate([jnp.ones((1,), bool), tok[1:] != tok[:-1]])
    rank = pos - lax.cummax(jnp.where(is_start, pos, 0))
    tile = tok // tm
    dst = (rank * tm + tok % tm) * SUBLANES
    edges = jnp.arange(nt + 1, dtype=jnp.int32) * tm
    bounds = jnp.sum(tok[None, :] < edges[:, None], axis=1).astype(jnp.int32)
    kmax = jnp.max(jnp.where(tile[None, :] == jnp.arange(nt, dtype=jnp.int32)[:, None], rank[None, :] + 1, 0),
                   axis=1).astype(jnp.int32)
    row = lambda i, *_: (i, 0)
    return pl.pallas_call(
        functools.partial(_combine_kernel, tm=tm),
        grid_spec=pltpu.PrefetchScalarGridSpec(
            num_scalar_prefetch=4,
            grid=(nt,),
            in_specs=[pl.BlockSpec(memory_space=pl.ANY), pl.BlockSpec((tm, D), row)],
            out_specs=pl.BlockSpec((tm, D), row),
            scratch_shapes=[
                pltpu.VMEM((2, (N_EXPERTS * tm + COMBINE_ISSUE_UNROLL) * SUBLANES, LANES), F32),
                pltpu.SemaphoreType.DMA((2,))],
        ),
        out_shape=jax.ShapeDtypeStruct((n, D), F32),
        compiler_params=_cparams(("arbitrary",)),
        name="combine",
    )(src * SUBLANES, dst, bounds, kmax, y_tiles, x)


def _final_norm_kernel(x_ref, g_ref, o_ref):
    x = x_ref[0]
    o_ref[0] = x * lax.rsqrt(jnp.mean(x * x, axis=-1, keepdims=True) + EPS) * g_ref[...]


def final_norm(x, g, tm):
    B, n, D = x.shape
    return pl.pallas_call(
        _final_norm_kernel,
        grid=(B, n // tm),
        in_specs=[pl.BlockSpec((1, tm, D), lambda b, i: (b, i, 0)), pl.BlockSpec((1, D), lambda b, i: (0, 0))],
        out_specs=pl.BlockSpec((1, tm, D), lambda b, i: (b, i, 0)),
        out_shape=jax.ShapeDtypeStruct((B, n, D), F32),
        compiler_params=_cparams(("parallel", "parallel")),
        name="final_norm",
    )(x, g.reshape(1, D))


def route(aff, cap):
    return lax.top_k(jnp.swapaxes(aff, 1, 2), cap)


TOKEN_TILE = 512


def kernel(x, c, ctx, c_ctx, w_ada, b_ada, norm1_g, norm2_g, w_in, w_out, gla_w_gate, gla_b_gate, gla_norm_g,
           gmlp_norm_g, gmlp_w_s, gmlp_b_s, lru_conv_w, lru_conv_b, lru_w_r, lru_b_r, lru_w_i, lru_b_i,
           lru_lambda, nat_rpb, moe_w_router, moe_w_gate, moe_w_up, moe_w_down, final_norm_g):
    B, N, D = x.shape
    M = ctx.shape[1]
    E = N_EXPERTS
    tm = TOKEN_TILE
    cap_l = max(1, EC_CAPACITY * N // E)
    cap_c = max(1, EC_CAPACITY * M // E)
    sc = jax.nn.silu(c)
    scc = jax.nn.silu(c_ctx)
    xc = ctx
    batch_ids = jnp.arange(B, dtype=jnp.int32)[:, None, None]
    tables = rope_tables(N)
    for l in range(DEPTH):
        need_ctx = l < DEPTH - 1
        mod = sc @ w_ada[l] + b_ada[l]
        mod_c = jnp.broadcast_to((scc @ w_ada[l] + b_ada[l])[None], (B, 6 * D))
        sh1, sc1, g1, sh2, sc2, g2 = jnp.split(mod, 6, axis=-1)
        csh1, csc1, cg1, csh2, csc2, cg2 = jnp.split(mod_c, 6, axis=-1)

        wp = permute_w_in(w_in[l])
        gla, gm, lru, nat, gate = norm_inproj(x, norm1_g[l], sc1, sh1, wp, tm)
        cgla, cgm, clru, cnat, cgate = norm_inproj(xc, norm1_g[l], csc1, csh1, wp, M)

        gla_o, gla_c = gla_mixer(gla, gate, cgla, cgate, gla_w_gate[l], gla_b_gate[l], gla_norm_g[l], tables,
                                 need_ctx, tm)
        gm_o = gmlp_mixer(gm, gmlp_norm_g[l], gmlp_w_s[l], gmlp_b_s[l], tm)
        lru_o, lru_c = lru_mixer(lru, clru, lru_conv_w[l], lru_conv_b[l], lru_w_r[l], lru_b_r[l], lru_w_i[l],
                                 lru_b_i[l], lru_lambda[l], need_ctx, tm)
        nat_o = nat_mixer(nat, cnat, nat_bias_tables(nat_rpb[l], N // GRID_W))

        wo = w_out[l].astype(BF16)
        wr = moe_w_router[l].astype(BF16)
        x, h2, aff = outproj_norm_router((gla_o, gm_o, lru_o, nat_o), wo, x, g1, norm2_g[l], sc2, sh2, wr, tm)
        gl, il = route(aff, cap_l)
        rows_l = jnp.swapaxes(il + batch_ids * N, 0, 1).reshape(E, B * cap_l)
        idx = rows_l
        gs = jnp.swapaxes(gl, 0, 1).reshape(E, B * cap_l)
        hc2 = None
        if need_ctx:
            gm_c = gmlp_mixer(cgm, gmlp_norm_g[l], gmlp_w_s[l], gmlp_b_s[l], M)
            nat_c = ctx_attention(cnat)
            xc, hc2, caff = outproj_norm_router((gla_c, gm_c, lru_c, nat_c), wo, xc, cg1, norm2_g[l], csc2, csh2,
                                                wr, M)
            gc, ic = route(caff, cap_c)
            rows_c = jnp.swapaxes(ic + batch_ids * M, 0, 1).reshape(E, B * cap_c)
            idx = jnp.concatenate([rows_l, rows_c], axis=1)
            gs = jnp.concatenate([gs, jnp.swapaxes(gc, 0, 1).reshape(E, B * cap_c)], axis=1)
        ys = moe_ffn((idx * SUBLANES).reshape(-1).astype(jnp.int32), h2, hc2, moe_w_gate, moe_w_up, moe_w_down, l,
                     gs[..., None], g2, cg2, B * cap_l, tf=512)
        x = combine_expert_outputs(x.reshape(B * N, D), ys[0], rows_l.reshape(-1).astype(jnp.int32)).reshape(B, N, D)
        if need_ctx:
            xc = xc.reshape(B * M, D).at[rows_c.reshape(-1)].add(ys[1].reshape(-1, D)).reshape(B, M, D)
    return final_norm(x, final_norm_g, tm)
```

```python
import functools

import numpy as np
import jax
import jax.numpy as jnp
from jax import lax
from jax.experimental import pallas as pl
from jax.experimental.pallas import tpu as pltpu

D_MODEL = 1024
DEPTH = 4
GRID_W = 64
N_GROUPS = 4
GROUP_W = D_MODEL // N_GROUPS
GLA_HEADS = 4
GLA_DK = GROUP_W // GLA_HEADS
GLA_W = GLA_HEADS * GLA_DK
GLA_GATE_RANK = 16
GLA_TAU = 16.0
GLA_CHUNK = 64
GMLP_GROUPS = 4
GMLP_W = GROUP_W
GMLP_CHUNK = 128
LRU_W = GROUP_W
LRU_BLOCKS = 4
LRU_C = 8.0
CONV_W = 4
NAT_HEADS = 4
NAT_DH = GROUP_W // NAT_HEADS
NAT_W = NAT_HEADS * NAT_DH
NAT_KR_MAX = 8
NAT_KC = 16
N_EXPERTS = 16
EXPERT_FF = 2 * D_MODEL
EC_CAPACITY = 2
ROPE_BASE = 10000.0
EPS = 1e-6

IN_SIZES = (GLA_W, GLA_W, GLA_W, GLA_W, GLA_GATE_RANK, GLA_GATE_RANK,
            GMLP_W, GMLP_W, LRU_W, LRU_W, NAT_W, NAT_W, NAT_W)
IN_COLS = sum(IN_SIZES)

V7X_VMEM_LIMIT_BYTES = 56 * 1024 * 1024
LANES = 128
SUBLANES = 8
F32 = jnp.float32
BF16 = jnp.bfloat16
NEG_BIG = -1e30
GATE_PAD = LANES

NT_DIMS = (((1,), (1,)), ((), ()))
TN_DIMS = (((0,), (0,)), ((), ()))


def _cparams(sem):
    return pltpu.CompilerParams(dimension_semantics=sem, vmem_limit_bytes=V7X_VMEM_LIMIT_BYTES)


def _dot(a, b):
    return jnp.dot(a, b, preferred_element_type=F32)


def _split3(x):
    hi = x.astype(BF16)
    r1 = x - hi.astype(F32)
    mid = r1.astype(BF16)
    lo = (r1 - mid.astype(F32)).astype(BF16)
    return hi, mid, lo


def _head_block_mask(n):
    r = lax.broadcasted_iota(jnp.int32, (n, n), 0) // GLA_DK
    c = lax.broadcasted_iota(jnp.int32, (n, n), 1) // GLA_DK
    return r == c


def _axial_rope(u, cos, sin):
    lane = lax.broadcasted_iota(jnp.int32, u.shape, 1)
    first = (lane % (GLA_DK // 2)) < (GLA_DK // 4)
    partner = jnp.where(first, pltpu.roll(u, GLA_W - GLA_DK // 4, axis=1), pltpu.roll(u, GLA_DK // 4, axis=1))
    return u * cos + partner * sin


def _norm_inproj_kernel(*refs, rope):
    refs = list(refs)
    x_ref, g_ref, sc_ref, sh_ref, w_ref = refs[:5]
    refs = refs[5:]
    if rope:
        cos_ref, sin_ref = refs[:2]
        refs = refs[2:]
    gla_ref, gmlp_ref, lru_ref, nat_ref, gate_ref = refs
    x = x_ref[0]
    y = x * lax.rsqrt(jnp.mean(x * x, axis=-1, keepdims=True) + EPS)
    h = ((y * g_ref[...]) * (1.0 + sc_ref[0]) + sh_ref[0]).astype(BF16)
    o = 0
    for ref in (gla_ref, gmlp_ref, lru_ref, nat_ref, gate_ref):
        w = ref.shape[-1]
        ref[0] = _dot(h, w_ref[:, o:o + w]).astype(ref.dtype)
        o += w
    if rope:
        for c in range(2):
            cols = slice(c * GLA_W, (c + 1) * GLA_W)
            gla_ref[0, :, cols] = _axial_rope(gla_ref[0, :, cols], cos_ref[...], sin_ref[...])


def permute_w_in(w):
    g0 = 4 * GLA_W
    g1 = g0 + 2 * GLA_GATE_RANK
    pad = jnp.zeros((w.shape[0], GATE_PAD - 2 * GLA_GATE_RANK), w.dtype)
    return jnp.concatenate([w[:, :g0], w[:, g1:], w[:, g0:g1], pad], axis=1).astype(BF16)


def norm_inproj(x, g, scale, shift, wp, tm, tables=None):
    B, n, D = x.shape
    C = wp.shape[1]
    widths = (4 * GLA_W, 2 * GMLP_W, 2 * LRU_W, 3 * NAT_W, GATE_PAD)
    dtypes = (F32, F32, F32, BF16, F32)
    row = lambda b, i: (b, i, 0)
    per_b = lambda b, i: (b, 0, 0)
    full = lambda b, i: (0, 0)
    rope_specs = [pl.BlockSpec((tm, GLA_W), lambda b, i: (i, 0))] * 2 if tables is not None else []
    return pl.pallas_call(
        functools.partial(_norm_inproj_kernel, rope=tables is not None),
        grid=(B, n // tm),
        in_specs=[
            pl.BlockSpec((1, tm, D), row),
            pl.BlockSpec((1, D), full),
            pl.BlockSpec((1, 1, D), per_b),
            pl.BlockSpec((1, 1, D), per_b),
            pl.BlockSpec((D, C), full),
        ] + rope_specs,
        out_specs=[pl.BlockSpec((1, tm, w), row) for w in widths],
        out_shape=[jax.ShapeDtypeStruct((B, n, w), dt) for w, dt in zip(widths, dtypes)],
        compiler_params=_cparams(("parallel", "parallel")),
        name="norm_inproj",
    )(x, g.reshape(1, D), scale.reshape(B, 1, D), shift.reshape(B, 1, D), wp, *(tables or ()))


def _gla_kernel(*refs, tm, reverse, combine):
    refs = list(refs)
    q_ref, k_ref, v_ref, gate_ref, wg_ref, bg_ref, tri_ref, ones_ref, s0_ref = refs[:9]
    refs = refs[9:]
    if combine:
        ob_ref, og_ref, ng_ref = refs[:3]
        refs = refs[3:]
    o_ref, sfin_ref, st, obuf = refs
    i = pl.program_id(1)
    nch = tm // GLA_CHUNK

    @pl.when(i == 0)
    def _():
        st[...] = s0_ref[0]

    q = q_ref[0] * (GLA_DK ** -0.5)
    k = k_ref[0]
    a = _dot(gate_ref[0].astype(BF16), wg_ref[...]) + bg_ref[...]
    la = jax.nn.log_sigmoid(a) / GLA_TAU
    pieces = _split3(la)
    b = sum(_dot(tri_ref[...], p) for p in pieces)
    bl = sum(_dot(ones_ref[...], p) for p in pieces)
    q_in = (q * jnp.exp(b)).astype(BF16)
    k_in = (k * jnp.exp(-b)).astype(BF16)
    k_end = (k * jnp.exp(bl - b)).astype(BF16)
    dec = jnp.exp(bl)
    vb = v_ref[0].astype(BF16)

    head_mask = _head_block_mask(GLA_W)
    cr = lax.broadcasted_iota(jnp.int32, (GLA_CHUNK, GLA_W), 0)
    cs = lax.broadcasted_iota(jnp.int32, (GLA_CHUNK, GLA_W), 1) % GLA_CHUNK
    causal = (cs >= cr) if reverse else (cs <= cr)
    zero = jnp.zeros((), BF16)
    chunks = range(nch - 1, -1, -1) if reverse else range(nch)
    for c in chunks:
        sl = slice(c * GLA_CHUNK, (c + 1) * GLA_CHUNK)
        kbd_t = jnp.where(head_mask, jnp.concatenate([k_in[sl]] * GLA_HEADS, axis=0), zero)
        vbd = jnp.where(head_mask, jnp.concatenate([vb[sl]] * GLA_HEADS, axis=0), zero)
        att = lax.dot_general(q_in[sl], kbd_t, NT_DIMS, preferred_element_type=F32)
        att = jnp.where(causal, att, 0.0).astype(BF16)
        s_t = st[...]
        o = _dot(att, vbd) + lax.dot_general(q_in[sl], s_t.astype(BF16), NT_DIMS, preferred_element_type=F32)
        upd = lax.dot_general(vb[sl], k_end[sl], TN_DIMS, preferred_element_type=F32)
        st[...] = s_t * dec[c * GLA_CHUNK:c * GLA_CHUNK + 1] + jnp.where(head_mask, upd, 0.0)
        obuf[sl, :] = o

    sfin_ref[0] = st[...]

    if not combine:
        o_ref[0] = obuf[...]
    else:
        o = obuf[...] + ob_ref[0]
        sq_hi, sq_mid, _ = _split3(o * o)
        avg = jnp.where(head_mask, 1.0 / GLA_DK, 0.0).astype(BF16)
        ms = _dot(sq_hi, avg) + _dot(sq_mid, avg)
        og = og_ref[0]
        o_ref[0] = (o * lax.rsqrt(ms + EPS) * ng_ref[...] * (og * jax.nn.sigmoid(og))).astype(o_ref.dtype)


def _chunk_sum_matrices(tm, reverse):
    r = np.arange(tm)[:, None]
    c = np.arange(tm)[None, :]
    same = (r // GLA_CHUNK) == (c // GLA_CHUNK)
    tri = same & ((c >= r) if reverse else (c <= r))
    return jnp.asarray(tri, BF16), jnp.asarray(same, BF16)


def gla_direction(gla, gate, wg, bg, s0, tm, reverse, combine=None):
    B, L, _ = gla.shape
    nt = L // tm
    W = GLA_W
    blk = (lambda i: nt - 1 - i) if reverse else (lambda i: i)
    col = lambda j: (lambda b, i: (b, blk(i), j))
    full = lambda b, i: (0, 0)
    tri, ones = _chunk_sum_matrices(tm, reverse)
    args = [gla, gla, gla, gate, wg, bg, tri, ones, s0]
    specs = [
        pl.BlockSpec((1, tm, W), col(0)), pl.BlockSpec((1, tm, W), col(1)), pl.BlockSpec((1, tm, W), col(2)),
        pl.BlockSpec((1, tm, GATE_PAD), col(0)),
        pl.BlockSpec((GATE_PAD, W), full), pl.BlockSpec((1, W), full),
        pl.BlockSpec((tm, tm), full), pl.BlockSpec((tm, tm), full),
        pl.BlockSpec((1, W, W), lambda b, i: (b, 0, 0)),
    ]
    if combine is not None:
        ob, norm_g = combine
        args += [ob, gla, norm_g]
        specs += [pl.BlockSpec((1, tm, W), col(0)), pl.BlockSpec((1, tm, W), col(3)), pl.BlockSpec((1, W), full)]
    return pl.pallas_call(
        functools.partial(_gla_kernel, tm=tm, reverse=reverse, combine=combine is not None),
        grid=(B, nt),
        in_specs=specs,
        out_specs=[pl.BlockSpec((1, tm, W), col(0)), pl.BlockSpec((1, W, W), lambda b, i: (b, 0, 0))],
        out_shape=[jax.ShapeDtypeStruct((B, L, W), F32 if combine is None else BF16),
                   jax.ShapeDtypeStruct((B, W, W), F32)],
        scratch_shapes=[pltpu.VMEM((W, W), F32), pltpu.VMEM((tm, W), F32)],
        compiler_params=_cparams(("parallel", "arbitrary")),
        name="gla_bwd" if reverse else "gla_fwd",
    )(*args)


def rope_tables(n):
    quarter = GLA_DK // 4
    pos = jnp.arange(n)
    inv = ROPE_BASE ** (-jnp.arange(quarter, dtype=F32) / quarter)
    row = (pos // GRID_W).astype(F32)
    colp = (pos % GRID_W).astype(F32)
    ar = row[:, None] * inv[None, :]
    ac = colp[:, None] * inv[None, :]
    cos = jnp.concatenate([jnp.cos(ar), jnp.cos(ar), jnp.cos(ac), jnp.cos(ac)], axis=1)
    sin = jnp.concatenate([-jnp.sin(ar), jnp.sin(ar), -jnp.sin(ac), jnp.sin(ac)], axis=1)
    return jnp.tile(cos, (1, GLA_HEADS)), jnp.tile(sin, (1, GLA_HEADS))


def gla_mixer(gla, gate, cgla, cgate, w_gate, b_gate, norm_g, need_ctx, tm):
    B, M, _ = cgla.shape
    zeros = jnp.zeros((B, GLA_W, GLA_W), F32)
    ng = jnp.tile(norm_g, GLA_HEADS).reshape(1, GLA_W)
    wgs, bgs = [], []
    for d in range(2):
        wg = jnp.zeros((GATE_PAD, GLA_W), F32).at[d * GLA_GATE_RANK:(d + 1) * GLA_GATE_RANK].set(w_gate[d])
        wgs.append(wg.astype(BF16))
        bgs.append(b_gate[d].reshape(1, GLA_W))
    ocb, sb = gla_direction(cgla, cgate, wgs[1], bgs[1], zeros, M, True)
    ob, _ = gla_direction(gla, gate, wgs[1], bgs[1], sb, tm, True)
    if need_ctx:
        oc, sf = gla_direction(cgla, cgate, wgs[0], bgs[0], zeros, M, False, combine=(ocb, ng))
    else:
        oc, sf = gla_direction(cgla, cgate, wgs[0], bgs[0], zeros, M, False)
    o, _ = gla_direction(gla, gate, wgs[0], bgs[0], sf, tm, False, combine=(ob, ng))
    return o, oc


def _gmlp_kernel(u_ref, v_ref, g_ref, w_ref, b_ref, o_ref, *, tm):
    v = v_ref[0]
    vn = (v * lax.rsqrt(jnp.mean(v * v, axis=-1, keepdims=True) + EPS) * g_ref[...]).astype(BF16)
    grp = lax.broadcasted_iota(jnp.int32, (GMLP_CHUNK, GMLP_W), 1) // (GMLP_W // GMLP_GROUPS)
    zero = jnp.zeros((), BF16)
    for c in range(tm // GMLP_CHUNK):
        sl = slice(c * GMLP_CHUNK, (c + 1) * GMLP_CHUNK)
        mixed = b_ref[...]
        for g in range(GMLP_GROUPS):
            mixed = mixed + _dot(w_ref[g], jnp.where(grp == g, vn[sl], zero))
        o_ref[0, sl, :] = (u_ref[0, sl, :] * mixed).astype(o_ref.dtype)


def gmlp_mixer(gm, norm_g, w_s, b_s, tm):
    B, L, _ = gm.shape
    W = GMLP_W
    bias = jnp.repeat(b_s.T, W // GMLP_GROUPS, axis=1)
    return pl.pallas_call(
        functools.partial(_gmlp_kernel, tm=tm),
        grid=(B, L // tm),
        in_specs=[
            pl.BlockSpec((1, tm, W), lambda b, i: (b, i, 0)),
            pl.BlockSpec((1, tm, W), lambda b, i: (b, i, 1)),
            pl.BlockSpec((1, W), lambda b, i: (0, 0)),
            pl.BlockSpec((GMLP_GROUPS, GMLP_CHUNK, GMLP_CHUNK), lambda b, i: (0, 0, 0)),
            pl.BlockSpec((GMLP_CHUNK, W), lambda b, i: (0, 0)),
        ],
        out_specs=pl.BlockSpec((1, tm, W), lambda b, i: (b, i, 0)),
        out_shape=jax.ShapeDtypeStruct((B, L, W), BF16),
        compiler_params=_cparams(("parallel", "parallel")),
        name="gmlp",
    )(gm, gm, norm_g.reshape(1, W), w_s.astype(BF16), bias)


LRU_UNROLL = 8


def _lru_kernel(*refs, tm, reverse, combine):
    refs = list(refs)
    x_ref, xp_ref, xn_ref, cw_ref, cb_ref, wr_ref, br_ref, wi_ref, bi_ref, ncs_ref, h0_ref = refs[:11]
    refs = refs[11:]
    if combine:
        hb_ref, ly_ref = refs[:2]
        refs = refs[2:]
    o_ref, ext, a_s, b_s, carry = refs
    W = LRU_W
    i = pl.program_id(1)
    nt = pl.num_programs(1)
    t = (nt - 1 - i) if reverse else i
    H = SUBLANES

    ext[0:H] = jnp.where(t > 0, xp_ref[0], 0.0)
    ext[H:H + tm] = x_ref[0]
    ext[H + tm:2 * H + tm] = jnp.where(t < nt - 1, xn_ref[0], 0.0)
    xc = cb_ref[...]
    for tap in range(CONV_W):
        xc = xc + cw_ref[tap:tap + 1, :] * ext[H - CONV_W // 2 + tap:H - CONV_W // 2 + tap + tm]
    xcb = xc.astype(BF16)
    r = jax.nn.sigmoid(_dot(xcb, wr_ref[...]) + br_ref[...])
    ig = jax.nn.sigmoid(_dot(xcb, wi_ref[...]) + bi_ref[...])
    log_a = ncs_ref[...] * r
    a = jnp.exp(log_a)
    b = jnp.sqrt(-jnp.tanh(log_a) * (a * a + 1.0)) * (ig * xc)

    rowi = lax.broadcasted_iota(jnp.int32, (tm, W), 0) % H
    for s in (1, 2, 4):
        if reverse:
            ok = rowi < H - s
            a_sh = pltpu.roll(a, tm - s, axis=0)
            b_sh = pltpu.roll(b, tm - s, axis=0)
        else:
            ok = rowi >= s
            a_sh = pltpu.roll(a, s, axis=0)
            b_sh = pltpu.roll(b, s, axis=0)
        b = a * jnp.where(ok, b_sh, 0.0) + b
        a = a * jnp.where(ok, a_sh, 1.0)
    a_s[...] = a
    b_s[...] = b

    @pl.when(i == 0)
    def _():
        carry[...] = jnp.broadcast_to(h0_ref[0], (H, W))

    ng = tm // H

    def body(j, h):
        for u in range(LRU_UNROLL):
            g = j * LRU_UNROLL + u
            g = (ng - 1 - g) if reverse else g
            rows = pl.ds(pl.multiple_of(g * H, H), H)
            hg = b_s[rows, :] + a_s[rows, :] * h
            b_s[rows, :] = hg
            h = jnp.broadcast_to(hg[0:1] if reverse else hg[H - 1:H], (H, W))
        return h

    carry[...] = lax.fori_loop(0, ng // LRU_UNROLL, body, carry[...])

    if combine:
        o_ref[0] = ((b_s[...] + hb_ref[0]) * jax.nn.gelu(ly_ref[0])).astype(o_ref.dtype)
    else:
        o_ref[0] = b_s[...]


def lru_direction(lru, conv_w, conv_b, wr, br, wi, bi, ncs, h0, tm, reverse, hb=None):
    B, L, _ = lru.shape
    nt = L // tm
    W = LRU_W
    H = SUBLANES
    nh = L // H
    per = tm // H
    blk = (lambda i: nt - 1 - i) if reverse else (lambda i: i)
    full = lambda b, i: (0, 0)
    args = [lru, lru, lru, conv_w, conv_b, wr, br, wi, bi, ncs, h0]
    specs = [
        pl.BlockSpec((1, tm, W), lambda b, i: (b, blk(i), 0)),
        pl.BlockSpec((1, H, W), lambda b, i: (b, jnp.maximum(blk(i) * per - 1, 0), 0)),
        pl.BlockSpec((1, H, W), lambda b, i: (b, jnp.minimum((blk(i) + 1) * per, nh - 1), 0)),
        pl.BlockSpec((CONV_W, W), full), pl.BlockSpec((1, W), full),
        pl.BlockSpec((W, W), full), pl.BlockSpec((1, W), full),
        pl.BlockSpec((W, W), full), pl.BlockSpec((1, W), full),
        pl.BlockSpec((1, W), full),
        pl.BlockSpec((1, 1, W), lambda b, i: (b, 0, 0)),
    ]
    if hb is not None:
        args += [hb, lru]
        specs += [pl.BlockSpec((1, tm, W), lambda b, i: (b, blk(i), 0)),
                  pl.BlockSpec((1, tm, W), lambda b, i: (b, blk(i), 1))]
    return pl.pallas_call(
        functools.partial(_lru_kernel, tm=tm, reverse=reverse, combine=hb is not None),
        grid=(B, nt),
        in_specs=specs,
        out_specs=pl.BlockSpec((1, tm, W), lambda b, i: (b, blk(i), 0)),
        out_shape=jax.ShapeDtypeStruct((B, L, W), F32 if hb is None else BF16),
        scratch_shapes=[pltpu.VMEM((tm + 2 * H, W), F32), pltpu.VMEM((tm, W), F32), pltpu.VMEM((tm, W), F32),
                        pltpu.VMEM((H, W), F32)],
        compiler_params=_cparams(("parallel", "arbitrary")),
        name="lru_bwd" if reverse else "lru_fwd",
    )(*args)


def _block_diag(w):
    G, n, _ = w.shape
    out = jnp.zeros((G * n, G * n), w.dtype)
    for g in range(G):
        out = out.at[g * n:(g + 1) * n, g * n:(g + 1) * n].set(w[g])
    return out


def lru_mixer(lru, clru, conv_w, conv_b, w_r, b_r, w_i, b_i, lam, need_ctx, tm):
    B, M, _ = clru.shape
    W = LRU_W
    cb = conv_b.reshape(1, W)
    ncs = -LRU_C * jax.nn.softplus(-lam.astype(F32))
    p = [(_block_diag(w_r[d]).astype(BF16), b_r[d].reshape(1, W), _block_diag(w_i[d]).astype(BF16),
          b_i[d].reshape(1, W), ncs[d].reshape(1, W)) for d in range(2)]
    zeros = jnp.zeros((B, 1, W), F32)
    hcb = lru_direction(clru, conv_w, cb, *p[1], zeros, M, True)
    hb = lru_direction(lru, conv_w, cb, *p[1], hcb[:, 0:1], tm, True)
    hcf = lru_direction(clru, conv_w, cb, *p[0], zeros, M, False)
    out = lru_direction(lru, conv_w, cb, *p[0], hcf[:, M - 1:M], tm, False, hb=hb)
    out_c = None
    if need_ctx:
        out_c = lru_direction(clru, conv_w, cb, *p[0], zeros, M, False, hb=hcb)
    return out, out_c


NAT_QROWS = 4
NAT_KROWS = 3 * NAT_QROWS
NAT_TQ = NAT_QROWS * GRID_W
NAT_TK = NAT_KROWS * GRID_W


def _softmax_pv(s_loc, s_ctx, v_loc, v_ctx):
    m = jnp.maximum(jnp.max(s_loc, axis=-1, keepdims=True), jnp.max(s_ctx, axis=-1, keepdims=True))
    e_loc = jnp.exp(s_loc - m)
    e_ctx = jnp.exp(s_ctx - m)
    den = jnp.sum(e_loc, axis=-1, keepdims=True) + jnp.sum(e_ctx, axis=-1, keepdims=True)
    o = _dot(e_loc.astype(BF16), v_loc) + _dot(e_ctx.astype(BF16), v_ctx)
    return o / den


def _nat_kernel(q_ref, kp_ref, kc_ref, kn_ref, vp_ref, vc_ref, vn_ref, ck_ref, cv_ref, bias_ref, o_ref):
    q = q_ref[0]
    ck = ck_ref[0]
    cv = cv_ref[0]
    scale = NAT_DH ** -0.5
    outs = []
    for h in range(NAT_HEADS):
        hs = slice(h * NAT_DH, (h + 1) * NAT_DH)
        qh = q[:, hs]
        s_loc = jnp.concatenate(
            [lax.dot_general(qh, k_ref[0][:, hs], NT_DIMS, preferred_element_type=F32)
             for k_ref in (kp_ref, kc_ref, kn_ref)], axis=-1) * scale + bias_ref[0, h]
        s_ctx = lax.dot_general(qh, ck[:, hs], NT_DIMS, preferred_element_type=F32) * scale
        v_loc = jnp.concatenate([v_ref[0][:, hs] for v_ref in (vp_ref, vc_ref, vn_ref)], axis=0)
        outs.append(_softmax_pv(s_loc, s_ctx, v_loc, cv[:, hs]))
    o_ref[0] = jnp.concatenate(outs, axis=-1).astype(o_ref.dtype)


def nat_bias_tables(rpb, rows):
    kr = NAT_KR_MAX
    qc = np.arange(GRID_W)
    kcol = np.arange(GRID_W)
    cs = np.clip(qc - NAT_KC // 2, 0, GRID_W - NAT_KC)
    valid_c = (kcol[None, :] >= cs[:, None]) & (kcol[None, :] < cs[:, None] + NAT_KC)
    edge = GRID_W - NAT_KC
    padded = jnp.pad(rpb.astype(F32), ((0, 0), (0, 0), (edge, edge)), mode="edge")
    by_col = jnp.stack([padded[:, :, GRID_W - 1 - q:2 * GRID_W - 1 - q] for q in range(GRID_W)], axis=2)
    by_col = jnp.pad(by_col, ((0, 0), (NAT_KROWS, NAT_KROWS), (0, 0), (0, 0)))
    tables = []
    for r0, ks in ((0, 0), (NAT_QROWS, NAT_QROWS - kr // 2), (rows - NAT_QROWS, rows - NAT_KROWS)):
        r = r0 + np.arange(NAT_QROWS)
        krow = ks + np.arange(NAT_KROWS)
        rs = np.clip(r - kr // 2, 0, rows - kr)
        valid_r = (krow[None, :] >= rs[:, None]) & (krow[None, :] < rs[:, None] + kr)
        starts = ks - r + NAT_KR_MAX - 1 + NAT_KROWS
        bias = jnp.stack([by_col[:, int(s):int(s) + NAT_KROWS] for s in starts], axis=1)
        bias = bias.transpose(0, 1, 3, 2, 4)
        mask = valid_r[:, None, :, None] & valid_c[None, :, None, :]
        bias = jnp.where(jnp.asarray(mask)[None], bias, NEG_BIG)
        tables.append(bias.reshape(rpb.shape[0], NAT_TQ, NAT_TK))
    return jnp.stack(tables)


def nat_mixer(nat, cnat, bias):
    B, N, _ = nat.shape
    M = cnat.shape[1]
    W = NAT_W
    nb = N // NAT_TQ
    T = NAT_TQ
    centre = lambda j: jnp.clip(j, 1, nb - 2)
    near = lambda c, d: (lambda b, j: (b, centre(j) + d, c))
    variant = lambda b, j: (jnp.where(j == 0, 0, jnp.where(j == nb - 1, 2, 1)), 0, 0, 0)
    return pl.pallas_call(
        _nat_kernel,
        grid=(B, nb),
        in_specs=[
            pl.BlockSpec((1, T, W), lambda b, j: (b, j, 0)),
            pl.BlockSpec((1, T, W), near(1, -1)), pl.BlockSpec((1, T, W), near(1, 0)), pl.BlockSpec((1, T, W), near(1, 1)),
            pl.BlockSpec((1, T, W), near(2, -1)), pl.BlockSpec((1, T, W), near(2, 0)), pl.BlockSpec((1, T, W), near(2, 1)),
            pl.BlockSpec((1, M, W), lambda b, j: (b, 0, 1)),
            pl.BlockSpec((1, M, W), lambda b, j: (b, 0, 2)),
            pl.BlockSpec((1, NAT_HEADS, NAT_TQ, NAT_TK), variant),
        ],
        out_specs=pl.BlockSpec((1, T, W), lambda b, j: (b, j, 0)),
        out_shape=jax.ShapeDtypeStruct((B, N, W), BF16),
        compiler_params=_cparams(("parallel", "arbitrary")),
        name="nat",
    )(nat, nat, nat, nat, nat, nat, nat, cnat, cnat, bias)


def _ctx_attn_kernel(q_ref, k_ref, v_ref, o_ref):
    q = q_ref[0]
    k = k_ref[0]
    v = v_ref[0]
    scale = NAT_DH ** -0.5
    outs = []
    for h in range(NAT_HEADS):
        hs = slice(h * NAT_DH, (h + 1) * NAT_DH)
        s = lax.dot_general(q[:, hs], k[:, hs], NT_DIMS, preferred_element_type=F32) * scale
        e = jnp.exp(s - jnp.max(s, axis=-1, keepdims=True))
        outs.append(_dot(e.astype(BF16), v[:, hs]) / jnp.sum(e, axis=-1, keepdims=True))
    o_ref[0] = jnp.concatenate(outs, axis=-1).astype(o_ref.dtype)


def ctx_attention(cnat):
    B, M, _ = cnat.shape
    W = NAT_W
    return pl.pallas_call(
        _ctx_attn_kernel,
        grid=(B,),
        in_specs=[pl.BlockSpec((1, M, W), lambda b, c=c: (b, 0, c)) for c in range(3)],
        out_specs=pl.BlockSpec((1, M, W), lambda b: (b, 0, 0)),
        out_shape=jax.ShapeDtypeStruct((B, M, W), BF16),
        compiler_params=_cparams(("parallel",)),
        name="ctx_attn",
    )(cnat, cnat, cnat)


def _outproj_kernel(m0_ref, m1_ref, m2_ref, m3_ref, w_ref, x_ref, g1_ref, n2_ref, sc_ref, sh_ref, wr_ref,
                    xo_ref, h2_ref, aff_ref):
    y = 0.0
    for gi, m_ref in enumerate((m0_ref, m1_ref, m2_ref, m3_ref)):
        y = y + _dot(m_ref[0].astype(BF16), w_ref[gi * GROUP_W:(gi + 1) * GROUP_W, :])
    x = x_ref[0] + g1_ref[0] * y
    xo_ref[0] = x
    xn = x * lax.rsqrt(jnp.mean(x * x, axis=-1, keepdims=True) + EPS)
    h2 = (xn * n2_ref[...]) * (1.0 + sc_ref[0]) + sh_ref[0]
    tm = h2.shape[0]
    for j in range(SUBLANES):
        h2_ref[pl.ds(j, tm, stride=SUBLANES), :] = h2[:, j * LANES:(j + 1) * LANES]
    logits = _dot(h2.astype(BF16), wr_ref[...])
    e = jnp.exp(logits - jnp.max(logits, axis=-1, keepdims=True))
    aff_ref[0] = e / jnp.sum(e, axis=-1, keepdims=True)


def outproj_norm_router(parts, w_out, x, g1, n2, scale2, shift2, w_router, tm):
    B, n, D = x.shape
    E = w_router.shape[1]
    row = lambda b, i: (b, i, 0)
    per_b = lambda b, i: (b, 0, 0)
    full = lambda b, i: (0, 0)
    return pl.pallas_call(
        _outproj_kernel,
        grid=(B, n // tm),
        in_specs=[pl.BlockSpec((1, tm, GROUP_W), row)] * N_GROUPS + [
            pl.BlockSpec((D, D), full),
            pl.BlockSpec((1, tm, D), row),
            pl.BlockSpec((1, 1, D), per_b),
            pl.BlockSpec((1, D), full),
            pl.BlockSpec((1, 1, D), per_b),
            pl.BlockSpec((1, 1, D), per_b),
            pl.BlockSpec((D, E), full),
        ],
        out_specs=[
            pl.BlockSpec((1, tm, D), row),
            pl.BlockSpec((tm * SUBLANES, LANES), lambda b, i: (b * (n // tm) + i, 0)),
            pl.BlockSpec((1, tm, E), row),
        ],
        out_shape=[
            jax.ShapeDtypeStruct((B, n, D), F32),
            jax.ShapeDtypeStruct((B * n * SUBLANES, LANES), F32),
            jax.ShapeDtypeStruct((B, n, E), F32),
        ],
        compiler_params=_cparams(("parallel", "parallel")),
        name="outproj_norm_router",
    )(*parts, w_out, x, g1.reshape(B, 1, D), n2.reshape(1, D), scale2.reshape(B, 1, D), shift2.reshape(B, 1, D),
      w_router)


MOE_ROW_CHUNK = 512
MOE_ISSUE_UNROLL = 8


def _moe_ffn_kernel(*refs, n_lat, n_ctx, n_batch):
    refs = list(refs)
    idx_ref, h_hbm = refs[:2]
    refs = refs[2:]
    if n_ctx:
        hc_hbm = refs.pop(0)
    wg_ref, wu_ref, wd_ref, gate_ref, g2_ref = refs[:5]
    refs = refs[5:]
    if n_ctx:
        cg2_ref = refs.pop(0)
    yl_ref = refs.pop(0)
    if n_ctx:
        yc_ref = refs.pop(0)
    xf, xb, sem = refs
    e = pl.program_id(0)
    f = pl.program_id(1)
    n_exp = pl.num_programs(0)
    last = pl.num_programs(1) - 1
    T = n_lat + n_ctx

    R = SUBLANES

    def start_rows(src, expert, first, count):
        def body(i, carry):
            for u in range(MOE_ISSUE_UNROLL):
                s = first + i * MOE_ISSUE_UNROLL + u
                r = pl.multiple_of(idx_ref[expert * T + s], R)
                pltpu.make_async_copy(src.at[pl.ds(r, R)], xf.at[pl.ds(pl.multiple_of(s * R, R), R)],
                                      sem.at[0]).start()
            return carry

        lax.fori_loop(0, count // MOE_ISSUE_UNROLL, body, 0)

    def start_gather(expert):
        start_rows(h_hbm, expert, 0, n_lat)
        if n_ctx:
            start_rows(hc_hbm, expert, n_lat, n_ctx)

    @pl.when(f == 0)
    def _():
        @pl.when(e == 0)
        def _():
            start_gather(e)

        pltpu.make_async_copy(h_hbm.at[pl.ds(0, T * R)], xf, sem.at[0]).wait()
        for j in range(R):
            xb[:, j * LANES:(j + 1) * LANES] = xf[pl.ds(j, T, stride=R), :].astype(BF16)

        @pl.when(e + 1 < n_exp)
        def _():
            start_gather(e + 1)

        yl_ref[...] = jnp.zeros_like(yl_ref)
        if n_ctx:
            yc_ref[...] = jnp.zeros_like(yc_ref)

    wg = wg_ref[0, 0].astype(BF16)
    wu = wu_ref[0, 0].astype(BF16)
    wd = wd_ref[0, 0].astype(BF16)

    def ffn(x):
        hg = _dot(x, wg)
        hu = _dot(x, wu)
        return _dot((hg * jax.nn.sigmoid(hg) * hu).astype(BF16), wd)

    def lat_rows(first, count, j):
        return pl.ds(first * R + j, count, stride=R)

    for i in range(n_lat // MOE_ROW_CHUNK):
        y = ffn(xb[i * MOE_ROW_CHUNK:(i + 1) * MOE_ROW_CHUNK, :])
        for j in range(R):
            yl_ref[lat_rows(i * MOE_ROW_CHUNK, MOE_ROW_CHUNK, j), :] += y[:, j * LANES:(j + 1) * LANES]
    if n_ctx:
        yc_ref[0] += ffn(xb[n_lat:T, :])

    @pl.when(f == last)
    def _():
        per_l = n_lat // n_batch
        for b in range(n_batch):
            gate = gate_ref[0, b * per_l:(b + 1) * per_l, :]
            for j in range(R):
                rows = lat_rows(b * per_l, per_l, j)
                yl_ref[rows, :] = yl_ref[rows, :] * gate * g2_ref[b:b + 1, j * LANES:(j + 1) * LANES]
        if n_ctx:
            per_c = n_ctx // n_batch
            for b in range(n_batch):
                rows = slice(b * per_c, (b + 1) * per_c)
                yc_ref[0, rows, :] = (yc_ref[0, rows, :] * gate_ref[0, n_lat + b * per_c:n_lat + (b + 1) * per_c, :]
                                      * cg2_ref[b:b + 1, :])


def moe_ffn(idx, h2, hc2, w_gate, w_up, w_down, layer, gates, g2, cg2, n_lat, tf):
    E, T, _ = gates.shape
    n_ctx = T - n_lat
    D = SUBLANES * LANES
    F = w_gate.shape[3]
    B = g2.shape[0]
    any_spec = pl.BlockSpec(memory_space=pl.ANY)
    per_e = lambda e, f, idx_ref: (e, 0, 0)
    full = lambda e, f, idx_ref: (0, 0)
    args = [h2] + ([hc2] if n_ctx else []) + [w_gate, w_up, w_down, gates, g2] + ([cg2] if n_ctx else [])
    specs = [any_spec] + ([any_spec] if n_ctx else []) + [
        pl.BlockSpec((1, 1, D, tf), lambda e, f, idx_ref: (layer, e, 0, f)),
        pl.BlockSpec((1, 1, D, tf), lambda e, f, idx_ref: (layer, e, 0, f)),
        pl.BlockSpec((1, 1, tf, D), lambda e, f, idx_ref: (layer, e, f, 0)),
        pl.BlockSpec((1, T, 1), per_e),
        pl.BlockSpec((B, D), full),
    ] + ([pl.BlockSpec((B, D), full)] if n_ctx else [])
    out_specs = [pl.BlockSpec((n_lat * SUBLANES, LANES), lambda e, f, idx_ref: (e, 0))] + (
        [pl.BlockSpec((1, n_ctx, D), per_e)] if n_ctx else [])
    out_shape = [jax.ShapeDtypeStruct((E * n_lat * SUBLANES, LANES), F32)] + (
        [jax.ShapeDtypeStruct((E, n_ctx, D), F32)] if n_ctx else [])
    return pl.pallas_call(
        functools.partial(_moe_ffn_kernel, n_lat=n_lat, n_ctx=n_ctx, n_batch=B),
        grid_spec=pltpu.PrefetchScalarGridSpec(
            num_scalar_prefetch=1,
            grid=(E, F // tf),
            in_specs=specs,
            out_specs=out_specs,
            scratch_shapes=[pltpu.VMEM((T * SUBLANES, LANES), F32), pltpu.VMEM((T, D), BF16),
                            pltpu.SemaphoreType.DMA((1,))],
        ),
        out_shape=out_shape,
        compiler_params=_cparams(("arbitrary", "arbitrary")),
        name="moe_ffn",
    )(idx, *args)


COMBINE_TILE = 128
COMBINE_ISSUE_UNROLL = 8


def _combine_kernel(src_ref, dst_ref, bnd_ref, kmax_ref, y_hbm, x_ref, o_ref, planes, sem, *, tm):
    i = pl.program_id(0)
    nt = pl.num_programs(0)
    R = SUBLANES
    U = COMBINE_ISSUE_UNROLL
    slab = tm * R

    def groups(t):
        return (bnd_ref[t + 1] - bnd_ref[t] + U - 1) // U

    def prepare(t, buf):
        def zero(k, carry):
            planes[buf, pl.ds(pl.multiple_of(k * slab, slab), slab), :] = jnp.zeros((slab, LANES), F32)
            return carry

        lax.fori_loop(0, kmax_ref[t], zero, 0)
        first = bnd_ref[t]
        final = bnd_ref[t + 1] - 1

        def issue(g, carry):
            for u in range(U):
                a = first + g * U + u
                live = a <= final
                a = jnp.minimum(a, final)
                s = pl.multiple_of(src_ref[a], R)
                d = pl.multiple_of(jnp.where(live, dst_ref[a], N_EXPERTS * slab + u * R), R)
                pltpu.make_async_copy(y_hbm.at[pl.ds(s, R)], planes.at[buf, pl.ds(d, R)], sem.at[buf]).start()
            return carry

        lax.fori_loop(0, groups(t), issue, 0)

    @pl.when(i == 0)
    def _():
        prepare(0, 0)

    @pl.when(i + 1 < nt)
    def _():
        prepare(i + 1, (i + 1) % 2)

    buf = i % 2
    n_groups = groups(i)
    p = 1
    while p * U <= tm * N_EXPERTS:
        @pl.when((n_groups & p) != 0)
        def _(p=p):
            pltpu.make_async_copy(y_hbm.at[pl.ds(0, p * U * R)], planes.at[buf, pl.ds(0, p * U * R)],
                                  sem.at[buf]).wait()
        p *= 2

    kmax = kmax_ref[i]

    def add(k, carry):
        planes[buf, 0:slab, :] += planes[buf, pl.ds(pl.multiple_of(k * slab, slab), slab), :]
        return carry

    lax.fori_loop(1, kmax, add, 0)

    @pl.when(kmax > 0)
    def _():
        for j in range(R):
            o_ref[:, j * LANES:(j + 1) * LANES] = (x_ref[:, j * LANES:(j + 1) * LANES]
                                                   + planes[buf, pl.ds(j, tm, stride=R), :])

    @pl.when(kmax == 0)
    def _():
        o_ref[...] = x_ref[...]


def combine_expert_outputs(x, y_tiles, token_rows):
    n, D = x.shape
    A = token_rows.shape[0]
    tm = COMBINE_TILE
    nt = n // tm
    tok, src = lax.sort_key_val(token_rows, jnp.arange(A, dtype=jnp.int32))
    pos = jnp.arange(A, dtype=jnp.int32)
    is_start = jnp.concatenate([jnp.ones((1,), bool), tok[1:] != tok[:-1]])
    rank = pos - lax.cummax(jnp.where(is_start, pos, 0))
    tile = tok // tm
    dst = (rank * tm + tok % tm) * SUBLANES
    edges = jnp.arange(nt + 1, dtype=jnp.int32) * tm
    bounds = jnp.sum(tok[None, :] < edges[:, None], axis=1).astype(jnp.int32)
    kmax = jnp.max(jnp.where(tile[None, :] == jnp.arange(nt, dtype=jnp.int32)[:, None], rank[None, :] + 1, 0),
                   axis=1).astype(jnp.int32)
    row = lambda i, *_: (i, 0)
    return pl.pallas_call(
        functools.partial(_combine_kernel, tm=tm),
        grid_spec=pltpu.PrefetchScalarGridSpec(
            num_scalar_prefetch=4,
            grid=(nt,),
            in_specs=[pl.BlockSpec(memory_space=pl.ANY), pl.BlockSpec((tm, D), row)],
            out_specs=pl.BlockSpec((tm, D), row),
            scratch_shapes=[
                pltpu.VMEM((2, (N_EXPERTS * tm + COMBINE_ISSUE_UNROLL) * SUBLANES, LANES), F32),
                pltpu.SemaphoreType.DMA((2,))],
        ),
        out_shape=jax.ShapeDtypeStruct((n, D), F32),
        compiler_params=_cparams(("arbitrary",)),
        name="combine",
    )(src * SUBLANES, dst, bounds, kmax, y_tiles, x)


def _final_norm_kernel(x_ref, g_ref, o_ref):
    x = x_ref[0]
    o_ref[0] = x * lax.rsqrt(jnp.mean(x * x, axis=-1, keepdims=True) + EPS) * g_ref[...]


def final_norm(x, g, tm):
    B, n, D = x.shape
    return pl.pallas_call(
        _final_norm_kernel,
        grid=(B, n // tm),
        in_specs=[pl.BlockSpec((1, tm, D), lambda b, i: (b, i, 0)), pl.BlockSpec((1, D), lambda b, i: (0, 0))],
        out_specs=pl.BlockSpec((1, tm, D), lambda b, i: (b, i, 0)),
        out_shape=jax.ShapeDtypeStruct((B, n, D), F32),
        compiler_params=_cparams(("parallel", "parallel")),
        name="final_norm",
    )(x, g.reshape(1, D))


def route(aff, cap):
    return lax.top_k(jnp.swapaxes(aff, 1, 2), cap)


TOKEN_TILE = 512


def kernel(x, c, ctx, c_ctx, w_ada, b_ada, norm1_g, norm2_g, w_in, w_out, gla_w_gate, gla_b_gate, gla_norm_g,
           gmlp_norm_g, gmlp_w_s, gmlp_b_s, lru_conv_w, lru_conv_b, lru_w_r, lru_b_r, lru_w_i, lru_b_i,
           lru_lambda, nat_rpb, moe_w_router, moe_w_gate, moe_w_up, moe_w_down, final_norm_g):
    B, N, D = x.shape
    M = ctx.shape[1]
    E = N_EXPERTS
    tm = TOKEN_TILE
    cap_l = max(1, EC_CAPACITY * N // E)
    cap_c = max(1, EC_CAPACITY * M // E)
    sc = jax.nn.silu(c)
    scc = jax.nn.silu(c_ctx)
    xc = ctx
    batch_ids = jnp.arange(B, dtype=jnp.int32)[:, None, None]
    tables = rope_tables(N)
    for l in range(DEPTH):
        need_ctx = l < DEPTH - 1
        mod = sc @ w_ada[l] + b_ada[l]
        mod_c = jnp.broadcast_to((scc @ w_ada[l] + b_ada[l])[None], (B, 6 * D))
        sh1, sc1, g1, sh2, sc2, g2 = jnp.split(mod, 6, axis=-1)
        csh1, csc1, cg1, csh2, csc2, cg2 = jnp.split(mod_c, 6, axis=-1)

        wp = permute_w_in(w_in[l])
        gla, gm, lru, nat, gate = norm_inproj(x, norm1_g[l], sc1, sh1, wp, tm, tables)
        cgla, cgm, clru, cnat, cgate = norm_inproj(xc, norm1_g[l], csc1, csh1, wp, M)

        gla_o, gla_c = gla_mixer(gla, gate, cgla, cgate, gla_w_gate[l], gla_b_gate[l], gla_norm_g[l], need_ctx, tm)
        gm_o = gmlp_mixer(gm, gmlp_norm_g[l], gmlp_w_s[l], gmlp_b_s[l], tm)
        lru_o, lru_c = lru_mixer(lru, clru, lru_conv_w[l], lru_conv_b[l], lru_w_r[l], lru_b_r[l], lru_w_i[l],
                                 lru_b_i[l], lru_lambda[l], need_ctx, tm)
        nat_o = nat_mixer(nat, cnat, nat_bias_tables(nat_rpb[l], N // GRID_W))

        wo = w_out[l].astype(BF16)
        wr = moe_w_router[l].astype(BF16)
        x, h2, aff = outproj_norm_router((gla_o, gm_o, lru_o, nat_o), wo, x, g1, norm2_g[l], sc2, sh2, wr, tm)
        gl, il = route(aff, cap_l)
        rows_l = jnp.swapaxes(il + batch_ids * N, 0, 1).reshape(E, B * cap_l)
        idx = rows_l
        gs = jnp.swapaxes(gl, 0, 1).reshape(E, B * cap_l)
        hc2 = None
        if need_ctx:
            gm_c = gmlp_mixer(cgm, gmlp_norm_g[l], gmlp_w_s[l], gmlp_b_s[l], M)
            nat_c = ctx_attention(cnat)
            xc, hc2, caff = outproj_norm_router((gla_c, gm_c, lru_c, nat_c), wo, xc, cg1, norm2_g[l], csc2, csh2,
                                                wr, M)
            gc, ic = route(caff, cap_c)
            rows_c = jnp.swapaxes(ic + batch_ids * M, 0, 1).reshape(E, B * cap_c)
            idx = jnp.concatenate([rows_l, rows_c], axis=1)
            gs = jnp.concatenate([gs, jnp.swapaxes(gc, 0, 1).reshape(E, B * cap_c)], axis=1)
        ys = moe_ffn((idx * SUBLANES).reshape(-1).astype(jnp.int32), h2, hc2, moe_w_gate, moe_w_up, moe_w_down, l,
                     gs[..., None], g2, cg2, B * cap_l, tf=512)
        x = combine_expert_outputs(x.reshape(B * N, D), ys[0], rows_l.reshape(-1).astype(jnp.int32)).reshape(B, N, D)
        if need_ctx:
            xc = xc.reshape(B * M, D).at[rows_c.reshape(-1)].add(ys[1].reshape(-1, D)).reshape(B, M, D)
    return final_norm(x, final_norm_g, tm)
```

```python
import functools

import numpy as np
import jax
import jax.numpy as jnp
from jax import lax
from jax.experimental import pallas as pl
from jax.experimental.pallas import tpu as pltpu

D_MODEL = 1024
DEPTH = 4
GRID_W = 64
N_GROUPS = 4
GROUP_W = D_MODEL // N_GROUPS
GLA_HEADS = 4
GLA_DK = GROUP_W // GLA_HEADS
GLA_W = GLA_HEADS * GLA_DK
GLA_GATE_RANK = 16
GLA_TAU = 16.0
GLA_CHUNK = 64
GMLP_GROUPS = 4
GMLP_W = GROUP_W
GMLP_CHUNK = 128
LRU_W = GROUP_W
LRU_BLOCKS = 4
LRU_C = 8.0
CONV_W = 4
NAT_HEADS = 4
NAT_DH = GROUP_W // NAT_HEADS
NAT_W = NAT_HEADS * NAT_DH
NAT_KR_MAX = 8
NAT_KC = 16
N_EXPERTS = 16
EXPERT_FF = 2 * D_MODEL
EC_CAPACITY = 2
ROPE_BASE = 10000.0
EPS = 1e-6

IN_SIZES = (GLA_W, GLA_W, GLA_W, GLA_W, GLA_GATE_RANK, GLA_GATE_RANK,
            GMLP_W, GMLP_W, LRU_W, LRU_W, NAT_W, NAT_W, NAT_W)
IN_COLS = sum(IN_SIZES)

V7X_VMEM_LIMIT_BYTES = 56 * 1024 * 1024
LANES = 128
SUBLANES = 8
F32 = jnp.float32
BF16 = jnp.bfloat16
NEG_BIG = -1e30
GATE_PAD = LANES

NT_DIMS = (((1,), (1,)), ((), ()))
TN_DIMS = (((0,), (0,)), ((), ()))


def _cparams(sem):
    return pltpu.CompilerParams(dimension_semantics=sem, vmem_limit_bytes=V7X_VMEM_LIMIT_BYTES)


def _dot(a, b):
    return jnp.dot(a, b, preferred_element_type=F32)


def _split3(x):
    hi = x.astype(BF16)
    r1 = x - hi.astype(F32)
    mid = r1.astype(BF16)
    lo = (r1 - mid.astype(F32)).astype(BF16)
    return hi, mid, lo


def _head_block_mask(n):
    r = lax.broadcasted_iota(jnp.int32, (n, n), 0) // GLA_DK
    c = lax.broadcasted_iota(jnp.int32, (n, n), 1) // GLA_DK
    return r == c


def _axial_rope(u, cos, sin):
    lane = lax.broadcasted_iota(jnp.int32, u.shape, 1)
    first = (lane % (GLA_DK // 2)) < (GLA_DK // 4)
    partner = jnp.where(first, pltpu.roll(u, GLA_W - GLA_DK // 4, axis=1), pltpu.roll(u, GLA_DK // 4, axis=1))
    return u * cos + partner * sin


def _norm_inproj_kernel(*refs, rope):
    refs = list(refs)
    x_ref, g_ref, sc_ref, sh_ref, w_ref = refs[:5]
    refs = refs[5:]
    if rope:
        cos_ref, sin_ref = refs[:2]
        refs = refs[2:]
    gla_ref, gmlp_ref, lru_ref, nat_ref, gate_ref = refs
    x = x_ref[0]
    y = x * lax.rsqrt(jnp.mean(x * x, axis=-1, keepdims=True) + EPS)
    h = ((y * g_ref[...]) * (1.0 + sc_ref[0]) + sh_ref[0]).astype(BF16)
    o = 0
    for ref in (gla_ref, gmlp_ref, lru_ref, nat_ref, gate_ref):
        w = ref.shape[-1]
        ref[0] = _dot(h, w_ref[:, o:o + w]).astype(ref.dtype)
        o += w
    if rope:
        for c in range(2):
            cols = slice(c * GLA_W, (c + 1) * GLA_W)
            gla_ref[0, :, cols] = _axial_rope(gla_ref[0, :, cols], cos_ref[...], sin_ref[...])


def permute_w_in(w):
    g0 = 4 * GLA_W
    g1 = g0 + 2 * GLA_GATE_RANK
    pad = jnp.zeros((w.shape[0], GATE_PAD - 2 * GLA_GATE_RANK), w.dtype)
    return jnp.concatenate([w[:, :g0], w[:, g1:], w[:, g0:g1], pad], axis=1).astype(BF16)


def norm_inproj(x, g, scale, shift, wp, tm, tables=None):
    B, n, D = x.shape
    C = wp.shape[1]
    widths = (4 * GLA_W, 2 * GMLP_W, 2 * LRU_W, 3 * NAT_W, GATE_PAD)
    dtypes = (F32, F32, F32, BF16, F32)
    row = lambda b, i: (b, i, 0)
    per_b = lambda b, i: (b, 0, 0)
    full = lambda b, i: (0, 0)
    rope_specs = [pl.BlockSpec((tm, GLA_W), lambda b, i: (i, 0))] * 2 if tables is not None else []
    return pl.pallas_call(
        functools.partial(_norm_inproj_kernel, rope=tables is not None),
        grid=(B, n // tm),
        in_specs=[
            pl.BlockSpec((1, tm, D), row),
            pl.BlockSpec((1, D), full),
            pl.BlockSpec((1, 1, D), per_b),
            pl.BlockSpec((1, 1, D), per_b),
            pl.BlockSpec((D, C), full),
        ] + rope_specs,
        out_specs=[pl.BlockSpec((1, tm, w), row) for w in widths],
        out_shape=[jax.ShapeDtypeStruct((B, n, w), dt) for w, dt in zip(widths, dtypes)],
        compiler_params=_cparams(("parallel", "parallel")),
        name="norm_inproj",
    )(x, g.reshape(1, D), scale.reshape(B, 1, D), shift.reshape(B, 1, D), wp, *(tables or ()))


def _gla_kernel(*refs, tm, reverse, combine):
    refs = list(refs)
    q_ref, k_ref, v_ref, gate_ref, wg_ref, bg_ref, tri_ref, ones_ref, s0_ref = refs[:9]
    refs = refs[9:]
    if combine:
        ob_ref, og_ref, ng_ref = refs[:3]
        refs = refs[3:]
    o_ref, sfin_ref, st, obuf = refs
    i = pl.program_id(1)
    nch = tm // GLA_CHUNK

    @pl.when(i == 0)
    def _():
        st[...] = s0_ref[0]

    q = q_ref[0] * (GLA_DK ** -0.5)
    k = k_ref[0]
    a = _dot(gate_ref[0].astype(BF16), wg_ref[...]) + bg_ref[...]
    la = jax.nn.log_sigmoid(a) / GLA_TAU
    pieces = _split3(la)
    b = sum(_dot(tri_ref[...], p) for p in pieces)
    bl = sum(_dot(ones_ref[...], p) for p in pieces)
    q_in = (q * jnp.exp(b)).astype(BF16)
    k_in = (k * jnp.exp(-b)).astype(BF16)
    k_end = (k * jnp.exp(bl - b)).astype(BF16)
    dec = jnp.exp(bl)
    vb = v_ref[0].astype(BF16)

    head_mask = _head_block_mask(GLA_W)
    cr = lax.broadcasted_iota(jnp.int32, (GLA_CHUNK, GLA_W), 0)
    cs = lax.broadcasted_iota(jnp.int32, (GLA_CHUNK, GLA_W), 1) % GLA_CHUNK
    causal = (cs >= cr) if reverse else (cs <= cr)
    zero = jnp.zeros((), BF16)
    chunks = range(nch - 1, -1, -1) if reverse else range(nch)
    for c in chunks:
        sl = slice(c * GLA_CHUNK, (c + 1) * GLA_CHUNK)
        kbd_t = jnp.where(head_mask, jnp.concatenate([k_in[sl]] * GLA_HEADS, axis=0), zero)
        vbd = jnp.where(head_mask, jnp.concatenate([vb[sl]] * GLA_HEADS, axis=0), zero)
        att = lax.dot_general(q_in[sl], kbd_t, NT_DIMS, preferred_element_type=F32)
        att = jnp.where(causal, att, 0.0).astype(BF16)
        s_t = st[...]
        o = _dot(att, vbd) + lax.dot_general(q_in[sl], s_t.astype(BF16), NT_DIMS, preferred_element_type=F32)
        upd = lax.dot_general(vb[sl], k_end[sl], TN_DIMS, preferred_element_type=F32)
        st[...] = s_t * dec[c * GLA_CHUNK:c * GLA_CHUNK + 1] + jnp.where(head_mask, upd, 0.0)
        obuf[sl, :] = o

    sfin_ref[0] = st[...]

    if not combine:
        o_ref[0] = obuf[...]
    else:
        o = obuf[...] + ob_ref[0]
        sq_hi, sq_mid, _ = _split3(o * o)
        avg = jnp.where(head_mask, 1.0 / GLA_DK, 0.0).astype(BF16)
        ms = _dot(sq_hi, avg) + _dot(sq_mid, avg)
        og = og_ref[0]
        o_ref[0] = (o * lax.rsqrt(ms + EPS) * ng_ref[...] * (og * jax.nn.sigmoid(og))).astype(o_ref.dtype)


def _chunk_sum_matrices(tm, reverse):
    r = np.arange(tm)[:, None]
    c = np.arange(tm)[None, :]
    same = (r // GLA_CHUNK) == (c // GLA_CHUNK)
    tri = same & ((c >= r) if reverse else (c <= r))
    return jnp.asarray(tri, BF16), jnp.asarray(same, BF16)


def gla_direction(gla, gate, wg, bg, s0, tm, reverse, combine=None):
    B, L, _ = gla.shape
    nt = L // tm
    W = GLA_W
    blk = (lambda i: nt - 1 - i) if reverse else (lambda i: i)
    col = lambda j: (lambda b, i: (b, blk(i), j))
    full = lambda b, i: (0, 0)
    tri, ones = _chunk_sum_matrices(tm, reverse)
    args = [gla, gla, gla, gate, wg, bg, tri, ones, s0]
    specs = [
        pl.BlockSpec((1, tm, W), col(0)), pl.BlockSpec((1, tm, W), col(1)), pl.BlockSpec((1, tm, W), col(2)),
        pl.BlockSpec((1, tm, GATE_PAD), col(0)),
        pl.BlockSpec((GATE_PAD, W), full), pl.BlockSpec((1, W), full),
        pl.BlockSpec((tm, tm), full), pl.BlockSpec((tm, tm), full),
        pl.BlockSpec((1, W, W), lambda b, i: (b, 0, 0)),
    ]
    if combine is not None:
        ob, norm_g = combine
        args += [ob, gla, norm_g]
        specs += [pl.BlockSpec((1, tm, W), col(0)), pl.BlockSpec((1, tm, W), col(3)), pl.BlockSpec((1, W), full)]
    return pl.pallas_call(
        functools.partial(_gla_kernel, tm=tm, reverse=reverse, combine=combine is not None),
        grid=(B, nt),
        in_specs=specs,
        out_specs=[pl.BlockSpec((1, tm, W), col(0)), pl.BlockSpec((1, W, W), lambda b, i: (b, 0, 0))],
        out_shape=[jax.ShapeDtypeStruct((B, L, W), F32 if combine is None else BF16),
                   jax.ShapeDtypeStruct((B, W, W), F32)],
        scratch_shapes=[pltpu.VMEM((W, W), F32), pltpu.VMEM((tm, W), F32)],
        compiler_params=_cparams(("parallel", "arbitrary")),
        name="gla_bwd" if reverse else "gla_fwd",
    )(*args)


def rope_tables(n):
    quarter = GLA_DK // 4
    pos = jnp.arange(n)
    inv = ROPE_BASE ** (-jnp.arange(quarter, dtype=F32) / quarter)
    row = (pos // GRID_W).astype(F32)
    colp = (pos % GRID_W).astype(F32)
    ar = row[:, None] * inv[None, :]
    ac = colp[:, None] * inv[None, :]
    cos = jnp.concatenate([jnp.cos(ar), jnp.cos(ar), jnp.cos(ac), jnp.cos(ac)], axis=1)
    sin = jnp.concatenate([-jnp.sin(ar), jnp.sin(ar), -jnp.sin(ac), jnp.sin(ac)], axis=1)
    return jnp.tile(cos, (1, GLA_HEADS)), jnp.tile(sin, (1, GLA_HEADS))


def gla_mixer(gla, gate, cgla, cgate, w_gate, b_gate, norm_g, need_ctx, tm):
    B, M, _ = cgla.shape
    zeros = jnp.zeros((B, GLA_W, GLA_W), F32)
    ng = jnp.tile(norm_g, GLA_HEADS).reshape(1, GLA_W)
    wgs, bgs = [], []
    for d in range(2):
        wg = jnp.pad(w_gate[d], ((d * GLA_GATE_RANK, GATE_PAD - (d + 1) * GLA_GATE_RANK), (0, 0)))
        wgs.append(wg.astype(BF16))
        bgs.append(b_gate[d].reshape(1, GLA_W))
    ocb, sb = gla_direction(cgla, cgate, wgs[1], bgs[1], zeros, M, True)
    ob, _ = gla_direction(gla, gate, wgs[1], bgs[1], sb, tm, True)
    if need_ctx:
        oc, sf = gla_direction(cgla, cgate, wgs[0], bgs[0], zeros, M, False, combine=(ocb, ng))
    else:
        oc, sf = gla_direction(cgla, cgate, wgs[0], bgs[0], zeros, M, False)
    o, _ = gla_direction(gla, gate, wgs[0], bgs[0], sf, tm, False, combine=(ob, ng))
    return o, oc


def _gmlp_kernel(u_ref, v_ref, g_ref, w_ref, b_ref, o_ref, *, tm):
    v = v_ref[0]
    vn = (v * lax.rsqrt(jnp.mean(v * v, axis=-1, keepdims=True) + EPS) * g_ref[...]).astype(BF16)
    grp = lax.broadcasted_iota(jnp.int32, (GMLP_CHUNK, GMLP_W), 1) // (GMLP_W // GMLP_GROUPS)
    zero = jnp.zeros((), BF16)
    for c in range(tm // GMLP_CHUNK):
        sl = slice(c * GMLP_CHUNK, (c + 1) * GMLP_CHUNK)
        mixed = b_ref[...]
        for g in range(GMLP_GROUPS):
            mixed = mixed + _dot(w_ref[g], jnp.where(grp == g, vn[sl], zero))
        o_ref[0, sl, :] = (u_ref[0, sl, :] * mixed).astype(o_ref.dtype)


def gmlp_mixer(gm, norm_g, w_s, b_s, tm):
    B, L, _ = gm.shape
    W = GMLP_W
    bias = jnp.repeat(b_s.T, W // GMLP_GROUPS, axis=1)
    return pl.pallas_call(
        functools.partial(_gmlp_kernel, tm=tm),
        grid=(B, L // tm),
        in_specs=[
            pl.BlockSpec((1, tm, W), lambda b, i: (b, i, 0)),
            pl.BlockSpec((1, tm, W), lambda b, i: (b, i, 1)),
            pl.BlockSpec((1, W), lambda b, i: (0, 0)),
            pl.BlockSpec((GMLP_GROUPS, GMLP_CHUNK, GMLP_CHUNK), lambda b, i: (0, 0, 0)),
            pl.BlockSpec((GMLP_CHUNK, W), lambda b, i: (0, 0)),
        ],
        out_specs=pl.BlockSpec((1, tm, W), lambda b, i: (b, i, 0)),
        out_shape=jax.ShapeDtypeStruct((B, L, W), BF16),
        compiler_params=_cparams(("parallel", "parallel")),
        name="gmlp",
    )(gm, gm, norm_g.reshape(1, W), w_s.astype(BF16), bias)


LRU_UNROLL = 8


def _lru_kernel(*refs, tm, reverse, combine):
    refs = list(refs)
    x_ref, xp_ref, xn_ref, cw_ref, cb_ref, wr_ref, br_ref, wi_ref, bi_ref, ncs_ref, h0_ref = refs[:11]
    refs = refs[11:]
    if combine:
        hb_ref, ly_ref = refs[:2]
        refs = refs[2:]
    o_ref, ext, a_s, b_s, carry = refs
    W = LRU_W
    i = pl.program_id(1)
    nt = pl.num_programs(1)
    t = (nt - 1 - i) if reverse else i
    H = SUBLANES

    ext[0:H] = jnp.where(t > 0, xp_ref[0], 0.0)
    ext[H:H + tm] = x_ref[0]
    ext[H + tm:2 * H + tm] = jnp.where(t < nt - 1, xn_ref[0], 0.0)
    xc = cb_ref[...]
    for tap in range(CONV_W):
        xc = xc + cw_ref[tap:tap + 1, :] * ext[H - CONV_W // 2 + tap:H - CONV_W // 2 + tap + tm]
    xcb = xc.astype(BF16)
    r = jax.nn.sigmoid(_dot(xcb, wr_ref[...]) + br_ref[...])
    ig = jax.nn.sigmoid(_dot(xcb, wi_ref[...]) + bi_ref[...])
    log_a = ncs_ref[...] * r
    a = jnp.exp(log_a)
    b = jnp.sqrt(-jnp.tanh(log_a) * (a * a + 1.0)) * (ig * xc)

    rowi = lax.broadcasted_iota(jnp.int32, (tm, W), 0) % H
    for s in (1, 2, 4):
        if reverse:
            ok = rowi < H - s
            a_sh = pltpu.roll(a, tm - s, axis=0)
            b_sh = pltpu.roll(b, tm - s, axis=0)
        else:
            ok = rowi >= s
            a_sh = pltpu.roll(a, s, axis=0)
            b_sh = pltpu.roll(b, s, axis=0)
        b = a * jnp.where(ok, b_sh, 0.0) + b
        a = a * jnp.where(ok, a_sh, 1.0)
    a_s[...] = a
    b_s[...] = b

    @pl.when(i == 0)
    def _():
        carry[...] = jnp.broadcast_to(h0_ref[0], (H, W))

    ng = tm // H

    def body(j, h):
        for u in range(LRU_UNROLL):
            g = j * LRU_UNROLL + u
            g = (ng - 1 - g) if reverse else g
            rows = pl.ds(pl.multiple_of(g * H, H), H)
            hg = b_s[rows, :] + a_s[rows, :] * h
            b_s[rows, :] = hg
            h = jnp.broadcast_to(hg[0:1] if reverse else hg[H - 1:H], (H, W))
        return h

    carry[...] = lax.fori_loop(0, ng // LRU_UNROLL, body, carry[...])

    if combine:
        o_ref[0] = ((b_s[...] + hb_ref[0]) * jax.nn.gelu(ly_ref[0])).astype(o_ref.dtype)
    else:
        o_ref[0] = b_s[...]


def lru_direction(lru, conv_w, conv_b, wr, br, wi, bi, ncs, h0, tm, reverse, hb=None):
    B, L, _ = lru.shape
    nt = L // tm
    W = LRU_W
    H = SUBLANES
    nh = L // H
    per = tm // H
    blk = (lambda i: nt - 1 - i) if reverse else (lambda i: i)
    full = lambda b, i: (0, 0)
    args = [lru, lru, lru, conv_w, conv_b, wr, br, wi, bi, ncs, h0]
    specs = [
        pl.BlockSpec((1, tm, W), lambda b, i: (b, blk(i), 0)),
        pl.BlockSpec((1, H, W), lambda b, i: (b, jnp.maximum(blk(i) * per - 1, 0), 0)),
        pl.BlockSpec((1, H, W), lambda b, i: (b, jnp.minimum((blk(i) + 1) * per, nh - 1), 0)),
        pl.BlockSpec((CONV_W, W), full), pl.BlockSpec((1, W), full),
        pl.BlockSpec((W, W), full), pl.BlockSpec((1, W), full),
        pl.BlockSpec((W, W), full), pl.BlockSpec((1, W), full),
        pl.BlockSpec((1, W), full),
        pl.BlockSpec((1, 1, W), lambda b, i: (b, 0, 0)),
    ]
    if hb is not None:
        args += [hb, lru]
        specs += [pl.BlockSpec((1, tm, W), lambda b, i: (b, blk(i), 0)),
                  pl.BlockSpec((1, tm, W), lambda b, i: (b, blk(i), 1))]
    return pl.pallas_call(
        functools.partial(_lru_kernel, tm=tm, reverse=reverse, combine=hb is not None),
        grid=(B, nt),
        in_specs=specs,
        out_specs=pl.BlockSpec((1, tm, W), lambda b, i: (b, blk(i), 0)),
        out_shape=jax.ShapeDtypeStruct((B, L, W), F32 if hb is None else BF16),
        scratch_shapes=[pltpu.VMEM((tm + 2 * H, W), F32), pltpu.VMEM((tm, W), F32), pltpu.VMEM((tm, W), F32),
                        pltpu.VMEM((H, W), F32)],
        compiler_params=_cparams(("parallel", "arbitrary")),
        name="lru_bwd" if reverse else "lru_fwd",
    )(*args)


def _block_diag(w):
    G, n, _ = w.shape
    same = jnp.eye(G, dtype=w.dtype)
    return (w[:, :, None, :] * same[:, None, :, None]).reshape(G * n, G * n)


def lru_mixer(lru, clru, conv_w, conv_b, w_r, b_r, w_i, b_i, lam, need_ctx, tm):
    B, M, _ = clru.shape
    W = LRU_W
    cb = conv_b.reshape(1, W)
    ncs = -LRU_C * jax.nn.softplus(-lam.astype(F32))
    p = [(_block_diag(w_r[d]).astype(BF16), b_r[d].reshape(1, W), _block_diag(w_i[d]).astype(BF16),
          b_i[d].reshape(1, W), ncs[d].reshape(1, W)) for d in range(2)]
    zeros = jnp.zeros((B, 1, W), F32)
    hcb = lru_direction(clru, conv_w, cb, *p[1], zeros, M, True)
    hb = lru_direction(lru, conv_w, cb, *p[1], hcb[:, 0:1], tm, True)
    hcf = lru_direction(clru, conv_w, cb, *p[0], zeros, M, False)
    out = lru_direction(lru, conv_w, cb, *p[0], hcf[:, M - 1:M], tm, False, hb=hb)
    out_c = None
    if need_ctx:
        out_c = lru_direction(clru, conv_w, cb, *p[0], zeros, M, False, hb=hcb)
    return out, out_c


NAT_QROWS = 4
NAT_KROWS = 3 * NAT_QROWS
NAT_TQ = NAT_QROWS * GRID_W
NAT_TK = NAT_KROWS * GRID_W


def _softmax_pv(s_loc, s_ctx, v_loc, v_ctx):
    m = jnp.maximum(jnp.max(s_loc, axis=-1, keepdims=True), jnp.max(s_ctx, axis=-1, keepdims=True))
    e_loc = jnp.exp(s_loc - m)
    e_ctx = jnp.exp(s_ctx - m)
    den = jnp.sum(e_loc, axis=-1, keepdims=True) + jnp.sum(e_ctx, axis=-1, keepdims=True)
    o = _dot(e_loc.astype(BF16), v_loc) + _dot(e_ctx.astype(BF16), v_ctx)
    return o / den


def _nat_kernel(q_ref, kp_ref, kc_ref, kn_ref, vp_ref, vc_ref, vn_ref, ck_ref, cv_ref, bias_ref, o_ref):
    q = q_ref[0]
    ck = ck_ref[0]
    cv = cv_ref[0]
    scale = NAT_DH ** -0.5
    outs = []
    for h in range(NAT_HEADS):
        hs = slice(h * NAT_DH, (h + 1) * NAT_DH)
        qh = q[:, hs]
        s_loc = jnp.concatenate(
            [lax.dot_general(qh, k_ref[0][:, hs], NT_DIMS, preferred_element_type=F32)
             for k_ref in (kp_ref, kc_ref, kn_ref)], axis=-1) * scale + bias_ref[0, h]
        s_ctx = lax.dot_general(qh, ck[:, hs], NT_DIMS, preferred_element_type=F32) * scale
        v_loc = jnp.concatenate([v_ref[0][:, hs] for v_ref in (vp_ref, vc_ref, vn_ref)], axis=0)
        outs.append(_softmax_pv(s_loc, s_ctx, v_loc, cv[:, hs]))
    o_ref[0] = jnp.concatenate(outs, axis=-1).astype(o_ref.dtype)


def nat_bias_tables(rpb, rows):
    kr = NAT_KR_MAX
    qc = np.arange(GRID_W)
    kcol = np.arange(GRID_W)
    cs = np.clip(qc - NAT_KC // 2, 0, GRID_W - NAT_KC)
    valid_c = (kcol[None, :] >= cs[:, None]) & (kcol[None, :] < cs[:, None] + NAT_KC)
    edge = GRID_W - NAT_KC
    padded = jnp.pad(rpb.astype(F32), ((0, 0), (0, 0), (edge, edge)), mode="edge")
    by_col = jnp.stack([padded[:, :, GRID_W - 1 - q:2 * GRID_W - 1 - q] for q in range(GRID_W)], axis=2)
    by_col = jnp.pad(by_col, ((0, 0), (NAT_KROWS, NAT_KROWS), (0, 0), (0, 0)))
    tables = []
    for r0, ks in ((0, 0), (NAT_QROWS, NAT_QROWS - kr // 2), (rows - NAT_QROWS, rows - NAT_KROWS)):
        r = r0 + np.arange(NAT_QROWS)
        krow = ks + np.arange(NAT_KROWS)
        rs = np.clip(r - kr // 2, 0, rows - kr)
        valid_r = (krow[None, :] >= rs[:, None]) & (krow[None, :] < rs[:, None] + kr)
        starts = ks - r + NAT_KR_MAX - 1 + NAT_KROWS
        bias = jnp.stack([by_col[:, int(s):int(s) + NAT_KROWS] for s in starts], axis=1)
        bias = bias.transpose(0, 1, 3, 2, 4)
        mask = valid_r[:, None, :, None] & valid_c[None, :, None, :]
        bias = jnp.where(jnp.asarray(mask)[None], bias, NEG_BIG)
        tables.append(bias.reshape(rpb.shape[0], NAT_TQ, NAT_TK))
    return jnp.stack(tables)


def nat_mixer(nat, cnat, bias):
    B, N, _ = nat.shape
    M = cnat.shape[1]
    W = NAT_W
    nb = N // NAT_TQ
    T = NAT_TQ
    centre = lambda j: jnp.clip(j, 1, nb - 2)
    near = lambda c, d: (lambda b, j: (b, centre(j) + d, c))
    variant = lambda b, j: (jnp.where(j == 0, 0, jnp.where(j == nb - 1, 2, 1)), 0, 0, 0)
    return pl.pallas_call(
        _nat_kernel,
        grid=(B, nb),
        in_specs=[
            pl.BlockSpec((1, T, W), lambda b, j: (b, j, 0)),
            pl.BlockSpec((1, T, W), near(1, -1)), pl.BlockSpec((1, T, W), near(1, 0)), pl.BlockSpec((1, T, W), near(1, 1)),
            pl.BlockSpec((1, T, W), near(2, -1)), pl.BlockSpec((1, T, W), near(2, 0)), pl.BlockSpec((1, T, W), near(2, 1)),
            pl.BlockSpec((1, M, W), lambda b, j: (b, 0, 1)),
            pl.BlockSpec((1, M, W), lambda b, j: (b, 0, 2)),
            pl.BlockSpec((1, NAT_HEADS, NAT_TQ, NAT_TK), variant),
        ],
        out_specs=pl.BlockSpec((1, T, W), lambda b, j: (b, j, 0)),
        out_shape=jax.ShapeDtypeStruct((B, N, W), BF16),
        compiler_params=_cparams(("parallel", "arbitrary")),
        name="nat",
    )(nat, nat, nat, nat, nat, nat, nat, cnat, cnat, bias)


def _ctx_attn_kernel(q_ref, k_ref, v_ref, o_ref):
    q = q_ref[0]
    k = k_ref[0]
    v = v_ref[0]
    scale = NAT_DH ** -0.5
    outs = []
    for h in range(NAT_HEADS):
        hs = slice(h * NAT_DH, (h + 1) * NAT_DH)
        s = lax.dot_general(q[:, hs], k[:, hs], NT_DIMS, preferred_element_type=F32) * scale
        e = jnp.exp(s - jnp.max(s, axis=-1, keepdims=True))
        outs.append(_dot(e.astype(BF16), v[:, hs]) / jnp.sum(e, axis=-1, keepdims=True))
    o_ref[0] = jnp.concatenate(outs, axis=-1).astype(o_ref.dtype)


def ctx_attention(cnat):
    B, M, _ = cnat.shape
    W = NAT_W
    return pl.pallas_call(
        _ctx_attn_kernel,
        grid=(B,),
        in_specs=[pl.BlockSpec((1, M, W), lambda b, c=c: (b, 0, c)) for c in range(3)],
        out_specs=pl.BlockSpec((1, M, W), lambda b: (b, 0, 0)),
        out_shape=jax.ShapeDtypeStruct((B, M, W), BF16),
        compiler_params=_cparams(("parallel",)),
        name="ctx_attn",
    )(cnat, cnat, cnat)


def _outproj_kernel(m0_ref, m1_ref, m2_ref, m3_ref, w_ref, x_ref, g1_ref, n2_ref, sc_ref, sh_ref, wr_ref,
                    xo_ref, h2_ref, aff_ref):
    y = 0.0
    for gi, m_ref in enumerate((m0_ref, m1_ref, m2_ref, m3_ref)):
        y = y + _dot(m_ref[0].astype(BF16), w_ref[gi * GROUP_W:(gi + 1) * GROUP_W, :])
    x = x_ref[0] + g1_ref[0] * y
    xo_ref[0] = x
    xn = x * lax.rsqrt(jnp.mean(x * x, axis=-1, keepdims=True) + EPS)
    h2 = (xn * n2_ref[...]) * (1.0 + sc_ref[0]) + sh_ref[0]
    tm = h2.shape[0]
    for j in range(SUBLANES):
        h2_ref[pl.ds(j, tm, stride=SUBLANES), :] = h2[:, j * LANES:(j + 1) * LANES]
    logits = _dot(h2.astype(BF16), wr_ref[...])
    e = jnp.exp(logits - jnp.max(logits, axis=-1, keepdims=True))
    aff_ref[0] = e / jnp.sum(e, axis=-1, keepdims=True)


def outproj_norm_router(parts, w_out, x, g1, n2, scale2, shift2, w_router, tm):
    B, n, D = x.shape
    E = w_router.shape[1]
    row = lambda b, i: (b, i, 0)
    per_b = lambda b, i: (b, 0, 0)
    full = lambda b, i: (0, 0)
    return pl.pallas_call(
        _outproj_kernel,
        grid=(B, n // tm),
        in_specs=[pl.BlockSpec((1, tm, GROUP_W), row)] * N_GROUPS + [
            pl.BlockSpec((D, D), full),
            pl.BlockSpec((1, tm, D), row),
            pl.BlockSpec((1, 1, D), per_b),
            pl.BlockSpec((1, D), full),
            pl.BlockSpec((1, 1, D), per_b),
            pl.BlockSpec((1, 1, D), per_b),
            pl.BlockSpec((D, E), full),
        ],
        out_specs=[
            pl.BlockSpec((1, tm, D), row),
            pl.BlockSpec((tm * SUBLANES, LANES), lambda b, i: (b * (n // tm) + i, 0)),
            pl.BlockSpec((1, tm, E), row),
        ],
        out_shape=[
            jax.ShapeDtypeStruct((B, n, D), F32),
            jax.ShapeDtypeStruct((B * n * SUBLANES, LANES), F32),
            jax.ShapeDtypeStruct((B, n, E), F32),
        ],
        compiler_params=_cparams(("parallel", "parallel")),
        name="outproj_norm_router",
    )(*parts, w_out, x, g1.reshape(B, 1, D), n2.reshape(1, D), scale2.reshape(B, 1, D), shift2.reshape(B, 1, D),
      w_router)


MOE_ROW_CHUNK = 512
MOE_ISSUE_UNROLL = 8


def _moe_ffn_kernel(*refs, n_lat, n_ctx, n_batch):
    refs = list(refs)
    idx_ref, h_hbm = refs[:2]
    refs = refs[2:]
    if n_ctx:
        hc_hbm = refs.pop(0)
    wg_ref, wu_ref, wd_ref, gate_ref, g2_ref = refs[:5]
    refs = refs[5:]
    if n_ctx:
        cg2_ref = refs.pop(0)
    yl_ref = refs.pop(0)
    if n_ctx:
        yc_ref = refs.pop(0)
    xf, xb, sem = refs
    e = pl.program_id(0)
    f = pl.program_id(1)
    n_exp = pl.num_programs(0)
    last = pl.num_programs(1) - 1
    T = n_lat + n_ctx

    R = SUBLANES

    def start_rows(src, expert, first, count):
        def body(i, carry):
            for u in range(MOE_ISSUE_UNROLL):
                s = first + i * MOE_ISSUE_UNROLL + u
                r = pl.multiple_of(idx_ref[expert * T + s], R)
                pltpu.make_async_copy(src.at[pl.ds(r, R)], xf.at[pl.ds(pl.multiple_of(s * R, R), R)],
                                      sem.at[0]).start()
            return carry

        lax.fori_loop(0, count // MOE_ISSUE_UNROLL, body, 0)

    def start_gather(expert):
        start_rows(h_hbm, expert, 0, n_lat)
        if n_ctx:
            start_rows(hc_hbm, expert, n_lat, n_ctx)

    @pl.when(f == 0)
    def _():
        @pl.when(e == 0)
        def _():
            start_gather(e)

        pltpu.make_async_copy(h_hbm.at[pl.ds(0, T * R)], xf, sem.at[0]).wait()
        for j in range(R):
            xb[:, j * LANES:(j + 1) * LANES] = xf[pl.ds(j, T, stride=R), :].astype(BF16)

        @pl.when(e + 1 < n_exp)
        def _():
            start_gather(e + 1)

        yl_ref[...] = jnp.zeros_like(yl_ref)
        if n_ctx:
            yc_ref[...] = jnp.zeros_like(yc_ref)

    wg = wg_ref[0, 0].astype(BF16)
    wu = wu_ref[0, 0].astype(BF16)
    wd = wd_ref[0, 0].astype(BF16)

    def ffn(x):
        hg = _dot(x, wg)
        hu = _dot(x, wu)
        return _dot((hg * jax.nn.sigmoid(hg) * hu).astype(BF16), wd)

    def lat_rows(first, count, j):
        return pl.ds(first * R + j, count, stride=R)

    for i in range(n_lat // MOE_ROW_CHUNK):
        y = ffn(xb[i * MOE_ROW_CHUNK:(i + 1) * MOE_ROW_CHUNK, :])
        for j in range(R):
            yl_ref[lat_rows(i * MOE_ROW_CHUNK, MOE_ROW_CHUNK, j), :] += y[:, j * LANES:(j + 1) * LANES]
    if n_ctx:
        yc_ref[0] += ffn(xb[n_lat:T, :])

    @pl.when(f == last)
    def _():
        per_l = n_lat // n_batch
        for b in range(n_batch):
            gate = gate_ref[0, b * per_l:(b + 1) * per_l, :]
            for j in range(R):
                rows = lat_rows(b * per_l, per_l, j)
                yl_ref[rows, :] = yl_ref[rows, :] * gate * g2_ref[b:b + 1, j * LANES:(j + 1) * LANES]
        if n_ctx:
            per_c = n_ctx // n_batch
            for b in range(n_batch):
                rows = slice(b * per_c, (b + 1) * per_c)
                yc_ref[0, rows, :] = (yc_ref[0, rows, :] * gate_ref[0, n_lat + b * per_c:n_lat + (b + 1) * per_c, :]
                                      * cg2_ref[b:b + 1, :])


def moe_ffn(idx, h2, hc2, w_gate, w_up, w_down, layer, gates, g2, cg2, n_lat, tf):
    E, T, _ = gates.shape
    n_ctx = T - n_lat
    D = SUBLANES * LANES
    F = w_gate.shape[3]
    B = g2.shape[0]
    any_spec = pl.BlockSpec(memory_space=pl.ANY)
    per_e = lambda e, f, idx_ref: (e, 0, 0)
    full = lambda e, f, idx_ref: (0, 0)
    args = [h2] + ([hc2] if n_ctx else []) + [w_gate, w_up, w_down, gates, g2] + ([cg2] if n_ctx else [])
    specs = [any_spec] + ([any_spec] if n_ctx else []) + [
        pl.BlockSpec((1, 1, D, tf), lambda e, f, idx_ref: (layer, e, 0, f)),
        pl.BlockSpec((1, 1, D, tf), lambda e, f, idx_ref: (layer, e, 0, f)),
        pl.BlockSpec((1, 1, tf, D), lambda e, f, idx_ref: (layer, e, f, 0)),
        pl.BlockSpec((1, T, 1), per_e),
        pl.BlockSpec((B, D), full),
    ] + ([pl.BlockSpec((B, D), full)] if n_ctx else [])
    out_specs = [pl.BlockSpec((n_lat * SUBLANES, LANES), lambda e, f, idx_ref: (e, 0))] + (
        [pl.BlockSpec((1, n_ctx, D), per_e)] if n_ctx else [])
    out_shape = [jax.ShapeDtypeStruct((E * n_lat * SUBLANES, LANES), F32)] + (
        [jax.ShapeDtypeStruct((E, n_ctx, D), F32)] if n_ctx else [])
    return pl.pallas_call(
        functools.partial(_moe_ffn_kernel, n_lat=n_lat, n_ctx=n_ctx, n_batch=B),
        grid_spec=pltpu.PrefetchScalarGridSpec(
            num_scalar_prefetch=1,
            grid=(E, F // tf),
            in_specs=specs,
            out_specs=out_specs,
            scratch_shapes=[pltpu.VMEM((T * SUBLANES, LANES), F32), pltpu.VMEM((T, D), BF16),
                            pltpu.SemaphoreType.DMA((1,))],
        ),
        out_shape=out_shape,
        compiler_params=_cparams(("arbitrary", "arbitrary")),
        name="moe_ffn",
    )(idx, *args)


COMBINE_TILE = 256
COMBINE_ISSUE_UNROLL = 8


def _combine_kernel(src_ref, dst_ref, bnd_ref, kmax_ref, y_hbm, x_ref, o_ref, planes, sem, *, tm):
    i = pl.program_id(0)
    nt = pl.num_programs(0)
    R = SUBLANES
    U = COMBINE_ISSUE_UNROLL
    slab = tm * R

    def groups(t):
        return (bnd_ref[t + 1] - bnd_ref[t] + U - 1) // U

    def prepare(t, buf):
        def zero(k, carry):
            planes[buf, pl.ds(pl.multiple_of(k * slab, slab), slab), :] = jnp.zeros((slab, LANES), F32)
            return carry

        lax.fori_loop(0, kmax_ref[t], zero, 0)
        first = bnd_ref[t]
        final = bnd_ref[t + 1] - 1

        def issue(g, carry):
            for u in range(U):
                a = first + g * U + u
                live = a <= final
                a = jnp.minimum(a, final)
                s = pl.multiple_of(src_ref[a], R)
                d = pl.multiple_of(jnp.where(live, dst_ref[a], N_EXPERTS * slab + u * R), R)
                pltpu.make_async_copy(y_hbm.at[pl.ds(s, R)], planes.at[buf, pl.ds(d, R)], sem.at[buf]).start()
            return carry

        lax.fori_loop(0, groups(t), issue, 0)

    @pl.when(i == 0)
    def _():
        prepare(0, 0)

    @pl.when(i + 1 < nt)
    def _():
        prepare(i + 1, (i + 1) % 2)

    buf = i % 2
    n_groups = groups(i)
    p = 1
    while p * U <= tm * N_EXPERTS:
        @pl.when((n_groups & p) != 0)
        def _(p=p):
            pltpu.make_async_copy(y_hbm.at[pl.ds(0, p * U * R)], planes.at[buf, pl.ds(0, p * U * R)],
                                  sem.at[buf]).wait()
        p *= 2

    kmax = kmax_ref[i]

    def add(k, carry):
        planes[buf, 0:slab, :] += planes[buf, pl.ds(pl.multiple_of(k * slab, slab), slab), :]
        return carry

    lax.fori_loop(1, kmax, add, 0)

    @pl.when(kmax > 0)
    def _():
        for j in range(R):
            o_ref[:, j * LANES:(j + 1) * LANES] = (x_ref[:, j * LANES:(j + 1) * LANES]
                                                   + planes[buf, pl.ds(j, tm, stride=R), :])

    @pl.when(kmax == 0)
    def _():
        o_ref[...] = x_ref[...]


def combine_expert_outputs(x, y_tiles, token_rows):
    n, D = x.shape
    A = token_rows.shape[0]
    tm = COMBINE_TILE
    nt = n // tm
    tok, src = lax.sort_key_val(token_rows, jnp.arange(A, dtype=jnp.int32))
    pos = jnp.arange(A, dtype=jnp.int32)
    is_start = jnp.concatenate([jnp.ones((1,), bool), tok[1:] != tok[:-1]])
    rank = pos - lax.cummax(jnp.where(is_start, pos, 0))
    tile = tok // tm
    dst = (rank * tm + tok % tm) * SUBLANES
    edges = jnp.arange(nt + 1, dtype=jnp.int32) * tm
    bounds = jnp.sum(tok[None, :] < edges[:, None], axis=1).astype(jnp.int32)
    kmax = jnp.max(jnp.where(tile[None, :] == jnp.arange(nt, dtype=jnp.int32)[:, None], rank[None, :] + 1, 0),
                   axis=1).astype(jnp.int32)
    row = lambda i, *_: (i, 0)
    return pl.pallas_call(
        functools.partial(_combine_kernel, tm=tm),
        grid_spec=pltpu.PrefetchScalarGridSpec(
            num_scalar_prefetch=4,
            grid=(nt,),
            in_specs=[pl.BlockSpec(memory_space=pl.ANY), pl.BlockSpec((tm, D), row)],
            out_specs=pl.BlockSpec((tm, D), row),
            scratch_shapes=[
                pltpu.VMEM((2, (N_EXPERTS * tm + COMBINE_ISSUE_UNROLL) * SUBLANES, LANES), F32),
                pltpu.SemaphoreType.DMA((2,))],
        ),
        out_shape=jax.ShapeDtypeStruct((n, D), F32),
        compiler_params=_cparams(("arbitrary",)),
        name="combine",
    )(src * SUBLANES, dst, bounds, kmax, y_tiles, x)


def _final_norm_kernel(x_ref, g_ref, o_ref):
    x = x_ref[0]
    o_ref[0] = x * lax.rsqrt(jnp.mean(x * x, axis=-1, keepdims=True) + EPS) * g_ref[...]


def final_norm(x, g, tm):
    B, n, D = x.shape
    return pl.pallas_call(
        _final_norm_kernel,
        grid=(B, n // tm),
        in_specs=[pl.BlockSpec((1, tm, D), lambda b, i: (b, i, 0)), pl.BlockSpec((1, D), lambda b, i: (0, 0))],
        out_specs=pl.BlockSpec((1, tm, D), lambda b, i: (b, i, 0)),
        out_shape=jax.ShapeDtypeStruct((B, n, D), F32),
        compiler_params=_cparams(("parallel", "parallel")),
        name="final_norm",
    )(x, g.reshape(1, D))


def route(aff, cap):
    return lax.top_k(jnp.swapaxes(aff, 1, 2), cap)


TOKEN_TILE = 512


def kernel(x, c, ctx, c_ctx, w_ada, b_ada, norm1_g, norm2_g, w_in, w_out, gla_w_gate, gla_b_gate, gla_norm_g,
           gmlp_norm_g, gmlp_w_s, gmlp_b_s, lru_conv_w, lru_conv_b, lru_w_r, lru_b_r, lru_w_i, lru_b_i,
           lru_lambda, nat_rpb, moe_w_router, moe_w_gate, moe_w_up, moe_w_down, final_norm_g):
    B, N, D = x.shape
    M = ctx.shape[1]
    E = N_EXPERTS
    tm = TOKEN_TILE
    cap_l = max(1, EC_CAPACITY * N // E)
    cap_c = max(1, EC_CAPACITY * M // E)
    sc = jax.nn.silu(c)
    scc = jax.nn.silu(c_ctx)
    xc = ctx
    batch_ids = jnp.arange(B, dtype=jnp.int32)[:, None, None]
    tables = rope_tables(N)
    for l in range(DEPTH):
        need_ctx = l < DEPTH - 1
        mod = sc @ w_ada[l] + b_ada[l]
        mod_c = jnp.broadcast_to((scc @ w_ada[l] + b_ada[l])[None], (B, 6 * D))
        sh1, sc1, g1, sh2, sc2, g2 = jnp.split(mod, 6, axis=-1)
        csh1, csc1, cg1, csh2, csc2, cg2 = jnp.split(mod_c, 6, axis=-1)

        wp = permute_w_in(w_in[l])
        gla, gm, lru, nat, gate = norm_inproj(x, norm1_g[l], sc1, sh1, wp, tm, tables)
        cgla, cgm, clru, cnat, cgate = norm_inproj(xc, norm1_g[l], csc1, csh1, wp, M)

        gla_o, gla_c = gla_mixer(gla, gate, cgla, cgate, gla_w_gate[l], gla_b_gate[l], gla_norm_g[l], need_ctx, tm)
        gm_o = gmlp_mixer(gm, gmlp_norm_g[l], gmlp_w_s[l], gmlp_b_s[l], tm)
        lru_o, lru_c = lru_mixer(lru, clru, lru_conv_w[l], lru_conv_b[l], lru_w_r[l], lru_b_r[l], lru_w_i[l],
                                 lru_b_i[l], lru_lambda[l], need_ctx, tm)
        nat_o = nat_mixer(nat, cnat, nat_bias_tables(nat_rpb[l], N // GRID_W))

        wo = w_out[l].astype(BF16)
        wr = moe_w_router[l].astype(BF16)
        x, h2, aff = outproj_norm_router((gla_o, gm_o, lru_o, nat_o), wo, x, g1, norm2_g[l], sc2, sh2, wr, tm)
        gl, il = route(aff, cap_l)
        rows_l = jnp.swapaxes(il + batch_ids * N, 0, 1).reshape(E, B * cap_l)
        idx = rows_l
        gs = jnp.swapaxes(gl, 0, 1).reshape(E, B * cap_l)
        hc2 = None
        if need_ctx:
            gm_c = gmlp_mixer(cgm, gmlp_norm_g[l], gmlp_w_s[l], gmlp_b_s[l], M)
            nat_c = ctx_attention(cnat)
            xc, hc2, caff = outproj_norm_router((gla_c, gm_c, lru_c, nat_c), wo, xc, cg1, norm2_g[l], csc2, csh2,
                                                wr, M)
            gc, ic = route(caff, cap_c)
            rows_c = jnp.swapaxes(ic + batch_ids * M, 0, 1).reshape(E, B * cap_c)
            idx = jnp.concatenate([rows_l, rows_c], axis=1)
            gs = jnp.concatenate([gs, jnp.swapaxes(gc, 0, 1).reshape(E, B * cap_c)], axis=1)
        ys = moe_ffn((idx * SUBLANES).reshape(-1).astype(jnp.int32), h2, hc2, moe_w_gate, moe_w_up, moe_w_down, l,
                     gs[..., None], g2, cg2, B * cap_l, tf=512)
        x = combine_expert_outputs(x.reshape(B * N, D), ys[0], rows_l.reshape(-1).astype(jnp.int32)).reshape(B, N, D)
        if need_ctx:
            xc = xc.reshape(B * M, D).at[rows_c.reshape(-1)].add(ys[1].reshape(-1, D)).reshape(B, M, D)
    return final_norm(x, final_norm_g, tm)
```

```python
import functools

import numpy as np
import jax
import jax.numpy as jnp
from jax import lax
from jax.experimental import pallas as pl
from jax.experimental.pallas import tpu as pltpu

D_MODEL = 1024
DEPTH = 4
GRID_W = 64
N_GROUPS = 4
GROUP_W = D_MODEL // N_GROUPS
GLA_HEADS = 4
GLA_DK = GROUP_W // GLA_HEADS
GLA_W = GLA_HEADS * GLA_DK
GLA_GATE_RANK = 16
GLA_TAU = 16.0
GLA_CHUNK = 64
GMLP_GROUPS = 4
GMLP_W = GROUP_W
GMLP_CHUNK = 128
LRU_W = GROUP_W
LRU_BLOCKS = 4
LRU_C = 8.0
CONV_W = 4
NAT_HEADS = 4
NAT_DH = GROUP_W // NAT_HEADS
NAT_W = NAT_HEADS * NAT_DH
NAT_KR_MAX = 8
NAT_KC = 16
N_EXPERTS = 16
EXPERT_FF = 2 * D_MODEL
EC_CAPACITY = 2
ROPE_BASE = 10000.0
EPS = 1e-6

IN_SIZES = (GLA_W, GLA_W, GLA_W, GLA_W, GLA_GATE_RANK, GLA_GATE_RANK,
            GMLP_W, GMLP_W, LRU_W, LRU_W, NAT_W, NAT_W, NAT_W)
IN_COLS = sum(IN_SIZES)

V7X_VMEM_LIMIT_BYTES = 56 * 1024 * 1024
LANES = 128
SUBLANES = 8
F32 = jnp.float32
BF16 = jnp.bfloat16
NEG_BIG = -1e30
GATE_PAD = LANES

NT_DIMS = (((1,), (1,)), ((), ()))
TN_DIMS = (((0,), (0,)), ((), ()))


def _cparams(sem):
    return pltpu.CompilerParams(dimension_semantics=sem, vmem_limit_bytes=V7X_VMEM_LIMIT_BYTES)


def _dot(a, b):
    return jnp.dot(a, b, preferred_element_type=F32)


def _split3(x):
    hi = x.astype(BF16)
    r1 = x - hi.astype(F32)
    mid = r1.astype(BF16)
    lo = (r1 - mid.astype(F32)).astype(BF16)
    return hi, mid, lo


def _head_block_mask(n):
    r = lax.broadcasted_iota(jnp.int32, (n, n), 0) // GLA_DK
    c = lax.broadcasted_iota(jnp.int32, (n, n), 1) // GLA_DK
    return r == c


def _axial_rope(u, cos, sin):
    lane = lax.broadcasted_iota(jnp.int32, u.shape, 1)
    first = (lane % (GLA_DK // 2)) < (GLA_DK // 4)
    partner = jnp.where(first, pltpu.roll(u, GLA_W - GLA_DK // 4, axis=1), pltpu.roll(u, GLA_DK // 4, axis=1))
    return u * cos + partner * sin


def _norm_inproj_kernel(*refs, rope):
    refs = list(refs)
    x_ref, g_ref, sc_ref, sh_ref, w_ref = refs[:5]
    refs = refs[5:]
    if rope:
        cos_ref, sin_ref = refs[:2]
        refs = refs[2:]
    gla_ref, gmlp_ref, lru_ref, nat_ref, gate_ref = refs
    x = x_ref[0]
    y = x * lax.rsqrt(jnp.mean(x * x, axis=-1, keepdims=True) + EPS)
    h = ((y * g_ref[...]) * (1.0 + sc_ref[0]) + sh_ref[0]).astype(BF16)
    o = 0
    for ref in (gla_ref, gmlp_ref, lru_ref, nat_ref, gate_ref):
        w = ref.shape[-1]
        ref[0] = _dot(h, w_ref[:, o:o + w]).astype(ref.dtype)
        o += w
    if rope:
        for c in range(2):
            cols = slice(c * GLA_W, (c + 1) * GLA_W)
            gla_ref[0, :, cols] = _axial_rope(gla_ref[0, :, cols], cos_ref[...], sin_ref[...])


def permute_w_in(w):
    g0 = 4 * GLA_W
    g1 = g0 + 2 * GLA_GATE_RANK
    pad = jnp.zeros((w.shape[0], GATE_PAD - 2 * GLA_GATE_RANK), w.dtype)
    return jnp.concatenate([w[:, :g0], w[:, g1:], w[:, g0:g1], pad], axis=1).astype(BF16)


def norm_inproj(x, g, scale, shift, wp, tm, tables=None):
    B, n, D = x.shape
    C = wp.shape[1]
    widths = (4 * GLA_W, 2 * GMLP_W, 2 * LRU_W, 3 * NAT_W, GATE_PAD)
    dtypes = (F32, F32, F32, BF16, F32)
    row = lambda b, i: (b, i, 0)
    per_b = lambda b, i: (b, 0, 0)
    full = lambda b, i: (0, 0)
    rope_specs = [pl.BlockSpec((tm, GLA_W), lambda b, i: (i, 0))] * 2 if tables is not None else []
    return pl.pallas_call(
        functools.partial(_norm_inproj_kernel, rope=tables is not None),
        grid=(B, n // tm),
        in_specs=[
            pl.BlockSpec((1, tm, D), row),
            pl.BlockSpec((1, D), full),
            pl.BlockSpec((1, 1, D), per_b),
            pl.BlockSpec((1, 1, D), per_b),
            pl.BlockSpec((D, C), full),
        ] + rope_specs,
        out_specs=[pl.BlockSpec((1, tm, w), row) for w in widths],
        out_shape=[jax.ShapeDtypeStruct((B, n, w), dt) for w, dt in zip(widths, dtypes)],
        compiler_params=_cparams(("parallel", "parallel")),
        name="norm_inproj",
    )(x, g.reshape(1, D), scale.reshape(B, 1, D), shift.reshape(B, 1, D), wp, *(tables or ()))


def _gla_kernel(*refs, tm, reverse, combine):
    refs = list(refs)
    q_ref, k_ref, v_ref, gate_ref, wg_ref, bg_ref, tri_ref, ones_ref, s0_ref = refs[:9]
    refs = refs[9:]
    if combine:
        ob_ref, og_ref, ng_ref = refs[:3]
        refs = refs[3:]
    o_ref, sfin_ref, st, obuf = refs
    i = pl.program_id(1)
    nch = tm // GLA_CHUNK

    @pl.when(i == 0)
    def _():
        st[...] = s0_ref[0]

    q = q_ref[0] * (GLA_DK ** -0.5)
    k = k_ref[0]
    a = _dot(gate_ref[0].astype(BF16), wg_ref[...]) + bg_ref[...]
    la = jax.nn.log_sigmoid(a) / GLA_TAU
    pieces = _split3(la)
    b = sum(_dot(tri_ref[...], p) for p in pieces)
    bl = sum(_dot(ones_ref[...], p) for p in pieces)
    q_in = (q * jnp.exp(b)).astype(BF16)
    k_in = (k * jnp.exp(-b)).astype(BF16)
    k_end = (k * jnp.exp(bl - b)).astype(BF16)
    dec = jnp.exp(bl)
    vb = v_ref[0].astype(BF16)

    head_mask = _head_block_mask(GLA_W)
    cr = lax.broadcasted_iota(jnp.int32, (GLA_CHUNK, GLA_W), 0)
    cs = lax.broadcasted_iota(jnp.int32, (GLA_CHUNK, GLA_W), 1) % GLA_CHUNK
    causal = (cs >= cr) if reverse else (cs <= cr)
    zero = jnp.zeros((), BF16)
    chunks = range(nch - 1, -1, -1) if reverse else range(nch)
    for c in chunks:
        sl = slice(c * GLA_CHUNK, (c + 1) * GLA_CHUNK)
        kbd_t = jnp.where(head_mask, jnp.concatenate([k_in[sl]] * GLA_HEADS, axis=0), zero)
        vbd = jnp.where(head_mask, jnp.concatenate([vb[sl]] * GLA_HEADS, axis=0), zero)
        att = lax.dot_general(q_in[sl], kbd_t, NT_DIMS, preferred_element_type=F32)
        att = jnp.where(causal, att, 0.0).astype(BF16)
        s_t = st[...]
        o = _dot(att, vbd) + lax.dot_general(q_in[sl], s_t.astype(BF16), NT_DIMS, preferred_element_type=F32)
        upd = lax.dot_general(vb[sl], k_end[sl], TN_DIMS, preferred_element_type=F32)
        st[...] = s_t * dec[c * GLA_CHUNK:c * GLA_CHUNK + 1] + jnp.where(head_mask, upd, 0.0)
        obuf[sl, :] = o

    sfin_ref[0] = st[...]

    if not combine:
        o_ref[0] = obuf[...]
    else:
        o = obuf[...] + ob_ref[0]
        sq_hi, sq_mid, _ = _split3(o * o)
        avg = jnp.where(head_mask, 1.0 / GLA_DK, 0.0).astype(BF16)
        ms = _dot(sq_hi, avg) + _dot(sq_mid, avg)
        og = og_ref[0]
        o_ref[0] = (o * lax.rsqrt(ms + EPS) * ng_ref[...] * (og * jax.nn.sigmoid(og))).astype(o_ref.dtype)


def _chunk_sum_matrices(tm, reverse):
    r = np.arange(tm)[:, None]
    c = np.arange(tm)[None, :]
    same = (r // GLA_CHUNK) == (c // GLA_CHUNK)
    tri = same & ((c >= r) if reverse else (c <= r))
    return jnp.asarray(tri, BF16), jnp.asarray(same, BF16)


def gla_direction(gla, gate, wg, bg, s0, tm, reverse, combine=None):
    B, L, _ = gla.shape
    nt = L // tm
    W = GLA_W
    blk = (lambda i: nt - 1 - i) if reverse else (lambda i: i)
    col = lambda j: (lambda b, i: (b, blk(i), j))
    full = lambda b, i: (0, 0)
    tri, ones = _chunk_sum_matrices(tm, reverse)
    args = [gla, gla, gla, gate, wg, bg, tri, ones, s0]
    specs = [
        pl.BlockSpec((1, tm, W), col(0)), pl.BlockSpec((1, tm, W), col(1)), pl.BlockSpec((1, tm, W), col(2)),
        pl.BlockSpec((1, tm, GATE_PAD), col(0)),
        pl.BlockSpec((GATE_PAD, W), full), pl.BlockSpec((1, W), full),
        pl.BlockSpec((tm, tm), full), pl.BlockSpec((tm, tm), full),
        pl.BlockSpec((1, W, W), lambda b, i: (b, 0, 0)),
    ]
    if combine is not None:
        ob, norm_g = combine
        args += [ob, gla, norm_g]
        specs += [pl.BlockSpec((1, tm, W), col(0)), pl.BlockSpec((1, tm, W), col(3)), pl.BlockSpec((1, W), full)]
    return pl.pallas_call(
        functools.partial(_gla_kernel, tm=tm, reverse=reverse, combine=combine is not None),
        grid=(B, nt),
        in_specs=specs,
        out_specs=[pl.BlockSpec((1, tm, W), col(0)), pl.BlockSpec((1, W, W), lambda b, i: (b, 0, 0))],
        out_shape=[jax.ShapeDtypeStruct((B, L, W), F32 if combine is None else BF16),
                   jax.ShapeDtypeStruct((B, W, W), F32)],
        scratch_shapes=[pltpu.VMEM((W, W), F32), pltpu.VMEM((tm, W), F32)],
        compiler_params=_cparams(("parallel", "arbitrary")),
        name="gla_bwd" if reverse else "gla_fwd",
    )(*args)


def rope_tables(n):
    quarter = GLA_DK // 4
    pos = jnp.arange(n)
    inv = ROPE_BASE ** (-jnp.arange(quarter, dtype=F32) / quarter)
    row = (pos // GRID_W).astype(F32)
    colp = (pos % GRID_W).astype(F32)
    ar = row[:, None] * inv[None, :]
    ac = colp[:, None] * inv[None, :]
    cos = jnp.concatenate([jnp.cos(ar), jnp.cos(ar), jnp.cos(ac), jnp.cos(ac)], axis=1)
    sin = jnp.concatenate([-jnp.sin(ar), jnp.sin(ar), -jnp.sin(ac), jnp.sin(ac)], axis=1)
    return jnp.tile(cos, (1, GLA_HEADS)), jnp.tile(sin, (1, GLA_HEADS))


def gla_mixer(gla, gate, cgla, cgate, w_gate, b_gate, norm_g, need_ctx, tm):
    B, M, _ = cgla.shape
    zeros = jnp.zeros((B, GLA_W, GLA_W), F32)
    ng = jnp.tile(norm_g, GLA_HEADS).reshape(1, GLA_W)
    wgs, bgs = [], []
    for d in range(2):
        wg = jnp.pad(w_gate[d], ((d * GLA_GATE_RANK, GATE_PAD - (d + 1) * GLA_GATE_RANK), (0, 0)))
        wgs.append(wg.astype(BF16))
        bgs.append(b_gate[d].reshape(1, GLA_W))
    ocb, sb = gla_direction(cgla, cgate, wgs[1], bgs[1], zeros, M, True)
    ob, _ = gla_direction(gla, gate, wgs[1], bgs[1], sb, tm, True)
    if need_ctx:
        oc, sf = gla_direction(cgla, cgate, wgs[0], bgs[0], zeros, M, False, combine=(ocb, ng))
    else:
        oc, sf = gla_direction(cgla, cgate, wgs[0], bgs[0], zeros, M, False)
    o, _ = gla_direction(gla, gate, wgs[0], bgs[0], sf, tm, False, combine=(ob, ng))
    return o, oc


def _gmlp_kernel(u_ref, v_ref, g_ref, w_ref, b_ref, o_ref, *, tm):
    v = v_ref[0]
    vn = (v * lax.rsqrt(jnp.mean(v * v, axis=-1, keepdims=True) + EPS) * g_ref[...]).astype(BF16)
    grp = lax.broadcasted_iota(jnp.int32, (GMLP_CHUNK, GMLP_W), 1) // (GMLP_W // GMLP_GROUPS)
    zero = jnp.zeros((), BF16)
    for c in range(tm // GMLP_CHUNK):
        sl = slice(c * GMLP_CHUNK, (c + 1) * GMLP_CHUNK)
        mixed = b_ref[...]
        for g in range(GMLP_GROUPS):
            mixed = mixed + _dot(w_ref[g], jnp.where(grp == g, vn[sl], zero))
        o_ref[0, sl, :] = (u_ref[0, sl, :] * mixed).astype(o_ref.dtype)


def gmlp_mixer(gm, norm_g, w_s, b_s, tm):
    B, L, _ = gm.shape
    W = GMLP_W
    bias = jnp.repeat(b_s.T, W // GMLP_GROUPS, axis=1)
    return pl.pallas_call(
        functools.partial(_gmlp_kernel, tm=tm),
        grid=(B, L // tm),
        in_specs=[
            pl.BlockSpec((1, tm, W), lambda b, i: (b, i, 0)),
            pl.BlockSpec((1, tm, W), lambda b, i: (b, i, 1)),
            pl.BlockSpec((1, W), lambda b, i: (0, 0)),
            pl.BlockSpec((GMLP_GROUPS, GMLP_CHUNK, GMLP_CHUNK), lambda b, i: (0, 0, 0)),
            pl.BlockSpec((GMLP_CHUNK, W), lambda b, i: (0, 0)),
        ],
        out_specs=pl.BlockSpec((1, tm, W), lambda b, i: (b, i, 0)),
        out_shape=jax.ShapeDtypeStruct((B, L, W), BF16),
        compiler_params=_cparams(("parallel", "parallel")),
        name="gmlp",
    )(gm, gm, norm_g.reshape(1, W), w_s.astype(BF16), bias)


LRU_UNROLL = 8


def _lru_kernel(*refs, tm, reverse, combine):
    refs = list(refs)
    x_ref, xp_ref, xn_ref, cw_ref, cb_ref, wr_ref, br_ref, wi_ref, bi_ref, ncs_ref, h0_ref = refs[:11]
    refs = refs[11:]
    if combine:
        hb_ref, ly_ref = refs[:2]
        refs = refs[2:]
    o_ref, ext, a_s, b_s, carry = refs
    W = LRU_W
    i = pl.program_id(1)
    nt = pl.num_programs(1)
    t = (nt - 1 - i) if reverse else i
    H = SUBLANES

    ext[0:H] = jnp.where(t > 0, xp_ref[0], 0.0)
    ext[H:H + tm] = x_ref[0]
    ext[H + tm:2 * H + tm] = jnp.where(t < nt - 1, xn_ref[0], 0.0)
    xc = cb_ref[...]
    for tap in range(CONV_W):
        xc = xc + cw_ref[tap:tap + 1, :] * ext[H - CONV_W // 2 + tap:H - CONV_W // 2 + tap + tm]
    xcb = xc.astype(BF16)
    r = jax.nn.sigmoid(_dot(xcb, wr_ref[...]) + br_ref[...])
    ig = jax.nn.sigmoid(_dot(xcb, wi_ref[...]) + bi_ref[...])
    log_a = ncs_ref[...] * r
    a = jnp.exp(log_a)
    b = jnp.sqrt(-jnp.tanh(log_a) * (a * a + 1.0)) * (ig * xc)

    rowi = lax.broadcasted_iota(jnp.int32, (tm, W), 0) % H
    for s in (1, 2, 4):
        if reverse:
            ok = rowi < H - s
            a_sh = pltpu.roll(a, tm - s, axis=0)
            b_sh = pltpu.roll(b, tm - s, axis=0)
        else:
            ok = rowi >= s
            a_sh = pltpu.roll(a, s, axis=0)
            b_sh = pltpu.roll(b, s, axis=0)
        b = a * jnp.where(ok, b_sh, 0.0) + b
        a = a * jnp.where(ok, a_sh, 1.0)
    a_s[...] = a
    b_s[...] = b

    @pl.when(i == 0)
    def _():
        carry[...] = jnp.broadcast_to(h0_ref[0], (H, W))

    ng = tm // H

    def body(j, h):
        for u in range(LRU_UNROLL):
            g = j * LRU_UNROLL + u
            g = (ng - 1 - g) if reverse else g
            rows = pl.ds(pl.multiple_of(g * H, H), H)
            hg = b_s[rows, :] + a_s[rows, :] * h
            b_s[rows, :] = hg
            h = jnp.broadcast_to(hg[0:1] if reverse else hg[H - 1:H], (H, W))
        return h

    carry[...] = lax.fori_loop(0, ng // LRU_UNROLL, body, carry[...])

    if combine:
        o_ref[0] = ((b_s[...] + hb_ref[0]) * jax.nn.gelu(ly_ref[0])).astype(o_ref.dtype)
    else:
        o_ref[0] = b_s[...]


def lru_direction(lru, conv_w, conv_b, wr, br, wi, bi, ncs, h0, tm, reverse, hb=None):
    B, L, _ = lru.shape
    nt = L // tm
    W = LRU_W
    H = SUBLANES
    nh = L // H
    per = tm // H
    blk = (lambda i: nt - 1 - i) if reverse else (lambda i: i)
    full = lambda b, i: (0, 0)
    args = [lru, lru, lru, conv_w, conv_b, wr, br, wi, bi, ncs, h0]
    specs = [
        pl.BlockSpec((1, tm, W), lambda b, i: (b, blk(i), 0)),
        pl.BlockSpec((1, H, W), lambda b, i: (b, jnp.maximum(blk(i) * per - 1, 0), 0)),
        pl.BlockSpec((1, H, W), lambda b, i: (b, jnp.minimum((blk(i) + 1) * per, nh - 1), 0)),
        pl.BlockSpec((CONV_W, W), full), pl.BlockSpec((1, W), full),
        pl.BlockSpec((W, W), full), pl.BlockSpec((1, W), full),
        pl.BlockSpec((W, W), full), pl.BlockSpec((1, W), full),
        pl.BlockSpec((1, W), full),
        pl.BlockSpec((1, 1, W), lambda b, i: (b, 0, 0)),
    ]
    if hb is not None:
        args += [hb, lru]
        specs += [pl.BlockSpec((1, tm, W), lambda b, i: (b, blk(i), 0)),
                  pl.BlockSpec((1, tm, W), lambda b, i: (b, blk(i), 1))]
    return pl.pallas_call(
        functools.partial(_lru_kernel, tm=tm, reverse=reverse, combine=hb is not None),
        grid=(B, nt),
        in_specs=specs,
        out_specs=pl.BlockSpec((1, tm, W), lambda b, i: (b, blk(i), 0)),
        out_shape=jax.ShapeDtypeStruct((B, L, W), F32 if hb is None else BF16),
        scratch_shapes=[pltpu.VMEM((tm + 2 * H, W), F32), pltpu.VMEM((tm, W), F32), pltpu.VMEM((tm, W), F32),
                        pltpu.VMEM((H, W), F32)],
        compiler_params=_cparams(("parallel", "arbitrary")),
        name="lru_bwd" if reverse else "lru_fwd",
    )(*args)


def _block_diag(w):
    G, n, _ = w.shape
    same = jnp.eye(G, dtype=w.dtype)
    return (w[:, :, None, :] * same[:, None, :, None]).reshape(G * n, G * n)


def lru_mixer(lru, clru, conv_w, conv_b, w_r, b_r, w_i, b_i, lam, need_ctx, tm):
    B, M, _ = clru.shape
    W = LRU_W
    cb = conv_b.reshape(1, W)
    ncs = -LRU_C * jax.nn.softplus(-lam.astype(F32))
    p = [(_block_diag(w_r[d]).astype(BF16), b_r[d].reshape(1, W), _block_diag(w_i[d]).astype(BF16),
          b_i[d].reshape(1, W), ncs[d].reshape(1, W)) for d in range(2)]
    zeros = jnp.zeros((B, 1, W), F32)
    hcb = lru_direction(clru, conv_w, cb, *p[1], zeros, M, True)
    hb = lru_direction(lru, conv_w, cb, *p[1], hcb[:, 0:1], tm, True)
    hcf = lru_direction(clru, conv_w, cb, *p[0], zeros, M, False)
    out = lru_direction(lru, conv_w, cb, *p[0], hcf[:, M - 1:M], tm, False, hb=hb)
    out_c = None
    if need_ctx:
        out_c = lru_direction(clru, conv_w, cb, *p[0], zeros, M, False, hb=hcb)
    return out, out_c


NAT_QROWS = 4
NAT_KROWS = 3 * NAT_QROWS
NAT_TQ = NAT_QROWS * GRID_W
NAT_TK = NAT_KROWS * GRID_W


def _softmax_pv(s_loc, s_ctx, v_loc, v_ctx):
    m = jnp.maximum(jnp.max(s_loc, axis=-1, keepdims=True), jnp.max(s_ctx, axis=-1, keepdims=True))
    e_loc = jnp.exp(s_loc - m)
    e_ctx = jnp.exp(s_ctx - m)
    den = jnp.sum(e_loc, axis=-1, keepdims=True) + jnp.sum(e_ctx, axis=-1, keepdims=True)
    o = _dot(e_loc.astype(BF16), v_loc) + _dot(e_ctx.astype(BF16), v_ctx)
    return o / den


def _nat_kernel(q_ref, kp_ref, kc_ref, kn_ref, vp_ref, vc_ref, vn_ref, ck_ref, cv_ref, bias_ref, o_ref):
    q = q_ref[0]
    ck = ck_ref[0]
    cv = cv_ref[0]
    scale = NAT_DH ** -0.5
    outs = []
    for h in range(NAT_HEADS):
        hs = slice(h * NAT_DH, (h + 1) * NAT_DH)
        qh = q[:, hs]
        s_loc = jnp.concatenate(
            [lax.dot_general(qh, k_ref[0][:, hs], NT_DIMS, preferred_element_type=F32)
             for k_ref in (kp_ref, kc_ref, kn_ref)], axis=-1) * scale + bias_ref[0, h]
        s_ctx = lax.dot_general(qh, ck[:, hs], NT_DIMS, preferred_element_type=F32) * scale
        v_loc = jnp.concatenate([v_ref[0][:, hs] for v_ref in (vp_ref, vc_ref, vn_ref)], axis=0)
        outs.append(_softmax_pv(s_loc, s_ctx, v_loc, cv[:, hs]))
    o_ref[0] = jnp.concatenate(outs, axis=-1).astype(o_ref.dtype)


def nat_bias_tables(rpb, rows):
    kr = NAT_KR_MAX
    qc = np.arange(GRID_W)
    kcol = np.arange(GRID_W)
    cs = np.clip(qc - NAT_KC // 2, 0, GRID_W - NAT_KC)
    valid_c = (kcol[None, :] >= cs[:, None]) & (kcol[None, :] < cs[:, None] + NAT_KC)
    edge = GRID_W - NAT_KC
    padded = jnp.pad(rpb.astype(F32), ((0, 0), (0, 0), (edge, edge)), mode="edge")
    by_col = jnp.stack([padded[:, :, GRID_W - 1 - q:2 * GRID_W - 1 - q] for q in range(GRID_W)], axis=2)
    by_col = jnp.pad(by_col, ((0, 0), (NAT_KROWS, NAT_KROWS), (0, 0), (0, 0)))
    tables = []
    for r0, ks in ((0, 0), (NAT_QROWS, NAT_QROWS - kr // 2), (rows - NAT_QROWS, rows - NAT_KROWS)):
        r = r0 + np.arange(NAT_QROWS)
        krow = ks + np.arange(NAT_KROWS)
        rs = np.clip(r - kr // 2, 0, rows - kr)
        valid_r = (krow[None, :] >= rs[:, None]) & (krow[None, :] < rs[:, None] + kr)
        starts = ks - r + NAT_KR_MAX - 1 + NAT_KROWS
        bias = jnp.stack([by_col[:, int(s):int(s) + NAT_KROWS] for s in starts], axis=1)
        bias = bias.transpose(0, 1, 3, 2, 4)
        mask = valid_r[:, None, :, None] & valid_c[None, :, None, :]
        bias = jnp.where(jnp.asarray(mask)[None], bias, NEG_BIG)
        tables.append(bias.reshape(rpb.shape[0], NAT_TQ, NAT_TK))
    return jnp.stack(tables)


def nat_mixer(nat, cnat, bias):
    B, N, _ = nat.shape
    M = cnat.shape[1]
    W = NAT_W
    nb = N // NAT_TQ
    T = NAT_TQ
    centre = lambda j: jnp.clip(j, 1, nb - 2)
    near = lambda c, d: (lambda b, j: (b, centre(j) + d, c))
    variant = lambda b, j: (jnp.where(j == 0, 0, jnp.where(j == nb - 1, 2, 1)), 0, 0, 0)
    return pl.pallas_call(
        _nat_kernel,
        grid=(B, nb),
        in_specs=[
            pl.BlockSpec((1, T, W), lambda b, j: (b, j, 0)),
            pl.BlockSpec((1, T, W), near(1, -1)), pl.BlockSpec((1, T, W), near(1, 0)), pl.BlockSpec((1, T, W), near(1, 1)),
            pl.BlockSpec((1, T, W), near(2, -1)), pl.BlockSpec((1, T, W), near(2, 0)), pl.BlockSpec((1, T, W), near(2, 1)),
            pl.BlockSpec((1, M, W), lambda b, j: (b, 0, 1)),
            pl.BlockSpec((1, M, W), lambda b, j: (b, 0, 2)),
            pl.BlockSpec((1, NAT_HEADS, NAT_TQ, NAT_TK), variant),
        ],
        out_specs=pl.BlockSpec((1, T, W), lambda b, j: (b, j, 0)),
        out_shape=jax.ShapeDtypeStruct((B, N, W), BF16),
        compiler_params=_cparams(("parallel", "arbitrary")),
        name="nat",
    )(nat, nat, nat, nat, nat, nat, nat, cnat, cnat, bias)


def _ctx_attn_kernel(q_ref, k_ref, v_ref, o_ref):
    q = q_ref[0]
    k = k_ref[0]
    v = v_ref[0]
    scale = NAT_DH ** -0.5
    outs = []
    for h in range(NAT_HEADS):
        hs = slice(h * NAT_DH, (h + 1) * NAT_DH)
        s = lax.dot_general(q[:, hs], k[:, hs], NT_DIMS, preferred_element_type=F32) * scale
        e = jnp.exp(s - jnp.max(s, axis=-1, keepdims=True))
        outs.append(_dot(e.astype(BF16), v[:, hs]) / jnp.sum(e, axis=-1, keepdims=True))
    o_ref[0] = jnp.concatenate(outs, axis=-1).astype(o_ref.dtype)


def ctx_attention(cnat):
    B, M, _ = cnat.shape
    W = NAT_W
    return pl.pallas_call(
        _ctx_attn_kernel,
        grid=(B,),
        in_specs=[pl.BlockSpec((1, M, W), lambda b, c=c: (b, 0, c)) for c in range(3)],
        out_specs=pl.BlockSpec((1, M, W), lambda b: (b, 0, 0)),
        out_shape=jax.ShapeDtypeStruct((B, M, W), BF16),
        compiler_params=_cparams(("parallel",)),
        name="ctx_attn",
    )(cnat, cnat, cnat)


def _outproj_kernel(m0_ref, m1_ref, m2_ref, m3_ref, w_ref, x_ref, g1_ref, n2_ref, sc_ref, sh_ref, wr_ref,
                    xo_ref, h2_ref, aff_ref):
    y = 0.0
    for gi, m_ref in enumerate((m0_ref, m1_ref, m2_ref, m3_ref)):
        y = y + _dot(m_ref[0].astype(BF16), w_ref[gi * GROUP_W:(gi + 1) * GROUP_W, :])
    x = x_ref[0] + g1_ref[0] * y
    xo_ref[0] = x
    xn = x * lax.rsqrt(jnp.mean(x * x, axis=-1, keepdims=True) + EPS)
    h2 = (xn * n2_ref[...]) * (1.0 + sc_ref[0]) + sh_ref[0]
    tm = h2.shape[0]
    for j in range(SUBLANES):
        h2_ref[pl.ds(j, tm, stride=SUBLANES), :] = h2[:, j * LANES:(j + 1) * LANES]
    logits = _dot(h2.astype(BF16), wr_ref[...])
    e = jnp.exp(logits - jnp.max(logits, axis=-1, keepdims=True))
    aff_ref[0] = e / jnp.sum(e, axis=-1, keepdims=True)


def outproj_norm_router(parts, w_out, x, g1, n2, scale2, shift2, w_router, tm):
    B, n, D = x.shape
    E = w_router.shape[1]
    row = lambda b, i: (b, i, 0)
    per_b = lambda b, i: (b, 0, 0)
    full = lambda b, i: (0, 0)
    return pl.pallas_call(
        _outproj_kernel,
        grid=(B, n // tm),
        in_specs=[pl.BlockSpec((1, tm, GROUP_W), row)] * N_GROUPS + [
            pl.BlockSpec((D, D), full),
            pl.BlockSpec((1, tm, D), row),
            pl.BlockSpec((1, 1, D), per_b),
            pl.BlockSpec((1, D), full),
            pl.BlockSpec((1, 1, D), per_b),
            pl.BlockSpec((1, 1, D), per_b),
            pl.BlockSpec((D, E), full),
        ],
        out_specs=[
            pl.BlockSpec((1, tm, D), row),
            pl.BlockSpec((tm * SUBLANES, LANES), lambda b, i: (b * (n // tm) + i, 0)),
            pl.BlockSpec((1, tm, E), row),
        ],
        out_shape=[
            jax.ShapeDtypeStruct((B, n, D), F32),
            jax.ShapeDtypeStruct((B * n * SUBLANES, LANES), F32),
            jax.ShapeDtypeStruct((B, n, E), F32),
        ],
        compiler_params=_cparams(("parallel", "parallel")),
        name="outproj_norm_router",
    )(*parts, w_out, x, g1.reshape(B, 1, D), n2.reshape(1, D), scale2.reshape(B, 1, D), shift2.reshape(B, 1, D),
      w_router)


MOE_ROW_CHUNK = 512
MOE_ISSUE_UNROLL = 8


def _moe_ffn_kernel(*refs, n_lat, n_ctx, n_batch):
    refs = list(refs)
    idx_ref, h_hbm = refs[:2]
    refs = refs[2:]
    if n_ctx:
        hc_hbm = refs.pop(0)
    wg_ref, wu_ref, wd_ref, gate_ref, g2_ref = refs[:5]
    refs = refs[5:]
    if n_ctx:
        cg2_ref = refs.pop(0)
    yl_hbm = refs.pop(0)
    if n_ctx:
        yc_ref = refs.pop(0)
    xf, xb, acc, ytiles, sem, out_sem = refs
    e = pl.program_id(0)
    f = pl.program_id(1)
    n_exp = pl.num_programs(0)
    last = pl.num_programs(1) - 1
    T = n_lat + n_ctx

    R = SUBLANES

    def start_rows(src, expert, first, count):
        def body(i, carry):
            for u in range(MOE_ISSUE_UNROLL):
                s = first + i * MOE_ISSUE_UNROLL + u
                r = pl.multiple_of(idx_ref[expert * T + s], R)
                pltpu.make_async_copy(src.at[pl.ds(r, R)], xf.at[pl.ds(pl.multiple_of(s * R, R), R)],
                                      sem.at[0]).start()
            return carry

        lax.fori_loop(0, count // MOE_ISSUE_UNROLL, body, 0)

    def start_gather(expert):
        start_rows(h_hbm, expert, 0, n_lat)
        if n_ctx:
            start_rows(hc_hbm, expert, n_lat, n_ctx)

    @pl.when(f == 0)
    def _():
        @pl.when(e == 0)
        def _():
            start_gather(e)

        pltpu.make_async_copy(h_hbm.at[pl.ds(0, T * R)], xf, sem.at[0]).wait()
        for j in range(R):
            xb[:, j * LANES:(j + 1) * LANES] = xf[pl.ds(j, T, stride=R), :].astype(BF16)

        @pl.when(e + 1 < n_exp)
        def _():
            start_gather(e + 1)

        acc[...] = jnp.zeros_like(acc)
        if n_ctx:
            yc_ref[...] = jnp.zeros_like(yc_ref)

    wg = wg_ref[0, 0].astype(BF16)
    wu = wu_ref[0, 0].astype(BF16)
    wd = wd_ref[0, 0].astype(BF16)

    def ffn(x):
        hg = _dot(x, wg)
        hu = _dot(x, wu)
        return _dot((hg * jax.nn.sigmoid(hg) * hu).astype(BF16), wd)

    def lat_rows(first, count, j):
        return pl.ds(first * R + j, count, stride=R)

    for i in range(n_lat // MOE_ROW_CHUNK):
        rows = slice(i * MOE_ROW_CHUNK, (i + 1) * MOE_ROW_CHUNK)
        acc[rows, :] += ffn(xb[rows, :])
    if n_ctx:
        yc_ref[0] += ffn(xb[n_lat:T, :])

    def out_copy(expert):
        dst = yl_hbm.at[pl.ds(pl.multiple_of(expert * (n_lat * R), n_lat * R), n_lat * R)]
        return pltpu.make_async_copy(ytiles, dst, out_sem.at[0])

    @pl.when(f == last)
    def _():
        @pl.when(e > 0)
        def _():
            out_copy(e - 1).wait()

        per_l = n_lat // n_batch
        for b in range(n_batch):
            rows = slice(b * per_l, (b + 1) * per_l)
            gate = gate_ref[0, rows, :]
            for j in range(R):
                lanes = slice(j * LANES, (j + 1) * LANES)
                ytiles[lat_rows(b * per_l, per_l, j), :] = acc[rows, lanes] * gate * g2_ref[b:b + 1, lanes]
        out_copy(e).start()

        @pl.when(e == n_exp - 1)
        def _():
            out_copy(e).wait()

        if n_ctx:
            per_c = n_ctx // n_batch
            for b in range(n_batch):
                rows = slice(b * per_c, (b + 1) * per_c)
                yc_ref[0, rows, :] = (yc_ref[0, rows, :] * gate_ref[0, n_lat + b * per_c:n_lat + (b + 1) * per_c, :]
                                      * cg2_ref[b:b + 1, :])


def moe_ffn(idx, h2, hc2, w_gate, w_up, w_down, layer, gates, g2, cg2, n_lat, tf):
    E, T, _ = gates.shape
    n_ctx = T - n_lat
    D = SUBLANES * LANES
    F = w_gate.shape[3]
    B = g2.shape[0]
    any_spec = pl.BlockSpec(memory_space=pl.ANY)
    per_e = lambda e, f, idx_ref: (e, 0, 0)
    full = lambda e, f, idx_ref: (0, 0)
    args = [h2] + ([hc2] if n_ctx else []) + [w_gate, w_up, w_down, gates, g2] + ([cg2] if n_ctx else [])
    specs = [any_spec] + ([any_spec] if n_ctx else []) + [
        pl.BlockSpec((1, 1, D, tf), lambda e, f, idx_ref: (layer, e, 0, f)),
        pl.BlockSpec((1, 1, D, tf), lambda e, f, idx_ref: (layer, e, 0, f)),
        pl.BlockSpec((1, 1, tf, D), lambda e, f, idx_ref: (layer, e, f, 0)),
        pl.BlockSpec((1, T, 1), per_e),
        pl.BlockSpec((B, D), full),
    ] + ([pl.BlockSpec((B, D), full)] if n_ctx else [])
    out_specs = [any_spec] + ([pl.BlockSpec((1, n_ctx, D), per_e)] if n_ctx else [])
    out_shape = [jax.ShapeDtypeStruct((E * n_lat * SUBLANES, LANES), F32)] + (
        [jax.ShapeDtypeStruct((E, n_ctx, D), F32)] if n_ctx else [])
    return pl.pallas_call(
        functools.partial(_moe_ffn_kernel, n_lat=n_lat, n_ctx=n_ctx, n_batch=B),
        grid_spec=pltpu.PrefetchScalarGridSpec(
            num_scalar_prefetch=1,
            grid=(E, F // tf),
            in_specs=specs,
            out_specs=out_specs,
            scratch_shapes=[pltpu.VMEM((T * SUBLANES, LANES), F32), pltpu.VMEM((T, D), BF16),
                            pltpu.VMEM((n_lat, D), F32), pltpu.VMEM((n_lat * SUBLANES, LANES), F32),
                            pltpu.SemaphoreType.DMA((1,)), pltpu.SemaphoreType.DMA((1,))],
        ),
        out_shape=out_shape,
        compiler_params=_cparams(("arbitrary", "arbitrary")),
        name="moe_ffn",
    )(idx, *args)


COMBINE_TILE = 256
COMBINE_ISSUE_UNROLL = 8


def _combine_kernel(src_ref, dst_ref, bnd_ref, kmax_ref, y_hbm, x_ref, o_ref, planes, sem, *, tm):
    i = pl.program_id(0)
    nt = pl.num_programs(0)
    R = SUBLANES
    U = COMBINE_ISSUE_UNROLL
    slab = tm * R

    def groups(t):
        return (bnd_ref[t + 1] - bnd_ref[t] + U - 1) // U

    def prepare(t, buf):
        def zero(k, carry):
            planes[buf, pl.ds(pl.multiple_of(k * slab, slab), slab), :] = jnp.zeros((slab, LANES), F32)
            return carry

        lax.fori_loop(0, kmax_ref[t], zero, 0)
        first = bnd_ref[t]
        final = bnd_ref[t + 1] - 1

        def issue(g, carry):
            for u in range(U):
                a = first + g * U + u
                live = a <= final
                a = jnp.minimum(a, final)
                s = pl.multiple_of(src_ref[a], R)
                d = pl.multiple_of(jnp.where(live, dst_ref[a], N_EXPERTS * slab + u * R), R)
                pltpu.make_async_copy(y_hbm.at[pl.ds(s, R)], planes.at[buf, pl.ds(d, R)], sem.at[buf]).start()
            return carry

        lax.fori_loop(0, groups(t), issue, 0)

    @pl.when(i == 0)
    def _():
        prepare(0, 0)

    @pl.when(i + 1 < nt)
    def _():
        prepare(i + 1, (i + 1) % 2)

    buf = i % 2
    n_groups = groups(i)
    p = 1
    while p * U <= tm * N_EXPERTS:
        @pl.when((n_groups & p) != 0)
        def _(p=p):
            pltpu.make_async_copy(y_hbm.at[pl.ds(0, p * U * R)], planes.at[buf, pl.ds(0, p * U * R)],
                                  sem.at[buf]).wait()
        p *= 2

    kmax = kmax_ref[i]

    def add(k, carry):
        planes[buf, 0:slab, :] += planes[buf, pl.ds(pl.multiple_of(k * slab, slab), slab), :]
        return carry

    lax.fori_loop(1, kmax, add, 0)

    @pl.when(kmax > 0)
    def _():
        for j in range(R):
            o_ref[:, j * LANES:(j + 1) * LANES] = (x_ref[:, j * LANES:(j + 1) * LANES]
                                                   + planes[buf, pl.ds(j, tm, stride=R), :])

    @pl.when(kmax == 0)
    def _():
        o_ref[...] = x_ref[...]


def combine_expert_outputs(x, y_tiles, token_rows):
    n, D = x.shape
    A = token_rows.shape[0]
    tm = COMBINE_TILE
    nt = n // tm
    tok, src = lax.sort_key_val(token_rows, jnp.arange(A, dtype=jnp.int32))
    pos = jnp.arange(A, dtype=jnp.int32)
    is_start = jnp.concatenate([jnp.ones((1,), bool), tok[1:] != tok[:-1]])
    rank = pos - lax.cummax(jnp.where(is_start, pos, 0))
    tile = tok // tm
    dst = (rank * tm + tok % tm) * SUBLANES
    edges = jnp.arange(nt + 1, dtype=jnp.int32) * tm
    bounds = jnp.sum(tok[None, :] < edges[:, None], axis=1).astype(jnp.int32)
    kmax = jnp.max(jnp.where(tile[None, :] == jnp.arange(nt, dtype=jnp.int32)[:, None], rank[None, :] + 1, 0),
                   axis=1).astype(jnp.int32)
    row = lambda i, *_: (i, 0)
    return pl.pallas_call(
        functools.partial(_combine_kernel, tm=tm),
        grid_spec=pltpu.PrefetchScalarGridSpec(
            num_scalar_prefetch=4,
            grid=(nt,),
            in_specs=[pl.BlockSpec(memory_space=pl.ANY), pl.BlockSpec((tm, D), row)],
            out_specs=pl.BlockSpec((tm, D), row),
            scratch_shapes=[
                pltpu.VMEM((2, (N_EXPERTS * tm + COMBINE_ISSUE_UNROLL) * SUBLANES, LANES), F32),
                pltpu.SemaphoreType.DMA((2,))],
        ),
        out_shape=jax.ShapeDtypeStruct((n, D), F32),
        compiler_params=_cparams(("arbitrary",)),
        name="combine",
    )(src * SUBLANES, dst, bounds, kmax, y_tiles, x)


def _final_norm_kernel(x_ref, g_ref, o_ref):
    x = x_ref[0]
    o_ref[0] = x * lax.rsqrt(jnp.mean(x * x, axis=-1, keepdims=True) + EPS) * g_ref[...]


def final_norm(x, g, tm):
    B, n, D = x.shape
    return pl.pallas_call(
        _final_norm_kernel,
        grid=(B, n // tm),
        in_specs=[pl.BlockSpec((1, tm, D), lambda b, i: (b, i, 0)), pl.BlockSpec((1, D), lambda b, i: (0, 0))],
        out_specs=pl.BlockSpec((1, tm, D), lambda b, i: (b, i, 0)),
        out_shape=jax.ShapeDtypeStruct((B, n, D), F32),
        compiler_params=_cparams(("parallel", "parallel")),
        name="final_norm",
    )(x, g.reshape(1, D))


def route(aff, cap):
    return lax.top_k(jnp.swapaxes(aff, 1, 2), cap)


TOKEN_TILE = 512


def kernel(x, c, ctx, c_ctx, w_ada, b_ada, norm1_g, norm2_g, w_in, w_out, gla_w_gate, gla_b_gate, gla_norm_g,
           gmlp_norm_g, gmlp_w_s, gmlp_b_s, lru_conv_w, lru_conv_b, lru_w_r, lru_b_r, lru_w_i, lru_b_i,
           lru_lambda, nat_rpb, moe_w_router, moe_w_gate, moe_w_up, moe_w_down, final_norm_g):
    B, N, D = x.shape
    M = ctx.shape[1]
    E = N_EXPERTS
    tm = TOKEN_TILE
    cap_l = max(1, EC_CAPACITY * N // E)
    cap_c = max(1, EC_CAPACITY * M // E)
    sc = jax.nn.silu(c)
    scc = jax.nn.silu(c_ctx)
    xc = ctx
    batch_ids = jnp.arange(B, dtype=jnp.int32)[:, None, None]
    tables = rope_tables(N)
    for l in range(DEPTH):
        need_ctx = l < DEPTH - 1
        mod = sc @ w_ada[l] + b_ada[l]
        mod_c = jnp.broadcast_to((scc @ w_ada[l] + b_ada[l])[None], (B, 6 * D))
        sh1, sc1, g1, sh2, sc2, g2 = jnp.split(mod, 6, axis=-1)
        csh1, csc1, cg1, csh2, csc2, cg2 = jnp.split(mod_c, 6, axis=-1)

        wp = permute_w_in(w_in[l])
        gla, gm, lru, nat, gate = norm_inproj(x, norm1_g[l], sc1, sh1, wp, tm, tables)
        cgla, cgm, clru, cnat, cgate = norm_inproj(xc, norm1_g[l], csc1, csh1, wp, M)

        gla_o, gla_c = gla_mixer(gla, gate, cgla, cgate, gla_w_gate[l], gla_b_gate[l], gla_norm_g[l], need_ctx, tm)
        gm_o = gmlp_mixer(gm, gmlp_norm_g[l], gmlp_w_s[l], gmlp_b_s[l], tm)
        lru_o, lru_c = lru_mixer(lru, clru, lru_conv_w[l], lru_conv_b[l], lru_w_r[l], lru_b_r[l], lru_w_i[l],
                                 lru_b_i[l], lru_lambda[l], need_ctx, tm)
        nat_o = nat_mixer(nat, cnat, nat_bias_tables(nat_rpb[l], N // GRID_W))

        wo = w_out[l].astype(BF16)
        wr = moe_w_router[l].astype(BF16)
        x, h2, aff = outproj_norm_router((gla_o, gm_o, lru_o, nat_o), wo, x, g1, norm2_g[l], sc2, sh2, wr, tm)
        gl, il = route(aff, cap_l)
        rows_l = jnp.swapaxes(il + batch_ids * N, 0, 1).reshape(E, B * cap_l)
        idx = rows_l
        gs = jnp.swapaxes(gl, 0, 1).reshape(E, B * cap_l)
        hc2 = None
        if need_ctx:
            gm_c = gmlp_mixer(cgm, gmlp_norm_g[l], gmlp_w_s[l], gmlp_b_s[l], M)
            nat_c = ctx_attention(cnat)
            xc, hc2, caff = outproj_norm_router((gla_c, gm_c, lru_c, nat_c), wo, xc, cg1, norm2_g[l], csc2, csh2,
                                                wr, M)
            gc, ic = route(caff, cap_c)
            rows_c = jnp.swapaxes(ic + batch_ids * M, 0, 1).reshape(E, B * cap_c)
            idx = jnp.concatenate([rows_l, rows_c], axis=1)
            gs = jnp.concatenate([gs, jnp.swapaxes(gc, 0, 1).reshape(E, B * cap_c)], axis=1)
        ys = moe_ffn((idx * SUBLANES).reshape(-1).astype(jnp.int32), h2, hc2, moe_w_gate, moe_w_up, moe_w_down, l,
                     gs[..., None], g2, cg2, B * cap_l, tf=512)
        x = combine_expert_outputs(x.reshape(B * N, D), ys[0], rows_l.reshape(-1).astype(jnp.int32)).reshape(B, N, D)
        if need_ctx:
            xc = xc.reshape(B * M, D).at[rows_c.reshape(-1)].add(ys[1].reshape(-1, D)).reshape(B, M, D)
    return final_norm(x, final_norm_g, tm)
```

```python
import functools

import numpy as np
import jax
import jax.numpy as jnp
from jax import lax
from jax.experimental import pallas as pl
from jax.experimental.pallas import tpu as pltpu

D_MODEL = 1024
DEPTH = 4
GRID_W = 64
N_GROUPS = 4
GROUP_W = D_MODEL // N_GROUPS
GLA_HEADS = 4
GLA_DK = GROUP_W // GLA_HEADS
GLA_W = GLA_HEADS * GLA_DK
GLA_GATE_RANK = 16
GLA_TAU = 16.0
GLA_CHUNK = 64
GMLP_GROUPS = 4
GMLP_W = GROUP_W
GMLP_CHUNK = 128
LRU_W = GROUP_W
LRU_BLOCKS = 4
LRU_C = 8.0
CONV_W = 4
NAT_HEADS = 4
NAT_DH = GROUP_W // NAT_HEADS
NAT_W = NAT_HEADS * NAT_DH
NAT_KR_MAX = 8
NAT_KC = 16
N_EXPERTS = 16
EXPERT_FF = 2 * D_MODEL
EC_CAPACITY = 2
ROPE_BASE = 10000.0
EPS = 1e-6

IN_SIZES = (GLA_W, GLA_W, GLA_W, GLA_W, GLA_GATE_RANK, GLA_GATE_RANK,
            GMLP_W, GMLP_W, LRU_W, LRU_W, NAT_W, NAT_W, NAT_W)
IN_COLS = sum(IN_SIZES)

V7X_VMEM_LIMIT_BYTES = 56 * 1024 * 1024
LANES = 128
SUBLANES = 8
F32 = jnp.float32
BF16 = jnp.bfloat16
NEG_BIG = -1e30
GATE_PAD = LANES

NT_DIMS = (((1,), (1,)), ((), ()))
TN_DIMS = (((0,), (0,)), ((), ()))


def _cparams(sem):
    return pltpu.CompilerParams(dimension_semantics=sem, vmem_limit_bytes=V7X_VMEM_LIMIT_BYTES)


def _dot(a, b):
    return jnp.dot(a, b, preferred_element_type=F32)


def _split3(x):
    hi = x.astype(BF16)
    r1 = x - hi.astype(F32)
    mid = r1.astype(BF16)
    lo = (r1 - mid.astype(F32)).astype(BF16)
    return hi, mid, lo


def _head_block_mask(n):
    r = lax.broadcasted_iota(jnp.int32, (n, n), 0) // GLA_DK
    c = lax.broadcasted_iota(jnp.int32, (n, n), 1) // GLA_DK
    return r == c


def _axial_rope(u, cos, sin):
    lane = lax.broadcasted_iota(jnp.int32, u.shape, 1)
    first = (lane % (GLA_DK // 2)) < (GLA_DK // 4)
    partner = jnp.where(first, pltpu.roll(u, GLA_W - GLA_DK // 4, axis=1), pltpu.roll(u, GLA_DK // 4, axis=1))
    return u * cos + partner * sin


def _norm_inproj_kernel(*refs, rope):
    refs = list(refs)
    x_ref, g_ref, sc_ref, sh_ref, w_ref = refs[:5]
    refs = refs[5:]
    if rope:
        cos_ref, sin_ref = refs[:2]
        refs = refs[2:]
    gla_ref, gmlp_ref, lru_ref, nat_ref, gate_ref = refs
    x = x_ref[0]
    y = x * lax.rsqrt(jnp.mean(x * x, axis=-1, keepdims=True) + EPS)
    h = ((y * g_ref[...]) * (1.0 + sc_ref[0]) + sh_ref[0]).astype(BF16)
    o = 0
    for ref in (gla_ref, gmlp_ref, lru_ref, nat_ref, gate_ref):
        w = ref.shape[-1]
        ref[0] = _dot(h, w_ref[:, o:o + w]).astype(ref.dtype)
        o += w
    if rope:
        for c in range(2):
            cols = slice(c * GLA_W, (c + 1) * GLA_W)
            gla_ref[0, :, cols] = _axial_rope(gla_ref[0, :, cols], cos_ref[...], sin_ref[...])


def permute_w_in(w):
    g0 = 4 * GLA_W
    g1 = g0 + 2 * GLA_GATE_RANK
    pad = jnp.zeros((w.shape[0], GATE_PAD - 2 * GLA_GATE_RANK), w.dtype)
    return jnp.concatenate([w[:, :g0], w[:, g1:], w[:, g0:g1], pad], axis=1).astype(BF16)


def norm_inproj(x, g, scale, shift, wp, tm, tables=None):
    B, n, D = x.shape
    C = wp.shape[1]
    widths = (4 * GLA_W, 2 * GMLP_W, 2 * LRU_W, 3 * NAT_W, GATE_PAD)
    dtypes = (F32, F32, F32, BF16, F32)
    row = lambda b, i: (b, i, 0)
    per_b = lambda b, i: (b, 0, 0)
    full = lambda b, i: (0, 0)
    rope_specs = [pl.BlockSpec((tm, GLA_W), lambda b, i: (i, 0))] * 2 if tables is not None else []
    return pl.pallas_call(
        functools.partial(_norm_inproj_kernel, rope=tables is not None),
        grid=(B, n // tm),
        in_specs=[
            pl.BlockSpec((1, tm, D), row),
            pl.BlockSpec((1, D), full),
            pl.BlockSpec((1, 1, D), per_b),
            pl.BlockSpec((1, 1, D), per_b),
            pl.BlockSpec((D, C), full),
        ] + rope_specs,
        out_specs=[pl.BlockSpec((1, tm, w), row) for w in widths],
        out_shape=[jax.ShapeDtypeStruct((B, n, w), dt) for w, dt in zip(widths, dtypes)],
        compiler_params=_cparams(("parallel", "parallel")),
        name="norm_inproj",
    )(x, g.reshape(1, D), scale.reshape(B, 1, D), shift.reshape(B, 1, D), wp, *(tables or ()))


def _gla_kernel(*refs, tm, reverse, combine):
    refs = list(refs)
    q_ref, k_ref, v_ref, gate_ref, wg_ref, bg_ref, tri_ref, ones_ref, s0_ref = refs[:9]
    refs = refs[9:]
    if combine:
        ob_ref, og_ref, ng_ref = refs[:3]
        refs = refs[3:]
    o_ref, sfin_ref, st, obuf = refs
    i = pl.program_id(1)
    nch = tm // GLA_CHUNK

    @pl.when(i == 0)
    def _():
        st[...] = s0_ref[0]

    q = q_ref[0] * (GLA_DK ** -0.5)
    k = k_ref[0]
    a = _dot(gate_ref[0].astype(BF16), wg_ref[...]) + bg_ref[...]
    la = jax.nn.log_sigmoid(a) / GLA_TAU
    pieces = _split3(la)
    b = sum(_dot(tri_ref[...], p) for p in pieces)
    bl = sum(_dot(ones_ref[...], p) for p in pieces)
    q_in = (q * jnp.exp(b)).astype(BF16)
    k_in = (k * jnp.exp(-b)).astype(BF16)
    k_end = (k * jnp.exp(bl - b)).astype(BF16)
    dec = jnp.exp(bl)
    vb = v_ref[0].astype(BF16)

    head_mask = _head_block_mask(GLA_W)
    cr = lax.broadcasted_iota(jnp.int32, (GLA_CHUNK, GLA_W), 0)
    cs = lax.broadcasted_iota(jnp.int32, (GLA_CHUNK, GLA_W), 1) % GLA_CHUNK
    causal = (cs >= cr) if reverse else (cs <= cr)
    zero = jnp.zeros((), BF16)
    chunks = range(nch - 1, -1, -1) if reverse else range(nch)
    for c in chunks:
        sl = slice(c * GLA_CHUNK, (c + 1) * GLA_CHUNK)
        kbd_t = jnp.where(head_mask, jnp.concatenate([k_in[sl]] * GLA_HEADS, axis=0), zero)
        vbd = jnp.where(head_mask, jnp.concatenate([vb[sl]] * GLA_HEADS, axis=0), zero)
        att = lax.dot_general(q_in[sl], kbd_t, NT_DIMS, preferred_element_type=F32)
        att = jnp.where(causal, att, 0.0).astype(BF16)
        s_t = st[...]
        o = _dot(att, vbd) + lax.dot_general(q_in[sl], s_t.astype(BF16), NT_DIMS, preferred_element_type=F32)
        upd = lax.dot_general(vb[sl], k_end[sl], TN_DIMS, preferred_element_type=F32)
        st[...] = s_t * dec[c * GLA_CHUNK:c * GLA_CHUNK + 1] + jnp.where(head_mask, upd, 0.0)
        obuf[sl, :] = o

    sfin_ref[0] = st[...]

    if not combine:
        o_ref[0] = obuf[...]
    else:
        o = obuf[...] + ob_ref[0]
        sq_hi, sq_mid, _ = _split3(o * o)
        avg = jnp.where(head_mask, 1.0 / GLA_DK, 0.0).astype(BF16)
        ms = _dot(sq_hi, avg) + _dot(sq_mid, avg)
        og = og_ref[0]
        o_ref[0] = (o * lax.rsqrt(ms + EPS) * ng_ref[...] * (og * jax.nn.sigmoid(og))).astype(o_ref.dtype)


def _chunk_sum_matrices(tm, reverse):
    r = np.arange(tm)[:, None]
    c = np.arange(tm)[None, :]
    same = (r // GLA_CHUNK) == (c // GLA_CHUNK)
    tri = same & ((c >= r) if reverse else (c <= r))
    return jnp.asarray(tri, BF16), jnp.asarray(same, BF16)


def gla_direction(gla, gate, wg, bg, s0, tm, reverse, combine=None):
    B, L, _ = gla.shape
    nt = L // tm
    W = GLA_W
    blk = (lambda i: nt - 1 - i) if reverse else (lambda i: i)
    col = lambda j: (lambda b, i: (b, blk(i), j))
    full = lambda b, i: (0, 0)
    tri, ones = _chunk_sum_matrices(tm, reverse)
    args = [gla, gla, gla, gate, wg, bg, tri, ones, s0]
    specs = [
        pl.BlockSpec((1, tm, W), col(0)), pl.BlockSpec((1, tm, W), col(1)), pl.BlockSpec((1, tm, W), col(2)),
        pl.BlockSpec((1, tm, GATE_PAD), col(0)),
        pl.BlockSpec((GATE_PAD, W), full), pl.BlockSpec((1, W), full),
        pl.BlockSpec((tm, tm), full), pl.BlockSpec((tm, tm), full),
        pl.BlockSpec((1, W, W), lambda b, i: (b, 0, 0)),
    ]
    if combine is not None:
        ob, norm_g = combine
        args += [ob, gla, norm_g]
        specs += [pl.BlockSpec((1, tm, W), col(0)), pl.BlockSpec((1, tm, W), col(3)), pl.BlockSpec((1, W), full)]
    return pl.pallas_call(
        functools.partial(_gla_kernel, tm=tm, reverse=reverse, combine=combine is not None),
        grid=(B, nt),
        in_specs=specs,
        out_specs=[pl.BlockSpec((1, tm, W), col(0)), pl.BlockSpec((1, W, W), lambda b, i: (b, 0, 0))],
        out_shape=[jax.ShapeDtypeStruct((B, L, W), F32 if combine is None else BF16),
                   jax.ShapeDtypeStruct((B, W, W), F32)],
        scratch_shapes=[pltpu.VMEM((W, W), F32), pltpu.VMEM((tm, W), F32)],
        compiler_params=_cparams(("parallel", "arbitrary")),
        name="gla_bwd" if reverse else "gla_fwd",
    )(*args)


def rope_tables(n):
    quarter = GLA_DK // 4
    pos = jnp.arange(n)
    inv = ROPE_BASE ** (-jnp.arange(quarter, dtype=F32) / quarter)
    row = (pos // GRID_W).astype(F32)
    colp = (pos % GRID_W).astype(F32)
    ar = row[:, None] * inv[None, :]
    ac = colp[:, None] * inv[None, :]
    cos = jnp.concatenate([jnp.cos(ar), jnp.cos(ar), jnp.cos(ac), jnp.cos(ac)], axis=1)
    sin = jnp.concatenate([-jnp.sin(ar), jnp.sin(ar), -jnp.sin(ac), jnp.sin(ac)], axis=1)
    return jnp.tile(cos, (1, GLA_HEADS)), jnp.tile(sin, (1, GLA_HEADS))


def gla_mixer(gla, gate, cgla, cgate, w_gate, b_gate, norm_g, need_ctx, tm):
    B, M, _ = cgla.shape
    zeros = jnp.zeros((B, GLA_W, GLA_W), F32)
    ng = jnp.tile(norm_g, GLA_HEADS).reshape(1, GLA_W)
    wgs, bgs = [], []
    for d in range(2):
        wg = jnp.pad(w_gate[d], ((d * GLA_GATE_RANK, GATE_PAD - (d + 1) * GLA_GATE_RANK), (0, 0)))
        wgs.append(wg.astype(BF16))
        bgs.append(b_gate[d].reshape(1, GLA_W))
    ocb, sb = gla_direction(cgla, cgate, wgs[1], bgs[1], zeros, M, True)
    ob, _ = gla_direction(gla, gate, wgs[1], bgs[1], sb, tm, True)
    if need_ctx:
        oc, sf = gla_direction(cgla, cgate, wgs[0], bgs[0], zeros, M, False, combine=(ocb, ng))
    else:
        oc, sf = gla_direction(cgla, cgate, wgs[0], bgs[0], zeros, M, False)
    o, _ = gla_direction(gla, gate, wgs[0], bgs[0], sf, tm, False, combine=(ob, ng))
    return o, oc


def _gmlp_kernel(u_ref, v_ref, g_ref, w_ref, b_ref, o_ref, *, tm):
    v = v_ref[0]
    vn = (v * lax.rsqrt(jnp.mean(v * v, axis=-1, keepdims=True) + EPS) * g_ref[...]).astype(BF16)
    grp = lax.broadcasted_iota(jnp.int32, (GMLP_CHUNK, GMLP_W), 1) // (GMLP_W // GMLP_GROUPS)
    zero = jnp.zeros((), BF16)
    for c in range(tm // GMLP_CHUNK):
        sl = slice(c * GMLP_CHUNK, (c + 1) * GMLP_CHUNK)
        mixed = b_ref[...]
        for g in range(GMLP_GROUPS):
            mixed = mixed + _dot(w_ref[g], jnp.where(grp == g, vn[sl], zero))
        o_ref[0, sl, :] = (u_ref[0, sl, :] * mixed).astype(o_ref.dtype)


def gmlp_mixer(gm, norm_g, w_s, b_s, tm):
    B, L, _ = gm.shape
    W = GMLP_W
    bias = jnp.repeat(b_s.T, W // GMLP_GROUPS, axis=1)
    return pl.pallas_call(
        functools.partial(_gmlp_kernel, tm=tm),
        grid=(B, L // tm),
        in_specs=[
            pl.BlockSpec((1, tm, W), lambda b, i: (b, i, 0)),
            pl.BlockSpec((1, tm, W), lambda b, i: (b, i, 1)),
            pl.BlockSpec((1, W), lambda b, i: (0, 0)),
            pl.BlockSpec((GMLP_GROUPS, GMLP_CHUNK, GMLP_CHUNK), lambda b, i: (0, 0, 0)),
            pl.BlockSpec((GMLP_CHUNK, W), lambda b, i: (0, 0)),
        ],
        out_specs=pl.BlockSpec((1, tm, W), lambda b, i: (b, i, 0)),
        out_shape=jax.ShapeDtypeStruct((B, L, W), BF16),
        compiler_params=_cparams(("parallel", "parallel")),
        name="gmlp",
    )(gm, gm, norm_g.reshape(1, W), w_s.astype(BF16), bias)


LRU_UNROLL = 8


def _lru_kernel(*refs, tm, reverse, combine):
    refs = list(refs)
    x_ref, xp_ref, xn_ref, cw_ref, cb_ref, wr_ref, br_ref, wi_ref, bi_ref, ncs_ref, h0_ref = refs[:11]
    refs = refs[11:]
    if combine:
        hb_ref, ly_ref = refs[:2]
        refs = refs[2:]
    o_ref, ext, a_s, b_s, carry = refs
    W = LRU_W
    i = pl.program_id(1)
    nt = pl.num_programs(1)
    t = (nt - 1 - i) if reverse else i
    H = SUBLANES

    ext[0:H] = jnp.where(t > 0, xp_ref[0], 0.0)
    ext[H:H + tm] = x_ref[0]
    ext[H + tm:2 * H + tm] = jnp.where(t < nt - 1, xn_ref[0], 0.0)
    xc = cb_ref[...]
    for tap in range(CONV_W):
        xc = xc + cw_ref[tap:tap + 1, :] * ext[H - CONV_W // 2 + tap:H - CONV_W // 2 + tap + tm]
    xcb = xc.astype(BF16)
    r = jax.nn.sigmoid(_dot(xcb, wr_ref[...]) + br_ref[...])
    ig = jax.nn.sigmoid(_dot(xcb, wi_ref[...]) + bi_ref[...])
    log_a = ncs_ref[...] * r
    a = jnp.exp(log_a)
    b = jnp.sqrt(-jnp.tanh(log_a) * (a * a + 1.0)) * (ig * xc)

    rowi = lax.broadcasted_iota(jnp.int32, (tm, W), 0) % H
    for s in (1, 2, 4):
        if reverse:
            ok = rowi < H - s
            a_sh = pltpu.roll(a, tm - s, axis=0)
            b_sh = pltpu.roll(b, tm - s, axis=0)
        else:
            ok = rowi >= s
            a_sh = pltpu.roll(a, s, axis=0)
            b_sh = pltpu.roll(b, s, axis=0)
        b = a * jnp.where(ok, b_sh, 0.0) + b
        a = a * jnp.where(ok, a_sh, 1.0)
    a_s[...] = a
    b_s[...] = b

    @pl.when(i == 0)
    def _():
        carry[...] = jnp.broadcast_to(h0_ref[0], (H, W))

    ng = tm // H

    def body(j, h):
        for u in range(LRU_UNROLL):
            g = j * LRU_UNROLL + u
            g = (ng - 1 - g) if reverse else g
            rows = pl.ds(pl.multiple_of(g * H, H), H)
            hg = b_s[rows, :] + a_s[rows, :] * h
            b_s[rows, :] = hg
            h = jnp.broadcast_to(hg[0:1] if reverse else hg[H - 1:H], (H, W))
        return h

    carry[...] = lax.fori_loop(0, ng // LRU_UNROLL, body, carry[...])

    if combine:
        o_ref[0] = ((b_s[...] + hb_ref[0]) * jax.nn.gelu(ly_ref[0])).astype(o_ref.dtype)
    else:
        o_ref[0] = b_s[...]


def lru_direction(lru, conv_w, conv_b, wr, br, wi, bi, ncs, h0, tm, reverse, hb=None):
    B, L, _ = lru.shape
    nt = L // tm
    W = LRU_W
    H = SUBLANES
    nh = L // H
    per = tm // H
    blk = (lambda i: nt - 1 - i) if reverse else (lambda i: i)
    full = lambda b, i: (0, 0)
    args = [lru, lru, lru, conv_w, conv_b, wr, br, wi, bi, ncs, h0]
    specs = [
        pl.BlockSpec((1, tm, W), lambda b, i: (b, blk(i), 0)),
        pl.BlockSpec((1, H, W), lambda b, i: (b, jnp.maximum(blk(i) * per - 1, 0), 0)),
        pl.BlockSpec((1, H, W), lambda b, i: (b, jnp.minimum((blk(i) + 1) * per, nh - 1), 0)),
        pl.BlockSpec((CONV_W, W), full), pl.BlockSpec((1, W), full),
        pl.BlockSpec((W, W), full), pl.BlockSpec((1, W), full),
        pl.BlockSpec((W, W), full), pl.BlockSpec((1, W), full),
        pl.BlockSpec((1, W), full),
        pl.BlockSpec((1, 1, W), lambda b, i: (b, 0, 0)),
    ]
    if hb is not None:
        args += [hb, lru]
        specs += [pl.BlockSpec((1, tm, W), lambda b, i: (b, blk(i), 0)),
                  pl.BlockSpec((1, tm, W), lambda b, i: (b, blk(i), 1))]
    return pl.pallas_call(
        functools.partial(_lru_kernel, tm=tm, reverse=reverse, combine=hb is not None),
        grid=(B, nt),
        in_specs=specs,
        out_specs=pl.BlockSpec((1, tm, W), lambda b, i: (b, blk(i), 0)),
        out_shape=jax.ShapeDtypeStruct((B, L, W), F32 if hb is None else BF16),
        scratch_shapes=[pltpu.VMEM((tm + 2 * H, W), F32), pltpu.VMEM((tm, W), F32), pltpu.VMEM((tm, W), F32),
                        pltpu.VMEM((H, W), F32)],
        compiler_params=_cparams(("parallel", "arbitrary")),
        name="lru_bwd" if reverse else "lru_fwd",
    )(*args)


def _block_diag(w):
    G, n, _ = w.shape
    same = jnp.eye(G, dtype=w.dtype)
    return (w[:, :, None, :] * same[:, None, :, None]).reshape(G * n, G * n)


def lru_mixer(lru, clru, conv_w, conv_b, w_r, b_r, w_i, b_i, lam, need_ctx, tm):
    B, M, _ = clru.shape
    W = LRU_W
    cb = conv_b.reshape(1, W)
    ncs = -LRU_C * jax.nn.softplus(-lam.astype(F32))
    p = [(_block_diag(w_r[d]).astype(BF16), b_r[d].reshape(1, W), _block_diag(w_i[d]).astype(BF16),
          b_i[d].reshape(1, W), ncs[d].reshape(1, W)) for d in range(2)]
    zeros = jnp.zeros((B, 1, W), F32)
    hcb = lru_direction(clru, conv_w, cb, *p[1], zeros, M, True)
    hb = lru_direction(lru, conv_w, cb, *p[1], hcb[:, 0:1], tm, True)
    hcf = lru_direction(clru, conv_w, cb, *p[0], zeros, M, False)
    out = lru_direction(lru, conv_w, cb, *p[0], hcf[:, M - 1:M], tm, False, hb=hb)
    out_c = None
    if need_ctx:
        out_c = lru_direction(clru, conv_w, cb, *p[0], zeros, M, False, hb=hcb)
    return out, out_c


NAT_QROWS = 4
NAT_KROWS = 3 * NAT_QROWS
NAT_TQ = NAT_QROWS * GRID_W
NAT_TK = NAT_KROWS * GRID_W


def _softmax_pv(s_loc, s_ctx, v_loc, v_ctx):
    m = jnp.maximum(jnp.max(s_loc, axis=-1, keepdims=True), jnp.max(s_ctx, axis=-1, keepdims=True))
    e_loc = jnp.exp(s_loc - m)
    e_ctx = jnp.exp(s_ctx - m)
    den = jnp.sum(e_loc, axis=-1, keepdims=True) + jnp.sum(e_ctx, axis=-1, keepdims=True)
    o = _dot(e_loc.astype(BF16), v_loc) + _dot(e_ctx.astype(BF16), v_ctx)
    return o / den


def _nat_kernel(q_ref, kp_ref, kc_ref, kn_ref, vp_ref, vc_ref, vn_ref, ck_ref, cv_ref, bias_ref, o_ref):
    q = q_ref[0]
    ck = ck_ref[0]
    cv = cv_ref[0]
    scale = NAT_DH ** -0.5
    outs = []
    for h in range(NAT_HEADS):
        hs = slice(h * NAT_DH, (h + 1) * NAT_DH)
        qh = q[:, hs]
        s_loc = jnp.concatenate(
            [lax.dot_general(qh, k_ref[0][:, hs], NT_DIMS, preferred_element_type=F32)
             for k_ref in (kp_ref, kc_ref, kn_ref)], axis=-1) * scale + bias_ref[0, h]
        s_ctx = lax.dot_general(qh, ck[:, hs], NT_DIMS, preferred_element_type=F32) * scale
        v_loc = jnp.concatenate([v_ref[0][:, hs] for v_ref in (vp_ref, vc_ref, vn_ref)], axis=0)
        outs.append(_softmax_pv(s_loc, s_ctx, v_loc, cv[:, hs]))
    o_ref[0] = jnp.concatenate(outs, axis=-1).astype(o_ref.dtype)


def nat_bias_tables(rpb, rows):
    kr = NAT_KR_MAX
    qc = np.arange(GRID_W)
    kcol = np.arange(GRID_W)
    cs = np.clip(qc - NAT_KC // 2, 0, GRID_W - NAT_KC)
    valid_c = (kcol[None, :] >= cs[:, None]) & (kcol[None, :] < cs[:, None] + NAT_KC)
    edge = GRID_W - NAT_KC
    padded = jnp.pad(rpb.astype(F32), ((0, 0), (0, 0), (edge, edge)), mode="edge")
    by_col = jnp.stack([padded[:, :, GRID_W - 1 - q:2 * GRID_W - 1 - q] for q in range(GRID_W)], axis=1)
    by_col = jnp.pad(by_col, ((0, 0), (0, 0), (NAT_KROWS, NAT_KROWS), (0, 0)))
    tables = []
    for r0, ks in ((0, 0), (NAT_QROWS, NAT_QROWS - kr // 2), (rows - NAT_QROWS, rows - NAT_KROWS)):
        r = r0 + np.arange(NAT_QROWS)
        krow = ks + np.arange(NAT_KROWS)
        rs = np.clip(r - kr // 2, 0, rows - kr)
        valid_r = (krow[None, :] >= rs[:, None]) & (krow[None, :] < rs[:, None] + kr)
        starts = ks - r + NAT_KR_MAX - 1 + NAT_KROWS
        bias = jnp.stack([by_col[:, :, int(s):int(s) + NAT_KROWS] for s in starts], axis=1)
        mask = valid_r[:, None, :, None] & valid_c[None, :, None, :]
        bias = jnp.where(jnp.asarray(mask)[None], bias, NEG_BIG)
        tables.append(bias.reshape(rpb.shape[0], NAT_TQ, NAT_TK))
    return jnp.stack(tables)


def nat_mixer(nat, cnat, bias, layer):
    B, N, _ = nat.shape
    M = cnat.shape[1]
    W = NAT_W
    nb = N // NAT_TQ
    T = NAT_TQ
    centre = lambda j: jnp.clip(j, 1, nb - 2)
    near = lambda c, d: (lambda b, j: (b, centre(j) + d, c))
    variant = lambda b, j: (jnp.where(j == 0, 0, jnp.where(j == nb - 1, 2, 1)), layer, 0, 0)
    return pl.pallas_call(
        _nat_kernel,
        grid=(B, nb),
        in_specs=[
            pl.BlockSpec((1, T, W), lambda b, j: (b, j, 0)),
            pl.BlockSpec((1, T, W), near(1, -1)), pl.BlockSpec((1, T, W), near(1, 0)), pl.BlockSpec((1, T, W), near(1, 1)),
            pl.BlockSpec((1, T, W), near(2, -1)), pl.BlockSpec((1, T, W), near(2, 0)), pl.BlockSpec((1, T, W), near(2, 1)),
            pl.BlockSpec((1, M, W), lambda b, j: (b, 0, 1)),
            pl.BlockSpec((1, M, W), lambda b, j: (b, 0, 2)),
            pl.BlockSpec((1, NAT_HEADS, NAT_TQ, NAT_TK), variant),
        ],
        out_specs=pl.BlockSpec((1, T, W), lambda b, j: (b, j, 0)),
        out_shape=jax.ShapeDtypeStruct((B, N, W), BF16),
        compiler_params=_cparams(("parallel", "arbitrary")),
        name="nat",
    )(nat, nat, nat, nat, nat, nat, nat, cnat, cnat, bias)


def _ctx_attn_kernel(q_ref, k_ref, v_ref, o_ref):
    q = q_ref[0]
    k = k_ref[0]
    v = v_ref[0]
    scale = NAT_DH ** -0.5
    outs = []
    for h in range(NAT_HEADS):
        hs = slice(h * NAT_DH, (h + 1) * NAT_DH)
        s = lax.dot_general(q[:, hs], k[:, hs], NT_DIMS, preferred_element_type=F32) * scale
        e = jnp.exp(s - jnp.max(s, axis=-1, keepdims=True))
        outs.append(_dot(e.astype(BF16), v[:, hs]) / jnp.sum(e, axis=-1, keepdims=True))
    o_ref[0] = jnp.concatenate(outs, axis=-1).astype(o_ref.dtype)


def ctx_attention(cnat):
    B, M, _ = cnat.shape
    W = NAT_W
    return pl.pallas_call(
        _ctx_attn_kernel,
        grid=(B,),
        in_specs=[pl.BlockSpec((1, M, W), lambda b, c=c: (b, 0, c)) for c in range(3)],
        out_specs=pl.BlockSpec((1, M, W), lambda b: (b, 0, 0)),
        out_shape=jax.ShapeDtypeStruct((B, M, W), BF16),
        compiler_params=_cparams(("parallel",)),
        name="ctx_attn",
    )(cnat, cnat, cnat)


def _outproj_kernel(m0_ref, m1_ref, m2_ref, m3_ref, w_ref, x_ref, g1_ref, n2_ref, sc_ref, sh_ref, wr_ref,
                    xo_ref, h2_ref, aff_ref):
    y = 0.0
    for gi, m_ref in enumerate((m0_ref, m1_ref, m2_ref, m3_ref)):
        y = y + _dot(m_ref[0].astype(BF16), w_ref[gi * GROUP_W:(gi + 1) * GROUP_W, :])
    x = x_ref[0] + g1_ref[0] * y
    xo_ref[0] = x
    xn = x * lax.rsqrt(jnp.mean(x * x, axis=-1, keepdims=True) + EPS)
    h2 = (xn * n2_ref[...]) * (1.0 + sc_ref[0]) + sh_ref[0]
    tm = h2.shape[0]
    for j in range(SUBLANES):
        h2_ref[pl.ds(j, tm, stride=SUBLANES), :] = h2[:, j * LANES:(j + 1) * LANES]
    logits = _dot(h2.astype(BF16), wr_ref[...])
    e = jnp.exp(logits - jnp.max(logits, axis=-1, keepdims=True))
    aff_ref[0] = e / jnp.sum(e, axis=-1, keepdims=True)


def outproj_norm_router(parts, w_out, x, g1, n2, scale2, shift2, w_router, tm):
    B, n, D = x.shape
    E = w_router.shape[1]
    row = lambda b, i: (b, i, 0)
    per_b = lambda b, i: (b, 0, 0)
    full = lambda b, i: (0, 0)
    return pl.pallas_call(
        _outproj_kernel,
        grid=(B, n // tm),
        in_specs=[pl.BlockSpec((1, tm, GROUP_W), row)] * N_GROUPS + [
            pl.BlockSpec((D, D), full),
            pl.BlockSpec((1, tm, D), row),
            pl.BlockSpec((1, 1, D), per_b),
            pl.BlockSpec((1, D), full),
            pl.BlockSpec((1, 1, D), per_b),
            pl.BlockSpec((1, 1, D), per_b),
            pl.BlockSpec((D, E), full),
        ],
        out_specs=[
            pl.BlockSpec((1, tm, D), row),
            pl.BlockSpec((tm * SUBLANES, LANES), lambda b, i: (b * (n // tm) + i, 0)),
            pl.BlockSpec((1, tm, E), row),
        ],
        out_shape=[
            jax.ShapeDtypeStruct((B, n, D), F32),
            jax.ShapeDtypeStruct((B * n * SUBLANES, LANES), F32),
            jax.ShapeDtypeStruct((B, n, E), F32),
        ],
        compiler_params=_cparams(("parallel", "parallel")),
        name="outproj_norm_router",
    )(*parts, w_out, x, g1.reshape(B, 1, D), n2.reshape(1, D), scale2.reshape(B, 1, D), shift2.reshape(B, 1, D),
      w_router)


MOE_ROW_CHUNK = 512
MOE_ISSUE_UNROLL = 8


def _moe_ffn_kernel(*refs, n_lat, n_ctx, n_batch):
    refs = list(refs)
    idx_ref, h_hbm = refs[:2]
    refs = refs[2:]
    if n_ctx:
        hc_hbm = refs.pop(0)
    wg_ref, wu_ref, wd_ref, gate_ref, g2_ref = refs[:5]
    refs = refs[5:]
    if n_ctx:
        cg2_ref = refs.pop(0)
    yl_hbm = refs.pop(0)
    if n_ctx:
        yc_ref = refs.pop(0)
    xf, xb, acc, ytiles, sem, out_sem = refs
    e = pl.program_id(0)
    f = pl.program_id(1)
    n_exp = pl.num_programs(0)
    last = pl.num_programs(1) - 1
    T = n_lat + n_ctx

    R = SUBLANES

    def start_rows(src, expert, first, count):
        def body(i, carry):
            for u in range(MOE_ISSUE_UNROLL):
                s = first + i * MOE_ISSUE_UNROLL + u
                r = pl.multiple_of(idx_ref[expert * T + s], R)
                pltpu.make_async_copy(src.at[pl.ds(r, R)], xf.at[pl.ds(pl.multiple_of(s * R, R), R)],
                                      sem.at[0]).start()
            return carry

        lax.fori_loop(0, count // MOE_ISSUE_UNROLL, body, 0)

    def start_gather(expert):
        start_rows(h_hbm, expert, 0, n_lat)
        if n_ctx:
            start_rows(hc_hbm, expert, n_lat, n_ctx)

    @pl.when(f == 0)
    def _():
        @pl.when(e == 0)
        def _():
            start_gather(e)

        pltpu.make_async_copy(h_hbm.at[pl.ds(0, T * R)], xf, sem.at[0]).wait()
        for j in range(R):
            xb[:, j * LANES:(j + 1) * LANES] = xf[pl.ds(j, T, stride=R), :].astype(BF16)

        @pl.when(e + 1 < n_exp)
        def _():
            start_gather(e + 1)

        acc[...] = jnp.zeros_like(acc)
        if n_ctx:
            yc_ref[...] = jnp.zeros_like(yc_ref)

    wg = wg_ref[0, 0].astype(BF16)
    wu = wu_ref[0, 0].astype(BF16)
    wd = wd_ref[0, 0].astype(BF16)

    def ffn(x):
        hg = _dot(x, wg)
        hu = _dot(x, wu)
        return _dot((hg * jax.nn.sigmoid(hg) * hu).astype(BF16), wd)

    def lat_rows(first, count, j):
        return pl.ds(first * R + j, count, stride=R)

    for i in range(n_lat // MOE_ROW_CHUNK):
        rows = slice(i * MOE_ROW_CHUNK, (i + 1) * MOE_ROW_CHUNK)
        acc[rows, :] += ffn(xb[rows, :])
    if n_ctx:
        yc_ref[0] += ffn(xb[n_lat:T, :])

    def out_copy(expert):
        dst = yl_hbm.at[pl.ds(pl.multiple_of(expert * (n_lat * R), n_lat * R), n_lat * R)]
        return pltpu.make_async_copy(ytiles, dst, out_sem.at[0])

    @pl.when(f == last)
    def _():
        @pl.when(e > 0)
        def _():
            out_copy(e - 1).wait()

        per_l = n_lat // n_batch
        for b in range(n_batch):
            rows = slice(b * per_l, (b + 1) * per_l)
            gate = gate_ref[0, rows, :]
            for j in range(R):
                lanes = slice(j * LANES, (j + 1) * LANES)
                ytiles[lat_rows(b * per_l, per_l, j), :] = acc[rows, lanes] * gate * g2_ref[b:b + 1, lanes]
        out_copy(e).start()

        @pl.when(e == n_exp - 1)
        def _():
            out_copy(e).wait()

        if n_ctx:
            per_c = n_ctx // n_batch
            for b in range(n_batch):
                rows = slice(b * per_c, (b + 1) * per_c)
                yc_ref[0, rows, :] = (yc_ref[0, rows, :] * gate_ref[0, n_lat + b * per_c:n_lat + (b + 1) * per_c, :]
                                      * cg2_ref[b:b + 1, :])


def moe_ffn(idx, h2, hc2, w_gate, w_up, w_down, layer, gates, g2, cg2, n_lat, tf):
    E, T, _ = gates.shape
    n_ctx = T - n_lat
    D = SUBLANES * LANES
    F = w_gate.shape[3]
    B = g2.shape[0]
    any_spec = pl.BlockSpec(memory_space=pl.ANY)
    per_e = lambda e, f, idx_ref: (e, 0, 0)
    full = lambda e, f, idx_ref: (0, 0)
    args = [h2] + ([hc2] if n_ctx else []) + [w_gate, w_up, w_down, gates, g2] + ([cg2] if n_ctx else [])
    specs = [any_spec] + ([any_spec] if n_ctx else []) + [
        pl.BlockSpec((1, 1, D, tf), lambda e, f, idx_ref: (layer, e, 0, f)),
        pl.BlockSpec((1, 1, D, tf), lambda e, f, idx_ref: (layer, e, 0, f)),
        pl.BlockSpec((1, 1, tf, D), lambda e, f, idx_ref: (layer, e, f, 0)),
        pl.BlockSpec((1, T, 1), per_e),
        pl.BlockSpec((B, D), full),
    ] + ([pl.BlockSpec((B, D), full)] if n_ctx else [])
    out_specs = [any_spec] + ([pl.BlockSpec((1, n_ctx, D), per_e)] if n_ctx else [])
    out_shape = [jax.ShapeDtypeStruct((E * n_lat * SUBLANES, LANES), F32)] + (
        [jax.ShapeDtypeStruct((E, n_ctx, D), F32)] if n_ctx else [])
    return pl.pallas_call(
        functools.partial(_moe_ffn_kernel, n_lat=n_lat, n_ctx=n_ctx, n_batch=B),
        grid_spec=pltpu.PrefetchScalarGridSpec(
            num_scalar_prefetch=1,
            grid=(E, F // tf),
            in_specs=specs,
            out_specs=out_specs,
            scratch_shapes=[pltpu.VMEM((T * SUBLANES, LANES), F32), pltpu.VMEM((T, D), BF16),
                            pltpu.VMEM((n_lat, D), F32), pltpu.VMEM((n_lat * SUBLANES, LANES), F32),
                            pltpu.SemaphoreType.DMA((1,)), pltpu.SemaphoreType.DMA((1,))],
        ),
        out_shape=out_shape,
        compiler_params=_cparams(("arbitrary", "arbitrary")),
        name="moe_ffn",
    )(idx, *args)


COMBINE_TILE = 256
COMBINE_ISSUE_UNROLL = 8


def _combine_kernel(src_ref, dst_ref, bnd_ref, kmax_ref, y_hbm, x_ref, o_ref, planes, sem, *, tm):
    i = pl.program_id(0)
    nt = pl.num_programs(0)
    R = SUBLANES
    U = COMBINE_ISSUE_UNROLL
    slab = tm * R

    def groups(t):
        return (bnd_ref[t + 1] - bnd_ref[t] + U - 1) // U

    def prepare(t, buf):
        def zero(k, carry):
            planes[buf, pl.ds(pl.multiple_of(k * slab, slab), slab), :] = jnp.zeros((slab, LANES), F32)
            return carry

        lax.fori_loop(0, kmax_ref[t], zero, 0)
        first = bnd_ref[t]
        final = bnd_ref[t + 1] - 1

        def issue(g, carry):
            for u in range(U):
                a = first + g * U + u
                live = a <= final
                a = jnp.minimum(a, final)
                s = pl.multiple_of(src_ref[a], R)
                d = pl.multiple_of(jnp.where(live, dst_ref[a], N_EXPERTS * slab + u * R), R)
                pltpu.make_async_copy(y_hbm.at[pl.ds(s, R)], planes.at[buf, pl.ds(d, R)], sem.at[buf]).start()
            return carry

        lax.fori_loop(0, groups(t), issue, 0)

    @pl.when(i == 0)
    def _():
        prepare(0, 0)

    @pl.when(i + 1 < nt)
    def _():
        prepare(i + 1, (i + 1) % 2)

    buf = i % 2
    n_groups = groups(i)
    p = 1
    while p * U <= tm * N_EXPERTS:
        @pl.when((n_groups & p) != 0)
        def _(p=p):
            pltpu.make_async_copy(y_hbm.at[pl.ds(0, p * U * R)], planes.at[buf, pl.ds(0, p * U * R)],
                                  sem.at[buf]).wait()
        p *= 2

    kmax = kmax_ref[i]

    def add(k, carry):
        planes[buf, 0:slab, :] += planes[buf, pl.ds(pl.multiple_of(k * slab, slab), slab), :]
        return carry

    lax.fori_loop(1, kmax, add, 0)

    @pl.when(kmax > 0)
    def _():
        for j in range(R):
            o_ref[:, j * LANES:(j + 1) * LANES] = (x_ref[:, j * LANES:(j + 1) * LANES]
                                                   + planes[buf, pl.ds(j, tm, stride=R), :])

    @pl.when(kmax == 0)
    def _():
        o_ref[...] = x_ref[...]


def combine_expert_outputs(x, y_tiles, token_rows):
    n, D = x.shape
    A = token_rows.shape[0]
    tm = COMBINE_TILE
    nt = n // tm
    tok, src = lax.sort_key_val(token_rows, jnp.arange(A, dtype=jnp.int32))
    pos = jnp.arange(A, dtype=jnp.int32)
    is_start = jnp.concatenate([jnp.ones((1,), bool), tok[1:] != tok[:-1]])
    rank = pos - lax.cummax(jnp.where(is_start, pos, 0))
    tile = tok // tm
    dst = (rank * tm + tok % tm) * SUBLANES
    edges = jnp.arange(nt + 1, dtype=jnp.int32) * tm
    bounds = jnp.sum(tok[None, :] < edges[:, None], axis=1).astype(jnp.int32)
    kmax = jnp.max(jnp.where(tile[None, :] == jnp.arange(nt, dtype=jnp.int32)[:, None], rank[None, :] + 1, 0),
                   axis=1).astype(jnp.int32)
    row = lambda i, *_: (i, 0)
    return pl.pallas_call(
        functools.partial(_combine_kernel, tm=tm),
        grid_spec=pltpu.PrefetchScalarGridSpec(
            num_scalar_prefetch=4,
            grid=(nt,),
            in_specs=[pl.BlockSpec(memory_space=pl.ANY), pl.BlockSpec((tm, D), row)],
            out_specs=pl.BlockSpec((tm, D), row),
            scratch_shapes=[
                pltpu.VMEM((2, (N_EXPERTS * tm + COMBINE_ISSUE_UNROLL) * SUBLANES, LANES), F32),
                pltpu.SemaphoreType.DMA((2,))],
        ),
        out_shape=jax.ShapeDtypeStruct((n, D), F32),
        compiler_params=_cparams(("arbitrary",)),
        name="combine",
    )(src * SUBLANES, dst, bounds, kmax, y_tiles, x)


def _final_norm_kernel(x_ref, g_ref, o_ref):
    x = x_ref[0]
    o_ref[0] = x * lax.rsqrt(jnp.mean(x * x, axis=-1, keepdims=True) + EPS) * g_ref[...]


def final_norm(x, g, tm):
    B, n, D = x.shape
    return pl.pallas_call(
        _final_norm_kernel,
        grid=(B, n // tm),
        in_specs=[pl.BlockSpec((1, tm, D), lambda b, i: (b, i, 0)), pl.BlockSpec((1, D), lambda b, i: (0, 0))],
        out_specs=pl.BlockSpec((1, tm, D), lambda b, i: (b, i, 0)),
        out_shape=jax.ShapeDtypeStruct((B, n, D), F32),
        compiler_params=_cparams(("parallel", "parallel")),
        name="final_norm",
    )(x, g.reshape(1, D))


def route(aff, cap):
    return lax.top_k(jnp.swapaxes(aff, 1, 2), cap)


TOKEN_TILE = 512


def kernel(x, c, ctx, c_ctx, w_ada, b_ada, norm1_g, norm2_g, w_in, w_out, gla_w_gate, gla_b_gate, gla_norm_g,
           gmlp_norm_g, gmlp_w_s, gmlp_b_s, lru_conv_w, lru_conv_b, lru_w_r, lru_b_r, lru_w_i, lru_b_i,
           lru_lambda, nat_rpb, moe_w_router, moe_w_gate, moe_w_up, moe_w_down, final_norm_g):
    B, N, D = x.shape
    M = ctx.shape[1]
    E = N_EXPERTS
    tm = TOKEN_TILE
    cap_l = max(1, EC_CAPACITY * N // E)
    cap_c = max(1, EC_CAPACITY * M // E)
    sc = jax.nn.silu(c)
    scc = jax.nn.silu(c_ctx)
    xc = ctx
    batch_ids = jnp.arange(B, dtype=jnp.int32)[:, None, None]
    tables = rope_tables(N)
    mods = jnp.einsum('bd,ldk->lbk', jnp.concatenate([sc, scc[None]], axis=0), w_ada) + b_ada[:, None, :]
    nat_bias = nat_bias_tables(nat_rpb.reshape(DEPTH * NAT_HEADS, *nat_rpb.shape[2:]), N // GRID_W)
    for l in range(DEPTH):
        need_ctx = l < DEPTH - 1
        mod = mods[l, :B]
        mod_c = jnp.broadcast_to(mods[l, B:], (B, 6 * D))
        sh1, sc1, g1, sh2, sc2, g2 = jnp.split(mod, 6, axis=-1)
        csh1, csc1, cg1, csh2, csc2, cg2 = jnp.split(mod_c, 6, axis=-1)

        wp = permute_w_in(w_in[l])
        gla, gm, lru, nat, gate = norm_inproj(x, norm1_g[l], sc1, sh1, wp, tm, tables)
        cgla, cgm, clru, cnat, cgate = norm_inproj(xc, norm1_g[l], csc1, csh1, wp, M)

        gla_o, gla_c = gla_mixer(gla, gate, cgla, cgate, gla_w_gate[l], gla_b_gate[l], gla_norm_g[l], need_ctx, tm)
        gm_o = gmlp_mixer(gm, gmlp_norm_g[l], gmlp_w_s[l], gmlp_b_s[l], tm)
        lru_o, lru_c = lru_mixer(lru, clru, lru_conv_w[l], lru_conv_b[l], lru_w_r[l], lru_b_r[l], lru_w_i[l],
                                 lru_b_i[l], lru_lambda[l], need_ctx, tm)
        nat_o = nat_mixer(nat, cnat, nat_bias, l)

        wo = w_out[l].astype(BF16)
        wr = moe_w_router[l].astype(BF16)
        x, h2, aff = outproj_norm_router((gla_o, gm_o, lru_o, nat_o), wo, x, g1, norm2_g[l], sc2, sh2, wr, tm)
        gl, il = route(aff, cap_l)
        rows_l = jnp.swapaxes(il + batch_ids * N, 0, 1).reshape(E, B * cap_l)
        idx = rows_l
        gs = jnp.swapaxes(gl, 0, 1).reshape(E, B * cap_l)
        hc2 = None
        if need_ctx:
            gm_c = gmlp_mixer(cgm, gmlp_norm_g[l], gmlp_w_s[l], gmlp_b_s[l], M)
            nat_c = ctx_attention(cnat)
            xc, hc2, caff = outproj_norm_router((gla_c, gm_c, lru_c, nat_c), wo, xc, cg1, norm2_g[l], csc2, csh2,
                                                wr, M)
            gc, ic = route(caff, cap_c)
            rows_c = jnp.swapaxes(ic + batch_ids * M, 0, 1).reshape(E, B * cap_c)
            idx = jnp.concatenate([rows_l, rows_c], axis=1)
            gs = jnp.concatenate([gs, jnp.swapaxes(gc, 0, 1).reshape(E, B * cap_c)], axis=1)
        ys = moe_ffn((idx * SUBLANES).reshape(-1).astype(jnp.int32), h2, hc2, moe_w_gate, moe_w_up, moe_w_down, l,
                     gs[..., None], g2, cg2, B * cap_l, tf=512)
        x = combine_expert_outputs(x.reshape(B * N, D), ys[0], rows_l.reshape(-1).astype(jnp.int32)).reshape(B, N, D)
        if need_ctx:
            xc = xc.reshape(B * M, D).at[rows_c.reshape(-1)].add(ys[1].reshape(-1, D)).reshape(B, M, D)
    return final_norm(x, final_norm_g, tm)
```

```python
import functools

import numpy as np
import jax
import jax.numpy as jnp
from jax import lax
from jax.experimental import pallas as pl
from jax.experimental.pallas import tpu as pltpu

D_MODEL = 1024
DEPTH = 4
GRID_W = 64
N_GROUPS = 4
GROUP_W = D_MODEL // N_GROUPS
GLA_HEADS = 4
GLA_DK = GROUP_W // GLA_HEADS
GLA_W = GLA_HEADS * GLA_DK
GLA_GATE_RANK = 16
GLA_TAU = 16.0
GLA_CHUNK = 64
GMLP_GROUPS = 4
GMLP_W = GROUP_W
GMLP_CHUNK = 128
LRU_W = GROUP_W
LRU_BLOCKS = 4
LRU_C = 8.0
CONV_W = 4
NAT_HEADS = 4
NAT_DH = GROUP_W // NAT_HEADS
NAT_W = NAT_HEADS * NAT_DH
NAT_KR_MAX = 8
NAT_KC = 16
N_EXPERTS = 16
EXPERT_FF = 2 * D_MODEL
EC_CAPACITY = 2
ROPE_BASE = 10000.0
EPS = 1e-6

IN_SIZES = (GLA_W, GLA_W, GLA_W, GLA_W, GLA_GATE_RANK, GLA_GATE_RANK,
            GMLP_W, GMLP_W, LRU_W, LRU_W, NAT_W, NAT_W, NAT_W)
IN_COLS = sum(IN_SIZES)

V7X_VMEM_LIMIT_BYTES = 56 * 1024 * 1024
LANES = 128
SUBLANES = 8
F32 = jnp.float32
BF16 = jnp.bfloat16
NEG_BIG = -1e30
GATE_PAD = LANES

NT_DIMS = (((1,), (1,)), ((), ()))
TN_DIMS = (((0,), (0,)), ((), ()))


def _cparams(sem):
    return pltpu.CompilerParams(dimension_semantics=sem, vmem_limit_bytes=V7X_VMEM_LIMIT_BYTES)


def _dot(a, b):
    return jnp.dot(a, b, preferred_element_type=F32)


def _split3(x):
    hi = x.astype(BF16)
    r1 = x - hi.astype(F32)
    mid = r1.astype(BF16)
    lo = (r1 - mid.astype(F32)).astype(BF16)
    return hi, mid, lo


def _head_block_mask(n):
    r = lax.broadcasted_iota(jnp.int32, (n, n), 0) // GLA_DK
    c = lax.broadcasted_iota(jnp.int32, (n, n), 1) // GLA_DK
    return r == c


def _axial_rope(u, cos, sin):
    lane = lax.broadcasted_iota(jnp.int32, u.shape, 1)
    first = (lane % (GLA_DK // 2)) < (GLA_DK // 4)
    partner = jnp.where(first, pltpu.roll(u, GLA_W - GLA_DK // 4, axis=1), pltpu.roll(u, GLA_DK // 4, axis=1))
    return u * cos + partner * sin


def _norm_inproj_kernel(*refs, rope):
    refs = list(refs)
    x_ref, g_ref, sc_ref, sh_ref, w_ref = refs[:5]
    refs = refs[5:]
    if rope:
        cos_ref, sin_ref = refs[:2]
        refs = refs[2:]
    gla_ref, gmlp_ref, lru_ref, nat_ref, gate_ref = refs
    x = x_ref[0]
    y = x * lax.rsqrt(jnp.mean(x * x, axis=-1, keepdims=True) + EPS)
    h = ((y * g_ref[...]) * (1.0 + sc_ref[0]) + sh_ref[0]).astype(BF16)
    o = 0
    for ref in (gla_ref, gmlp_ref, lru_ref, nat_ref, gate_ref):
        w = ref.shape[-1]
        ref[0] = _dot(h, w_ref[:, o:o + w]).astype(ref.dtype)
        o += w
    if rope:
        for c in range(2):
            cols = slice(c * GLA_W, (c + 1) * GLA_W)
            gla_ref[0, :, cols] = _axial_rope(gla_ref[0, :, cols], cos_ref[...], sin_ref[...])


def permute_w_in(w):
    g0 = 4 * GLA_W
    g1 = g0 + 2 * GLA_GATE_RANK
    pad = jnp.zeros((w.shape[0], GATE_PAD - 2 * GLA_GATE_RANK), w.dtype)
    return jnp.concatenate([w[:, :g0], w[:, g1:], w[:, g0:g1], pad], axis=1).astype(BF16)


def norm_inproj(x, g, scale, shift, wp, tm, tables=None):
    B, n, D = x.shape
    C = wp.shape[1]
    widths = (4 * GLA_W, 2 * GMLP_W, 2 * LRU_W, 3 * NAT_W, GATE_PAD)
    dtypes = (F32, F32, F32, BF16, F32)
    row = lambda b, i: (b, i, 0)
    per_b = lambda b, i: (b, 0, 0)
    full = lambda b, i: (0, 0)
    rope_specs = [pl.BlockSpec((tm, GLA_W), lambda b, i: (i, 0))] * 2 if tables is not None else []
    return pl.pallas_call(
        functools.partial(_norm_inproj_kernel, rope=tables is not None),
        grid=(B, n // tm),
        in_specs=[
            pl.BlockSpec((1, tm, D), row),
            pl.BlockSpec((1, D), full),
            pl.BlockSpec((1, 1, D), per_b),
            pl.BlockSpec((1, 1, D), per_b),
            pl.BlockSpec((D, C), full),
        ] + rope_specs,
        out_specs=[pl.BlockSpec((1, tm, w), row) for w in widths],
        out_shape=[jax.ShapeDtypeStruct((B, n, w), dt) for w, dt in zip(widths, dtypes)],
        compiler_params=_cparams(("parallel", "parallel")),
        name="norm_inproj",
    )(x, g.reshape(1, D), scale.reshape(B, 1, D), shift.reshape(B, 1, D), wp, *(tables or ()))


def _gla_kernel(*refs, tm, reverse, combine):
    refs = list(refs)
    q_ref, k_ref, v_ref, gate_ref, wg_ref, bg_ref, tri_ref, ones_ref, s0_ref = refs[:9]
    refs = refs[9:]
    if combine:
        ob_ref, og_ref, ng_ref = refs[:3]
        refs = refs[3:]
    o_ref, sfin_ref, st, obuf = refs
    i = pl.program_id(1)
    nch = tm // GLA_CHUNK

    @pl.when(i == 0)
    def _():
        st[...] = s0_ref[0]

    q = q_ref[0] * (GLA_DK ** -0.5)
    k = k_ref[0]
    a = _dot(gate_ref[0].astype(BF16), wg_ref[...]) + bg_ref[...]
    la = jax.nn.log_sigmoid(a) / GLA_TAU
    pieces = _split3(la)
    b = sum(_dot(tri_ref[...], p) for p in pieces)
    bl = sum(_dot(ones_ref[...], p) for p in pieces)
    q_in = (q * jnp.exp(b)).astype(BF16)
    k_in = (k * jnp.exp(-b)).astype(BF16)
    k_end = (k * jnp.exp(bl - b)).astype(BF16)
    dec = jnp.exp(bl)
    vb = v_ref[0].astype(BF16)

    head_mask = _head_block_mask(GLA_W)
    cr = lax.broadcasted_iota(jnp.int32, (GLA_CHUNK, GLA_W), 0)
    cs = lax.broadcasted_iota(jnp.int32, (GLA_CHUNK, GLA_W), 1) % GLA_CHUNK
    causal = (cs >= cr) if reverse else (cs <= cr)
    zero = jnp.zeros((), BF16)
    chunks = range(nch - 1, -1, -1) if reverse else range(nch)
    for c in chunks:
        sl = slice(c * GLA_CHUNK, (c + 1) * GLA_CHUNK)
        kbd_t = jnp.where(head_mask, jnp.concatenate([k_in[sl]] * GLA_HEADS, axis=0), zero)
        vbd = jnp.where(head_mask, jnp.concatenate([vb[sl]] * GLA_HEADS, axis=0), zero)
        att = lax.dot_general(q_in[sl], kbd_t, NT_DIMS, preferred_element_type=F32)
        att = jnp.where(causal, att, 0.0).astype(BF16)
        s_t = st[...]
        o = _dot(att, vbd) + lax.dot_general(q_in[sl], s_t.astype(BF16), NT_DIMS, preferred_element_type=F32)
        upd = lax.dot_general(vb[sl], k_end[sl], TN_DIMS, preferred_element_type=F32)
        st[...] = s_t * dec[c * GLA_CHUNK:c * GLA_CHUNK + 1] + jnp.where(head_mask, upd, 0.0)
        obuf[sl, :] = o

    sfin_ref[0] = st[...]

    if not combine:
        o_ref[0] = obuf[...]
    else:
        o = obuf[...] + ob_ref[0]
        sq_hi, sq_mid, _ = _split3(o * o)
        avg = jnp.where(head_mask, 1.0 / GLA_DK, 0.0).astype(BF16)
        ms = _dot(sq_hi, avg) + _dot(sq_mid, avg)
        og = og_ref[0]
        o_ref[0] = (o * lax.rsqrt(ms + EPS) * ng_ref[...] * (og * jax.nn.sigmoid(og))).astype(o_ref.dtype)


def _chunk_sum_matrices(tm, reverse):
    r = np.arange(tm)[:, None]
    c = np.arange(tm)[None, :]
    same = (r // GLA_CHUNK) == (c // GLA_CHUNK)
    tri = same & ((c >= r) if reverse else (c <= r))
    return jnp.asarray(tri, BF16), jnp.asarray(same, BF16)


def gla_direction(gla, gate, wg, bg, s0, tm, reverse, combine=None):
    B, L, _ = gla.shape
    nt = L // tm
    W = GLA_W
    blk = (lambda i: nt - 1 - i) if reverse else (lambda i: i)
    col = lambda j: (lambda b, i: (b, blk(i), j))
    full = lambda b, i: (0, 0)
    tri, ones = _chunk_sum_matrices(tm, reverse)
    args = [gla, gla, gla, gate, wg, bg, tri, ones, s0]
    specs = [
        pl.BlockSpec((1, tm, W), col(0)), pl.BlockSpec((1, tm, W), col(1)), pl.BlockSpec((1, tm, W), col(2)),
        pl.BlockSpec((1, tm, GATE_PAD), col(0)),
        pl.BlockSpec((GATE_PAD, W), full), pl.BlockSpec((1, W), full),
        pl.BlockSpec((tm, tm), full), pl.BlockSpec((tm, tm), full),
        pl.BlockSpec((1, W, W), lambda b, i: (b, 0, 0)),
    ]
    if combine is not None:
        ob, norm_g = combine
        args += [ob, gla, norm_g]
        specs += [pl.BlockSpec((1, tm, W), col(0)), pl.BlockSpec((1, tm, W), col(3)), pl.BlockSpec((1, W), full)]
    return pl.pallas_call(
        functools.partial(_gla_kernel, tm=tm, reverse=reverse, combine=combine is not None),
        grid=(B, nt),
        in_specs=specs,
        out_specs=[pl.BlockSpec((1, tm, W), col(0)), pl.BlockSpec((1, W, W), lambda b, i: (b, 0, 0))],
        out_shape=[jax.ShapeDtypeStruct((B, L, W), F32 if combine is None else BF16),
                   jax.ShapeDtypeStruct((B, W, W), F32)],
        scratch_shapes=[pltpu.VMEM((W, W), F32), pltpu.VMEM((tm, W), F32)],
        compiler_params=_cparams(("parallel", "arbitrary")),
        name="gla_bwd" if reverse else "gla_fwd",
    )(*args)


def rope_tables(n):
    quarter = GLA_DK // 4
    pos = jnp.arange(n)
    inv = ROPE_BASE ** (-jnp.arange(quarter, dtype=F32) / quarter)
    row = (pos // GRID_W).astype(F32)
    colp = (pos % GRID_W).astype(F32)
    ar = row[:, None] * inv[None, :]
    ac = colp[:, None] * inv[None, :]
    cos = jnp.concatenate([jnp.cos(ar), jnp.cos(ar), jnp.cos(ac), jnp.cos(ac)], axis=1)
    sin = jnp.concatenate([-jnp.sin(ar), jnp.sin(ar), -jnp.sin(ac), jnp.sin(ac)], axis=1)
    return jnp.tile(cos, (1, GLA_HEADS)), jnp.tile(sin, (1, GLA_HEADS))


def gla_mixer(gla, gate, cgla, cgate, w_gate, b_gate, norm_g, need_ctx, tm):
    B, M, _ = cgla.shape
    zeros = jnp.zeros((B, GLA_W, GLA_W), F32)
    ng = jnp.tile(norm_g, GLA_HEADS).reshape(1, GLA_W)
    wgs, bgs = [], []
    for d in range(2):
        wg = jnp.pad(w_gate[d], ((d * GLA_GATE_RANK, GATE_PAD - (d + 1) * GLA_GATE_RANK), (0, 0)))
        wgs.append(wg.astype(BF16))
        bgs.append(b_gate[d].reshape(1, GLA_W))
    ocb, sb = gla_direction(cgla, cgate, wgs[1], bgs[1], zeros, M, True)
    ob, _ = gla_direction(gla, gate, wgs[1], bgs[1], sb, tm, True)
    if need_ctx:
        oc, sf = gla_direction(cgla, cgate, wgs[0], bgs[0], zeros, M, False, combine=(ocb, ng))
    else:
        oc, sf = gla_direction(cgla, cgate, wgs[0], bgs[0], zeros, M, False)
    o, _ = gla_direction(gla, gate, wgs[0], bgs[0], sf, tm, False, combine=(ob, ng))
    return o, oc


def _gmlp_kernel(u_ref, v_ref, g_ref, w_ref, b_ref, o_ref, *, tm):
    v = v_ref[0]
    vn = (v * lax.rsqrt(jnp.mean(v * v, axis=-1, keepdims=True) + EPS) * g_ref[...]).astype(BF16)
    grp = lax.broadcasted_iota(jnp.int32, (GMLP_CHUNK, GMLP_W), 1) // (GMLP_W // GMLP_GROUPS)
    zero = jnp.zeros((), BF16)
    for c in range(tm // GMLP_CHUNK):
        sl = slice(c * GMLP_CHUNK, (c + 1) * GMLP_CHUNK)
        mixed = b_ref[...]
        for g in range(GMLP_GROUPS):
            mixed = mixed + _dot(w_ref[g], jnp.where(grp == g, vn[sl], zero))
        o_ref[0, sl, :] = (u_ref[0, sl, :] * mixed).astype(o_ref.dtype)


def gmlp_mixer(gm, norm_g, w_s, b_s, tm):
    B, L, _ = gm.shape
    W = GMLP_W
    bias = jnp.repeat(b_s.T, W // GMLP_GROUPS, axis=1)
    return pl.pallas_call(
        functools.partial(_gmlp_kernel, tm=tm),
        grid=(B, L // tm),
        in_specs=[
            pl.BlockSpec((1, tm, W), lambda b, i: (b, i, 0)),
            pl.BlockSpec((1, tm, W), lambda b, i: (b, i, 1)),
            pl.BlockSpec((1, W), lambda b, i: (0, 0)),
            pl.BlockSpec((GMLP_GROUPS, GMLP_CHUNK, GMLP_CHUNK), lambda b, i: (0, 0, 0)),
            pl.BlockSpec((GMLP_CHUNK, W), lambda b, i: (0, 0)),
        ],
        out_specs=pl.BlockSpec((1, tm, W), lambda b, i: (b, i, 0)),
        out_shape=jax.ShapeDtypeStruct((B, L, W), BF16),
        compiler_params=_cparams(("parallel", "parallel")),
        name="gmlp",
    )(gm, gm, norm_g.reshape(1, W), w_s.astype(BF16), bias)


LRU_UNROLL = 8


def _lru_kernel(*refs, tm, reverse, combine):
    refs = list(refs)
    x_ref, xp_ref, xn_ref, cw_ref, cb_ref, wr_ref, br_ref, wi_ref, bi_ref, ncs_ref, h0_ref = refs[:11]
    refs = refs[11:]
    if combine:
        hb_ref, ly_ref = refs[:2]
        refs = refs[2:]
    o_ref, ext, a_s, b_s, carry = refs
    W = LRU_W
    i = pl.program_id(1)
    nt = pl.num_programs(1)
    t = (nt - 1 - i) if reverse else i
    H = SUBLANES

    ext[0:H] = jnp.where(t > 0, xp_ref[0], 0.0)
    ext[H:H + tm] = x_ref[0]
    ext[H + tm:2 * H + tm] = jnp.where(t < nt - 1, xn_ref[0], 0.0)
    xc = cb_ref[...]
    for tap in range(CONV_W):
        xc = xc + cw_ref[tap:tap + 1, :] * ext[H - CONV_W // 2 + tap:H - CONV_W // 2 + tap + tm]
    xcb = xc.astype(BF16)
    r = jax.nn.sigmoid(_dot(xcb, wr_ref[...]) + br_ref[...])
    ig = jax.nn.sigmoid(_dot(xcb, wi_ref[...]) + bi_ref[...])
    log_a = ncs_ref[...] * r
    a = jnp.exp(log_a)
    b = jnp.sqrt(-jnp.tanh(log_a) * (a * a + 1.0)) * (ig * xc)

    rowi = lax.broadcasted_iota(jnp.int32, (tm, W), 0) % H
    for s in (1, 2, 4):
        if reverse:
            ok = rowi < H - s
            a_sh = pltpu.roll(a, tm - s, axis=0)
            b_sh = pltpu.roll(b, tm - s, axis=0)
        else:
            ok = rowi >= s
            a_sh = pltpu.roll(a, s, axis=0)
            b_sh = pltpu.roll(b, s, axis=0)
        b = a * jnp.where(ok, b_sh, 0.0) + b
        a = a * jnp.where(ok, a_sh, 1.0)
    a_s[...] = a
    b_s[...] = b

    @pl.when(i == 0)
    def _():
        carry[...] = jnp.broadcast_to(h0_ref[0], (H, W))

    ng = tm // H

    def body(j, h):
        for u in range(LRU_UNROLL):
            g = j * LRU_UNROLL + u
            g = (ng - 1 - g) if reverse else g
            rows = pl.ds(pl.multiple_of(g * H, H), H)
            hg = b_s[rows, :] + a_s[rows, :] * h
            b_s[rows, :] = hg
            h = jnp.broadcast_to(hg[0:1] if reverse else hg[H - 1:H], (H, W))
        return h

    carry[...] = lax.fori_loop(0, ng // LRU_UNROLL, body, carry[...])

    if combine:
        o_ref[0] = ((b_s[...] + hb_ref[0]) * jax.nn.gelu(ly_ref[0])).astype(o_ref.dtype)
    else:
        o_ref[0] = b_s[...]


def lru_direction(lru, conv_w, conv_b, wr, br, wi, bi, ncs, h0, tm, reverse, hb=None):
    B, L, _ = lru.shape
    nt = L // tm
    W = LRU_W
    H = SUBLANES
    nh = L // H
    per = tm // H
    blk = (lambda i: nt - 1 - i) if reverse else (lambda i: i)
    full = lambda b, i: (0, 0)
    args = [lru, lru, lru, conv_w, conv_b, wr, br, wi, bi, ncs, h0]
    specs = [
        pl.BlockSpec((1, tm, W), lambda b, i: (b, blk(i), 0)),
        pl.BlockSpec((1, H, W), lambda b, i: (b, jnp.maximum(blk(i) * per - 1, 0), 0)),
        pl.BlockSpec((1, H, W), lambda b, i: (b, jnp.minimum((blk(i) + 1) * per, nh - 1), 0)),
        pl.BlockSpec((CONV_W, W), full), pl.BlockSpec((1, W), full),
        pl.BlockSpec((W, W), full), pl.BlockSpec((1, W), full),
        pl.BlockSpec((W, W), full), pl.BlockSpec((1, W), full),
        pl.BlockSpec((1, W), full),
        pl.BlockSpec((1, 1, W), lambda b, i: (b, 0, 0)),
    ]
    if hb is not None:
        args += [hb, lru]
        specs += [pl.BlockSpec((1, tm, W), lambda b, i: (b, blk(i), 0)),
                  pl.BlockSpec((1, tm, W), lambda b, i: (b, blk(i), 1))]
    return pl.pallas_call(
        functools.partial(_lru_kernel, tm=tm, reverse=reverse, combine=hb is not None),
        grid=(B, nt),
        in_specs=specs,
        out_specs=pl.BlockSpec((1, tm, W), lambda b, i: (b, blk(i), 0)),
        out_shape=jax.ShapeDtypeStruct((B, L, W), F32 if hb is None else BF16),
        scratch_shapes=[pltpu.VMEM((tm + 2 * H, W), F32), pltpu.VMEM((tm, W), F32), pltpu.VMEM((tm, W), F32),
                        pltpu.VMEM((H, W), F32)],
        compiler_params=_cparams(("parallel", "arbitrary")),
        name="lru_bwd" if reverse else "lru_fwd",
    )(*args)


def _block_diag(w):
    G, n, _ = w.shape
    same = jnp.eye(G, dtype=w.dtype)
    return (w[:, :, None, :] * same[:, None, :, None]).reshape(G * n, G * n)


def lru_mixer(lru, clru, conv_w, conv_b, w_r, b_r, w_i, b_i, lam, need_ctx, tm):
    B, M, _ = clru.shape
    W = LRU_W
    cb = conv_b.reshape(1, W)
    ncs = -LRU_C * jax.nn.softplus(-lam.astype(F32))
    p = [(_block_diag(w_r[d]).astype(BF16), b_r[d].reshape(1, W), _block_diag(w_i[d]).astype(BF16),
          b_i[d].reshape(1, W), ncs[d].reshape(1, W)) for d in range(2)]
    zeros = jnp.zeros((B, 1, W), F32)
    hcb = lru_direction(clru, conv_w, cb, *p[1], zeros, M, True)
    hb = lru_direction(lru, conv_w, cb, *p[1], hcb[:, 0:1], tm, True)
    hcf = lru_direction(clru, conv_w, cb, *p[0], zeros, M, False)
    out = lru_direction(lru, conv_w, cb, *p[0], hcf[:, M - 1:M], tm, False, hb=hb)
    out_c = None
    if need_ctx:
        out_c = lru_direction(clru, conv_w, cb, *p[0], zeros, M, False, hb=hcb)
    return out, out_c


NAT_QROWS = 4
NAT_KROWS = 3 * NAT_QROWS
NAT_TQ = NAT_QROWS * GRID_W
NAT_TK = NAT_KROWS * GRID_W


def _softmax_pv(s_loc, s_ctx, v_loc, v_ctx):
    m = jnp.maximum(jnp.max(s_loc, axis=-1, keepdims=True), jnp.max(s_ctx, axis=-1, keepdims=True))
    e_loc = jnp.exp(s_loc - m)
    e_ctx = jnp.exp(s_ctx - m)
    den = jnp.sum(e_loc, axis=-1, keepdims=True) + jnp.sum(e_ctx, axis=-1, keepdims=True)
    o = _dot(e_loc.astype(BF16), v_loc) + _dot(e_ctx.astype(BF16), v_ctx)
    return o / den


def _nat_kernel(q_ref, kp_ref, kc_ref, kn_ref, vp_ref, vc_ref, vn_ref, ck_ref, cv_ref, bias_ref, o_ref):
    q = q_ref[0]
    ck = ck_ref[0]
    cv = cv_ref[0]
    scale = NAT_DH ** -0.5
    outs = []
    for h in range(NAT_HEADS):
        hs = slice(h * NAT_DH, (h + 1) * NAT_DH)
        qh = q[:, hs]
        s_loc = jnp.concatenate(
            [lax.dot_general(qh, k_ref[0][:, hs], NT_DIMS, preferred_element_type=F32)
             for k_ref in (kp_ref, kc_ref, kn_ref)], axis=-1) * scale + bias_ref[0, h]
        s_ctx = lax.dot_general(qh, ck[:, hs], NT_DIMS, preferred_element_type=F32) * scale
        v_loc = jnp.concatenate([v_ref[0][:, hs] for v_ref in (vp_ref, vc_ref, vn_ref)], axis=0)
        outs.append(_softmax_pv(s_loc, s_ctx, v_loc, cv[:, hs]))
    o_ref[0] = jnp.concatenate(outs, axis=-1).astype(o_ref.dtype)


def nat_bias_tables(rpb, rows):
    kr = NAT_KR_MAX
    qc = np.arange(GRID_W)
    kcol = np.arange(GRID_W)
    cs = np.clip(qc - NAT_KC // 2, 0, GRID_W - NAT_KC)
    valid_c = (kcol[None, :] >= cs[:, None]) & (kcol[None, :] < cs[:, None] + NAT_KC)
    edge = GRID_W - NAT_KC
    padded = jnp.pad(rpb.astype(F32), ((0, 0), (0, 0), (edge, edge)), mode="edge")
    by_col = jnp.stack([padded[:, :, GRID_W - 1 - q:2 * GRID_W - 1 - q] for q in range(GRID_W)], axis=1)
    by_col = jnp.pad(by_col, ((0, 0), (0, 0), (NAT_KROWS, NAT_KROWS), (0, 0)))
    tables = []
    for r0, ks in ((0, 0), (NAT_QROWS, NAT_QROWS - kr // 2), (rows - NAT_QROWS, rows - NAT_KROWS)):
        r = r0 + np.arange(NAT_QROWS)
        krow = ks + np.arange(NAT_KROWS)
        rs = np.clip(r - kr // 2, 0, rows - kr)
        valid_r = (krow[None, :] >= rs[:, None]) & (krow[None, :] < rs[:, None] + kr)
        starts = ks - r + NAT_KR_MAX - 1 + NAT_KROWS
        bias = jnp.stack([by_col[:, :, int(s):int(s) + NAT_KROWS] for s in starts], axis=1)
        mask = valid_r[:, None, :, None] & valid_c[None, :, None, :]
        bias = jnp.where(jnp.asarray(mask)[None], bias, NEG_BIG)
        tables.append(bias.reshape(rpb.shape[0], NAT_TQ, NAT_TK))
    return jnp.stack(tables)


def nat_mixer(nat, cnat, bias, layer):
    B, N, _ = nat.shape
    M = cnat.shape[1]
    W = NAT_W
    nb = N // NAT_TQ
    T = NAT_TQ
    centre = lambda j: jnp.clip(j, 1, nb - 2)
    near = lambda c, d: (lambda b, j: (b, centre(j) + d, c))
    variant = lambda b, j: (jnp.where(j == 0, 0, jnp.where(j == nb - 1, 2, 1)), layer, 0, 0)
    return pl.pallas_call(
        _nat_kernel,
        grid=(B, nb),
        in_specs=[
            pl.BlockSpec((1, T, W), lambda b, j: (b, j, 0)),
            pl.BlockSpec((1, T, W), near(1, -1)), pl.BlockSpec((1, T, W), near(1, 0)), pl.BlockSpec((1, T, W), near(1, 1)),
            pl.BlockSpec((1, T, W), near(2, -1)), pl.BlockSpec((1, T, W), near(2, 0)), pl.BlockSpec((1, T, W), near(2, 1)),
            pl.BlockSpec((1, M, W), lambda b, j: (b, 0, 1)),
            pl.BlockSpec((1, M, W), lambda b, j: (b, 0, 2)),
            pl.BlockSpec((1, NAT_HEADS, NAT_TQ, NAT_TK), variant),
        ],
        out_specs=pl.BlockSpec((1, T, W), lambda b, j: (b, j, 0)),
        out_shape=jax.ShapeDtypeStruct((B, N, W), BF16),
        compiler_params=_cparams(("parallel", "arbitrary")),
        name="nat",
    )(nat, nat, nat, nat, nat, nat, nat, cnat, cnat, bias)


def _ctx_attn_kernel(q_ref, k_ref, v_ref, o_ref):
    q = q_ref[0]
    k = k_ref[0]
    v = v_ref[0]
    scale = NAT_DH ** -0.5
    outs = []
    for h in range(NAT_HEADS):
        hs = slice(h * NAT_DH, (h + 1) * NAT_DH)
        s = lax.dot_general(q[:, hs], k[:, hs], NT_DIMS, preferred_element_type=F32) * scale
        e = jnp.exp(s - jnp.max(s, axis=-1, keepdims=True))
        outs.append(_dot(e.astype(BF16), v[:, hs]) / jnp.sum(e, axis=-1, keepdims=True))
    o_ref[0] = jnp.concatenate(outs, axis=-1).astype(o_ref.dtype)


def ctx_attention(cnat):
    B, M, _ = cnat.shape
    W = NAT_W
    return pl.pallas_call(
        _ctx_attn_kernel,
        grid=(B,),
        in_specs=[pl.BlockSpec((1, M, W), lambda b, c=c: (b, 0, c)) for c in range(3)],
        out_specs=pl.BlockSpec((1, M, W), lambda b: (b, 0, 0)),
        out_shape=jax.ShapeDtypeStruct((B, M, W), BF16),
        compiler_params=_cparams(("parallel",)),
        name="ctx_attn",
    )(cnat, cnat, cnat)


def _outproj_kernel(m0_ref, m1_ref, m2_ref, m3_ref, w_ref, x_ref, g1_ref, n2_ref, sc_ref, sh_ref, wr_ref,
                    xo_ref, h2_ref, aff_ref):
    y = 0.0
    for gi, m_ref in enumerate((m0_ref, m1_ref, m2_ref, m3_ref)):
        y = y + _dot(m_ref[0].astype(BF16), w_ref[gi * GROUP_W:(gi + 1) * GROUP_W, :])
    x = x_ref[0] + g1_ref[0] * y
    xo_ref[0] = x
    xn = x * lax.rsqrt(jnp.mean(x * x, axis=-1, keepdims=True) + EPS)
    h2 = (xn * n2_ref[...]) * (1.0 + sc_ref[0]) + sh_ref[0]
    tm = h2.shape[0]
    for j in range(SUBLANES):
        h2_ref[pl.ds(j, tm, stride=SUBLANES), :] = h2[:, j * LANES:(j + 1) * LANES]
    logits = _dot(h2.astype(BF16), wr_ref[...])
    e = jnp.exp(logits - jnp.max(logits, axis=-1, keepdims=True))
    aff_ref[0] = e / jnp.sum(e, axis=-1, keepdims=True)


def outproj_norm_router(parts, w_out, x, g1, n2, scale2, shift2, w_router, tm):
    B, n, D = x.shape
    E = w_router.shape[1]
    row = lambda b, i: (b, i, 0)
    per_b = lambda b, i: (b, 0, 0)
    full = lambda b, i: (0, 0)
    return pl.pallas_call(
        _outproj_kernel,
        grid=(B, n // tm),
        in_specs=[pl.BlockSpec((1, tm, GROUP_W), row)] * N_GROUPS + [
            pl.BlockSpec((D, D), full),
            pl.BlockSpec((1, tm, D), row),
            pl.BlockSpec((1, 1, D), per_b),
            pl.BlockSpec((1, D), full),
            pl.BlockSpec((1, 1, D), per_b),
            pl.BlockSpec((1, 1, D), per_b),
            pl.BlockSpec((D, E), full),
        ],
        out_specs=[
            pl.BlockSpec((1, tm, D), row),
            pl.BlockSpec((tm * SUBLANES, LANES), lambda b, i: (b * (n // tm) + i, 0)),
            pl.BlockSpec((1, tm, E), row),
        ],
        out_shape=[
            jax.ShapeDtypeStruct((B, n, D), F32),
            jax.ShapeDtypeStruct((B * n * SUBLANES, LANES), F32),
            jax.ShapeDtypeStruct((B, n, E), F32),
        ],
        compiler_params=_cparams(("parallel", "parallel")),
        name="outproj_norm_router",
    )(*parts, w_out, x, g1.reshape(B, 1, D), n2.reshape(1, D), scale2.reshape(B, 1, D), shift2.reshape(B, 1, D),
      w_router)


MOE_ROW_CHUNK = 512
MOE_ISSUE_UNROLL = 8


def _moe_ffn_kernel(*refs, n_lat, n_ctx, n_batch):
    refs = list(refs)
    idx_ref, h_hbm = refs[:2]
    refs = refs[2:]
    if n_ctx:
        hc_hbm = refs.pop(0)
    wg_ref, wu_ref, wd_ref, gate_ref, g2_ref = refs[:5]
    refs = refs[5:]
    if n_ctx:
        cg2_ref = refs.pop(0)
    yl_hbm = refs.pop(0)
    if n_ctx:
        yc_ref = refs.pop(0)
    xf, xb, acc, ytiles, sem, out_sem = refs
    e = pl.program_id(0)
    f = pl.program_id(1)
    n_exp = pl.num_programs(0)
    last = pl.num_programs(1) - 1
    T = n_lat + n_ctx

    R = SUBLANES

    def start_rows(src, expert, first, count):
        def body(i, carry):
            for u in range(MOE_ISSUE_UNROLL):
                s = first + i * MOE_ISSUE_UNROLL + u
                r = pl.multiple_of(idx_ref[expert * T + s], R)
                pltpu.make_async_copy(src.at[pl.ds(r, R)], xf.at[pl.ds(pl.multiple_of(s * R, R), R)],
                                      sem.at[0]).start()
            return carry

        lax.fori_loop(0, count // MOE_ISSUE_UNROLL, body, 0)

    def start_gather(expert):
        start_rows(h_hbm, expert, 0, n_lat)
        if n_ctx:
            start_rows(hc_hbm, expert, n_lat, n_ctx)

    @pl.when(f == 0)
    def _():
        @pl.when(e == 0)
        def _():
            start_gather(e)

        pltpu.make_async_copy(h_hbm.at[pl.ds(0, T * R)], xf, sem.at[0]).wait()
        for j in range(R):
            xb[:, j * LANES:(j + 1) * LANES] = xf[pl.ds(j, T, stride=R), :].astype(BF16)

        @pl.when(e + 1 < n_exp)
        def _():
            start_gather(e + 1)

        acc[...] = jnp.zeros_like(acc)
        if n_ctx:
            yc_ref[...] = jnp.zeros_like(yc_ref)

    wg = wg_ref[0, 0].astype(BF16)
    wu = wu_ref[0, 0].astype(BF16)
    wd = wd_ref[0, 0].astype(BF16)

    def ffn(x):
        hg = _dot(x, wg)
        hu = _dot(x, wu)
        return _dot((hg * jax.nn.sigmoid(hg) * hu).astype(BF16), wd)

    def lat_rows(first, count, j):
        return pl.ds(first * R + j, count, stride=R)

    for i in range(n_lat // MOE_ROW_CHUNK):
        rows = slice(i * MOE_ROW_CHUNK, (i + 1) * MOE_ROW_CHUNK)
        acc[rows, :] += ffn(xb[rows, :])
    if n_ctx:
        yc_ref[0] += ffn(xb[n_lat:T, :])

    def out_copy(expert):
        dst = yl_hbm.at[pl.ds(pl.multiple_of(expert * (n_lat * R), n_lat * R), n_lat * R)]
        return pltpu.make_async_copy(ytiles, dst, out_sem.at[0])

    @pl.when(f == last)
    def _():
        @pl.when(e > 0)
        def _():
            out_copy(e - 1).wait()

        per_l = n_lat // n_batch
        for b in range(n_batch):
            rows = slice(b * per_l, (b + 1) * per_l)
            gate = gate_ref[0, rows, :]
            for j in range(R):
                lanes = slice(j * LANES, (j + 1) * LANES)
                ytiles[lat_rows(b * per_l, per_l, j), :] = acc[rows, lanes] * gate * g2_ref[b:b + 1, lanes]
        out_copy(e).start()

        @pl.when(e == n_exp - 1)
        def _():
            out_copy(e).wait()

        if n_ctx:
            per_c = n_ctx // n_batch
            for b in range(n_batch):
                rows = slice(b * per_c, (b + 1) * per_c)
                yc_ref[0, rows, :] = (yc_ref[0, rows, :] * gate_ref[0, n_lat + b * per_c:n_lat + (b + 1) * per_c, :]
                                      * cg2_ref[b:b + 1, :])


def moe_ffn(idx, h2, hc2, w_gate, w_up, w_down, layer, gates, g2, cg2, n_lat, tf):
    E, T, _ = gates.shape
    n_ctx = T - n_lat
    D = SUBLANES * LANES
    F = w_gate.shape[3]
    B = g2.shape[0]
    any_spec = pl.BlockSpec(memory_space=pl.ANY)
    per_e = lambda e, f, idx_ref: (e, 0, 0)
    full = lambda e, f, idx_ref: (0, 0)
    args = [h2] + ([hc2] if n_ctx else []) + [w_gate, w_up, w_down, gates, g2] + ([cg2] if n_ctx else [])
    specs = [any_spec] + ([any_spec] if n_ctx else []) + [
        pl.BlockSpec((1, 1, D, tf), lambda e, f, idx_ref: (layer, e, 0, f)),
        pl.BlockSpec((1, 1, D, tf), lambda e, f, idx_ref: (layer, e, 0, f)),
        pl.BlockSpec((1, 1, tf, D), lambda e, f, idx_ref: (layer, e, f, 0)),
        pl.BlockSpec((1, T, 1), per_e),
        pl.BlockSpec((B, D), full),
    ] + ([pl.BlockSpec((B, D), full)] if n_ctx else [])
    out_specs = [any_spec] + ([pl.BlockSpec((1, n_ctx, D), per_e)] if n_ctx else [])
    out_shape = [jax.ShapeDtypeStruct((E * n_lat * SUBLANES, LANES), F32)] + (
        [jax.ShapeDtypeStruct((E, n_ctx, D), F32)] if n_ctx else [])
    return pl.pallas_call(
        functools.partial(_moe_ffn_kernel, n_lat=n_lat, n_ctx=n_ctx, n_batch=B),
        grid_spec=pltpu.PrefetchScalarGridSpec(
            num_scalar_prefetch=1,
            grid=(E, F // tf),
            in_specs=specs,
            out_specs=out_specs,
            scratch_shapes=[pltpu.VMEM((T * SUBLANES, LANES), F32), pltpu.VMEM((T, D), BF16),
                            pltpu.VMEM((n_lat, D), F32), pltpu.VMEM((n_lat * SUBLANES, LANES), F32),
                            pltpu.SemaphoreType.DMA((1,)), pltpu.SemaphoreType.DMA((1,))],
        ),
        out_shape=out_shape,
        compiler_params=_cparams(("arbitrary", "arbitrary")),
        name="moe_ffn",
    )(idx, *args)


COMBINE_TILE = 256
COMBINE_ISSUE_UNROLL = 8


def _combine_kernel(src_ref, dst_ref, bnd_ref, kmax_ref, y_hbm, x_ref, *refs, tm, norm):
    if norm:
        g_ref, o_ref, planes, sem = refs
    else:
        o_ref, planes, sem = refs
    i = pl.program_id(0)
    nt = pl.num_programs(0)
    R = SUBLANES
    U = COMBINE_ISSUE_UNROLL
    slab = tm * R

    def groups(t):
        return (bnd_ref[t + 1] - bnd_ref[t] + U - 1) // U

    def prepare(t, buf):
        def zero(k, carry):
            planes[buf, pl.ds(pl.multiple_of(k * slab, slab), slab), :] = jnp.zeros((slab, LANES), F32)
            return carry

        lax.fori_loop(0, jnp.maximum(kmax_ref[t], 1), zero, 0)
        first = bnd_ref[t]
        final = bnd_ref[t + 1] - 1

        def issue(g, carry):
            for u in range(U):
                a = first + g * U + u
                live = a <= final
                a = jnp.minimum(a, final)
                s = pl.multiple_of(src_ref[a], R)
                d = pl.multiple_of(jnp.where(live, dst_ref[a], N_EXPERTS * slab + u * R), R)
                pltpu.make_async_copy(y_hbm.at[pl.ds(s, R)], planes.at[buf, pl.ds(d, R)], sem.at[buf]).start()
            return carry

        lax.fori_loop(0, groups(t), issue, 0)

    @pl.when(i == 0)
    def _():
        prepare(0, 0)

    @pl.when(i + 1 < nt)
    def _():
        prepare(i + 1, (i + 1) % 2)

    buf = i % 2
    n_groups = groups(i)
    p = 1
    while p * U <= tm * N_EXPERTS:
        @pl.when((n_groups & p) != 0)
        def _(p=p):
            pltpu.make_async_copy(y_hbm.at[pl.ds(0, p * U * R)], planes.at[buf, pl.ds(0, p * U * R)],
                                  sem.at[buf]).wait()
        p *= 2

    kmax = kmax_ref[i]

    def add(k, carry):
        planes[buf, 0:slab, :] += planes[buf, pl.ds(pl.multiple_of(k * slab, slab), slab), :]
        return carry

    lax.fori_loop(1, kmax, add, 0)

    sumsq = jnp.zeros((tm, 1), F32)
    for j in range(R):
        lanes = slice(j * LANES, (j + 1) * LANES)
        y = x_ref[:, lanes] + planes[buf, pl.ds(j, tm, stride=R), :]
        o_ref[:, lanes] = y
        if norm:
            sumsq = sumsq + jnp.sum(y * y, axis=-1, keepdims=True)
    if norm:
        o_ref[...] = o_ref[...] * lax.rsqrt(sumsq / (R * LANES) + EPS) * g_ref[...]


def combine_expert_outputs(x, y_tiles, token_rows, norm_g=None):
    n, D = x.shape
    A = token_rows.shape[0]
    tm = COMBINE_TILE
    nt = n // tm
    tok, src = lax.sort_key_val(token_rows, jnp.arange(A, dtype=jnp.int32))
    pos = jnp.arange(A, dtype=jnp.int32)
    is_start = jnp.concatenate([jnp.ones((1,), bool), tok[1:] != tok[:-1]])
    rank = pos - lax.cummax(jnp.where(is_start, pos, 0))
    tile = tok // tm
    dst = (rank * tm + tok % tm) * SUBLANES
    edges = jnp.arange(nt + 1, dtype=jnp.int32) * tm
    bounds = jnp.sum(tok[None, :] < edges[:, None], axis=1).astype(jnp.int32)
    kmax = jnp.max(jnp.where(tile[None, :] == jnp.arange(nt, dtype=jnp.int32)[:, None], rank[None, :] + 1, 0),
                   axis=1).astype(jnp.int32)
    row = lambda i, *_: (i, 0)
    norm = norm_g is not None
    extra = [norm_g.reshape(1, D)] if norm else []
    return pl.pallas_call(
        functools.partial(_combine_kernel, tm=tm, norm=norm),
        grid_spec=pltpu.PrefetchScalarGridSpec(
            num_scalar_prefetch=4,
            grid=(nt,),
            in_specs=[pl.BlockSpec(memory_space=pl.ANY), pl.BlockSpec((tm, D), row)] + (
                [pl.BlockSpec((1, D), lambda i, *_: (0, 0))] if norm else []),
            out_specs=pl.BlockSpec((tm, D), row),
            scratch_shapes=[
                pltpu.VMEM((2, (N_EXPERTS * tm + COMBINE_ISSUE_UNROLL) * SUBLANES, LANES), F32),
                pltpu.SemaphoreType.DMA((2,))],
        ),
        out_shape=jax.ShapeDtypeStruct((n, D), F32),
        compiler_params=_cparams(("arbitrary",)),
        name="combine",
    )(src * SUBLANES, dst, bounds, kmax, y_tiles, x, *extra)


def route(aff, cap):
    return lax.top_k(jnp.swapaxes(aff, 1, 2), cap)


TOKEN_TILE = 512


def kernel(x, c, ctx, c_ctx, w_ada, b_ada, norm1_g, norm2_g, w_in, w_out, gla_w_gate, gla_b_gate, gla_norm_g,
           gmlp_norm_g, gmlp_w_s, gmlp_b_s, lru_conv_w, lru_conv_b, lru_w_r, lru_b_r, lru_w_i, lru_b_i,
           lru_lambda, nat_rpb, moe_w_router, moe_w_gate, moe_w_up, moe_w_down, final_norm_g):
    B, N, D = x.shape
    M = ctx.shape[1]
    E = N_EXPERTS
    tm = TOKEN_TILE
    cap_l = max(1, EC_CAPACITY * N // E)
    cap_c = max(1, EC_CAPACITY * M // E)
    sc = jax.nn.silu(c)
    scc = jax.nn.silu(c_ctx)
    xc = ctx
    batch_ids = jnp.arange(B, dtype=jnp.int32)[:, None, None]
    tables = rope_tables(N)
    mods = jnp.einsum('bd,ldk->lbk', jnp.concatenate([sc, scc[None]], axis=0), w_ada) + b_ada[:, None, :]
    nat_bias = nat_bias_tables(nat_rpb.reshape(DEPTH * NAT_HEADS, *nat_rpb.shape[2:]), N // GRID_W)
    for l in range(DEPTH):
        need_ctx = l < DEPTH - 1
        mod = mods[l, :B]
        mod_c = jnp.broadcast_to(mods[l, B:], (B, 6 * D))
        sh1, sc1, g1, sh2, sc2, g2 = jnp.split(mod, 6, axis=-1)
        csh1, csc1, cg1, csh2, csc2, cg2 = jnp.split(mod_c, 6, axis=-1)

        wp = permute_w_in(w_in[l])
        gla, gm, lru, nat, gate = norm_inproj(x, norm1_g[l], sc1, sh1, wp, tm, tables)
        cgla, cgm, clru, cnat, cgate = norm_inproj(xc, norm1_g[l], csc1, csh1, wp, M)

        gla_o, gla_c = gla_mixer(gla, gate, cgla, cgate, gla_w_gate[l], gla_b_gate[l], gla_norm_g[l], need_ctx, tm)
        gm_o = gmlp_mixer(gm, gmlp_norm_g[l], gmlp_w_s[l], gmlp_b_s[l], tm)
        lru_o, lru_c = lru_mixer(lru, clru, lru_conv_w[l], lru_conv_b[l], lru_w_r[l], lru_b_r[l], lru_w_i[l],
                                 lru_b_i[l], lru_lambda[l], need_ctx, tm)
        nat_o = nat_mixer(nat, cnat, nat_bias, l)

        wo = w_out[l].astype(BF16)
        wr = moe_w_router[l].astype(BF16)
        x, h2, aff = outproj_norm_router((gla_o, gm_o, lru_o, nat_o), wo, x, g1, norm2_g[l], sc2, sh2, wr, tm)
        gl, il = route(aff, cap_l)
        rows_l = jnp.swapaxes(il + batch_ids * N, 0, 1).reshape(E, B * cap_l)
        idx = rows_l
        gs = jnp.swapaxes(gl, 0, 1).reshape(E, B * cap_l)
        hc2 = None
        if need_ctx:
            gm_c = gmlp_mixer(cgm, gmlp_norm_g[l], gmlp_w_s[l], gmlp_b_s[l], M)
            nat_c = ctx_attention(cnat)
            xc, hc2, caff = outproj_norm_router((gla_c, gm_c, lru_c, nat_c), wo, xc, cg1, norm2_g[l], csc2, csh2,
                                                wr, M)
            gc, ic = route(caff, cap_c)
            rows_c = jnp.swapaxes(ic + batch_ids * M, 0, 1).reshape(E, B * cap_c)
            idx = jnp.concatenate([rows_l, rows_c], axis=1)
            gs = jnp.concatenate([gs, jnp.swapaxes(gc, 0, 1).reshape(E, B * cap_c)], axis=1)
        ys = moe_ffn((idx * SUBLANES).reshape(-1).astype(jnp.int32), h2, hc2, moe_w_gate, moe_w_up, moe_w_down, l,
                     gs[..., None], g2, cg2, B * cap_l, tf=512)
        x = combine_expert_outputs(x.reshape(B * N, D), ys[0], rows_l.reshape(-1).astype(jnp.int32),
                                   None if need_ctx else final_norm_g).reshape(B, N, D)
        if need_ctx:
            xc = xc.reshape(B * M, D).at[rows_c.reshape(-1)].add(ys[1].reshape(-1, D)).reshape(B, M, D)
    return x
```

```python
import functools

import numpy as np
import jax
import jax.numpy as jnp
from jax import lax
from jax.experimental import pallas as pl
from jax.experimental.pallas import tpu as pltpu

D_MODEL = 1024
DEPTH = 4
GRID_W = 64
N_GROUPS = 4
GROUP_W = D_MODEL // N_GROUPS
GLA_HEADS = 4
GLA_DK = GROUP_W // GLA_HEADS
GLA_W = GLA_HEADS * GLA_DK
GLA_GATE_RANK = 16
GLA_TAU = 16.0
GLA_CHUNK = 64
GMLP_GROUPS = 4
GMLP_W = GROUP_W
GMLP_CHUNK = 128
LRU_W = GROUP_W
LRU_BLOCKS = 4
LRU_C = 8.0
CONV_W = 4
NAT_HEADS = 4
NAT_DH = GROUP_W // NAT_HEADS
NAT_W = NAT_HEADS * NAT_DH
NAT_KR_MAX = 8
NAT_KC = 16
N_EXPERTS = 16
EXPERT_FF = 2 * D_MODEL
EC_CAPACITY = 2
ROPE_BASE = 10000.0
EPS = 1e-6

IN_SIZES = (GLA_W, GLA_W, GLA_W, GLA_W, GLA_GATE_RANK, GLA_GATE_RANK,
            GMLP_W, GMLP_W, LRU_W, LRU_W, NAT_W, NAT_W, NAT_W)
IN_COLS = sum(IN_SIZES)

V7X_VMEM_LIMIT_BYTES = 56 * 1024 * 1024
LANES = 128
SUBLANES = 8
F32 = jnp.float32
BF16 = jnp.bfloat16
NEG_BIG = -1e30
GATE_PAD = LANES

NT_DIMS = (((1,), (1,)), ((), ()))
TN_DIMS = (((0,), (0,)), ((), ()))


def _cparams(sem):
    return pltpu.CompilerParams(dimension_semantics=sem, vmem_limit_bytes=V7X_VMEM_LIMIT_BYTES)


def _dot(a, b):
    return jnp.dot(a, b, preferred_element_type=F32)


def _split3(x):
    hi = x.astype(BF16)
    r1 = x - hi.astype(F32)
    mid = r1.astype(BF16)
    lo = (r1 - mid.astype(F32)).astype(BF16)
    return hi, mid, lo


def _head_block_mask(n):
    r = lax.broadcasted_iota(jnp.int32, (n, n), 0) // GLA_DK
    c = lax.broadcasted_iota(jnp.int32, (n, n), 1) // GLA_DK
    return r == c


def _axial_rope(u, cos, sin):
    lane = lax.broadcasted_iota(jnp.int32, u.shape, 1)
    first = (lane % (GLA_DK // 2)) < (GLA_DK // 4)
    partner = jnp.where(first, pltpu.roll(u, GLA_W - GLA_DK // 4, axis=1), pltpu.roll(u, GLA_DK // 4, axis=1))
    return u * cos + partner * sin


def _norm_inproj_kernel(*refs, rope):
    refs = list(refs)
    x_ref, g_ref, sc_ref, sh_ref, w_ref = refs[:5]
    refs = refs[5:]
    if rope:
        cos_ref, sin_ref = refs[:2]
        refs = refs[2:]
    gla_ref, gmlp_ref, lru_ref, nat_ref, gate_ref = refs
    x = x_ref[0]
    y = x * lax.rsqrt(jnp.mean(x * x, axis=-1, keepdims=True) + EPS)
    h = ((y * g_ref[...]) * (1.0 + sc_ref[0]) + sh_ref[0]).astype(BF16)
    o = 0
    for ref in (gla_ref, gmlp_ref, lru_ref, nat_ref, gate_ref):
        w = ref.shape[-1]
        ref[0] = _dot(h, w_ref[:, o:o + w]).astype(ref.dtype)
        o += w
    if rope:
        for c in range(2):
            cols = slice(c * GLA_W, (c + 1) * GLA_W)
            gla_ref[0, :, cols] = _axial_rope(gla_ref[0, :, cols], cos_ref[...], sin_ref[...])


def permute_w_in(w):
    g0 = 4 * GLA_W
    g1 = g0 + 2 * GLA_GATE_RANK
    pad = jnp.zeros((w.shape[0], GATE_PAD - 2 * GLA_GATE_RANK), w.dtype)
    return jnp.concatenate([w[:, :g0], w[:, g1:], w[:, g0:g1], pad], axis=1).astype(BF16)


def norm_inproj(x, g, scale, shift, wp, tm, tables=None):
    B, n, D = x.shape
    C = wp.shape[1]
    widths = (4 * GLA_W, 2 * GMLP_W, 2 * LRU_W, 3 * NAT_W, GATE_PAD)
    dtypes = (F32, F32, F32, BF16, F32)
    row = lambda b, i: (b, i, 0)
    per_b = lambda b, i: (b, 0, 0)
    full = lambda b, i: (0, 0)
    rope_specs = [pl.BlockSpec((tm, GLA_W), lambda b, i: (i, 0))] * 2 if tables is not None else []
    return pl.pallas_call(
        functools.partial(_norm_inproj_kernel, rope=tables is not None),
        grid=(B, n // tm),
        in_specs=[
            pl.BlockSpec((1, tm, D), row),
            pl.BlockSpec((1, D), full),
            pl.BlockSpec((1, 1, D), per_b),
            pl.BlockSpec((1, 1, D), per_b),
            pl.BlockSpec((D, C), full),
        ] + rope_specs,
        out_specs=[pl.BlockSpec((1, tm, w), row) for w in widths],
        out_shape=[jax.ShapeDtypeStruct((B, n, w), dt) for w, dt in zip(widths, dtypes)],
        compiler_params=_cparams(("parallel", "parallel")),
        name="norm_inproj",
    )(x, g.reshape(1, D), scale.reshape(B, 1, D), shift.reshape(B, 1, D), wp, *(tables or ()))


def _gla_kernel(*refs, tm, reverse, combine):
    refs = list(refs)
    q_ref, k_ref, v_ref, gate_ref, wg_ref, bg_ref, tri_ref, ones_ref, s0_ref = refs[:9]
    refs = refs[9:]
    if combine:
        ob_ref, og_ref, ng_ref = refs[:3]
        refs = refs[3:]
    o_ref, sfin_ref, st, obuf = refs
    i = pl.program_id(1)
    nch = tm // GLA_CHUNK

    @pl.when(i == 0)
    def _():
        st[...] = s0_ref[0]

    q = q_ref[0] * (GLA_DK ** -0.5)
    k = k_ref[0]
    a = _dot(gate_ref[0].astype(BF16), wg_ref[...]) + bg_ref[...]
    la = jax.nn.log_sigmoid(a) / GLA_TAU
    pieces = _split3(la)
    cw = tri_ref.shape[0]
    halves = [slice(h * cw, (h + 1) * cw) for h in range(tm // cw)]
    b = jnp.concatenate([sum(_dot(tri_ref[...], p[h]) for p in pieces) for h in halves], axis=0)
    bl = jnp.concatenate([sum(_dot(ones_ref[...], p[h]) for p in pieces) for h in halves], axis=0)
    q_in = (q * jnp.exp(b)).astype(BF16)
    k_in = (k * jnp.exp(-b)).astype(BF16)
    k_end = (k * jnp.exp(bl - b)).astype(BF16)
    dec = jnp.exp(bl)
    vb = v_ref[0].astype(BF16)

    head_mask = _head_block_mask(GLA_W)
    cr = lax.broadcasted_iota(jnp.int32, (GLA_CHUNK, GLA_W), 0)
    cs = lax.broadcasted_iota(jnp.int32, (GLA_CHUNK, GLA_W), 1) % GLA_CHUNK
    causal = (cs >= cr) if reverse else (cs <= cr)
    zero = jnp.zeros((), BF16)
    chunks = range(nch - 1, -1, -1) if reverse else range(nch)
    for c in chunks:
        sl = slice(c * GLA_CHUNK, (c + 1) * GLA_CHUNK)
        kbd_t = jnp.where(head_mask, jnp.concatenate([k_in[sl]] * GLA_HEADS, axis=0), zero)
        vbd = jnp.where(head_mask, jnp.concatenate([vb[sl]] * GLA_HEADS, axis=0), zero)
        att = lax.dot_general(q_in[sl], kbd_t, NT_DIMS, preferred_element_type=F32)
        att = jnp.where(causal, att, 0.0).astype(BF16)
        s_t = st[...]
        o = _dot(att, vbd) + lax.dot_general(q_in[sl], s_t.astype(BF16), NT_DIMS, preferred_element_type=F32)
        upd = lax.dot_general(vb[sl], k_end[sl], TN_DIMS, preferred_element_type=F32)
        st[...] = s_t * dec[c * GLA_CHUNK:c * GLA_CHUNK + 1] + jnp.where(head_mask, upd, 0.0)
        obuf[sl, :] = o

    sfin_ref[0] = st[...]

    if not combine:
        o_ref[0] = obuf[...]
    else:
        o = obuf[...] + ob_ref[0]
        sq_hi, sq_mid, _ = _split3(o * o)
        avg = jnp.where(head_mask, 1.0 / GLA_DK, 0.0).astype(BF16)
        ms = _dot(sq_hi, avg) + _dot(sq_mid, avg)
        og = og_ref[0]
        o_ref[0] = (o * lax.rsqrt(ms + EPS) * ng_ref[...] * (og * jax.nn.sigmoid(og))).astype(o_ref.dtype)


def _chunk_sum_matrices(tm, reverse):
    r = np.arange(tm)[:, None]
    c = np.arange(tm)[None, :]
    same = (r // GLA_CHUNK) == (c // GLA_CHUNK)
    tri = same & ((c >= r) if reverse else (c <= r))
    return jnp.asarray(tri, BF16), jnp.asarray(same, BF16)


def gla_direction(gla, gate, wg, bg, s0, tm, reverse, combine=None):
    B, L, _ = gla.shape
    nt = L // tm
    W = GLA_W
    blk = (lambda i: nt - 1 - i) if reverse else (lambda i: i)
    col = lambda j: (lambda b, i: (b, blk(i), j))
    full = lambda b, i: (0, 0)
    cw = min(tm, 256)
    tri, ones = _chunk_sum_matrices(cw, reverse)
    args = [gla, gla, gla, gate, wg, bg, tri, ones, s0]
    specs = [
        pl.BlockSpec((1, tm, W), col(0)), pl.BlockSpec((1, tm, W), col(1)), pl.BlockSpec((1, tm, W), col(2)),
        pl.BlockSpec((1, tm, GATE_PAD), col(0)),
        pl.BlockSpec((GATE_PAD, W), full), pl.BlockSpec((1, W), full),
        pl.BlockSpec((cw, cw), full), pl.BlockSpec((cw, cw), full),
        pl.BlockSpec((1, W, W), lambda b, i: (b, 0, 0)),
    ]
    if combine is not None:
        ob, norm_g = combine
        args += [ob, gla, norm_g]
        specs += [pl.BlockSpec((1, tm, W), col(0)), pl.BlockSpec((1, tm, W), col(3)), pl.BlockSpec((1, W), full)]
    return pl.pallas_call(
        functools.partial(_gla_kernel, tm=tm, reverse=reverse, combine=combine is not None),
        grid=(B, nt),
        in_specs=specs,
        out_specs=[pl.BlockSpec((1, tm, W), col(0)), pl.BlockSpec((1, W, W), lambda b, i: (b, 0, 0))],
        out_shape=[jax.ShapeDtypeStruct((B, L, W), F32 if combine is None else BF16),
                   jax.ShapeDtypeStruct((B, W, W), F32)],
        scratch_shapes=[pltpu.VMEM((W, W), F32), pltpu.VMEM((tm, W), F32)],
        compiler_params=_cparams(("parallel", "arbitrary")),
        name="gla_bwd" if reverse else "gla_fwd",
    )(*args)


def rope_tables(n):
    quarter = GLA_DK // 4
    pos = jnp.arange(n)
    inv = ROPE_BASE ** (-jnp.arange(quarter, dtype=F32) / quarter)
    row = (pos // GRID_W).astype(F32)
    colp = (pos % GRID_W).astype(F32)
    ar = row[:, None] * inv[None, :]
    ac = colp[:, None] * inv[None, :]
    cos = jnp.concatenate([jnp.cos(ar), jnp.cos(ar), jnp.cos(ac), jnp.cos(ac)], axis=1)
    sin = jnp.concatenate([-jnp.sin(ar), jnp.sin(ar), -jnp.sin(ac), jnp.sin(ac)], axis=1)
    return jnp.tile(cos, (1, GLA_HEADS)), jnp.tile(sin, (1, GLA_HEADS))


def gla_mixer(gla, gate, cgla, cgate, w_gate, b_gate, norm_g, need_ctx, tm):
    B, M, _ = cgla.shape
    zeros = jnp.zeros((B, GLA_W, GLA_W), F32)
    ng = jnp.tile(norm_g, GLA_HEADS).reshape(1, GLA_W)
    wgs, bgs = [], []
    for d in range(2):
        wg = jnp.pad(w_gate[d], ((d * GLA_GATE_RANK, GATE_PAD - (d + 1) * GLA_GATE_RANK), (0, 0)))
        wgs.append(wg.astype(BF16))
        bgs.append(b_gate[d].reshape(1, GLA_W))
    ocb, sb = gla_direction(cgla, cgate, wgs[1], bgs[1], zeros, M, True)
    ob, _ = gla_direction(gla, gate, wgs[1], bgs[1], sb, tm, True)
    if need_ctx:
        oc, sf = gla_direction(cgla, cgate, wgs[0], bgs[0], zeros, M, False, combine=(ocb, ng))
    else:
        oc, sf = gla_direction(cgla, cgate, wgs[0], bgs[0], zeros, M, False)
    o, _ = gla_direction(gla, gate, wgs[0], bgs[0], sf, tm, False, combine=(ob, ng))
    return o, oc


def _gmlp_kernel(u_ref, v_ref, g_ref, w_ref, b_ref, o_ref, *, tm):
    v = v_ref[0]
    vn = (v * lax.rsqrt(jnp.mean(v * v, axis=-1, keepdims=True) + EPS) * g_ref[...]).astype(BF16)
    grp = lax.broadcasted_iota(jnp.int32, (GMLP_CHUNK, GMLP_W), 1) // (GMLP_W // GMLP_GROUPS)
    zero = jnp.zeros((), BF16)
    for c in range(tm // GMLP_CHUNK):
        sl = slice(c * GMLP_CHUNK, (c + 1) * GMLP_CHUNK)
        mixed = b_ref[...]
        for g in range(GMLP_GROUPS):
            mixed = mixed + _dot(w_ref[g], jnp.where(grp == g, vn[sl], zero))
        o_ref[0, sl, :] = (u_ref[0, sl, :] * mixed).astype(o_ref.dtype)


def gmlp_mixer(gm, norm_g, w_s, b_s, tm):
    B, L, _ = gm.shape
    W = GMLP_W
    bias = jnp.repeat(b_s.T, W // GMLP_GROUPS, axis=1)
    return pl.pallas_call(
        functools.partial(_gmlp_kernel, tm=tm),
        grid=(B, L // tm),
        in_specs=[
            pl.BlockSpec((1, tm, W), lambda b, i: (b, i, 0)),
            pl.BlockSpec((1, tm, W), lambda b, i: (b, i, 1)),
            pl.BlockSpec((1, W), lambda b, i: (0, 0)),
            pl.BlockSpec((GMLP_GROUPS, GMLP_CHUNK, GMLP_CHUNK), lambda b, i: (0, 0, 0)),
            pl.BlockSpec((GMLP_CHUNK, W), lambda b, i: (0, 0)),
        ],
        out_specs=pl.BlockSpec((1, tm, W), lambda b, i: (b, i, 0)),
        out_shape=jax.ShapeDtypeStruct((B, L, W), BF16),
        compiler_params=_cparams(("parallel", "parallel")),
        name="gmlp",
    )(gm, gm, norm_g.reshape(1, W), w_s.astype(BF16), bias)


LRU_UNROLL = 8


def _lru_kernel(*refs, tm, reverse, combine):
    refs = list(refs)
    x_ref, xp_ref, xn_ref, cw_ref, cb_ref, wr_ref, br_ref, wi_ref, bi_ref, ncs_ref, h0_ref = refs[:11]
    refs = refs[11:]
    if combine:
        hb_ref, ly_ref = refs[:2]
        refs = refs[2:]
    o_ref, ext, a_s, b_s, carry = refs
    W = LRU_W
    i = pl.program_id(1)
    nt = pl.num_programs(1)
    t = (nt - 1 - i) if reverse else i
    H = SUBLANES

    ext[0:H] = jnp.where(t > 0, xp_ref[0], 0.0)
    ext[H:H + tm] = x_ref[0]
    ext[H + tm:2 * H + tm] = jnp.where(t < nt - 1, xn_ref[0], 0.0)
    xc = cb_ref[...]
    for tap in range(CONV_W):
        xc = xc + cw_ref[tap:tap + 1, :] * ext[H - CONV_W // 2 + tap:H - CONV_W // 2 + tap + tm]
    xcb = xc.astype(BF16)
    r = jax.nn.sigmoid(_dot(xcb, wr_ref[...]) + br_ref[...])
    ig = jax.nn.sigmoid(_dot(xcb, wi_ref[...]) + bi_ref[...])
    log_a = ncs_ref[...] * r
    a = jnp.exp(log_a)
    b = jnp.sqrt(-jnp.tanh(log_a) * (a * a + 1.0)) * (ig * xc)

    rowi = lax.broadcasted_iota(jnp.int32, (tm, W), 0) % H
    for s in (1, 2, 4):
        if reverse:
            ok = rowi < H - s
            a_sh = pltpu.roll(a, tm - s, axis=0)
            b_sh = pltpu.roll(b, tm - s, axis=0)
        else:
            ok = rowi >= s
            a_sh = pltpu.roll(a, s, axis=0)
            b_sh = pltpu.roll(b, s, axis=0)
        b = a * jnp.where(ok, b_sh, 0.0) + b
        a = a * jnp.where(ok, a_sh, 1.0)
    a_s[...] = a
    b_s[...] = b

    @pl.when(i == 0)
    def _():
        carry[...] = jnp.broadcast_to(h0_ref[0], (H, W))

    ng = tm // H

    def body(j, h):
        for u in range(LRU_UNROLL):
            g = j * LRU_UNROLL + u
            g = (ng - 1 - g) if reverse else g
            rows = pl.ds(pl.multiple_of(g * H, H), H)
            hg = b_s[rows, :] + a_s[rows, :] * h
            b_s[rows, :] = hg
            h = jnp.broadcast_to(hg[0:1] if reverse else hg[H - 1:H], (H, W))
        return h

    carry[...] = lax.fori_loop(0, ng // LRU_UNROLL, body, carry[...])

    if combine:
        o_ref[0] = ((b_s[...] + hb_ref[0]) * jax.nn.gelu(ly_ref[0])).astype(o_ref.dtype)
    else:
        o_ref[0] = b_s[...]


def lru_direction(lru, conv_w, conv_b, wr, br, wi, bi, ncs, h0, tm, reverse, hb=None):
    B, L, _ = lru.shape
    nt = L // tm
    W = LRU_W
    H = SUBLANES
    nh = L // H
    per = tm // H
    blk = (lambda i: nt - 1 - i) if reverse else (lambda i: i)
    full = lambda b, i: (0, 0)
    args = [lru, lru, lru, conv_w, conv_b, wr, br, wi, bi, ncs, h0]
    specs = [
        pl.BlockSpec((1, tm, W), lambda b, i: (b, blk(i), 0)),
        pl.BlockSpec((1, H, W), lambda b, i: (b, jnp.maximum(blk(i) * per - 1, 0), 0)),
        pl.BlockSpec((1, H, W), lambda b, i: (b, jnp.minimum((blk(i) + 1) * per, nh - 1), 0)),
        pl.BlockSpec((CONV_W, W), full), pl.BlockSpec((1, W), full),
        pl.BlockSpec((W, W), full), pl.BlockSpec((1, W), full),
        pl.BlockSpec((W, W), full), pl.BlockSpec((1, W), full),
        pl.BlockSpec((1, W), full),
        pl.BlockSpec((1, 1, W), lambda b, i: (b, 0, 0)),
    ]
    if hb is not None:
        args += [hb, lru]
        specs += [pl.BlockSpec((1, tm, W), lambda b, i: (b, blk(i), 0)),
                  pl.BlockSpec((1, tm, W), lambda b, i: (b, blk(i), 1))]
    return pl.pallas_call(
        functools.partial(_lru_kernel, tm=tm, reverse=reverse, combine=hb is not None),
        grid=(B, nt),
        in_specs=specs,
        out_specs=pl.BlockSpec((1, tm, W), lambda b, i: (b, blk(i), 0)),
        out_shape=jax.ShapeDtypeStruct((B, L, W), F32 if hb is None else BF16),
        scratch_shapes=[pltpu.VMEM((tm + 2 * H, W), F32), pltpu.VMEM((tm, W), F32), pltpu.VMEM((tm, W), F32),
                        pltpu.VMEM((H, W), F32)],
        compiler_params=_cparams(("parallel", "arbitrary")),
        name="lru_bwd" if reverse else "lru_fwd",
    )(*args)


def _block_diag(w):
    G, n, _ = w.shape
    same = jnp.eye(G, dtype=w.dtype)
    return (w[:, :, None, :] * same[:, None, :, None]).reshape(G * n, G * n)


def lru_mixer(lru, clru, conv_w, conv_b, w_r, b_r, w_i, b_i, lam, need_ctx, tm):
    B, M, _ = clru.shape
    W = LRU_W
    cb = conv_b.reshape(1, W)
    ncs = -LRU_C * jax.nn.softplus(-lam.astype(F32))
    p = [(_block_diag(w_r[d]).astype(BF16), b_r[d].reshape(1, W), _block_diag(w_i[d]).astype(BF16),
          b_i[d].reshape(1, W), ncs[d].reshape(1, W)) for d in range(2)]
    zeros = jnp.zeros((B, 1, W), F32)
    hcb = lru_direction(clru, conv_w, cb, *p[1], zeros, M, True)
    hb = lru_direction(lru, conv_w, cb, *p[1], hcb[:, 0:1], tm, True)
    hcf = lru_direction(clru, conv_w, cb, *p[0], zeros, M, False)
    out = lru_direction(lru, conv_w, cb, *p[0], hcf[:, M - 1:M], tm, False, hb=hb)
    out_c = None
    if need_ctx:
        out_c = lru_direction(clru, conv_w, cb, *p[0], zeros, M, False, hb=hcb)
    return out, out_c


NAT_QROWS = 4
NAT_KROWS = 3 * NAT_QROWS
NAT_TQ = NAT_QROWS * GRID_W
NAT_TK = NAT_KROWS * GRID_W


def _softmax_pv(s_loc, s_ctx, v_loc, v_ctx):
    m = jnp.maximum(jnp.max(s_loc, axis=-1, keepdims=True), jnp.max(s_ctx, axis=-1, keepdims=True))
    e_loc = jnp.exp(s_loc - m)
    e_ctx = jnp.exp(s_ctx - m)
    den = jnp.sum(e_loc, axis=-1, keepdims=True) + jnp.sum(e_ctx, axis=-1, keepdims=True)
    o = _dot(e_loc.astype(BF16), v_loc) + _dot(e_ctx.astype(BF16), v_ctx)
    return o / den


def _nat_kernel(q_ref, kp_ref, kc_ref, kn_ref, vp_ref, vc_ref, vn_ref, ck_ref, cv_ref, bias_ref, o_ref):
    q = q_ref[0]
    ck = ck_ref[0]
    cv = cv_ref[0]
    scale = NAT_DH ** -0.5
    outs = []
    for h in range(NAT_HEADS):
        hs = slice(h * NAT_DH, (h + 1) * NAT_DH)
        qh = q[:, hs]
        s_loc = jnp.concatenate(
            [lax.dot_general(qh, k_ref[0][:, hs], NT_DIMS, preferred_element_type=F32)
             for k_ref in (kp_ref, kc_ref, kn_ref)], axis=-1) * scale + bias_ref[0, h]
        s_ctx = lax.dot_general(qh, ck[:, hs], NT_DIMS, preferred_element_type=F32) * scale
        v_loc = jnp.concatenate([v_ref[0][:, hs] for v_ref in (vp_ref, vc_ref, vn_ref)], axis=0)
        outs.append(_softmax_pv(s_loc, s_ctx, v_loc, cv[:, hs]))
    o_ref[0] = jnp.concatenate(outs, axis=-1).astype(o_ref.dtype)


def nat_bias_tables(rpb, rows):
    kr = NAT_KR_MAX
    qc = np.arange(GRID_W)
    kcol = np.arange(GRID_W)
    cs = np.clip(qc - NAT_KC // 2, 0, GRID_W - NAT_KC)
    valid_c = (kcol[None, :] >= cs[:, None]) & (kcol[None, :] < cs[:, None] + NAT_KC)
    edge = GRID_W - NAT_KC
    padded = jnp.pad(rpb.astype(F32), ((0, 0), (0, 0), (edge, edge)), mode="edge")
    by_col = jnp.stack([padded[:, :, GRID_W - 1 - q:2 * GRID_W - 1 - q] for q in range(GRID_W)], axis=1)
    by_col = jnp.pad(by_col, ((0, 0), (0, 0), (NAT_KROWS, NAT_KROWS), (0, 0)))
    tables = []
    for r0, ks in ((0, 0), (NAT_QROWS, NAT_QROWS - kr // 2), (rows - NAT_QROWS, rows - NAT_KROWS)):
        r = r0 + np.arange(NAT_QROWS)
        krow = ks + np.arange(NAT_KROWS)
        rs = np.clip(r - kr // 2, 0, rows - kr)
        valid_r = (krow[None, :] >= rs[:, None]) & (krow[None, :] < rs[:, None] + kr)
        starts = ks - r + NAT_KR_MAX - 1 + NAT_KROWS
        bias = jnp.stack([by_col[:, :, int(s):int(s) + NAT_KROWS] for s in starts], axis=1)
        mask = valid_r[:, None, :, None] & valid_c[None, :, None, :]
        bias = jnp.where(jnp.asarray(mask)[None], bias, NEG_BIG)
        tables.append(bias.reshape(rpb.shape[0], NAT_TQ, NAT_TK))
    return jnp.stack(tables)


def nat_mixer(nat, cnat, bias, layer):
    B, N, _ = nat.shape
    M = cnat.shape[1]
    W = NAT_W
    nb = N // NAT_TQ
    T = NAT_TQ
    centre = lambda j: jnp.clip(j, 1, nb - 2)
    near = lambda c, d: (lambda b, j: (b, centre(j) + d, c))
    variant = lambda b, j: (jnp.where(j == 0, 0, jnp.where(j == nb - 1, 2, 1)), layer, 0, 0)
    return pl.pallas_call(
        _nat_kernel,
        grid=(B, nb),
        in_specs=[
            pl.BlockSpec((1, T, W), lambda b, j: (b, j, 0)),
            pl.BlockSpec((1, T, W), near(1, -1)), pl.BlockSpec((1, T, W), near(1, 0)), pl.BlockSpec((1, T, W), near(1, 1)),
            pl.BlockSpec((1, T, W), near(2, -1)), pl.BlockSpec((1, T, W), near(2, 0)), pl.BlockSpec((1, T, W), near(2, 1)),
            pl.BlockSpec((1, M, W), lambda b, j: (b, 0, 1)),
            pl.BlockSpec((1, M, W), lambda b, j: (b, 0, 2)),
            pl.BlockSpec((1, NAT_HEADS, NAT_TQ, NAT_TK), variant),
        ],
        out_specs=pl.BlockSpec((1, T, W), lambda b, j: (b, j, 0)),
        out_shape=jax.ShapeDtypeStruct((B, N, W), BF16),
        compiler_params=_cparams(("parallel", "arbitrary")),
        name="nat",
    )(nat, nat, nat, nat, nat, nat, nat, cnat, cnat, bias)


def _ctx_attn_kernel(q_ref, k_ref, v_ref, o_ref):
    q = q_ref[0]
    k = k_ref[0]
    v = v_ref[0]
    scale = NAT_DH ** -0.5
    outs = []
    for h in range(NAT_HEADS):
        hs = slice(h * NAT_DH, (h + 1) * NAT_DH)
        s = lax.dot_general(q[:, hs], k[:, hs], NT_DIMS, preferred_element_type=F32) * scale
        e = jnp.exp(s - jnp.max(s, axis=-1, keepdims=True))
        outs.append(_dot(e.astype(BF16), v[:, hs]) / jnp.sum(e, axis=-1, keepdims=True))
    o_ref[0] = jnp.concatenate(outs, axis=-1).astype(o_ref.dtype)


def ctx_attention(cnat):
    B, M, _ = cnat.shape
    W = NAT_W
    return pl.pallas_call(
        _ctx_attn_kernel,
        grid=(B,),
        in_specs=[pl.BlockSpec((1, M, W), lambda b, c=c: (b, 0, c)) for c in range(3)],
        out_specs=pl.BlockSpec((1, M, W), lambda b: (b, 0, 0)),
        out_shape=jax.ShapeDtypeStruct((B, M, W), BF16),
        compiler_params=_cparams(("parallel",)),
        name="ctx_attn",
    )(cnat, cnat, cnat)


def _outproj_kernel(m0_ref, m1_ref, m2_ref, m3_ref, w_ref, x_ref, g1_ref, n2_ref, sc_ref, sh_ref, wr_ref,
                    xo_ref, h2_ref, aff_ref):
    y = 0.0
    for gi, m_ref in enumerate((m0_ref, m1_ref, m2_ref, m3_ref)):
        y = y + _dot(m_ref[0].astype(BF16), w_ref[gi * GROUP_W:(gi + 1) * GROUP_W, :])
    x = x_ref[0] + g1_ref[0] * y
    xo_ref[0] = x
    xn = x * lax.rsqrt(jnp.mean(x * x, axis=-1, keepdims=True) + EPS)
    h2 = (xn * n2_ref[...]) * (1.0 + sc_ref[0]) + sh_ref[0]
    tm = h2.shape[0]
    for j in range(SUBLANES):
        h2_ref[pl.ds(j, tm, stride=SUBLANES), :] = h2[:, j * LANES:(j + 1) * LANES]
    logits = _dot(h2.astype(BF16), wr_ref[...])
    e = jnp.exp(logits - jnp.max(logits, axis=-1, keepdims=True))
    aff_ref[0] = e / jnp.sum(e, axis=-1, keepdims=True)


def outproj_norm_router(parts, w_out, x, g1, n2, scale2, shift2, w_router, tm):
    B, n, D = x.shape
    E = w_router.shape[1]
    row = lambda b, i: (b, i, 0)
    per_b = lambda b, i: (b, 0, 0)
    full = lambda b, i: (0, 0)
    return pl.pallas_call(
        _outproj_kernel,
        grid=(B, n // tm),
        in_specs=[pl.BlockSpec((1, tm, GROUP_W), row)] * N_GROUPS + [
            pl.BlockSpec((D, D), full),
            pl.BlockSpec((1, tm, D), row),
            pl.BlockSpec((1, 1, D), per_b),
            pl.BlockSpec((1, D), full),
            pl.BlockSpec((1, 1, D), per_b),
            pl.BlockSpec((1, 1, D), per_b),
            pl.BlockSpec((D, E), full),
        ],
        out_specs=[
            pl.BlockSpec((1, tm, D), row),
            pl.BlockSpec((tm * SUBLANES, LANES), lambda b, i: (b * (n // tm) + i, 0)),
            pl.BlockSpec((1, tm, E), row),
        ],
        out_shape=[
            jax.ShapeDtypeStruct((B, n, D), F32),
            jax.ShapeDtypeStruct((B * n * SUBLANES, LANES), F32),
            jax.ShapeDtypeStruct((B, n, E), F32),
        ],
        compiler_params=_cparams(("parallel", "parallel")),
        name="outproj_norm_router",
    )(*parts, w_out, x, g1.reshape(B, 1, D), n2.reshape(1, D), scale2.reshape(B, 1, D), shift2.reshape(B, 1, D),
      w_router)


MOE_ROW_CHUNK = 512
MOE_ISSUE_UNROLL = 8


def _moe_ffn_kernel(*refs, n_lat, n_ctx, n_batch):
    refs = list(refs)
    idx_ref, h_hbm = refs[:2]
    refs = refs[2:]
    if n_ctx:
        hc_hbm = refs.pop(0)
    wg_ref, wu_ref, wd_ref, gate_ref, g2_ref = refs[:5]
    refs = refs[5:]
    if n_ctx:
        cg2_ref = refs.pop(0)
    yl_hbm = refs.pop(0)
    if n_ctx:
        yc_ref = refs.pop(0)
    xf, xb, acc, ytiles, sem, out_sem = refs
    e = pl.program_id(0)
    f = pl.program_id(1)
    n_exp = pl.num_programs(0)
    last = pl.num_programs(1) - 1
    T = n_lat + n_ctx

    R = SUBLANES

    def start_rows(src, expert, first, count):
        def body(i, carry):
            for u in range(MOE_ISSUE_UNROLL):
                s = first + i * MOE_ISSUE_UNROLL + u
                r = pl.multiple_of(idx_ref[expert * T + s], R)
                pltpu.make_async_copy(src.at[pl.ds(r, R)], xf.at[pl.ds(pl.multiple_of(s * R, R), R)],
                                      sem.at[0]).start()
            return carry

        lax.fori_loop(0, count // MOE_ISSUE_UNROLL, body, 0)

    def start_gather(expert):
        start_rows(h_hbm, expert, 0, n_lat)
        if n_ctx:
            start_rows(hc_hbm, expert, n_lat, n_ctx)

    @pl.when(f == 0)
    def _():
        @pl.when(e == 0)
        def _():
            start_gather(e)

        pltpu.make_async_copy(h_hbm.at[pl.ds(0, T * R)], xf, sem.at[0]).wait()
        for j in range(R):
            xb[:, j * LANES:(j + 1) * LANES] = xf[pl.ds(j, T, stride=R), :].astype(BF16)

        @pl.when(e + 1 < n_exp)
        def _():
            start_gather(e + 1)

        acc[...] = jnp.zeros_like(acc)
        if n_ctx:
            yc_ref[...] = jnp.zeros_like(yc_ref)

    wg = wg_ref[0, 0].astype(BF16)
    wu = wu_ref[0, 0].astype(BF16)
    wd = wd_ref[0, 0].astype(BF16)

    def ffn(x):
        hg = _dot(x, wg)
        hu = _dot(x, wu)
        return _dot((hg * jax.nn.sigmoid(hg) * hu).astype(BF16), wd)

    def lat_rows(first, count, j):
        return pl.ds(first * R + j, count, stride=R)

    for i in range(n_lat // MOE_ROW_CHUNK):
        rows = slice(i * MOE_ROW_CHUNK, (i + 1) * MOE_ROW_CHUNK)
        acc[rows, :] += ffn(xb[rows, :])
    if n_ctx:
        yc_ref[0] += ffn(xb[n_lat:T, :])

    def out_copy(expert):
        dst = yl_hbm.at[pl.ds(pl.multiple_of(expert * (n_lat * R), n_lat * R), n_lat * R)]
        return pltpu.make_async_copy(ytiles, dst, out_sem.at[0])

    @pl.when(f == last)
    def _():
        @pl.when(e > 0)
        def _():
            out_copy(e - 1).wait()

        per_l = n_lat // n_batch
        for b in range(n_batch):
            rows = slice(b * per_l, (b + 1) * per_l)
            gate = gate_ref[0, rows, :]
            for j in range(R):
                lanes = slice(j * LANES, (j + 1) * LANES)
                ytiles[lat_rows(b * per_l, per_l, j), :] = acc[rows, lanes] * gate * g2_ref[b:b + 1, lanes]
        out_copy(e).start()

        @pl.when(e == n_exp - 1)
        def _():
            out_copy(e).wait()

        if n_ctx:
            per_c = n_ctx // n_batch
            for b in range(n_batch):
                rows = slice(b * per_c, (b + 1) * per_c)
                yc_ref[0, rows, :] = (yc_ref[0, rows, :] * gate_ref[0, n_lat + b * per_c:n_lat + (b + 1) * per_c, :]
                                      * cg2_ref[b:b + 1, :])


def moe_ffn(idx, h2, hc2, w_gate, w_up, w_down, layer, gates, g2, cg2, n_lat, tf):
    E, T, _ = gates.shape
    n_ctx = T - n_lat
    D = SUBLANES * LANES
    F = w_gate.shape[3]
    B = g2.shape[0]
    any_spec = pl.BlockSpec(memory_space=pl.ANY)
    per_e = lambda e, f, idx_ref: (e, 0, 0)
    full = lambda e, f, idx_ref: (0, 0)
    args = [h2] + ([hc2] if n_ctx else []) + [w_gate, w_up, w_down, gates, g2] + ([cg2] if n_ctx else [])
    specs = [any_spec] + ([any_spec] if n_ctx else []) + [
        pl.BlockSpec((1, 1, D, tf), lambda e, f, idx_ref: (layer, e, 0, f)),
        pl.BlockSpec((1, 1, D, tf), lambda e, f, idx_ref: (layer, e, 0, f)),
        pl.BlockSpec((1, 1, tf, D), lambda e, f, idx_ref: (layer, e, f, 0)),
        pl.BlockSpec((1, T, 1), per_e),
        pl.BlockSpec((B, D), full),
    ] + ([pl.BlockSpec((B, D), full)] if n_ctx else [])
    out_specs = [any_spec] + ([pl.BlockSpec((1, n_ctx, D), per_e)] if n_ctx else [])
    out_shape = [jax.ShapeDtypeStruct((E * n_lat * SUBLANES, LANES), F32)] + (
        [jax.ShapeDtypeStruct((E, n_ctx, D), F32)] if n_ctx else [])
    return pl.pallas_call(
        functools.partial(_moe_ffn_kernel, n_lat=n_lat, n_ctx=n_ctx, n_batch=B),
        grid_spec=pltpu.PrefetchScalarGridSpec(
            num_scalar_prefetch=1,
            grid=(E, F // tf),
            in_specs=specs,
            out_specs=out_specs,
            scratch_shapes=[pltpu.VMEM((T * SUBLANES, LANES), F32), pltpu.VMEM((T, D), BF16),
                            pltpu.VMEM((n_lat, D), F32), pltpu.VMEM((n_lat * SUBLANES, LANES), F32),
                            pltpu.SemaphoreType.DMA((1,)), pltpu.SemaphoreType.DMA((1,))],
        ),
        out_shape=out_shape,
        compiler_params=_cparams(("arbitrary", "arbitrary")),
        name="moe_ffn",
    )(idx, *args)


COMBINE_TILE = 256
COMBINE_ISSUE_UNROLL = 8


def _combine_kernel(src_ref, dst_ref, bnd_ref, kmax_ref, y_hbm, x_ref, *refs, tm, norm):
    if norm:
        g_ref, o_ref, planes, sem = refs
    else:
        o_ref, planes, sem = refs
    i = pl.program_id(0)
    nt = pl.num_programs(0)
    R = SUBLANES
    U = COMBINE_ISSUE_UNROLL
    slab = tm * R

    def groups(t):
        return (bnd_ref[t + 1] - bnd_ref[t] + U - 1) // U

    def prepare(t, buf):
        def zero(k, carry):
            planes[buf, pl.ds(pl.multiple_of(k * slab, slab), slab), :] = jnp.zeros((slab, LANES), F32)
            return carry

        lax.fori_loop(0, jnp.maximum(kmax_ref[t], 1), zero, 0)
        first = bnd_ref[t]
        final = bnd_ref[t + 1] - 1

        def issue(g, carry):
            for u in range(U):
                a = first + g * U + u
                live = a <= final
                a = jnp.minimum(a, final)
                s = pl.multiple_of(src_ref[a], R)
                d = pl.multiple_of(jnp.where(live, dst_ref[a], N_EXPERTS * slab + u * R), R)
                pltpu.make_async_copy(y_hbm.at[pl.ds(s, R)], planes.at[buf, pl.ds(d, R)], sem.at[buf]).start()
            return carry

        lax.fori_loop(0, groups(t), issue, 0)

    @pl.when(i == 0)
    def _():
        prepare(0, 0)

    @pl.when(i + 1 < nt)
    def _():
        prepare(i + 1, (i + 1) % 2)

    buf = i % 2
    n_groups = groups(i)
    p = 1
    while p * U <= tm * N_EXPERTS:
        @pl.when((n_groups & p) != 0)
        def _(p=p):
            pltpu.make_async_copy(y_hbm.at[pl.ds(0, p * U * R)], planes.at[buf, pl.ds(0, p * U * R)],
                                  sem.at[buf]).wait()
        p *= 2

    kmax = kmax_ref[i]

    def add(k, carry):
        planes[buf, 0:slab, :] += planes[buf, pl.ds(pl.multiple_of(k * slab, slab), slab), :]
        return carry

    lax.fori_loop(1, kmax, add, 0)

    sumsq = jnp.zeros((tm, 1), F32)
    for j in range(R):
        lanes = slice(j * LANES, (j + 1) * LANES)
        y = x_ref[:, lanes] + planes[buf, pl.ds(j, tm, stride=R), :]
        o_ref[:, lanes] = y
        if norm:
            sumsq = sumsq + jnp.sum(y * y, axis=-1, keepdims=True)
    if norm:
        o_ref[...] = o_ref[...] * lax.rsqrt(sumsq / (R * LANES) + EPS) * g_ref[...]


def combine_expert_outputs(x, y_tiles, token_rows, norm_g=None):
    n, D = x.shape
    A = token_rows.shape[0]
    tm = COMBINE_TILE
    nt = n // tm
    tok, src = lax.sort_key_val(token_rows, jnp.arange(A, dtype=jnp.int32))
    pos = jnp.arange(A, dtype=jnp.int32)
    is_start = jnp.concatenate([jnp.ones((1,), bool), tok[1:] != tok[:-1]])
    rank = pos - lax.cummax(jnp.where(is_start, pos, 0))
    tile = tok // tm
    dst = (rank * tm + tok % tm) * SUBLANES
    edges = jnp.arange(nt + 1, dtype=jnp.int32) * tm
    bounds = jnp.sum(tok[None, :] < edges[:, None], axis=1).astype(jnp.int32)
    kmax = jnp.max(jnp.where(tile[None, :] == jnp.arange(nt, dtype=jnp.int32)[:, None], rank[None, :] + 1, 0),
                   axis=1).astype(jnp.int32)
    row = lambda i, *_: (i, 0)
    norm = norm_g is not None
    extra = [norm_g.reshape(1, D)] if norm else []
    return pl.pallas_call(
        functools.partial(_combine_kernel, tm=tm, norm=norm),
        grid_spec=pltpu.PrefetchScalarGridSpec(
            num_scalar_prefetch=4,
            grid=(nt,),
            in_specs=[pl.BlockSpec(memory_space=pl.ANY), pl.BlockSpec((tm, D), row)] + (
                [pl.BlockSpec((1, D), lambda i, *_: (0, 0))] if norm else []),
            out_specs=pl.BlockSpec((tm, D), row),
            scratch_shapes=[
                pltpu.VMEM((2, (N_EXPERTS * tm + COMBINE_ISSUE_UNROLL) * SUBLANES, LANES), F32),
                pltpu.SemaphoreType.DMA((2,))],
        ),
        out_shape=jax.ShapeDtypeStruct((n, D), F32),
        compiler_params=_cparams(("arbitrary",)),
        name="combine",
    )(src * SUBLANES, dst, bounds, kmax, y_tiles, x, *extra)


def route(aff, cap):
    return lax.top_k(jnp.swapaxes(aff, 1, 2), cap)


TOKEN_TILE = 512


def kernel(x, c, ctx, c_ctx, w_ada, b_ada, norm1_g, norm2_g, w_in, w_out, gla_w_gate, gla_b_gate, gla_norm_g,
           gmlp_norm_g, gmlp_w_s, gmlp_b_s, lru_conv_w, lru_conv_b, lru_w_r, lru_b_r, lru_w_i, lru_b_i,
           lru_lambda, nat_rpb, moe_w_router, moe_w_gate, moe_w_up, moe_w_down, final_norm_g):
    B, N, D = x.shape
    M = ctx.shape[1]
    E = N_EXPERTS
    tm = TOKEN_TILE
    cap_l = max(1, EC_CAPACITY * N // E)
    cap_c = max(1, EC_CAPACITY * M // E)
    sc = jax.nn.silu(c)
    scc = jax.nn.silu(c_ctx)
    xc = ctx
    batch_ids = jnp.arange(B, dtype=jnp.int32)[:, None, None]
    tables = rope_tables(N)
    mods = jnp.einsum('bd,ldk->lbk', jnp.concatenate([sc, scc[None]], axis=0), w_ada) + b_ada[:, None, :]
    nat_bias = nat_bias_tables(nat_rpb.reshape(DEPTH * NAT_HEADS, *nat_rpb.shape[2:]), N // GRID_W)
    for l in range(DEPTH):
        need_ctx = l < DEPTH - 1
        mod = mods[l, :B]
        mod_c = jnp.broadcast_to(mods[l, B:], (B, 6 * D))
        sh1, sc1, g1, sh2, sc2, g2 = jnp.split(mod, 6, axis=-1)
        csh1, csc1, cg1, csh2, csc2, cg2 = jnp.split(mod_c, 6, axis=-1)

        wp = permute_w_in(w_in[l])
        gla, gm, lru, nat, gate = norm_inproj(x, norm1_g[l], sc1, sh1, wp, tm, tables)
        cgla, cgm, clru, cnat, cgate = norm_inproj(xc, norm1_g[l], csc1, csh1, wp, M)

        gla_o, gla_c = gla_mixer(gla, gate, cgla, cgate, gla_w_gate[l], gla_b_gate[l], gla_norm_g[l], need_ctx, tm)
        gm_o = gmlp_mixer(gm, gmlp_norm_g[l], gmlp_w_s[l], gmlp_b_s[l], tm)
        lru_o, lru_c = lru_mixer(lru, clru, lru_conv_w[l], lru_conv_b[l], lru_w_r[l], lru_b_r[l], lru_w_i[l],
                                 lru_b_i[l], lru_lambda[l], need_ctx, tm)
        nat_o = nat_mixer(nat, cnat, nat_bias, l)

        wo = w_out[l].astype(BF16)
        wr = moe_w_router[l].astype(BF16)
        x, h2, aff = outproj_norm_router((gla_o, gm_o, lru_o, nat_o), wo, x, g1, norm2_g[l], sc2, sh2, wr, tm)
        gl, il = route(aff, cap_l)
        rows_l = jnp.swapaxes(il + batch_ids * N, 0, 1).reshape(E, B * cap_l)
        idx = rows_l
        gs = jnp.swapaxes(gl, 0, 1).reshape(E, B * cap_l)
        hc2 = None
        if need_ctx:
            gm_c = gmlp_mixer(cgm, gmlp_norm_g[l], gmlp_w_s[l], gmlp_b_s[l], M)
            nat_c = ctx_attention(cnat)
            xc, hc2, caff = outproj_norm_router((gla_c, gm_c, lru_c, nat_c), wo, xc, cg1, norm2_g[l], csc2, csh2,
                                                wr, M)
            gc, ic = route(caff, cap_c)
            rows_c = jnp.swapaxes(ic + batch_ids * M, 0, 1).reshape(E, B * cap_c)
            idx = jnp.concatenate([rows_l, rows_c], axis=1)
            gs = jnp.concatenate([gs, jnp.swapaxes(gc, 0, 1).reshape(E, B * cap_c)], axis=1)
        ys = moe_ffn((idx * SUBLANES).reshape(-1).astype(jnp.int32), h2, hc2, moe_w_gate, moe_w_up, moe_w_down, l,
                     gs[..., None], g2, cg2, B * cap_l, tf=512)
        x = combine_expert_outputs(x.reshape(B * N, D), ys[0], rows_l.reshape(-1).astype(jnp.int32),
                                   None if need_ctx else final_norm_g).reshape(B, N, D)
        if need_ctx:
            xc = xc.reshape(B * M, D).at[rows_c.reshape(-1)].add(ys[1].reshape(-1, D)).reshape(B, M, D)
    return x
```

```python
import functools

import numpy as np
import jax
import jax.numpy as jnp
from jax import lax
from jax.experimental import pallas as pl
from jax.experimental.pallas import tpu as pltpu

D_MODEL = 1024
DEPTH = 4
GRID_W = 64
N_GROUPS = 4
GROUP_W = D_MODEL // N_GROUPS
GLA_HEADS = 4
GLA_DK = GROUP_W // GLA_HEADS
GLA_W = GLA_HEADS * GLA_DK
GLA_GATE_RANK = 16
GLA_TAU = 16.0
GLA_CHUNK = 64
GMLP_GROUPS = 4
GMLP_W = GROUP_W
GMLP_CHUNK = 128
LRU_W = GROUP_W
LRU_BLOCKS = 4
LRU_C = 8.0
CONV_W = 4
NAT_HEADS = 4
NAT_DH = GROUP_W // NAT_HEADS
NAT_W = NAT_HEADS * NAT_DH
NAT_KR_MAX = 8
NAT_KC = 16
N_EXPERTS = 16
EXPERT_FF = 2 * D_MODEL
EC_CAPACITY = 2
ROPE_BASE = 10000.0
EPS = 1e-6

IN_SIZES = (GLA_W, GLA_W, GLA_W, GLA_W, GLA_GATE_RANK, GLA_GATE_RANK,
            GMLP_W, GMLP_W, LRU_W, LRU_W, NAT_W, NAT_W, NAT_W)
IN_COLS = sum(IN_SIZES)

V7X_VMEM_LIMIT_BYTES = 56 * 1024 * 1024
LANES = 128
SUBLANES = 8
F32 = jnp.float32
BF16 = jnp.bfloat16
NEG_BIG = -1e30
GATE_PAD = LANES

NT_DIMS = (((1,), (1,)), ((), ()))
TN_DIMS = (((0,), (0,)), ((), ()))


def _cparams(sem):
    return pltpu.CompilerParams(dimension_semantics=sem, vmem_limit_bytes=V7X_VMEM_LIMIT_BYTES)


def _dot(a, b):
    return jnp.dot(a, b, preferred_element_type=F32)


def _split3(x):
    hi = x.astype(BF16)
    r1 = x - hi.astype(F32)
    mid = r1.astype(BF16)
    lo = (r1 - mid.astype(F32)).astype(BF16)
    return hi, mid, lo


def _head_block_mask(n):
    r = lax.broadcasted_iota(jnp.int32, (n, n), 0) // GLA_DK
    c = lax.broadcasted_iota(jnp.int32, (n, n), 1) // GLA_DK
    return r == c


def _axial_rope(u, cos, sin):
    lane = lax.broadcasted_iota(jnp.int32, u.shape, 1)
    first = (lane % (GLA_DK // 2)) < (GLA_DK // 4)
    partner = jnp.where(first, pltpu.roll(u, GLA_W - GLA_DK // 4, axis=1), pltpu.roll(u, GLA_DK // 4, axis=1))
    return u * cos + partner * sin


def _norm_inproj_kernel(*refs, rope):
    refs = list(refs)
    x_ref, g_ref, sc_ref, sh_ref, w_ref = refs[:5]
    refs = refs[5:]
    if rope:
        cos_ref, sin_ref = refs[:2]
        refs = refs[2:]
    gla_ref, gmlp_ref, lru_ref, nat_ref, gate_ref = refs
    x = x_ref[0]
    y = x * lax.rsqrt(jnp.mean(x * x, axis=-1, keepdims=True) + EPS)
    h = ((y * g_ref[...]) * (1.0 + sc_ref[0]) + sh_ref[0]).astype(BF16)
    o = 0
    for ref in (gla_ref, gmlp_ref, lru_ref, nat_ref, gate_ref):
        w = ref.shape[-1]
        ref[0] = _dot(h, w_ref[:, o:o + w]).astype(ref.dtype)
        o += w
    if rope:
        for c in range(2):
            cols = slice(c * GLA_W, (c + 1) * GLA_W)
            gla_ref[0, :, cols] = _axial_rope(gla_ref[0, :, cols], cos_ref[...], sin_ref[...])


def permute_w_in(w):
    g0 = 4 * GLA_W
    g1 = g0 + 2 * GLA_GATE_RANK
    pad = jnp.zeros((w.shape[0], GATE_PAD - 2 * GLA_GATE_RANK), w.dtype)
    return jnp.concatenate([w[:, :g0], w[:, g1:], w[:, g0:g1], pad], axis=1).astype(BF16)


def norm_inproj(x, g, scale, shift, wp, tm, tables=None):
    B, n, D = x.shape
    C = wp.shape[1]
    widths = (4 * GLA_W, 2 * GMLP_W, 2 * LRU_W, 3 * NAT_W, GATE_PAD)
    dtypes = (F32, F32, F32, BF16, F32)
    row = lambda b, i: (b, i, 0)
    per_b = lambda b, i: (b, 0, 0)
    full = lambda b, i: (0, 0)
    rope_specs = [pl.BlockSpec((tm, GLA_W), lambda b, i: (i, 0))] * 2 if tables is not None else []
    return pl.pallas_call(
        functools.partial(_norm_inproj_kernel, rope=tables is not None),
        grid=(B, n // tm),
        in_specs=[
            pl.BlockSpec((1, tm, D), row),
            pl.BlockSpec((1, D), full),
            pl.BlockSpec((1, 1, D), per_b),
            pl.BlockSpec((1, 1, D), per_b),
            pl.BlockSpec((D, C), full),
        ] + rope_specs,
        out_specs=[pl.BlockSpec((1, tm, w), row) for w in widths],
        out_shape=[jax.ShapeDtypeStruct((B, n, w), dt) for w, dt in zip(widths, dtypes)],
        compiler_params=_cparams(("parallel", "parallel")),
        name="norm_inproj",
    )(x, g.reshape(1, D), scale.reshape(B, 1, D), shift.reshape(B, 1, D), wp, *(tables or ()))


def _gla_kernel(*refs, tm, reverse, combine):
    refs = list(refs)
    q_ref, k_ref, v_ref, gate_ref, wg_ref, bg_ref, tri_ref, ones_ref, s0_ref = refs[:9]
    refs = refs[9:]
    if combine:
        ob_ref, og_ref, ng_ref = refs[:3]
        refs = refs[3:]
    o_ref, sfin_ref, st, obuf = refs
    i = pl.program_id(1)
    nch = tm // GLA_CHUNK

    @pl.when(i == 0)
    def _():
        st[...] = s0_ref[0]

    q = q_ref[0] * (GLA_DK ** -0.5)
    k = k_ref[0]
    a = _dot(gate_ref[0].astype(BF16), wg_ref[...]) + bg_ref[...]
    la = jax.nn.log_sigmoid(a) / GLA_TAU
    pieces = _split3(la)
    cw = tri_ref.shape[0]
    halves = [slice(h * cw, (h + 1) * cw) for h in range(tm // cw)]
    b = jnp.concatenate([sum(_dot(tri_ref[...], p[h]) for p in pieces) for h in halves], axis=0)
    bl = jnp.concatenate([sum(_dot(ones_ref[...], p[h]) for p in pieces) for h in halves], axis=0)
    q_in = (q * jnp.exp(b)).astype(BF16)
    k_in = (k * jnp.exp(-b)).astype(BF16)
    k_end = (k * jnp.exp(bl - b)).astype(BF16)
    dec = jnp.exp(bl)
    vb = v_ref[0].astype(BF16)

    head_mask = _head_block_mask(GLA_W)
    cr = lax.broadcasted_iota(jnp.int32, (GLA_CHUNK, GLA_W), 0)
    cs = lax.broadcasted_iota(jnp.int32, (GLA_CHUNK, GLA_W), 1) % GLA_CHUNK
    causal = (cs >= cr) if reverse else (cs <= cr)
    zero = jnp.zeros((), BF16)
    chunks = range(nch - 1, -1, -1) if reverse else range(nch)
    for c in chunks:
        sl = slice(c * GLA_CHUNK, (c + 1) * GLA_CHUNK)
        kbd_t = jnp.where(head_mask, jnp.concatenate([k_in[sl]] * GLA_HEADS, axis=0), zero)
        vbd = jnp.where(head_mask, jnp.concatenate([vb[sl]] * GLA_HEADS, axis=0), zero)
        att = lax.dot_general(q_in[sl], kbd_t, NT_DIMS, preferred_element_type=F32)
        att = jnp.where(causal, att, 0.0).astype(BF16)
        s_t = st[...]
        o = _dot(att, vbd) + lax.dot_general(q_in[sl], s_t.astype(BF16), NT_DIMS, preferred_element_type=F32)
        upd = lax.dot_general(vb[sl], k_end[sl], TN_DIMS, preferred_element_type=F32)
        st[...] = s_t * dec[c * GLA_CHUNK:c * GLA_CHUNK + 1] + jnp.where(head_mask, upd, 0.0)
        obuf[sl, :] = o

    sfin_ref[0] = st[...]

    if not combine:
        o_ref[0] = obuf[...]
    else:
        o = obuf[...] + ob_ref[0]
        sq_hi, sq_mid, _ = _split3(o * o)
        avg = jnp.where(head_mask, 1.0 / GLA_DK, 0.0).astype(BF16)
        ms = _dot(sq_hi, avg) + _dot(sq_mid, avg)
        og = og_ref[0]
        o_ref[0] = (o * lax.rsqrt(ms + EPS) * ng_ref[...] * (og * jax.nn.sigmoid(og))).astype(o_ref.dtype)


def _chunk_sum_matrices(tm, reverse):
    r = np.arange(tm)[:, None]
    c = np.arange(tm)[None, :]
    same = (r // GLA_CHUNK) == (c // GLA_CHUNK)
    tri = same & ((c >= r) if reverse else (c <= r))
    return jnp.asarray(tri, BF16), jnp.asarray(same, BF16)


def gla_direction(gla, gate, wg, bg, s0, tm, reverse, combine=None):
    B, L, _ = gla.shape
    nt = L // tm
    W = GLA_W
    blk = (lambda i: nt - 1 - i) if reverse else (lambda i: i)
    col = lambda j: (lambda b, i: (b, blk(i), j))
    full = lambda b, i: (0, 0)
    cw = min(tm, 256)
    tri, ones = _chunk_sum_matrices(cw, reverse)
    args = [gla, gla, gla, gate, wg, bg, tri, ones, s0]
    specs = [
        pl.BlockSpec((1, tm, W), col(0)), pl.BlockSpec((1, tm, W), col(1)), pl.BlockSpec((1, tm, W), col(2)),
        pl.BlockSpec((1, tm, GATE_PAD), col(0)),
        pl.BlockSpec((GATE_PAD, W), full), pl.BlockSpec((1, W), full),
        pl.BlockSpec((cw, cw), full), pl.BlockSpec((cw, cw), full),
        pl.BlockSpec((1, W, W), lambda b, i: (b, 0, 0)),
    ]
    if combine is not None:
        ob, norm_g = combine
        args += [ob, gla, norm_g]
        specs += [pl.BlockSpec((1, tm, W), col(0)), pl.BlockSpec((1, tm, W), col(3)), pl.BlockSpec((1, W), full)]
    return pl.pallas_call(
        functools.partial(_gla_kernel, tm=tm, reverse=reverse, combine=combine is not None),
        grid=(B, nt),
        in_specs=specs,
        out_specs=[pl.BlockSpec((1, tm, W), col(0)), pl.BlockSpec((1, W, W), lambda b, i: (b, 0, 0))],
        out_shape=[jax.ShapeDtypeStruct((B, L, W), F32 if combine is None else BF16),
                   jax.ShapeDtypeStruct((B, W, W), F32)],
        scratch_shapes=[pltpu.VMEM((W, W), F32), pltpu.VMEM((tm, W), F32)],
        compiler_params=_cparams(("parallel", "arbitrary")),
        name="gla_bwd" if reverse else "gla_fwd",
    )(*args)


def rope_tables(n):
    quarter = GLA_DK // 4
    pos = jnp.arange(n)
    inv = ROPE_BASE ** (-jnp.arange(quarter, dtype=F32) / quarter)
    row = (pos // GRID_W).astype(F32)
    colp = (pos % GRID_W).astype(F32)
    ar = row[:, None] * inv[None, :]
    ac = colp[:, None] * inv[None, :]
    cos = jnp.concatenate([jnp.cos(ar), jnp.cos(ar), jnp.cos(ac), jnp.cos(ac)], axis=1)
    sin = jnp.concatenate([-jnp.sin(ar), jnp.sin(ar), -jnp.sin(ac), jnp.sin(ac)], axis=1)
    return jnp.tile(cos, (1, GLA_HEADS)), jnp.tile(sin, (1, GLA_HEADS))


def gla_mixer(gla, gate, cgla, cgate, w_gate, b_gate, norm_g, need_ctx, tm):
    B, M, _ = cgla.shape
    zeros = jnp.zeros((B, GLA_W, GLA_W), F32)
    ng = jnp.tile(norm_g, GLA_HEADS).reshape(1, GLA_W)
    wgs, bgs = [], []
    for d in range(2):
        wg = jnp.pad(w_gate[d], ((d * GLA_GATE_RANK, GATE_PAD - (d + 1) * GLA_GATE_RANK), (0, 0)))
        wgs.append(wg.astype(BF16))
        bgs.append(b_gate[d].reshape(1, GLA_W))
    ocb, sb = gla_direction(cgla, cgate, wgs[1], bgs[1], zeros, M, True)
    ob, _ = gla_direction(gla, gate, wgs[1], bgs[1], sb, tm, True)
    if need_ctx:
        oc, sf = gla_direction(cgla, cgate, wgs[0], bgs[0], zeros, M, False, combine=(ocb, ng))
    else:
        oc, sf = gla_direction(cgla, cgate, wgs[0], bgs[0], zeros, M, False)
    o, _ = gla_direction(gla, gate, wgs[0], bgs[0], sf, tm, False, combine=(ob, ng))
    return o, oc


def _gmlp_kernel(u_ref, v_ref, g_ref, w_ref, b_ref, o_ref, *, tm):
    v = v_ref[0]
    vn = (v * lax.rsqrt(jnp.mean(v * v, axis=-1, keepdims=True) + EPS) * g_ref[...]).astype(BF16)
    grp = lax.broadcasted_iota(jnp.int32, (GMLP_CHUNK, GMLP_W), 1) // (GMLP_W // GMLP_GROUPS)
    zero = jnp.zeros((), BF16)
    for c in range(tm // GMLP_CHUNK):
        sl = slice(c * GMLP_CHUNK, (c + 1) * GMLP_CHUNK)
        mixed = b_ref[...]
        for g in range(GMLP_GROUPS):
            mixed = mixed + _dot(w_ref[g], jnp.where(grp == g, vn[sl], zero))
        o_ref[0, sl, :] = (u_ref[0, sl, :] * mixed).astype(o_ref.dtype)


def gmlp_mixer(gm, norm_g, w_s, b_s, tm):
    B, L, _ = gm.shape
    W = GMLP_W
    bias = jnp.repeat(b_s.T, W // GMLP_GROUPS, axis=1)
    return pl.pallas_call(
        functools.partial(_gmlp_kernel, tm=tm),
        grid=(B, L // tm),
        in_specs=[
            pl.BlockSpec((1, tm, W), lambda b, i: (b, i, 0)),
            pl.BlockSpec((1, tm, W), lambda b, i: (b, i, 1)),
            pl.BlockSpec((1, W), lambda b, i: (0, 0)),
            pl.BlockSpec((GMLP_GROUPS, GMLP_CHUNK, GMLP_CHUNK), lambda b, i: (0, 0, 0)),
            pl.BlockSpec((GMLP_CHUNK, W), lambda b, i: (0, 0)),
        ],
        out_specs=pl.BlockSpec((1, tm, W), lambda b, i: (b, i, 0)),
        out_shape=jax.ShapeDtypeStruct((B, L, W), BF16),
        compiler_params=_cparams(("parallel", "parallel")),
        name="gmlp",
    )(gm, gm, norm_g.reshape(1, W), w_s.astype(BF16), bias)


LRU_UNROLL = 8


def _lru_kernel(*refs, tm, reverse, combine):
    refs = list(refs)
    x_ref, xp_ref, xn_ref, cw_ref, cb_ref, wr_ref, br_ref, wi_ref, bi_ref, ncs_ref, h0_ref = refs[:11]
    refs = refs[11:]
    if combine:
        hb_ref, ly_ref = refs[:2]
        refs = refs[2:]
    o_ref, ext, a_s, b_s, carry = refs
    W = LRU_W
    i = pl.program_id(1)
    nt = pl.num_programs(1)
    t = (nt - 1 - i) if reverse else i
    H = SUBLANES

    ext[0:H] = jnp.where(t > 0, xp_ref[0], 0.0)
    ext[H:H + tm] = x_ref[0]
    ext[H + tm:2 * H + tm] = jnp.where(t < nt - 1, xn_ref[0], 0.0)
    xc = cb_ref[...]
    for tap in range(CONV_W):
        xc = xc + cw_ref[tap:tap + 1, :] * ext[H - CONV_W // 2 + tap:H - CONV_W // 2 + tap + tm]
    xcb = xc.astype(BF16)
    r = jax.nn.sigmoid(_dot(xcb, wr_ref[...]) + br_ref[...])
    ig = jax.nn.sigmoid(_dot(xcb, wi_ref[...]) + bi_ref[...])
    log_a = ncs_ref[...] * r
    a = jnp.exp(log_a)
    b = jnp.sqrt(-jnp.tanh(log_a) * (a * a + 1.0)) * (ig * xc)

    rowi = lax.broadcasted_iota(jnp.int32, (tm, W), 0) % H
    for s in (1, 2, 4):
        if reverse:
            ok = rowi < H - s
            a_sh = pltpu.roll(a, tm - s, axis=0)
            b_sh = pltpu.roll(b, tm - s, axis=0)
        else:
            ok = rowi >= s
            a_sh = pltpu.roll(a, s, axis=0)
            b_sh = pltpu.roll(b, s, axis=0)
        b = a * jnp.where(ok, b_sh, 0.0) + b
        a = a * jnp.where(ok, a_sh, 1.0)
    a_s[...] = a
    b_s[...] = b

    @pl.when(i == 0)
    def _():
        carry[...] = jnp.broadcast_to(h0_ref[0], (H, W))

    ng = tm // H

    def body(j, h):
        for u in range(LRU_UNROLL):
            g = j * LRU_UNROLL + u
            g = (ng - 1 - g) if reverse else g
            rows = pl.ds(pl.multiple_of(g * H, H), H)
            hg = b_s[rows, :] + a_s[rows, :] * h
            b_s[rows, :] = hg
            h = jnp.broadcast_to(hg[0:1] if reverse else hg[H - 1:H], (H, W))
        return h

    carry[...] = lax.fori_loop(0, ng // LRU_UNROLL, body, carry[...])

    if combine:
        o_ref[0] = ((b_s[...] + hb_ref[0]) * jax.nn.gelu(ly_ref[0])).astype(o_ref.dtype)
    else:
        o_ref[0] = b_s[...]


def lru_direction(lru, conv_w, conv_b, wr, br, wi, bi, ncs, h0, tm, reverse, hb=None):
    B, L, _ = lru.shape
    nt = L // tm
    W = LRU_W
    H = SUBLANES
    nh = L // H
    per = tm // H
    blk = (lambda i: nt - 1 - i) if reverse else (lambda i: i)
    full = lambda b, i: (0, 0)
    args = [lru, lru, lru, conv_w, conv_b, wr, br, wi, bi, ncs, h0]
    specs = [
        pl.BlockSpec((1, tm, W), lambda b, i: (b, blk(i), 0)),
        pl.BlockSpec((1, H, W), lambda b, i: (b, jnp.maximum(blk(i) * per - 1, 0), 0)),
        pl.BlockSpec((1, H, W), lambda b, i: (b, jnp.minimum((blk(i) + 1) * per, nh - 1), 0)),
        pl.BlockSpec((CONV_W, W), full), pl.BlockSpec((1, W), full),
        pl.BlockSpec((W, W), full), pl.BlockSpec((1, W), full),
        pl.BlockSpec((W, W), full), pl.BlockSpec((1, W), full),
        pl.BlockSpec((1, W), full),
        pl.BlockSpec((1, 1, W), lambda b, i: (b, 0, 0)),
    ]
    if hb is not None:
        args += [hb, lru]
        specs += [pl.BlockSpec((1, tm, W), lambda b, i: (b, blk(i), 0)),
                  pl.BlockSpec((1, tm, W), lambda b, i: (b, blk(i), 1))]
    return pl.pallas_call(
        functools.partial(_lru_kernel, tm=tm, reverse=reverse, combine=hb is not None),
        grid=(B, nt),
        in_specs=specs,
        out_specs=pl.BlockSpec((1, tm, W), lambda b, i: (b, blk(i), 0)),
        out_shape=jax.ShapeDtypeStruct((B, L, W), F32 if hb is None else BF16),
        scratch_shapes=[pltpu.VMEM((tm + 2 * H, W), F32), pltpu.VMEM((tm, W), F32), pltpu.VMEM((tm, W), F32),
                        pltpu.VMEM((H, W), F32)],
        compiler_params=_cparams(("parallel", "arbitrary")),
        name="lru_bwd" if reverse else "lru_fwd",
    )(*args)


def _block_diag(w):
    G, n, _ = w.shape
    same = jnp.eye(G, dtype=w.dtype)
    return (w[:, :, None, :] * same[:, None, :, None]).reshape(G * n, G * n)


def lru_mixer(lru, clru, conv_w, conv_b, w_r, b_r, w_i, b_i, lam, need_ctx, tm):
    B, M, _ = clru.shape
    W = LRU_W
    cb = conv_b.reshape(1, W)
    ncs = -LRU_C * jax.nn.softplus(-lam.astype(F32))
    p = [(_block_diag(w_r[d]).astype(BF16), b_r[d].reshape(1, W), _block_diag(w_i[d]).astype(BF16),
          b_i[d].reshape(1, W), ncs[d].reshape(1, W)) for d in range(2)]
    zeros = jnp.zeros((B, 1, W), F32)
    hcb = lru_direction(clru, conv_w, cb, *p[1], zeros, M, True)
    hb = lru_direction(lru, conv_w, cb, *p[1], hcb[:, 0:1], tm, True)
    hcf = lru_direction(clru, conv_w, cb, *p[0], zeros, M, False)
    out = lru_direction(lru, conv_w, cb, *p[0], hcf[:, M - 1:M], tm, False, hb=hb)
    out_c = None
    if need_ctx:
        out_c = lru_direction(clru, conv_w, cb, *p[0], zeros, M, False, hb=hcb)
    return out, out_c


NAT_QROWS = 4
NAT_KROWS = 3 * NAT_QROWS
NAT_TQ = NAT_QROWS * GRID_W
NAT_TK = NAT_KROWS * GRID_W


def _softmax_pv(s_loc, s_ctx, v_loc, v_ctx):
    m = jnp.maximum(jnp.max(s_loc, axis=-1, keepdims=True), jnp.max(s_ctx, axis=-1, keepdims=True))
    e_loc = jnp.exp(s_loc - m)
    e_ctx = jnp.exp(s_ctx - m)
    den = jnp.sum(e_loc, axis=-1, keepdims=True) + jnp.sum(e_ctx, axis=-1, keepdims=True)
    o = _dot(e_loc.astype(BF16), v_loc) + _dot(e_ctx.astype(BF16), v_ctx)
    return o / den


def _nat_kernel(q_ref, kp_ref, kc_ref, kn_ref, vp_ref, vc_ref, vn_ref, ck_ref, cv_ref, bias_ref, o_ref):
    q = q_ref[0]
    ck = ck_ref[0]
    cv = cv_ref[0]
    scale = NAT_DH ** -0.5
    outs = []
    for h in range(NAT_HEADS):
        hs = slice(h * NAT_DH, (h + 1) * NAT_DH)
        qh = q[:, hs]
        s_loc = jnp.concatenate(
            [lax.dot_general(qh, k_ref[0][:, hs], NT_DIMS, preferred_element_type=F32)
             for k_ref in (kp_ref, kc_ref, kn_ref)], axis=-1) * scale + bias_ref[0, h]
        s_ctx = lax.dot_general(qh, ck[:, hs], NT_DIMS, preferred_element_type=F32) * scale
        v_loc = jnp.concatenate([v_ref[0][:, hs] for v_ref in (vp_ref, vc_ref, vn_ref)], axis=0)
        outs.append(_softmax_pv(s_loc, s_ctx, v_loc, cv[:, hs]))
    o_ref[0] = jnp.concatenate(outs, axis=-1).astype(o_ref.dtype)


def nat_bias_tables(rpb, rows):
    kr = NAT_KR_MAX
    qc = np.arange(GRID_W)
    kcol = np.arange(GRID_W)
    cs = np.clip(qc - NAT_KC // 2, 0, GRID_W - NAT_KC)
    valid_c = (kcol[None, :] >= cs[:, None]) & (kcol[None, :] < cs[:, None] + NAT_KC)
    edge = GRID_W - NAT_KC
    padded = jnp.pad(rpb.astype(F32), ((0, 0), (0, 0), (edge, edge)), mode="edge")
    by_col = jnp.stack([padded[:, :, GRID_W - 1 - q:2 * GRID_W - 1 - q] for q in range(GRID_W)], axis=1)
    by_col = jnp.pad(by_col, ((0, 0), (0, 0), (NAT_KROWS, NAT_KROWS), (0, 0)))
    tables = []
    for r0, ks in ((0, 0), (NAT_QROWS, NAT_QROWS - kr // 2), (rows - NAT_QROWS, rows - NAT_KROWS)):
        r = r0 + np.arange(NAT_QROWS)
        krow = ks + np.arange(NAT_KROWS)
        rs = np.clip(r - kr // 2, 0, rows - kr)
        valid_r = (krow[None, :] >= rs[:, None]) & (krow[None, :] < rs[:, None] + kr)
        starts = ks - r + NAT_KR_MAX - 1 + NAT_KROWS
        bias = jnp.stack([by_col[:, :, int(s):int(s) + NAT_KROWS] for s in starts], axis=1)
        mask = valid_r[:, None, :, None] & valid_c[None, :, None, :]
        bias = jnp.where(jnp.asarray(mask)[None], bias, NEG_BIG)
        tables.append(bias.reshape(rpb.shape[0], NAT_TQ, NAT_TK))
    return jnp.stack(tables)


def nat_mixer(nat, cnat, bias, layer):
    B, N, _ = nat.shape
    M = cnat.shape[1]
    W = NAT_W
    nb = N // NAT_TQ
    T = NAT_TQ
    centre = lambda j: jnp.clip(j, 1, nb - 2)
    near = lambda c, d: (lambda b, j: (b, centre(j) + d, c))
    variant = lambda b, j: (jnp.where(j == 0, 0, jnp.where(j == nb - 1, 2, 1)), layer, 0, 0)
    return pl.pallas_call(
        _nat_kernel,
        grid=(B, nb),
        in_specs=[
            pl.BlockSpec((1, T, W), lambda b, j: (b, j, 0)),
            pl.BlockSpec((1, T, W), near(1, -1)), pl.BlockSpec((1, T, W), near(1, 0)), pl.BlockSpec((1, T, W), near(1, 1)),
            pl.BlockSpec((1, T, W), near(2, -1)), pl.BlockSpec((1, T, W), near(2, 0)), pl.BlockSpec((1, T, W), near(2, 1)),
            pl.BlockSpec((1, M, W), lambda b, j: (b, 0, 1)),
            pl.BlockSpec((1, M, W), lambda b, j: (b, 0, 2)),
            pl.BlockSpec((1, NAT_HEADS, NAT_TQ, NAT_TK), variant),
        ],
        out_specs=pl.BlockSpec((1, T, W), lambda b, j: (b, j, 0)),
        out_shape=jax.ShapeDtypeStruct((B, N, W), BF16),
        compiler_params=_cparams(("parallel", "arbitrary")),
        name="nat",
    )(nat, nat, nat, nat, nat, nat, nat, cnat, cnat, bias)


def _ctx_attn_kernel(q_ref, k_ref, v_ref, o_ref):
    q = q_ref[0]
    k = k_ref[0]
    v = v_ref[0]
    scale = NAT_DH ** -0.5
    outs = []
    for h in range(NAT_HEADS):
        hs = slice(h * NAT_DH, (h + 1) * NAT_DH)
        s = lax.dot_general(q[:, hs], k[:, hs], NT_DIMS, preferred_element_type=F32) * scale
        e = jnp.exp(s - jnp.max(s, axis=-1, keepdims=True))
        outs.append(_dot(e.astype(BF16), v[:, hs]) / jnp.sum(e, axis=-1, keepdims=True))
    o_ref[0] = jnp.concatenate(outs, axis=-1).astype(o_ref.dtype)


def ctx_attention(cnat):
    B, M, _ = cnat.shape
    W = NAT_W
    return pl.pallas_call(
        _ctx_attn_kernel,
        grid=(B,),
        in_specs=[pl.BlockSpec((1, M, W), lambda b, c=c: (b, 0, c)) for c in range(3)],
        out_specs=pl.BlockSpec((1, M, W), lambda b: (b, 0, 0)),
        out_shape=jax.ShapeDtypeStruct((B, M, W), BF16),
        compiler_params=_cparams(("parallel",)),
        name="ctx_attn",
    )(cnat, cnat, cnat)


def _outproj_kernel(m0_ref, m1_ref, m2_ref, m3_ref, w_ref, x_ref, g1_ref, n2_ref, sc_ref, sh_ref, wr_ref,
                    xo_ref, h2_ref, aff_ref):
    y = 0.0
    for gi, m_ref in enumerate((m0_ref, m1_ref, m2_ref, m3_ref)):
        y = y + _dot(m_ref[0].astype(BF16), w_ref[gi * GROUP_W:(gi + 1) * GROUP_W, :])
    x = x_ref[0] + g1_ref[0] * y
    xo_ref[0] = x
    xn = x * lax.rsqrt(jnp.mean(x * x, axis=-1, keepdims=True) + EPS)
    h2 = (xn * n2_ref[...]) * (1.0 + sc_ref[0]) + sh_ref[0]
    tm = h2.shape[0]
    for j in range(SUBLANES):
        h2_ref[pl.ds(j, tm, stride=SUBLANES), :] = h2[:, j * LANES:(j + 1) * LANES]
    logits = _dot(h2.astype(BF16), wr_ref[...])
    e = jnp.exp(logits - jnp.max(logits, axis=-1, keepdims=True))
    aff_ref[0] = e / jnp.sum(e, axis=-1, keepdims=True)


def outproj_norm_router(parts, w_out, x, g1, n2, scale2, shift2, w_router, tm):
    B, n, D = x.shape
    E = w_router.shape[1]
    row = lambda b, i: (b, i, 0)
    per_b = lambda b, i: (b, 0, 0)
    full = lambda b, i: (0, 0)
    return pl.pallas_call(
        _outproj_kernel,
        grid=(B, n // tm),
        in_specs=[pl.BlockSpec((1, tm, GROUP_W), row)] * N_GROUPS + [
            pl.BlockSpec((D, D), full),
            pl.BlockSpec((1, tm, D), row),
            pl.BlockSpec((1, 1, D), per_b),
            pl.BlockSpec((1, D), full),
            pl.BlockSpec((1, 1, D), per_b),
            pl.BlockSpec((1, 1, D), per_b),
            pl.BlockSpec((D, E), full),
        ],
        out_specs=[
            pl.BlockSpec((1, tm, D), row),
            pl.BlockSpec((tm * SUBLANES, LANES), lambda b, i: (b * (n // tm) + i, 0)),
            pl.BlockSpec((1, tm, E), row),
        ],
        out_shape=[
            jax.ShapeDtypeStruct((B, n, D), F32),
            jax.ShapeDtypeStruct((B * n * SUBLANES, LANES), F32),
            jax.ShapeDtypeStruct((B, n, E), F32),
        ],
        compiler_params=_cparams(("parallel", "parallel")),
        name="outproj_norm_router",
    )(*parts, w_out, x, g1.reshape(B, 1, D), n2.reshape(1, D), scale2.reshape(B, 1, D), shift2.reshape(B, 1, D),
      w_router)


MOE_ROW_CHUNK = 512
MOE_ISSUE_UNROLL = 8


def _moe_ffn_kernel(*refs, n_lat, n_ctx, n_batch):
    refs = list(refs)
    idx_ref, h_hbm = refs[:2]
    refs = refs[2:]
    if n_ctx:
        hc_hbm = refs.pop(0)
    wg_ref, wu_ref, wd_ref, gate_ref, g2_ref = refs[:5]
    refs = refs[5:]
    if n_ctx:
        cg2_ref = refs.pop(0)
    yl_hbm = refs.pop(0)
    if n_ctx:
        yc_ref = refs.pop(0)
    xf, xb, acc, ytiles, sem, out_sem = refs
    e = pl.program_id(0)
    f = pl.program_id(1)
    n_exp = pl.num_programs(0)
    last = pl.num_programs(1) - 1
    T = n_lat + n_ctx

    R = SUBLANES

    def start_rows(src, expert, first, count):
        def body(i, carry):
            for u in range(MOE_ISSUE_UNROLL):
                s = first + i * MOE_ISSUE_UNROLL + u
                r = pl.multiple_of(idx_ref[expert * T + s], R)
                pltpu.make_async_copy(src.at[pl.ds(r, R)], xf.at[pl.ds(pl.multiple_of(s * R, R), R)],
                                      sem.at[0]).start()
            return carry

        lax.fori_loop(0, count // MOE_ISSUE_UNROLL, body, 0)

    def start_gather(expert):
        start_rows(h_hbm, expert, 0, n_lat)
        if n_ctx:
            start_rows(hc_hbm, expert, n_lat, n_ctx)

    @pl.when(f == 0)
    def _():
        @pl.when(e == 0)
        def _():
            start_gather(e)

        pltpu.make_async_copy(h_hbm.at[pl.ds(0, T * R)], xf, sem.at[0]).wait()
        for j in range(R):
            xb[:, j * LANES:(j + 1) * LANES] = xf[pl.ds(j, T, stride=R), :].astype(BF16)

        @pl.when(e + 1 < n_exp)
        def _():
            start_gather(e + 1)

        acc[...] = jnp.zeros_like(acc)
        if n_ctx:
            yc_ref[...] = jnp.zeros_like(yc_ref)

    wg = wg_ref[0, 0].astype(BF16)
    wu = wu_ref[0, 0].astype(BF16)
    wd = wd_ref[0, 0].astype(BF16)

    def ffn(x):
        hg = _dot(x, wg)
        hu = _dot(x, wu)
        return _dot((hg * jax.nn.sigmoid(hg) * hu).astype(BF16), wd)

    def lat_rows(first, count, j):
        return pl.ds(first * R + j, count, stride=R)

    for i in range(n_lat // MOE_ROW_CHUNK):
        rows = slice(i * MOE_ROW_CHUNK, (i + 1) * MOE_ROW_CHUNK)
        acc[rows, :] += ffn(xb[rows, :])
    if n_ctx:
        yc_ref[0] += ffn(xb[n_lat:T, :])

    def out_copy(expert):
        dst = yl_hbm.at[pl.ds(pl.multiple_of(expert * (n_lat * R), n_lat * R), n_lat * R)]
        return pltpu.make_async_copy(ytiles, dst, out_sem.at[0])

    @pl.when(f == last)
    def _():
        @pl.when(e > 0)
        def _():
            out_copy(e - 1).wait()

        per_l = n_lat // n_batch
        for b in range(n_batch):
            rows = slice(b * per_l, (b + 1) * per_l)
            gate = gate_ref[0, rows, :]
            for j in range(R):
                lanes = slice(j * LANES, (j + 1) * LANES)
                ytiles[lat_rows(b * per_l, per_l, j), :] = acc[rows, lanes] * gate * g2_ref[b:b + 1, lanes]
        out_copy(e).start()

        @pl.when(e == n_exp - 1)
        def _():
            out_copy(e).wait()

        if n_ctx:
            per_c = n_ctx // n_batch
            for b in range(n_batch):
                rows = slice(b * per_c, (b + 1) * per_c)
                yc_ref[0, rows, :] = (yc_ref[0, rows, :] * gate_ref[0, n_lat + b * per_c:n_lat + (b + 1) * per_c, :]
                                      * cg2_ref[b:b + 1, :])


def moe_ffn(idx, h2, hc2, w_gate, w_up, w_down, layer, gates, g2, cg2, n_lat, tf):
    E, T, _ = gates.shape
    n_ctx = T - n_lat
    D = SUBLANES * LANES
    F = w_gate.shape[3]
    B = g2.shape[0]
    any_spec = pl.BlockSpec(memory_space=pl.ANY)
    per_e = lambda e, f, idx_ref: (e, 0, 0)
    full = lambda e, f, idx_ref: (0, 0)
    args = [h2] + ([hc2] if n_ctx else []) + [w_gate, w_up, w_down, gates, g2] + ([cg2] if n_ctx else [])
    specs = [any_spec] + ([any_spec] if n_ctx else []) + [
        pl.BlockSpec((1, 1, D, tf), lambda e, f, idx_ref: (layer, e, 0, f)),
        pl.BlockSpec((1, 1, D, tf), lambda e, f, idx_ref: (layer, e, 0, f)),
        pl.BlockSpec((1, 1, tf, D), lambda e, f, idx_ref: (layer, e, f, 0)),
        pl.BlockSpec((1, T, 1), per_e),
        pl.BlockSpec((B, D), full),
    ] + ([pl.BlockSpec((B, D), full)] if n_ctx else [])
    out_specs = [any_spec] + ([pl.BlockSpec((1, n_ctx, D), per_e)] if n_ctx else [])
    out_shape = [jax.ShapeDtypeStruct((E * n_lat * SUBLANES, LANES), F32)] + (
        [jax.ShapeDtypeStruct((E, n_ctx, D), F32)] if n_ctx else [])
    return pl.pallas_call(
        functools.partial(_moe_ffn_kernel, n_lat=n_lat, n_ctx=n_ctx, n_batch=B),
        grid_spec=pltpu.PrefetchScalarGridSpec(
            num_scalar_prefetch=1,
            grid=(E, F // tf),
            in_specs=specs,
            out_specs=out_specs,
            scratch_shapes=[pltpu.VMEM((T * SUBLANES, LANES), F32), pltpu.VMEM((T, D), BF16),
                            pltpu.VMEM((n_lat, D), F32), pltpu.VMEM((n_lat * SUBLANES, LANES), F32),
                            pltpu.SemaphoreType.DMA((1,)), pltpu.SemaphoreType.DMA((1,))],
        ),
        out_shape=out_shape,
        compiler_params=_cparams(("arbitrary", "arbitrary")),
        name="moe_ffn",
    )(idx, *args)


COMBINE_TILE = 256
COMBINE_ISSUE_UNROLL = 8


def _combine_kernel(src_ref, dst_ref, bnd_ref, kmax_ref, y_hbm, x_ref, *refs, tm, norm):
    if norm:
        g_ref, o_ref, planes, sem = refs
    else:
        o_ref, planes, sem = refs
    i = pl.program_id(0)
    nt = pl.num_programs(0)
    R = SUBLANES
    U = COMBINE_ISSUE_UNROLL
    slab = tm * R

    def groups(t):
        return (bnd_ref[t + 1] - bnd_ref[t] + U - 1) // U

    def prepare(t, buf):
        def zero(k, carry):
            planes[buf, pl.ds(pl.multiple_of(k * slab, slab), slab), :] = jnp.zeros((slab, LANES), F32)
            return carry

        lax.fori_loop(0, jnp.maximum(kmax_ref[t], 1), zero, 0)
        first = bnd_ref[t]
        final = bnd_ref[t + 1] - 1

        def issue(g, carry):
            for u in range(U):
                a = first + g * U + u
                live = a <= final
                a = jnp.minimum(a, final)
                s = pl.multiple_of(src_ref[a], R)
                d = pl.multiple_of(jnp.where(live, dst_ref[a], N_EXPERTS * slab + u * R), R)
                pltpu.make_async_copy(y_hbm.at[pl.ds(s, R)], planes.at[buf, pl.ds(d, R)],
                                      sem.at[buf]).start(priority=u % 2)
            return carry

        lax.fori_loop(0, groups(t), issue, 0)

    @pl.when(i == 0)
    def _():
        prepare(0, 0)

    @pl.when(i + 1 < nt)
    def _():
        prepare(i + 1, (i + 1) % 2)

    buf = i % 2
    n_groups = groups(i)
    p = 1
    while p * U <= tm * N_EXPERTS:
        @pl.when((n_groups & p) != 0)
        def _(p=p):
            pltpu.make_async_copy(y_hbm.at[pl.ds(0, p * U * R)], planes.at[buf, pl.ds(0, p * U * R)],
                                  sem.at[buf]).wait()
        p *= 2

    kmax = kmax_ref[i]

    def add(k, carry):
        planes[buf, 0:slab, :] += planes[buf, pl.ds(pl.multiple_of(k * slab, slab), slab), :]
        return carry

    lax.fori_loop(1, kmax, add, 0)

    sumsq = jnp.zeros((tm, 1), F32)
    for j in range(R):
        lanes = slice(j * LANES, (j + 1) * LANES)
        y = x_ref[:, lanes] + planes[buf, pl.ds(j, tm, stride=R), :]
        o_ref[:, lanes] = y
        if norm:
            sumsq = sumsq + jnp.sum(y * y, axis=-1, keepdims=True)
    if norm:
        o_ref[...] = o_ref[...] * lax.rsqrt(sumsq / (R * LANES) + EPS) * g_ref[...]


def combine_expert_outputs(x, y_tiles, token_rows, norm_g=None):
    n, D = x.shape
    A = token_rows.shape[0]
    tm = COMBINE_TILE
    nt = n // tm
    tok, src = lax.sort_key_val(token_rows, jnp.arange(A, dtype=jnp.int32))
    pos = jnp.arange(A, dtype=jnp.int32)
    is_start = jnp.concatenate([jnp.ones((1,), bool), tok[1:] != tok[:-1]])
    rank = pos - lax.cummax(jnp.where(is_start, pos, 0))
    tile = tok // tm
    dst = (rank * tm + tok % tm) * SUBLANES
    edges = jnp.arange(nt + 1, dtype=jnp.int32) * tm
    bounds = jnp.sum(tok[None, :] < edges[:, None], axis=1).astype(jnp.int32)
    kmax = jnp.max(jnp.where(tile[None, :] == jnp.arange(nt, dtype=jnp.int32)[:, None], rank[None, :] + 1, 0),
                   axis=1).astype(jnp.int32)
    row = lambda i, *_: (i, 0)
    norm = norm_g is not None
    extra = [norm_g.reshape(1, D)] if norm else []
    return pl.pallas_call(
        functools.partial(_combine_kernel, tm=tm, norm=norm),
        grid_spec=pltpu.PrefetchScalarGridSpec(
            num_scalar_prefetch=4,
            grid=(nt,),
            in_specs=[pl.BlockSpec(memory_space=pl.ANY), pl.BlockSpec((tm, D), row)] + (
                [pl.BlockSpec((1, D), lambda i, *_: (0, 0))] if norm else []),
            out_specs=pl.BlockSpec((tm, D), row),
            scratch_shapes=[
                pltpu.VMEM((2, (N_EXPERTS * tm + COMBINE_ISSUE_UNROLL) * SUBLANES, LANES), F32),
                pltpu.SemaphoreType.DMA((2,))],
        ),
        out_shape=jax.ShapeDtypeStruct((n, D), F32),
        compiler_params=_cparams(("arbitrary",)),
        name="combine",
    )(src * SUBLANES, dst, bounds, kmax, y_tiles, x, *extra)


def route(aff, cap):
    return lax.top_k(jnp.swapaxes(aff, 1, 2), cap)


TOKEN_TILE = 512


def kernel(x, c, ctx, c_ctx, w_ada, b_ada, norm1_g, norm2_g, w_in, w_out, gla_w_gate, gla_b_gate, gla_norm_g,
           gmlp_norm_g, gmlp_w_s, gmlp_b_s, lru_conv_w, lru_conv_b, lru_w_r, lru_b_r, lru_w_i, lru_b_i,
           lru_lambda, nat_rpb, moe_w_router, moe_w_gate, moe_w_up, moe_w_down, final_norm_g):
    B, N, D = x.shape
    M = ctx.shape[1]
    E = N_EXPERTS
    tm = TOKEN_TILE
    cap_l = max(1, EC_CAPACITY * N // E)
    cap_c = max(1, EC_CAPACITY * M // E)
    sc = jax.nn.silu(c)
    scc = jax.nn.silu(c_ctx)
    xc = ctx
    batch_ids = jnp.arange(B, dtype=jnp.int32)[:, None, None]
    tables = rope_tables(N)
    mods = jnp.einsum('bd,ldk->lbk', jnp.concatenate([sc, scc[None]], axis=0), w_ada) + b_ada[:, None, :]
    nat_bias = nat_bias_tables(nat_rpb.reshape(DEPTH * NAT_HEADS, *nat_rpb.shape[2:]), N // GRID_W)
    for l in range(DEPTH):
        need_ctx = l < DEPTH - 1
        mod = mods[l, :B]
        mod_c = jnp.broadcast_to(mods[l, B:], (B, 6 * D))
        sh1, sc1, g1, sh2, sc2, g2 = jnp.split(mod, 6, axis=-1)
        csh1, csc1, cg1, csh2, csc2, cg2 = jnp.split(mod_c, 6, axis=-1)

        wp = permute_w_in(w_in[l])
        gla, gm, lru, nat, gate = norm_inproj(x, norm1_g[l], sc1, sh1, wp, tm, tables)
        cgla, cgm, clru, cnat, cgate = norm_inproj(xc, norm1_g[l], csc1, csh1, wp, M)

        gla_o, gla_c = gla_mixer(gla, gate, cgla, cgate, gla_w_gate[l], gla_b_gate[l], gla_norm_g[l], need_ctx, tm)
        gm_o = gmlp_mixer(gm, gmlp_norm_g[l], gmlp_w_s[l], gmlp_b_s[l], tm)
        lru_o, lru_c = lru_mixer(lru, clru, lru_conv_w[l], lru_conv_b[l], lru_w_r[l], lru_b_r[l], lru_w_i[l],
                                 lru_b_i[l], lru_lambda[l], need_ctx, tm)
        nat_o = nat_mixer(nat, cnat, nat_bias, l)

        wo = w_out[l].astype(BF16)
        wr = moe_w_router[l].astype(BF16)
        x, h2, aff = outproj_norm_router((gla_o, gm_o, lru_o, nat_o), wo, x, g1, norm2_g[l], sc2, sh2, wr, tm)
        gl, il = route(aff, cap_l)
        rows_l = jnp.swapaxes(il + batch_ids * N, 0, 1).reshape(E, B * cap_l)
        idx = rows_l
        gs = jnp.swapaxes(gl, 0, 1).reshape(E, B * cap_l)
        hc2 = None
        if need_ctx:
            gm_c = gmlp_mixer(cgm, gmlp_norm_g[l], gmlp_w_s[l], gmlp_b_s[l], M)
            nat_c = ctx_attention(cnat)
            xc, hc2, caff = outproj_norm_router((gla_c, gm_c, lru_c, nat_c), wo, xc, cg1, norm2_g[l], csc2, csh2,
                                                wr, M)
            gc, ic = route(caff, cap_c)
            rows_c = jnp.swapaxes(ic + batch_ids * M, 0, 1).reshape(E, B * cap_c)
            idx = jnp.concatenate([rows_l, rows_c], axis=1)
            gs = jnp.concatenate([gs, jnp.swapaxes(gc, 0, 1).reshape(E, B * cap_c)], axis=1)
        ys = moe_ffn((idx * SUBLANES).reshape(-1).astype(jnp.int32), h2, hc2, moe_w_gate, moe_w_up, moe_w_down, l,
                     gs[..., None], g2, cg2, B * cap_l, tf=512)
        x = combine_expert_outputs(x.reshape(B * N, D), ys[0], rows_l.reshape(-1).astype(jnp.int32),
                                   None if need_ctx else final_norm_g).reshape(B, N, D)
        if need_ctx:
            xc = xc.reshape(B * M, D).at[rows_c.reshape(-1)].add(ys[1].reshape(-1, D)).reshape(B, M, D)
    return x
```
